```python
import jax
import jax.numpy as jnp
from jax import lax
import numpy as np


D_MODEL = 1024
BATCH = 2
SEQ = 8192
DEPTH = 4

GRID_W = 64
CTX_LEN = 256
MLA_HEADS = 8
MLA_NOPE = 64
MLA_ROPE = 32
MLA_V = 64
MLA_Q_LORA = 256
MLA_KV_LORA = 128
MLA_SCALE = (MLA_NOPE + MLA_ROPE) ** -0.5
NA_HEADS = 4
NA_HEAD_DIM = 64
NA_WIN_R = 8
NA_WIN_C = 16
NA_COL_BLOCK = 16
NA_KEY_COLS = 32
NA_SCALE = NA_HEAD_DIM ** -0.5
FT_GROUPS = 4
FT_GROUP_DIM = 64
N_EXPERTS = 16
N_GROUPS = 4
EXPERTS_PER_GROUP = N_EXPERTS // N_GROUPS
TOP_K = 2
EXPERT_FF = 256

ROPE_THETA = 10000.0
NORM_EPS = 1e-6
Q_BLOCK = 128
NEG_INF = -1e30
N_BRANCH = 3
MLA_WIDTH = MLA_HEADS * MLA_V
NA_WIDTH = NA_HEADS * NA_HEAD_DIM
FT_WIDTH = FT_GROUPS * FT_GROUP_DIM
IN_SPLITS = (MLA_Q_LORA, MLA_KV_LORA, MLA_ROPE, NA_WIDTH, NA_WIDTH, NA_WIDTH, FT_WIDTH)
IN_WIDTH = sum(IN_SPLITS)

kernel_name = 'hybrid_mla_natten_fnet_groupmoe_block'


def rmsnorm(x, g):
    xf = x.astype(jnp.float32)
    y = xf * lax.rsqrt(jnp.mean(xf * xf, axis=-1, keepdims=True) + NORM_EPS)
    return (y * g.astype(jnp.float32)).astype(x.dtype)


def split_cols(a, sizes):
    return jnp.split(a, [int(o) for o in np.cumsum(sizes)[:-1]], axis=-1)


def axial_rope_tables(n_tok):
    n_freq = MLA_ROPE // 4
    inv_freq = ROPE_THETA ** (-jnp.arange(n_freq, dtype=jnp.float32) / n_freq)
    t = jnp.arange(n_tok, dtype=jnp.int32)
    row = (t // GRID_W).astype(jnp.float32)
    col = (t % GRID_W).astype(jnp.float32)
    ang = jnp.concatenate([row[:, None] * inv_freq, col[:, None] * inv_freq], axis=-1)
    return jnp.cos(ang), jnp.sin(ang)


def apply_rope(x, cos, sin):
    xf = x.astype(jnp.float32)
    half = xf.shape[-1] // 2
    x1, x2 = xf[..., :half], xf[..., half:]
    return jnp.concatenate([x1 * cos - x2 * sin, x2 * cos + x1 * sin], axis=-1).astype(x.dtype)


def dense_block_attention(q, k, v, scale):
    b, nq, h, dk = q.shape
    dv = v.shape[-1]
    nb = nq // Q_BLOCK
    qb = q.reshape(b, nb, Q_BLOCK, h, dk).transpose(1, 0, 2, 3, 4)

    def one_block(qblk):
        s = jnp.einsum('bqhd,bkhd->bhqk', qblk, k, preferred_element_type=jnp.float32) * scale
        p = jax.nn.softmax(s, axis=-1).astype(v.dtype)
        return jnp.einsum('bhqk,bkhd->bqhd', p, v)

    o = lax.map(one_block, qb)
    return o.transpose(1, 0, 2, 3, 4).reshape(b, nq, h, dv)


def mla_q(q_lat, q_norm_g, w_q_up, cos, sin):
    b, n, _ = q_lat.shape
    q = (rmsnorm(q_lat, q_norm_g) @ w_q_up).reshape(b, n, MLA_HEADS, MLA_NOPE + MLA_ROPE)
    if cos is None:
        return q
    q_nope, q_pe = q[..., :MLA_NOPE], q[..., MLA_NOPE:]
    return jnp.concatenate([q_nope, apply_rope(q_pe, cos[:, None, :], sin[:, None, :])], axis=-1)


def mla_kv(kv_lat, k_rope, kv_norm_g, w_kv_up, cos, sin):
    b, n, _ = kv_lat.shape
    kv = (rmsnorm(kv_lat, kv_norm_g) @ w_kv_up).reshape(b, n, MLA_HEADS, MLA_NOPE + MLA_V)
    k_nope, v = kv[..., :MLA_NOPE], kv[..., MLA_NOPE:]
    if cos is not None:
        k_rope = apply_rope(k_rope, cos, sin)
    k_pe = jnp.broadcast_to(k_rope[:, :, None, :], (b, n, MLA_HEADS, MLA_ROPE))
    return jnp.concatenate([k_nope, k_pe], axis=-1), v


def neighborhood_attention(q, k, v, k_ctx, v_ctx, rpb):
    b, s, hn, d = q.shape
    rows = s // GRID_W
    wr = min(NA_WIN_R, rows)
    n_cb = GRID_W // NA_COL_BLOCK
    r = jnp.arange(rows)
    row_idx = jnp.clip(r - wr // 2, 0, rows - wr)[:, None] + jnp.arange(wr)
    cb = jnp.arange(n_cb)
    col_idx = jnp.clip(cb * NA_COL_BLOCK - NA_WIN_C // 2, 0, GRID_W - NA_KEY_COLS)[:, None] + jnp.arange(NA_KEY_COLS)
    qcol = cb[:, None] * NA_COL_BLOCK + jnp.arange(NA_COL_BLOCK)
    win_start = jnp.clip(qcol - NA_WIN_C // 2, 0, GRID_W - NA_WIN_C)
    col_mask = (col_idx[:, None, :] >= win_start[..., None]) & (col_idx[:, None, :] < win_start[..., None] + NA_WIN_C)
    kg = k.reshape(b, rows, GRID_W, hn, d)[:, row_idx[:, None, :, None], col_idx[None, :, None, :]]
    vg = v.reshape(b, rows, GRID_W, hn, d)[:, row_idx[:, None, :, None], col_idx[None, :, None, :]]
    qg = q.reshape(b, rows, n_cb, NA_COL_BLOCK, hn, d)
    s_loc = jnp.einsum('brcqhd,brcwkhd->brchqwk', qg, kg, preferred_element_type=jnp.float32) * NA_SCALE
    dr_idx = row_idx - r[:, None] + (NA_WIN_R - 1)
    dc_idx = jnp.clip(col_idx[:, None, :] - qcol[:, :, None], -(NA_WIN_C - 1), NA_WIN_C - 1) + (NA_WIN_C - 1)
    bias = rpb.astype(jnp.float32)[:, dr_idx[:, None, None, :, None], dc_idx[None, :, :, None, :]]
    bias = jnp.transpose(bias, (1, 2, 0, 3, 4, 5))
    s_loc = jnp.where(col_mask[None, None, :, None, :, None, :], s_loc + bias[None], NEG_INF)
    s_loc = s_loc.reshape(b, rows, n_cb, hn, NA_COL_BLOCK, wr * NA_KEY_COLS)
    s_ctx = jnp.einsum('brcqhd,bkhd->brchqk', qg, k_ctx, preferred_element_type=jnp.float32) * NA_SCALE
    p = jax.nn.softmax(jnp.concatenate([s_loc, s_ctx], axis=-1), axis=-1).astype(v.dtype)
    n_loc = wr * NA_KEY_COLS
    p_loc = p[..., :n_loc].reshape(b, rows, n_cb, hn, NA_COL_BLOCK, wr, NA_KEY_COLS)
    p_ctx = p[..., n_loc:]
    o = jnp.einsum('brchqwk,brcwkhd->brcqhd', p_loc, vg) + jnp.einsum('brchqk,bkhd->brcqhd', p_ctx, v_ctx)
    return o.reshape(b, s, hn, d)


def fourier_mix(u):
    b, n, _ = u.shape
    ug = u.reshape(b, n, FT_GROUPS, FT_GROUP_DIM).astype(jnp.float32)
    f = jnp.fft.fft2(ug, axes=(1, 3), norm='ortho').real
    return f.reshape(b, n, FT_WIDTH).astype(u.dtype)


def merge_branches(h, o_a, o_b, o_c, w_gate, b_gate, w_br_mla, w_br_na, w_br_ft, w_out):
    g = jax.nn.sigmoid(h @ w_gate + b_gate)
    g_a, g_b, g_c = jnp.split(g, N_BRANCH, axis=-1)
    m = g_a * (o_a @ w_br_mla) + g_b * (o_b @ w_br_na) + g_c * (o_c @ w_br_ft)
    return m @ w_out


def to_heads(t):
    return t.reshape(t.shape[0], t.shape[1], NA_HEADS, NA_HEAD_DIM)


def token_mixers(h_lat, h_ctx, cos, sin, w_in, q_norm_g, w_q_up, kv_norm_g, w_kv_up, rpb,
                 w_gate, b_gate, w_br_mla, w_br_na, w_br_ft, w_out, need_ctx):
    b, s, _ = h_lat.shape
    pl = split_cols(h_lat @ w_in, IN_SPLITS)
    pc = split_cols(h_ctx @ w_in, IN_SPLITS)
    q_l = mla_q(pl[0], q_norm_g, w_q_up, cos, sin)
    k_l, v_l = mla_kv(pl[1], pl[2], kv_norm_g, w_kv_up, cos, sin)
    k_c, v_c = mla_kv(pc[1], pc[2], kv_norm_g, w_kv_up, None, None)
    o_a = dense_block_attention(q_l, jnp.concatenate([k_l, k_c], axis=1),
                                jnp.concatenate([v_l, v_c], axis=1), MLA_SCALE).reshape(b, s, MLA_WIDTH)
    na_kc, na_vc = to_heads(pc[4]), to_heads(pc[5])
    o_b = neighborhood_attention(to_heads(pl[3]), to_heads(pl[4]), to_heads(pl[5]), na_kc, na_vc, rpb).reshape(b, s, NA_WIDTH)
    o_c = fourier_mix(pl[6])
    y_lat = merge_branches(h_lat, o_a, o_b, o_c, w_gate, b_gate, w_br_mla, w_br_na, w_br_ft, w_out)
    if not need_ctx:
        return y_lat, None
    n_c = h_ctx.shape[1]
    q_c = mla_q(pc[0], q_norm_g, w_q_up, None, None)
    oc_a = dense_block_attention(q_c, k_c, v_c, MLA_SCALE).reshape(b, n_c, MLA_WIDTH)
    oc_b = dense_block_attention(to_heads(pc[3]), na_kc, na_vc, NA_SCALE).reshape(b, n_c, NA_WIDTH)
    oc_c = fourier_mix(pc[6])
    y_ctx = merge_branches(h_ctx, oc_a, oc_b, oc_c, w_gate, b_gate, w_br_mla, w_br_na, w_br_ft, w_out)
    return y_lat, y_ctx


def grouped_moe(h, router_w, router_bias, w_g, w_u, w_d):
    b, n, _ = h.shape
    scores = jax.nn.sigmoid(jnp.einsum('bnd,de->bne', h, router_w, preferred_element_type=jnp.float32))
    sel = scores + router_bias.astype(jnp.float32)
    grp_score = lax.top_k(sel.reshape(b, n, N_GROUPS, EXPERTS_PER_GROUP), TOP_K)[0].sum(-1)
    grp_mask = jax.nn.one_hot(jnp.argmax(grp_score, axis=-1), N_GROUPS, dtype=jnp.bool_)
    exp_mask = jnp.repeat(grp_mask, EXPERTS_PER_GROUP, axis=-1)
    _, top_idx = lax.top_k(jnp.where(exp_mask, sel, -jnp.inf), TOP_K)
    top_w = jnp.take_along_axis(scores, top_idx, axis=-1)
    top_w = top_w / jnp.sum(top_w, axis=-1, keepdims=True)
    gate = jnp.sum(jax.nn.one_hot(top_idx, N_EXPERTS, dtype=jnp.float32) * top_w[..., None], axis=-2)
    hid = jax.nn.silu(jnp.einsum('bnd,edf->bnef', h, w_g)) * jnp.einsum('bnd,edf->bnef', h, w_u)
    hid = hid * gate[..., None].astype(h.dtype)
    return jnp.einsum('bnef,efd->bnd', hid, w_d)


def setup_inputs(seed: int = 0) -> dict:
    key = jax.random.key(seed)
    ks = jax.random.split(key, 32)
    cnt = [0]

    def nrm(shape, scale):
        k = ks[cnt[0]]
        cnt[0] += 1
        return jax.random.normal(k, shape, jnp.float32) * scale

    D = D_MODEL
    L = DEPTH
    return {
        'x': nrm((BATCH, SEQ, D), 1.0),
        'c': nrm((BATCH, D), 1.0),
        'ctx': nrm((BATCH, CTX_LEN, D), 1.0),
        'c_ctx': nrm((D,), 1.0),
        'w_ada': nrm((L, D, 6 * D), 0.5 * D ** -0.5),
        'b_ada': nrm((L, 6 * D), 0.02),
        'norm1_g': 1.0 + nrm((L, D), 0.05),
        'norm2_g': 1.0 + nrm((L, D), 0.05),
        'w_in': nrm((L, D, IN_WIDTH), D ** -0.5),
        'q_norm_g': 1.0 + nrm((L, MLA_Q_LORA), 0.05),
        'w_q_up': nrm((L, MLA_Q_LORA, MLA_HEADS * (MLA_NOPE + MLA_ROPE)), MLA_Q_LORA ** -0.5),
        'kv_norm_g': 1.0 + nrm((L, MLA_KV_LORA), 0.05),
        'w_kv_up': nrm((L, MLA_KV_LORA, MLA_HEADS * (MLA_NOPE + MLA_V)), MLA_KV_LORA ** -0.5),
        'na_rpb': nrm((L, NA_HEADS, 2 * NA_WIN_R - 1, 2 * NA_WIN_C - 1), 0.1),
        'w_gate': nrm((L, D, N_BRANCH * D), D ** -0.5),
        'b_gate': nrm((L, N_BRANCH * D), 0.02),
        'w_br_mla': nrm((L, MLA_WIDTH, D), MLA_WIDTH ** -0.5),
        'w_br_na': nrm((L, NA_WIDTH, D), NA_WIDTH ** -0.5),
        'w_br_ft': nrm((L, FT_WIDTH, D), FT_WIDTH ** -0.5),
        'w_out': nrm((L, D, D), D ** -0.5),
        'router_w': nrm((D, N_EXPERTS), D ** -0.5),
        'router_bias': nrm((N_EXPERTS,), 0.01),
        'w_e_gate': nrm((L, N_EXPERTS, D, EXPERT_FF), D ** -0.5),
        'w_e_up': nrm((L, N_EXPERTS, D, EXPERT_FF), D ** -0.5),
        'w_e_down': nrm((L, N_EXPERTS, EXPERT_FF, D), EXPERT_FF ** -0.5),
        'final_norm_g': 1.0 + nrm((D,), 0.05),
    }


def reference(x, c, ctx, c_ctx, w_ada, b_ada, norm1_g, norm2_g, w_in, q_norm_g, w_q_up, kv_norm_g,
              w_kv_up, na_rpb, w_gate, b_gate, w_br_mla, w_br_na, w_br_ft, w_out, router_w, router_bias,
              w_e_gate, w_e_up, w_e_down, final_norm_g):
    cos, sin = axial_rope_tables(x.shape[1])
    silu_c = jax.nn.silu(c)
    silu_cc = jax.nn.silu(c_ctx)
    x_lat, x_ctx = x, ctx
    for l in range(DEPTH):
        need_ctx = l < DEPTH - 1
        mod = silu_c @ w_ada[l] + b_ada[l]
        mod_c = silu_cc @ w_ada[l] + b_ada[l]
        sh1, sc1, g1, sh2, sc2, g2 = jnp.split(mod[:, None, :], 6, axis=-1)
        csh1, csc1, cg1, csh2, csc2, cg2 = jnp.split(mod_c, 6, axis=-1)
        h_lat = rmsnorm(x_lat, norm1_g[l]) * (1.0 + sc1) + sh1
        h_ctx = rmsnorm(x_ctx, norm1_g[l]) * (1.0 + csc1) + csh1
        y_lat, y_ctx = token_mixers(h_lat, h_ctx, cos, sin, w_in[l], q_norm_g[l], w_q_up[l], kv_norm_g[l],
                                    w_kv_up[l], na_rpb[l], w_gate[l], b_gate[l], w_br_mla[l], w_br_na[l],
                                    w_br_ft[l], w_out[l], need_ctx)
        x_lat = x_lat + g1 * y_lat
        h_lat = rmsnorm(x_lat, norm2_g[l]) * (1.0 + sc2) + sh2
        x_lat = x_lat + g2 * grouped_moe(h_lat, router_w, router_bias, w_e_gate[l], w_e_up[l], w_e_down[l])
        if need_ctx:
            x_ctx = x_ctx + cg1 * y_ctx
            h_ctx = rmsnorm(x_ctx, norm2_g[l]) * (1.0 + csc2) + csh2
            x_ctx = x_ctx + cg2 * grouped_moe(h_ctx, router_w, router_bias, w_e_gate[l], w_e_up[l], w_e_down[l])
    return rmsnorm(x_lat, final_norm_g)
```

```python
import functools
import math

import jax
import jax.numpy as jnp
import numpy as np
from jax import lax
from jax.experimental import pallas as pl
from jax.experimental.pallas import tpu as pltpu

F32 = jnp.float32
BF16 = jnp.bfloat16

GRID_W = 64
MLA_HEADS = 8
MLA_NOPE = 64
MLA_ROPE = 32
MLA_V = 64
MLA_Q_LORA = 256
MLA_KV_LORA = 128
MLA_SCALE = (MLA_NOPE + MLA_ROPE) ** -0.5
NA_HEADS = 4
NA_HEAD_DIM = 64
NA_WIN_R = 8
NA_WIN_C = 16
NA_SCALE = NA_HEAD_DIM ** -0.5
NA_WIDTH = NA_HEADS * NA_HEAD_DIM
FT_GROUPS = 4
FT_GROUP_DIM = 64
FT_WIDTH = FT_GROUPS * FT_GROUP_DIM
N_EXPERTS = 16
N_GROUPS = 4
EXPERTS_PER_GROUP = N_EXPERTS // N_GROUPS
ROPE_THETA = 10000.0
NORM_EPS = 1e-6
MASK_VALUE = -1e30
LOG2E = math.log2(math.e)

LANES = 128
V7X_VMEM_LIMIT_BYTES = 56 * 1024 * 1024

HEAD_PAD = LANES
NA_ROWS_PER_BLOCK = 8
NA_KEY_ROWS = 16
EXPERTS_PER_STEP = 4


def _cparams(n_axes):
    return pltpu.CompilerParams(
        dimension_semantics=("arbitrary",) * n_axes,
        vmem_limit_bytes=V7X_VMEM_LIMIT_BYTES,
    )


def _rms(x, g):
    return x * lax.rsqrt(jnp.mean(x * x, axis=-1, keepdims=True) + NORM_EPS) * g


def _dot(a, b):
    return jnp.dot(a, b, preferred_element_type=F32)


def _dot_nt(a, b):
    return lax.dot_general(a, b, (((1,), (1,)), ((), ())), preferred_element_type=F32)


def _mod_kernel(c_ref, w_ref, b_ref, o_ref):
    c = c_ref[...]
    o_ref[...] = _dot(c * jax.nn.sigmoid(c), w_ref[...]) + b_ref[...]


def _modulation(c_rows, w_ada, b_ada):
    n_layers, d, width = w_ada.shape
    rows = c_rows.shape[0]
    tn = 1536
    return pl.pallas_call(
        _mod_kernel,
        grid=(n_layers, width // tn),
        in_specs=[
            pl.BlockSpec((rows, d), lambda l, j: (0, 0)),
            pl.BlockSpec((None, d, tn), lambda l, j: (l, 0, j)),
            pl.BlockSpec((None, 1, tn), lambda l, j: (l, 0, j)),
        ],
        out_specs=pl.BlockSpec((None, rows, tn), lambda l, j: (l, 0, j)),
        out_shape=jax.ShapeDtypeStruct((n_layers, rows, width), F32),
        compiler_params=_cparams(2),
        name="modulation",
    )(c_rows, w_ada, b_ada.reshape(n_layers, 1, width))


def _proj_kernel(x_ref, mod_ref, n1g_ref, win_ref, qg_ref, wqm_ref, wqr_ref, kvg_ref, wkk_ref, wkv_ref,
                 cos_ref, sin_ref, q_ref, k_ref, v_ref, nq_ref, nk_ref, nv_ref, ft_ref):
    d = x_ref.shape[-1]
    x = x_ref[...]
    mod = mod_ref[...]
    sh1, sc1 = mod[:, 0:d], mod[:, d:2 * d]
    h = _rms(x, n1g_ref[...]) * (1.0 + sc1) + sh1
    p = _dot(h.astype(BF16), win_ref[...])
    cos = cos_ref[...]
    sin = sin_ref[...]
    o = 0
    q_lat = p[:, o:o + MLA_Q_LORA]
    o += MLA_Q_LORA
    kv_lat = p[:, o:o + MLA_KV_LORA]
    o += MLA_KV_LORA
    kr_a = p[:, o:o + HEAD_PAD]
    o += HEAD_PAD
    kr_b = p[:, o:o + HEAD_PAD]
    o += HEAD_PAD
    nq_ref[...] = p[:, o:o + NA_WIDTH].astype(BF16)
    o += NA_WIDTH
    nk_ref[...] = p[:, o:o + NA_WIDTH].astype(BF16)
    o += NA_WIDTH
    nv_ref[...] = p[:, o:o + NA_WIDTH].astype(BF16)
    o += NA_WIDTH
    ft_ref[...] = p[:, o:o + FT_WIDTH].astype(BF16)

    qn = _rms(q_lat, qg_ref[...]).astype(BF16)
    qm = _dot(qn, wqm_ref[...])
    qr = _dot(qn, wqr_ref[...])
    kvn = _rms(kv_lat, kvg_ref[...]).astype(BF16)
    kn = _dot(kvn, wkk_ref[...])
    v_ref[...] = _dot(kvn, wkv_ref[...]).astype(BF16)
    kr = kr_a * cos + kr_b * sin
    q_scale = MLA_SCALE * LOG2E
    for hd in range(MLA_HEADS):
        sl = slice(hd * HEAD_PAD, (hd + 1) * HEAD_PAD)
        q_ref[:, sl] = ((qm[:, sl] * cos + qr[:, sl] * sin) * q_scale).astype(BF16)
        k_ref[:, sl] = (kn[:, sl] + kr).astype(BF16)


def _project(x, mod, mod_row, lw, cos_t, sin_t):
    nb, n, d = x.shape
    tm = min(512, n)
    win = lw["w_in"]
    wide = MLA_HEADS * HEAD_PAD

    def full(a):
        return pl.BlockSpec(a.shape, lambda b, i: (0,) * a.ndim)

    def tok(width):
        return pl.BlockSpec((None, tm, width), lambda b, i: (b, i, 0))

    outs = [wide, wide, MLA_HEADS * MLA_V, NA_WIDTH, NA_WIDTH, NA_WIDTH, FT_WIDTH]
    return pl.pallas_call(
        _proj_kernel,
        grid=(nb, n // tm),
        in_specs=[
            tok(d),
            pl.BlockSpec((None, 1, mod.shape[-1]), lambda b, i: (mod_row(b), 0, 0)),
            full(lw["norm1_g"]), full(win), full(lw["q_norm_g"]), full(lw["wq_main"]), full(lw["wq_rot"]),
            full(lw["kv_norm_g"]), full(lw["wkv_k"]), full(lw["wkv_v"]),
            pl.BlockSpec((tm, HEAD_PAD), lambda b, i: (i, 0)),
            pl.BlockSpec((tm, HEAD_PAD), lambda b, i: (i, 0)),
        ],
        out_specs=[tok(w) for w in outs],
        out_shape=[jax.ShapeDtypeStruct((nb, n, w), BF16) for w in outs],
        compiler_params=_cparams(2),
        name="project",
    )(x, mod, lw["norm1_g"], win, lw["q_norm_g"], lw["wq_main"], lw["wq_rot"], lw["kv_norm_g"],
      lw["wkv_k"], lw["wkv_v"], cos_t, sin_t)


def _mla_kernel(q_ref, kl_ref, kc_ref, vl_ref, vc_ref, o_ref, m_scr, l_scr, acc_scr, *, tk, n_chunks):
    m_scr[...] = jnp.full(m_scr.shape, MASK_VALUE, F32)
    l_scr[...] = jnp.zeros(l_scr.shape, F32)
    acc_scr[...] = jnp.zeros(acc_scr.shape, F32)

    def step(hd, k, v):
        q = q_ref[:, hd * HEAD_PAD:(hd + 1) * HEAD_PAD]
        s = _dot_nt(q, k)
        m_prev = m_scr[hd]
        m_new = jnp.maximum(m_prev, jnp.max(s, axis=1, keepdims=True))
        alpha = jnp.exp2(m_prev - m_new)
        p = jnp.exp2(s - m_new)
        l_scr[hd] = alpha * l_scr[hd] + jnp.sum(p, axis=1, keepdims=True)
        acc_scr[hd] = alpha * acc_scr[hd] + _dot(p.astype(BF16), v)
        m_scr[hd] = m_new

    def body(c, carry):
        off = pl.multiple_of(c * tk, tk)
        for hd in range(2):
            step(hd, kl_ref[pl.ds(off, tk), hd * HEAD_PAD:(hd + 1) * HEAD_PAD], vl_ref[pl.ds(off, tk), :])
        return carry

    lax.fori_loop(0, n_chunks, body, 0)
    for hd in range(2):
        step(hd, kc_ref[:, hd * HEAD_PAD:(hd + 1) * HEAD_PAD], vc_ref[...])
    lane = lax.broadcasted_iota(jnp.int32, o_ref.shape, 1)
    o0 = acc_scr[0] / l_scr[0]
    o1 = acc_scr[1] / l_scr[1]
    o_ref[...] = jnp.where(lane < MLA_V, o0, o1).astype(BF16)


def _mla_attention(q, k_lat, k_ctx, v_lat, v_ctx):
    nb, s, _ = q.shape
    c = k_ctx.shape[1]
    tq = min(512, s)
    tk = min(1024, s)
    pairs = MLA_HEADS // 2
    kern = functools.partial(_mla_kernel, tk=tk, n_chunks=s // tk)
    return pl.pallas_call(
        kern,
        grid=(nb, pairs, s // tq),
        in_specs=[
            pl.BlockSpec((None, tq, 2 * HEAD_PAD), lambda b, hp, i: (b, i, hp)),
            pl.BlockSpec((None, s, 2 * HEAD_PAD), lambda b, hp, i: (b, 0, hp)),
            pl.BlockSpec((None, c, 2 * HEAD_PAD), lambda b, hp, i: (b, 0, hp)),
            pl.BlockSpec((None, s, 2 * MLA_V), lambda b, hp, i: (b, 0, hp)),
            pl.BlockSpec((None, c, 2 * MLA_V), lambda b, hp, i: (b, 0, hp)),
        ],
        out_specs=pl.BlockSpec((None, tq, 2 * MLA_V), lambda b, hp, i: (b, i, hp)),
        out_shape=jax.ShapeDtypeStruct((nb, s, MLA_HEADS * MLA_V), BF16),
        scratch_shapes=[
            pltpu.VMEM((2, tq, 1), F32),
            pltpu.VMEM((2, tq, 1), F32),
            pltpu.VMEM((2, tq, 2 * MLA_V), F32),
        ],
        compiler_params=_cparams(3),
        name="mla_attention",
    )(q, k_lat, k_ctx, v_lat, v_ctx)


def _mla_ctx_kernel(q_ref, k_ref, v_ref, o_ref):
    outs = []
    for hd in range(2):
        sl = slice(hd * HEAD_PAD, (hd + 1) * HEAD_PAD)
        s = _dot_nt(q_ref[:, sl], k_ref[:, sl])
        p = jnp.exp2(s - jnp.max(s, axis=1, keepdims=True))
        outs.append(_dot(p.astype(BF16), v_ref[...]) / jnp.sum(p, axis=1, keepdims=True))
    lane = lax.broadcasted_iota(jnp.int32, o_ref.shape, 1)
    o_ref[...] = jnp.where(lane < MLA_V, outs[0], outs[1]).astype(BF16)


def _mla_ctx_attention(q, k, v):
    nb, c, _ = q.shape
    pairs = MLA_HEADS // 2
    return pl.pallas_call(
        _mla_ctx_kernel,
        grid=(nb, pairs),
        in_specs=[
            pl.BlockSpec((None, c, 2 * HEAD_PAD), lambda b, hp: (b, 0, hp)),
            pl.BlockSpec((None, c, 2 * HEAD_PAD), lambda b, hp: (b, 0, hp)),
            pl.BlockSpec((None, c, 2 * MLA_V), lambda b, hp: (b, 0, hp)),
        ],
        out_specs=pl.BlockSpec((None, c, 2 * MLA_V), lambda b, hp: (b, 0, hp)),
        out_shape=jax.ShapeDtypeStruct((nb, c, MLA_HEADS * MLA_V), BF16),
        compiler_params=_cparams(2),
        name="mla_ctx_attention",
    )(q, k, v)


def _na_kernel(q_ref, k_ref, v_ref, kc_ref, vc_ref, bias_ref, o_ref, *, grid_rows):
    rb = pl.program_id(2)
    start_row = jnp.clip(rb * NA_ROWS_PER_BLOCK - NA_WIN_R // 2, 0, grid_rows - NA_KEY_ROWS)
    start = pl.multiple_of(start_row * GRID_W, (NA_WIN_R // 2) * GRID_W)
    n_keys = NA_KEY_ROWS * GRID_W
    q = q_ref[...]
    kw = k_ref[pl.ds(start, n_keys), :]
    vw = v_ref[pl.ds(start, n_keys), :]
    kc = kc_ref[...]
    vc = vc_ref[...]
    lane = lax.broadcasted_iota(jnp.int32, q.shape, 1)
    outs = []
    for hd in range(2):
        in_head = (lane >= NA_HEAD_DIM) if hd else (lane < NA_HEAD_DIM)
        qh = jnp.where(in_head, q, jnp.zeros_like(q))
        s = _dot_nt(qh, kw) * NA_SCALE + bias_ref[hd]
        sc = _dot_nt(qh, kc) * NA_SCALE
        m = jnp.maximum(jnp.max(s, axis=1, keepdims=True), jnp.max(sc, axis=1, keepdims=True))
        p = jnp.exp(s - m)
        pc = jnp.exp(sc - m)
        denom = jnp.sum(p, axis=1, keepdims=True) + jnp.sum(pc, axis=1, keepdims=True)
        outs.append((_dot(p.astype(BF16), vw) + _dot(pc.astype(BF16), vc)) / denom)
    o_ref[...] = jnp.where(lane < NA_HEAD_DIM, outs[0], outs[1]).astype(BF16)


def _na_attention(q, k, v, kc, vc, bias):
    nb, s, _ = q.shape
    c = kc.shape[1]
    grid_rows = s // GRID_W
    tq = NA_ROWS_PER_BLOCK * GRID_W
    n_blocks = grid_rows // NA_ROWS_PER_BLOCK
    n_keys = NA_KEY_ROWS * GRID_W
    pairs = NA_HEADS // 2

    def variant(rb):
        return jnp.where(rb == 0, 0, jnp.where(rb == n_blocks - 1, 2, 1))

    kern = functools.partial(_na_kernel, grid_rows=grid_rows)
    return pl.pallas_call(
        kern,
        grid=(nb, pairs, n_blocks),
        in_specs=[
            pl.BlockSpec((None, tq, LANES), lambda b, hp, rb: (b, rb, hp)),
            pl.BlockSpec((None, s, LANES), lambda b, hp, rb: (b, 0, hp)),
            pl.BlockSpec((None, s, LANES), lambda b, hp, rb: (b, 0, hp)),
            pl.BlockSpec((None, c, LANES), lambda b, hp, rb: (b, 0, hp)),
            pl.BlockSpec((None, c, LANES), lambda b, hp, rb: (b, 0, hp)),
            pl.BlockSpec((None, 2, tq, n_keys), lambda b, hp, rb: (variant(rb), hp, 0, 0)),
        ],
        out_specs=pl.BlockSpec((None, tq, LANES), lambda b, hp, rb: (b, rb, hp)),
        out_shape=jax.ShapeDtypeStruct((nb, s, NA_WIDTH), BF16),
        compiler_params=_cparams(3),
        name="na_attention",
    )(q, k, v, kc, vc, bias)


def _na_ctx_kernel(q_ref, k_ref, v_ref, o_ref):
    q = q_ref[...]
    k = k_ref[...]
    v = v_ref[...]
    lane = lax.broadcasted_iota(jnp.int32, q.shape, 1)
    outs = []
    for hd in range(2):
        in_head = (lane >= NA_HEAD_DIM) if hd else (lane < NA_HEAD_DIM)
        qh = jnp.where(in_head, q, jnp.zeros_like(q))
        s = _dot_nt(qh, k) * NA_SCALE
        p = jnp.exp(s - jnp.max(s, axis=1, keepdims=True))
        outs.append(_dot(p.astype(BF16), v) / jnp.sum(p, axis=1, keepdims=True))
    o_ref[...] = jnp.where(lane < NA_HEAD_DIM, outs[0], outs[1]).astype(BF16)


def _na_ctx_attention(q, k, v):
    nb, c, _ = q.shape
    spec = pl.BlockSpec((None, c, LANES), lambda b, hp: (b, 0, hp))
    return pl.pallas_call(
        _na_ctx_kernel,
        grid=(nb, NA_HEADS // 2),
        in_specs=[spec, spec, spec],
        out_specs=spec,
        out_shape=jax.ShapeDtypeStruct((nb, c, NA_WIDTH), BF16),
        compiler_params=_cparams(2),
        name="na_ctx_attention",
    )(q, k, v)


def _na_bias_tables(rpb, grid_rows):
    qr = np.arange(NA_ROWS_PER_BLOCK)[:, None]
    kr = np.arange(NA_KEY_ROWS)[None, :]
    n_dr = 2 * NA_WIN_R - 1
    n_dc = 2 * NA_WIN_C - 1
    row_onehot = np.zeros((3, NA_ROWS_PER_BLOCK, NA_KEY_ROWS, n_dr), np.float32)
    row_valid = np.zeros((3, NA_ROWS_PER_BLOCK, NA_KEY_ROWS), bool)
    n_blocks = grid_rows // NA_ROWS_PER_BLOCK
    for vi, rb in enumerate((0, 1, n_blocks - 1)):
        r = rb * NA_ROWS_PER_BLOCK + qr
        start_row = min(max(rb * NA_ROWS_PER_BLOCK - NA_WIN_R // 2, 0), grid_rows - NA_KEY_ROWS)
        key_row = start_row + kr
        win = np.clip(r - NA_WIN_R // 2, 0, grid_rows - NA_WIN_R)
        valid = (key_row >= win) & (key_row < win + NA_WIN_R)
        dr = np.clip(key_row - r + (NA_WIN_R - 1), 0, n_dr - 1)
        row_valid[vi] = valid
        row_onehot[vi] = (dr[..., None] == np.arange(n_dr)) & valid[..., None]
    qc = np.arange(GRID_W)[:, None]
    kcol = np.arange(GRID_W)[None, :]
    wstart = np.clip(qc - NA_WIN_C // 2, 0, GRID_W - NA_WIN_C)
    col_valid = (kcol >= wstart) & (kcol < wstart + NA_WIN_C)
    dc = np.clip(kcol - qc, -(NA_WIN_C - 1), NA_WIN_C - 1) + (NA_WIN_C - 1)
    col_onehot = ((dc[..., None] == np.arange(n_dc)) & col_valid[..., None]).astype(np.float32)
    t = jnp.einsum("lhij,cdj->lhicd", rpb.astype(F32), jnp.asarray(col_onehot), precision=lax.Precision.HIGHEST)
    t = jnp.einsum("vqki,lhicd->vlhqckd", jnp.asarray(row_onehot), t, precision=lax.Precision.HIGHEST)
    valid = row_valid[:, None, None, :, None, :, None] & col_valid[None, None, None, None, :, None, :]
    t = jnp.where(jnp.asarray(valid), t, MASK_VALUE)
    n_layers, n_heads = rpb.shape[:2]
    return t.reshape(3, n_layers, n_heads, NA_ROWS_PER_BLOCK * GRID_W, NA_KEY_ROWS * GRID_W)


FT_N1 = 64


def _ft_stage1_kernel(w_ref, x_ref, o_ref):
    o_ref[...] = _dot(w_ref[...], x_ref[...])


def _ft_stage2_kernel(are_ref, aim_ref, tc_ref, ts_ref, w_ref, cd_ref, sd_ref, o_ref, *, scale):
    are = are_ref[...]
    aim = aim_ref[...]
    tc = jnp.concatenate([tc_ref[...]] * (FT_WIDTH // LANES), axis=1)
    ts = jnp.concatenate([ts_ref[...]] * (FT_WIDTH // LANES), axis=1)
    bre = are * tc + aim * ts
    bim = aim * tc - are * ts
    bb = jnp.concatenate([bre, bim], axis=0).astype(BF16)
    xk = _dot(w_ref[...], bb)
    n2 = are.shape[0]
    f = _dot(xk[:n2].astype(BF16), cd_ref[...]) + _dot(xk[n2:].astype(BF16), sd_ref[...])
    o_ref[...] = (f * scale).astype(BF16)


def _dft_mats(n):
    ang = 2.0 * np.pi * np.outer(np.arange(n), np.arange(n)) / n
    return np.cos(ang), np.sin(ang)


def _channel_dft():
    c, s = _dft_mats(FT_GROUP_DIM)
    eye = np.eye(FT_GROUPS)
    return jnp.asarray(np.kron(eye, c), BF16), jnp.asarray(np.kron(eye, s), BF16)


def _fourier_lat(u):
    nb, n, _ = u.shape
    n2 = n // FT_N1
    cols = n2 * FT_WIDTH
    c1, s1 = _dft_mats(FT_N1)
    w1 = jnp.asarray(np.concatenate([c1, -s1], axis=0), BF16)
    tcol = min(2048, cols)
    a = pl.pallas_call(
        _ft_stage1_kernel,
        grid=(nb, cols // tcol),
        in_specs=[
            pl.BlockSpec(w1.shape, lambda b, j: (0, 0)),
            pl.BlockSpec((None, FT_N1, tcol), lambda b, j: (b, 0, j)),
        ],
        out_specs=pl.BlockSpec((None, 2 * FT_N1, tcol), lambda b, j: (b, 0, j)),
        out_shape=jax.ShapeDtypeStruct((nb, 2 * FT_N1, cols), F32),
        compiler_params=_cparams(2),
        name="fourier_stage1",
    )(w1, u.reshape(nb, FT_N1, cols))
    a = a.reshape(nb, 2 * FT_N1, n2, FT_WIDTH)

    ang = 2.0 * np.pi * np.outer(np.arange(FT_N1), np.arange(n2)) / n
    tc = jnp.asarray(np.repeat(np.cos(ang)[:, :, None], LANES, axis=2), F32)
    ts = jnp.asarray(np.repeat(np.sin(ang)[:, :, None], LANES, axis=2), F32)
    c3, s3 = _dft_mats(n2)
    w3 = jnp.asarray(np.block([[c3, s3], [-s3, c3]]), BF16)
    cd, sd = _channel_dft()
    kern = functools.partial(_ft_stage2_kernel, scale=1.0 / math.sqrt(n * FT_GROUP_DIM))
    out = pl.pallas_call(
        kern,
        grid=(nb, FT_N1),
        in_specs=[
            pl.BlockSpec((None, None, n2, FT_WIDTH), lambda b, k1: (b, k1, 0, 0)),
            pl.BlockSpec((None, None, n2, FT_WIDTH), lambda b, k1: (b, FT_N1 + k1, 0, 0)),
            pl.BlockSpec((None, n2, LANES), lambda b, k1: (k1, 0, 0)),
            pl.BlockSpec((None, n2, LANES), lambda b, k1: (k1, 0, 0)),
            pl.BlockSpec(w3.shape, lambda b, k1: (0, 0)),
            pl.BlockSpec(cd.shape, lambda b, k1: (0, 0)),
            pl.BlockSpec(sd.shape, lambda b, k1: (0, 0)),
        ],
        out_specs=pl.BlockSpec((None, n2, FT_WIDTH), lambda b, k1: (b, 0, k1)),
        out_shape=jax.ShapeDtypeStruct((nb, n2, FT_N1 * FT_WIDTH), BF16),
        compiler_params=_cparams(2),
        name="fourier_stage2",
    )(a, a, tc, ts, w3, cd, sd)
    return out.reshape(nb, n, FT_WIDTH)


def _ft_ctx_kernel(u_ref, cn_ref, sn_ref, cd_ref, sd_ref, o_ref, *, scale):
    u = u_ref[...]
    y1 = _dot(u, cd_ref[...]).astype(BF16)
    y2 = _dot(u, sd_ref[...]).astype(BF16)
    f = _dot(cn_ref[...], y1) - _dot(sn_ref[...], y2)
    o_ref[...] = (f * scale).astype(BF16)


def _fourier_ctx(u):
    nb, n, _ = u.shape
    cn, sn = _dft_mats(n)
    cn, sn = jnp.asarray(cn, BF16), jnp.asarray(sn, BF16)
    cd, sd = _channel_dft()
    kern = functools.partial(_ft_ctx_kernel, scale=1.0 / math.sqrt(n * FT_GROUP_DIM))
    sq = pl.BlockSpec((n, n), lambda b: (0, 0))
    ch = pl.BlockSpec(cd.shape, lambda b: (0, 0))
    tok = pl.BlockSpec((None, n, FT_WIDTH), lambda b: (b, 0, 0))
    return pl.pallas_call(
        kern,
        grid=(nb,),
        in_specs=[tok, sq, sq, ch, ch],
        out_specs=tok,
        out_shape=jax.ShapeDtypeStruct((nb, n, FT_WIDTH), BF16),
        compiler_params=_cparams(1),
        name="fourier_ctx",
    )(u, cn, sn, cd, sd)


def _merge_kernel(x_ref, oa_ref, ob_ref, oc_ref, mod_ref, n1g_ref, n2g_ref, wgate_ref, bgate_ref,
                  wba_ref, wbb_ref, wbc_ref, wout_ref, x1_ref, h2_ref):
    d = x_ref.shape[-1]
    x = x_ref[...]
    mod = mod_ref[...]
    sh1, sc1, g1, sh2, sc2 = (mod[:, i * d:(i + 1) * d] for i in range(5))
    h = _rms(x, n1g_ref[...]) * (1.0 + sc1) + sh1
    gate = jax.nn.sigmoid(_dot(h.astype(BF16), wgate_ref[...]) + bgate_ref[...])
    m = gate[:, 0:d] * _dot(oa_ref[...], wba_ref[...])
    m += gate[:, d:2 * d] * _dot(ob_ref[...], wbb_ref[...])
    m += gate[:, 2 * d:3 * d] * _dot(oc_ref[...], wbc_ref[...])
    y = _dot(m.astype(BF16), wout_ref[...])
    x1 = x + g1 * y
    x1_ref[...] = x1
    h2_ref[...] = (_rms(x1, n2g_ref[...]) * (1.0 + sc2) + sh2).astype(BF16)


def _merge(x, o_a, o_b, o_c, mod, mod_row, lw):
    nb, n, d = x.shape
    tm = min(256, n)

    def full(a):
        return pl.BlockSpec(a.shape, lambda b, i: (0,) * a.ndim)

    def tok(width):
        return pl.BlockSpec((None, tm, width), lambda b, i: (b, i, 0))

    weights = [lw["norm1_g"], lw["norm2_g"], lw["w_gate"], lw["b_gate"], lw["w_br_mla"], lw["w_br_na"],
               lw["w_br_ft"], lw["w_out"]]
    return pl.pallas_call(
        _merge_kernel,
        grid=(nb, n // tm),
        in_specs=[tok(d), tok(o_a.shape[-1]), tok(o_b.shape[-1]), tok(o_c.shape[-1]),
                  pl.BlockSpec((None, 1, mod.shape[-1]), lambda b, i: (mod_row(b), 0, 0))]
        + [full(w) for w in weights],
        out_specs=[tok(d), tok(d)],
        out_shape=[jax.ShapeDtypeStruct((nb, n, d), F32), jax.ShapeDtypeStruct((nb, n, d), BF16)],
        compiler_params=_cparams(2),
        name="merge",
    )(x, o_a, o_b, o_c, mod, *weights)


def _route(logits, router_bias):
    lane = lax.broadcasted_iota(jnp.int32, logits.shape, 1)
    neg = -jnp.inf
    big = jnp.int32(LANES)
    scores = jax.nn.sigmoid(logits)
    sel = scores + router_bias

    def top2(vals):
        m1 = jnp.max(vals, axis=1, keepdims=True)
        i1 = jnp.min(jnp.where(vals == m1, lane, big), axis=1, keepdims=True)
        rest = jnp.where(lane == i1, neg, vals)
        m2 = jnp.max(rest, axis=1, keepdims=True)
        i2 = jnp.min(jnp.where(rest == m2, lane, big), axis=1, keepdims=True)
        return m1, i1, m2, i2

    grp_scores = []
    for g in range(N_GROUPS):
        in_grp = (lane >= g * EXPERTS_PER_GROUP) & (lane < (g + 1) * EXPERTS_PER_GROUP)
        m1, _, m2, _ = top2(jnp.where(in_grp, sel, neg))
        grp_scores.append(m1 + m2)
    best = functools.reduce(jnp.maximum, grp_scores)
    grp = jnp.full(best.shape, N_GROUPS - 1, jnp.int32)
    for g in range(N_GROUPS - 2, -1, -1):
        grp = jnp.where(grp_scores[g] == best, g, grp)
    in_sel = (lane >= grp * EXPERTS_PER_GROUP) & (lane < (grp + 1) * EXPERTS_PER_GROUP)
    _, i1, _, i2 = top2(jnp.where(in_sel, sel, neg))
    hit1 = lane == i1
    hit2 = lane == i2
    w1 = jnp.sum(jnp.where(hit1, scores, 0.0), axis=1, keepdims=True)
    w2 = jnp.sum(jnp.where(hit2, scores, 0.0), axis=1, keepdims=True)
    total = w1 + w2
    return jnp.where(hit1, w1 / total, 0.0) + jnp.where(hit2, w2 / total, 0.0)


def _moe_kernel(h_ref, x_ref, mod_ref, rw_ref, rb_ref, wg_ref, wu_ref, wd_ref, o_ref, gate_scr, acc_scr):
    d = x_ref.shape[-1]
    step = pl.program_id(2)
    n_steps = pl.num_programs(2)
    h = h_ref[...]

    @pl.when(step == 0)
    def _():
        gate = _route(_dot(h, rw_ref[...]), rb_ref[...])
        for c in range(N_EXPERTS // EXPERTS_PER_STEP):
            shift = (LANES - c * EXPERTS_PER_STEP) % LANES
            gate_scr[c] = pltpu.roll(gate, shift, axis=1) if shift else gate
        acc_scr[...] = jnp.zeros(acc_scr.shape, F32)

    gate = gate_scr[step]
    acc = acc_scr[...]
    for j in range(EXPERTS_PER_STEP):
        a = _dot(h, wg_ref[j])
        u = _dot(h, wu_ref[j])
        hid = (a * jax.nn.sigmoid(a)) * u * gate[:, j:j + 1]
        acc += _dot(hid.astype(BF16), wd_ref[j])
    acc_scr[...] = acc

    @pl.when(step == n_steps - 1)
    def _():
        g2 = mod_ref[:, 5 * d:6 * d]
        o_ref[...] = x_ref[...] + g2 * acc


def _moe(h2, x1, mod, mod_row, lw, router_w, router_b):
    nb, n, d = x1.shape
    tm = min(1024, n)
    n_steps = N_EXPERTS // EXPERTS_PER_STEP
    ff = lw["w_e_gate"].shape[-1]
    tok = pl.BlockSpec((None, tm, d), lambda b, i, c: (b, i, 0))
    return pl.pallas_call(
        _moe_kernel,
        grid=(nb, n // tm, n_steps),
        in_specs=[
            tok, tok,
            pl.BlockSpec((None, 1, mod.shape[-1]), lambda b, i, c: (mod_row(b), 0, 0)),
            pl.BlockSpec(router_w.shape, lambda b, i, c: (0, 0)),
            pl.BlockSpec(router_b.shape, lambda b, i, c: (0, 0)),
            pl.BlockSpec((EXPERTS_PER_STEP, d, ff), lambda b, i, c: (c, 0, 0)),
            pl.BlockSpec((EXPERTS_PER_STEP, d, ff), lambda b, i, c: (c, 0, 0)),
            pl.BlockSpec((EXPERTS_PER_STEP, ff, d), lambda b, i, c: (c, 0, 0)),
        ],
        out_specs=tok,
        out_shape=jax.ShapeDtypeStruct((nb, n, d), F32),
        scratch_shapes=[pltpu.VMEM((n_steps, tm, LANES), F32), pltpu.VMEM((tm, d), F32)],
        compiler_params=_cparams(3),
        name="moe",
    )(h2, x1, mod, router_w, router_b, lw["w_e_gate"], lw["w_e_up"], lw["w_e_down"])


def _final_norm_kernel(x_ref, g_ref, o_ref):
    o_ref[...] = _rms(x_ref[...], g_ref[...])


def _final_norm(x, g):
    nb, n, d = x.shape
    tm = min(512, n)
    tok = pl.BlockSpec((None, tm, d), lambda b, i: (b, i, 0))
    return pl.pallas_call(
        _final_norm_kernel,
        grid=(nb, n // tm),
        in_specs=[tok, pl.BlockSpec((1, d), lambda b, i: (0, 0))],
        out_specs=tok,
        out_shape=jax.ShapeDtypeStruct((nb, n, d), F32),
        compiler_params=_cparams(2),
        name="final_norm",
    )(x, g.reshape(1, d))


def _rotate_cols(w):
    half = w.shape[-1] // 2
    return jnp.concatenate([-w[..., half:], w[..., :half]], axis=-1)


def _prepare_weights(w_in, q_norm_g, w_q_up, kv_norm_g, w_kv_up, norm1_g, norm2_g, w_gate, b_gate, w_br_mla,
                     w_br_na, w_br_ft, w_out, w_e_gate, w_e_up, w_e_down):
    n_layers, d, _ = w_in.shape
    splits = np.cumsum([MLA_Q_LORA, MLA_KV_LORA, MLA_ROPE, NA_WIDTH, NA_WIDTH, NA_WIDTH])
    w_q, w_kv, w_kr, w_nq, w_nk, w_nv, w_ft = jnp.split(w_in, [int(s) for s in splits], axis=-1)

    def in_rope_slot(w):
        zeros_lo = jnp.zeros(w.shape[:-1] + (MLA_NOPE,), w.dtype)
        zeros_hi = jnp.zeros(w.shape[:-1] + (HEAD_PAD - MLA_NOPE - MLA_ROPE,), w.dtype)
        return jnp.concatenate([zeros_lo, w, zeros_hi], axis=-1)

    win = jnp.concatenate([w_q, w_kv, in_rope_slot(w_kr), in_rope_slot(_rotate_cols(w_kr)), w_nq, w_nk, w_nv, w_ft],
                          axis=-1).astype(BF16)
    wq = w_q_up.reshape(n_layers, MLA_Q_LORA, MLA_HEADS, MLA_NOPE + MLA_ROPE)
    wq_nope, wq_pe = wq[..., :MLA_NOPE], wq[..., MLA_NOPE:]
    pad = jnp.zeros(wq_pe.shape[:-1] + (HEAD_PAD - MLA_NOPE - MLA_ROPE,), wq.dtype)
    wq_main = jnp.concatenate([wq_nope, wq_pe, pad], axis=-1)
    wq_rot = jnp.concatenate([jnp.zeros_like(wq_nope), _rotate_cols(wq_pe), pad], axis=-1)
    wkv = w_kv_up.reshape(n_layers, MLA_KV_LORA, MLA_HEADS, MLA_NOPE + MLA_V)
    wk_nope, wv = wkv[..., :MLA_NOPE], wkv[..., MLA_NOPE:]
    wkv_k = jnp.concatenate([wk_nope, jnp.zeros(wk_nope.shape[:-1] + (HEAD_PAD - MLA_NOPE,), wkv.dtype)], axis=-1)
    wide = MLA_HEADS * HEAD_PAD
    return {
        "w_in": win,
        "wq_main": wq_main.reshape(n_layers, MLA_Q_LORA, wide).astype(BF16),
        "wq_rot": wq_rot.reshape(n_layers, MLA_Q_LORA, wide).astype(BF16),
        "wkv_k": wkv_k.reshape(n_layers, MLA_KV_LORA, wide).astype(BF16),
        "wkv_v": wv.reshape(n_layers, MLA_KV_LORA, MLA_HEADS * MLA_V).astype(BF16),
        "q_norm_g": q_norm_g.reshape(n_layers, 1, -1),
        "kv_norm_g": kv_norm_g.reshape(n_layers, 1, -1),
        "norm1_g": norm1_g.reshape(n_layers, 1, d),
        "norm2_g": norm2_g.reshape(n_layers, 1, d),
        "w_gate": w_gate.astype(BF16),
        "b_gate": b_gate.reshape(n_layers, 1, -1),
        "w_br_mla": w_br_mla.astype(BF16),
        "w_br_na": w_br_na.astype(BF16),
        "w_br_ft": w_br_ft.astype(BF16),
        "w_out": w_out.astype(BF16),
        "w_e_gate": w_e_gate.astype(BF16),
        "w_e_up": w_e_up.astype(BF16),
        "w_e_down": w_e_down.astype(BF16),
    }


def _rope_tables(s, c):
    n_freq = MLA_ROPE // 4
    inv_freq = ROPE_THETA ** (-jnp.arange(n_freq, dtype=F32) / n_freq)
    t = jnp.arange(s, dtype=jnp.int32)
    row = (t // GRID_W).astype(F32)
    col = (t % GRID_W).astype(F32)
    ang = jnp.concatenate([row[:, None] * inv_freq, col[:, None] * inv_freq], axis=-1)
    cos, sin = jnp.cos(ang), jnp.sin(ang)
    pad = HEAD_PAD - MLA_NOPE - MLA_ROPE
    cos_lat = jnp.concatenate([jnp.ones((s, MLA_NOPE), F32), cos, cos, jnp.zeros((s, pad), F32)], axis=-1)
    sin_lat = jnp.concatenate([jnp.zeros((s, MLA_NOPE), F32), sin, sin, jnp.zeros((s, pad), F32)], axis=-1)
    cos_ctx = jnp.concatenate([jnp.ones((c, MLA_NOPE + MLA_ROPE), F32), jnp.zeros((c, pad), F32)], axis=-1)
    sin_ctx = jnp.zeros((c, HEAD_PAD), F32)
    return cos_lat, sin_lat, cos_ctx, sin_ctx


@jax.jit
def _forward(x, c, ctx, c_ctx, w_ada, b_ada, norm1_g, norm2_g, w_in, q_norm_g, w_q_up, kv_norm_g, w_kv_up,
             na_rpb, w_gate, b_gate, w_br_mla, w_br_na, w_br_ft, w_out, router_w, router_bias, w_e_gate,
             w_e_up, w_e_down, final_norm_g):
    nb, s, d = x.shape
    n_ctx = ctx.shape[1]
    n_layers = w_ada.shape[0]
    grid_rows = s // GRID_W

    mod_rows = 8
    c_rows = jnp.zeros((mod_rows, d), F32).at[:nb].set(c).at[nb].set(c_ctx)
    mod_all = _modulation(c_rows, w_ada, b_ada).reshape(n_layers, mod_rows, 1, 6 * d)

    weights = _prepare_weights(w_in, q_norm_g, w_q_up, kv_norm_g, w_kv_up, norm1_g, norm2_g, w_gate, b_gate,
                               w_br_mla, w_br_na, w_br_ft, w_out, w_e_gate, w_e_up, w_e_down)
    bias_all = _na_bias_tables(na_rpb, grid_rows)
    cos_lat, sin_lat, cos_ctx, sin_ctx = _rope_tables(s, n_ctx)
    router_w_p = jnp.zeros((d, LANES), F32).at[:, :N_EXPERTS].set(router_w).astype(BF16)
    router_b_p = jnp.zeros((1, LANES), F32).at[0, :N_EXPERTS].set(router_bias)

    def lat_row(b):
        return b

    def ctx_row(b):
        return nb

    x_lat, x_ctx = x, ctx
    for l in range(n_layers):
        lw = {k: v[l] for k, v in weights.items()}
        mod = mod_all[l]
        need_ctx = l < n_layers - 1
        q_l, k_l, v_l, nq_l, nk_l, nv_l, ft_l = _project(x_lat, mod, lat_row, lw, cos_lat, sin_lat)
        q_c, k_c, v_c, nq_c, nk_c, nv_c, ft_c = _project(x_ctx, mod, ctx_row, lw, cos_ctx, sin_ctx)
        o_a = _mla_attention(q_l, k_l, k_c, v_l, v_c)
        o_b = _na_attention(nq_l, nk_l, nv_l, nk_c, nv_c, bias_all[:, l])
        o_c = _fourier_lat(ft_l)
        x1, h2 = _merge(x_lat, o_a, o_b, o_c, mod, lat_row, lw)
        x_lat = _moe(h2, x1, mod, lat_row, lw, router_w_p, router_b_p)
        if need_ctx:
            oc_a = _mla_ctx_attention(q_c, k_c, v_c)
            oc_b = _na_ctx_attention(nq_c, nk_c, nv_c)
            oc_c = _fourier_ctx(ft_c)
            x1c, h2c = _merge(x_ctx, oc_a, oc_b, oc_c, mod, ctx_row, lw)
            x_ctx = _moe(h2c, x1c, mod, ctx_row, lw, router_w_p, router_b_p)
    return _final_norm(x_lat, final_norm_g)


def kernel(x, c, ctx, c_ctx, w_ada, b_ada, norm1_g, norm2_g, w_in, q_norm_g, w_q_up, kv_norm_g, w_kv_up, na_rpb, w_gate, b_gate, w_br_mla, w_br_na, w_br_ft, w_out, router_w, router_bias, w_e_gate, w_e_up, w_e_down, final_norm_g):
    return _forward(x, c, ctx, c_ctx, w_ada, b_ada, norm1_g, norm2_g, w_in, q_norm_g, w_q_up, kv_norm_g,
                    w_kv_up, na_rpb, w_gate, b_gate, w_br_mla, w_br_na, w_br_ft, w_out, router_w, router_bias,
                    w_e_gate, w_e_up, w_e_down, final_norm_g)
```

```python
import functools
import math

import jax
import jax.numpy as jnp
import numpy as np
from jax import lax
from jax.experimental import pallas as pl
from jax.experimental.pallas import tpu as pltpu

F32 = jnp.float32
BF16 = jnp.bfloat16

GRID_W = 64
MLA_HEADS = 8
MLA_NOPE = 64
MLA_ROPE = 32
MLA_V = 64
MLA_Q_LORA = 256
MLA_KV_LORA = 128
MLA_SCALE = (MLA_NOPE + MLA_ROPE) ** -0.5
NA_HEADS = 4
NA_HEAD_DIM = 64
NA_WIN_R = 8
NA_WIN_C = 16
NA_SCALE = NA_HEAD_DIM ** -0.5
NA_WIDTH = NA_HEADS * NA_HEAD_DIM
FT_GROUPS = 4
FT_GROUP_DIM = 64
FT_WIDTH = FT_GROUPS * FT_GROUP_DIM
N_EXPERTS = 16
N_GROUPS = 4
EXPERTS_PER_GROUP = N_EXPERTS // N_GROUPS
ROPE_THETA = 10000.0
NORM_EPS = 1e-6
MASK_VALUE = -1e30
LOG2E = math.log2(math.e)

LANES = 128
V7X_VMEM_LIMIT_BYTES = 56 * 1024 * 1024

HEAD_PAD = LANES
NA_ROWS_PER_BLOCK = 8
NA_KEY_ROWS = 16
EXPERTS_PER_STEP = 4
MLA_MAX_KV_CHUNK = 2816


def _cparams(n_axes):
    return pltpu.CompilerParams(
        dimension_semantics=("arbitrary",) * n_axes,
        vmem_limit_bytes=V7X_VMEM_LIMIT_BYTES,
    )


def _rms(x, g):
    return x * lax.rsqrt(jnp.mean(x * x, axis=-1, keepdims=True) + NORM_EPS) * g


def _dot(a, b):
    return jnp.dot(a, b, preferred_element_type=F32)


def _dot_nt(a, b):
    return lax.dot_general(a, b, (((1,), (1,)), ((), ())), preferred_element_type=F32)


def _mod_kernel(c_ref, w_ref, b_ref, o_ref):
    c = c_ref[...]
    o_ref[...] = _dot(c * jax.nn.sigmoid(c), w_ref[...]) + b_ref[...]


def _modulation(c_rows, w_ada, b_ada):
    n_layers, d, width = w_ada.shape
    rows = c_rows.shape[0]
    tn = 1536
    return pl.pallas_call(
        _mod_kernel,
        grid=(n_layers, width // tn),
        in_specs=[
            pl.BlockSpec((rows, d), lambda l, j: (0, 0)),
            pl.BlockSpec((None, d, tn), lambda l, j: (l, 0, j)),
            pl.BlockSpec((None, 1, tn), lambda l, j: (l, 0, j)),
        ],
        out_specs=pl.BlockSpec((None, rows, tn), lambda l, j: (l, 0, j)),
        out_shape=jax.ShapeDtypeStruct((n_layers, rows, width), F32),
        compiler_params=_cparams(2),
        name="modulation",
    )(c_rows, w_ada, b_ada.reshape(n_layers, 1, width))


def _proj_kernel(x_ref, mod_ref, n1g_ref, win_ref, qg_ref, wqm_ref, wqr_ref, kvg_ref, wkk_ref, wkv_ref,
                 cos_ref, sin_ref, q_ref, k_ref, v_ref, nq_ref, nk_ref, nv_ref, ft_ref):
    d = x_ref.shape[-1]
    x = x_ref[...]
    mod = mod_ref[...]
    sh1, sc1 = mod[:, 0:d], mod[:, d:2 * d]
    h = _rms(x, n1g_ref[...]) * (1.0 + sc1) + sh1
    p = _dot(h.astype(BF16), win_ref[...])
    cos = cos_ref[...]
    sin = sin_ref[...]
    o = 0
    q_lat = p[:, o:o + MLA_Q_LORA]
    o += MLA_Q_LORA
    kv_lat = p[:, o:o + MLA_KV_LORA]
    o += MLA_KV_LORA
    kr_a = p[:, o:o + HEAD_PAD]
    o += HEAD_PAD
    kr_b = p[:, o:o + HEAD_PAD]
    o += HEAD_PAD
    nq_ref[...] = p[:, o:o + NA_WIDTH].astype(BF16)
    o += NA_WIDTH
    nk_ref[...] = p[:, o:o + NA_WIDTH].astype(BF16)
    o += NA_WIDTH
    nv_ref[...] = p[:, o:o + NA_WIDTH].astype(BF16)
    o += NA_WIDTH
    ft_ref[...] = p[:, o:o + FT_WIDTH].astype(BF16)

    qn = _rms(q_lat, qg_ref[...]).astype(BF16)
    qm = _dot(qn, wqm_ref[...])
    qr = _dot(qn, wqr_ref[...])
    kvn = _rms(kv_lat, kvg_ref[...]).astype(BF16)
    kn = _dot(kvn, wkk_ref[...])
    v = _dot(kvn, wkv_ref[...])
    v_lane = lax.broadcasted_iota(jnp.int32, v.shape, 1) & (HEAD_PAD - 1)
    v_ref[...] = jnp.where(v_lane == MLA_V, 1.0, v).astype(BF16)
    kr = kr_a * cos + kr_b * sin
    q_scale = MLA_SCALE * LOG2E
    for hd in range(MLA_HEADS):
        sl = slice(hd * HEAD_PAD, (hd + 1) * HEAD_PAD)
        q_ref[:, sl] = ((qm[:, sl] * cos + qr[:, sl] * sin) * q_scale).astype(BF16)
        k_ref[:, sl] = (kn[:, sl] + kr).astype(BF16)


def _project(x, mod, mod_row, lw, cos_t, sin_t):
    nb, n, d = x.shape
    tm = min(512, n)
    win = lw["w_in"]
    wide = MLA_HEADS * HEAD_PAD

    def full(a):
        return pl.BlockSpec(a.shape, lambda b, i: (0,) * a.ndim)

    def tok(width):
        return pl.BlockSpec((None, tm, width), lambda b, i: (b, i, 0))

    outs = [wide, wide, wide, NA_WIDTH, NA_WIDTH, NA_WIDTH, FT_WIDTH]
    return pl.pallas_call(
        _proj_kernel,
        grid=(nb, n // tm),
        in_specs=[
            tok(d),
            pl.BlockSpec((None, 1, mod.shape[-1]), lambda b, i: (mod_row(b), 0, 0)),
            full(lw["norm1_g"]), full(win), full(lw["q_norm_g"]), full(lw["wq_main"]), full(lw["wq_rot"]),
            full(lw["kv_norm_g"]), full(lw["wkv_k"]), full(lw["wkv_v"]),
            pl.BlockSpec((tm, HEAD_PAD), lambda b, i: (i, 0)),
            pl.BlockSpec((tm, HEAD_PAD), lambda b, i: (i, 0)),
        ],
        out_specs=[tok(w) for w in outs],
        out_shape=[jax.ShapeDtypeStruct((nb, n, w), BF16) for w in outs],
        compiler_params=_cparams(2),
        name="project",
    )(x, mod, lw["norm1_g"], win, lw["q_norm_g"], lw["wq_main"], lw["wq_rot"], lw["kv_norm_g"],
      lw["wkv_k"], lw["wkv_v"], cos_t, sin_t)


def _kv_chunk(p):
    best = LANES
    for n in range(1, p // LANES + 1):
        tk = p // n
        if p % n == 0 and tk % LANES == 0 and tk <= MLA_MAX_KV_CHUNK:
            best = max(best, tk)
    return best


def _mla_kernel(q_ref, k_ref, v_ref, o_ref, s0_scr, s1_scr, p0_scr, p1_scr, mx_scr, m_scr, alpha_scr, acc_scr,
                *, tq, tk, n_q, n_c):
    def head(hd):
        return slice(hd * HEAD_PAD, (hd + 1) * HEAD_PAD)

    def q_rows(qt):
        return pl.ds(pl.multiple_of(qt * tq, tq), tq)

    def k_rows(c):
        return pl.ds(pl.multiple_of(c * tk, tk), tk)

    def scores(hd, qt, c, s_scr):
        s = _dot_nt(q_ref[q_rows(qt), head(hd)], k_ref[k_rows(c), head(hd)])
        s_scr[...] = s
        mx_scr[hd] = jnp.broadcast_to(jnp.max(s, axis=1, keepdims=True), mx_scr.shape[1:])

    def softmax(hd, s_scr, p_scr, first):
        m_prev = jnp.where(first, MASK_VALUE, m_scr[hd])
        m_new = jnp.maximum(m_prev, mx_scr[hd])
        alpha_scr[hd] = jnp.exp2(m_prev - m_new)
        m_scr[hd] = m_new
        p_scr[...] = jnp.exp2(s_scr[...] - jnp.tile(m_new, (1, tk // LANES))).astype(BF16)

    def accumulate(hd, p_scr, qt, c):
        acc = alpha_scr[hd] * acc_scr[hd] + _dot(p_scr[...], v_ref[k_rows(c), head(hd)])
        acc_scr[hd] = acc
        o_ref[q_rows(qt), head(hd)] = (acc / acc[:, MLA_V:MLA_V + 1]).astype(BF16)

    lane = lax.broadcasted_iota(jnp.int32, acc_scr.shape, 2)
    acc_scr[...] = jnp.where(lane == MLA_V, 1.0, 0.0)
    m_scr[...] = jnp.full(m_scr.shape, MASK_VALUE, F32)
    alpha_scr[...] = jnp.ones(alpha_scr.shape, F32)
    p1_scr[...] = jnp.zeros(p1_scr.shape, BF16)
    scores(0, 0, 0, s0_scr)

    def body(i, carry):
        qt, c = carry
        first = c == 0
        last = c == n_c - 1
        c_prev = jnp.where(first, n_c - 1, c - 1)
        qt_prev = jnp.maximum(jnp.where(first, qt - 1, qt), 0)
        c_next = jnp.where(last, 0, c + 1)
        qt_next = jnp.where(last, qt + 1, qt)
        scores(1, qt, c, s1_scr)
        softmax(0, s0_scr, p0_scr, first)
        accumulate(1, p1_scr, qt_prev, c_prev)
        scores(0, jnp.minimum(qt_next, n_q - 1), c_next, s0_scr)
        softmax(1, s1_scr, p1_scr, first)
        accumulate(0, p0_scr, qt, c)
        return qt_next, c_next

    lax.fori_loop(0, n_q * n_c, body, (jnp.int32(0), jnp.int32(0)))
    accumulate(1, p1_scr, n_q - 1, n_c - 1)


def _mla_attention(q, k_all, v_all):
    nb, s, _ = q.shape
    p = k_all.shape[1]
    tq = min(512, s)
    tk = _kv_chunk(p)
    pairs = MLA_HEADS // 2
    kern = functools.partial(_mla_kernel, tq=tq, tk=tk, n_q=s // tq, n_c=p // tk)
    return pl.pallas_call(
        kern,
        grid=(nb, pairs),
        in_specs=[
            pl.BlockSpec((None, s, 2 * HEAD_PAD), lambda b, hp: (b, 0, hp)),
            pl.BlockSpec((None, p, 2 * HEAD_PAD), lambda b, hp: (b, 0, hp)),
            pl.BlockSpec((None, p, 2 * HEAD_PAD), lambda b, hp: (b, 0, hp)),
        ],
        out_specs=pl.BlockSpec((None, s, 2 * HEAD_PAD), lambda b, hp: (b, 0, hp)),
        out_shape=jax.ShapeDtypeStruct((nb, s, MLA_HEADS * HEAD_PAD), BF16),
        scratch_shapes=[
            pltpu.VMEM((tq, tk), F32),
            pltpu.VMEM((tq, tk), F32),
            pltpu.VMEM((tq, tk), BF16),
            pltpu.VMEM((tq, tk), BF16),
            pltpu.VMEM((2, tq, LANES), F32),
            pltpu.VMEM((2, tq, LANES), F32),
            pltpu.VMEM((2, tq, LANES), F32),
            pltpu.VMEM((2, tq, HEAD_PAD), F32),
        ],
        compiler_params=_cparams(2),
        name="mla_attention",
    )(q, k_all, v_all)


def _mla_ctx_kernel(q_ref, k_ref, v_ref, o_ref):
    for hd in range(2):
        sl = slice(hd * HEAD_PAD, (hd + 1) * HEAD_PAD)
        s = _dot_nt(q_ref[:, sl], k_ref[:, sl])
        p = jnp.exp2(s - jnp.max(s, axis=1, keepdims=True))
        acc = _dot(p.astype(BF16), v_ref[:, sl])
        o_ref[:, sl] = (acc / acc[:, MLA_V:MLA_V + 1]).astype(BF16)


def _mla_ctx_attention(q, k, v):
    nb, c, _ = q.shape
    pairs = MLA_HEADS // 2
    spec = pl.BlockSpec((None, c, 2 * HEAD_PAD), lambda b, hp: (b, 0, hp))
    return pl.pallas_call(
        _mla_ctx_kernel,
        grid=(nb, pairs),
        in_specs=[spec, spec, spec],
        out_specs=spec,
        out_shape=jax.ShapeDtypeStruct((nb, c, MLA_HEADS * HEAD_PAD), BF16),
        compiler_params=_cparams(2),
        name="mla_ctx_attention",
    )(q, k, v)


def _na_kernel(q_ref, k_ref, v_ref, kc_ref, vc_ref, bias_ref, o_ref, *, grid_rows):
    rb = pl.program_id(2)
    start_row = jnp.clip(rb * NA_ROWS_PER_BLOCK - NA_WIN_R // 2, 0, grid_rows - NA_KEY_ROWS)
    start = pl.multiple_of(start_row * GRID_W, (NA_WIN_R // 2) * GRID_W)
    n_keys = NA_KEY_ROWS * GRID_W
    q = q_ref[...]
    kw = k_ref[pl.ds(start, n_keys), :]
    vw = v_ref[pl.ds(start, n_keys), :]
    kc = kc_ref[...]
    vc = vc_ref[...]
    lane = lax.broadcasted_iota(jnp.int32, q.shape, 1)
    outs = []
    for hd in range(2):
        in_head = (lane >= NA_HEAD_DIM) if hd else (lane < NA_HEAD_DIM)
        qh = jnp.where(in_head, q, jnp.zeros_like(q))
        s = _dot_nt(qh, kw) * NA_SCALE + bias_ref[hd]
        sc = _dot_nt(qh, kc) * NA_SCALE
        m = jnp.maximum(jnp.max(s, axis=1, keepdims=True), jnp.max(sc, axis=1, keepdims=True))
        p = jnp.exp(s - m)
        pc = jnp.exp(sc - m)
        denom = jnp.sum(p, axis=1, keepdims=True) + jnp.sum(pc, axis=1, keepdims=True)
        outs.append((_dot(p.astype(BF16), vw) + _dot(pc.astype(BF16), vc)) / denom)
    o_ref[...] = jnp.where(lane < NA_HEAD_DIM, outs[0], outs[1]).astype(BF16)


def _na_attention(q, k, v, kc, vc, bias):
    nb, s, _ = q.shape
    c = kc.shape[1]
    grid_rows = s // GRID_W
    tq = NA_ROWS_PER_BLOCK * GRID_W
    n_blocks = grid_rows // NA_ROWS_PER_BLOCK
    n_keys = NA_KEY_ROWS * GRID_W
    pairs = NA_HEADS // 2

    def variant(rb):
        return jnp.where(rb == 0, 0, jnp.where(rb == n_blocks - 1, 2, 1))

    kern = functools.partial(_na_kernel, grid_rows=grid_rows)
    return pl.pallas_call(
        kern,
        grid=(nb, pairs, n_blocks),
        in_specs=[
            pl.BlockSpec((None, tq, LANES), lambda b, hp, rb: (b, rb, hp)),
            pl.BlockSpec((None, s, LANES), lambda b, hp, rb: (b, 0, hp)),
            pl.BlockSpec((None, s, LANES), lambda b, hp, rb: (b, 0, hp)),
            pl.BlockSpec((None, c, LANES), lambda b, hp, rb: (b, 0, hp)),
            pl.BlockSpec((None, c, LANES), lambda b, hp, rb: (b, 0, hp)),
            pl.BlockSpec((None, 2, tq, n_keys), lambda b, hp, rb: (variant(rb), hp, 0, 0)),
        ],
        out_specs=pl.BlockSpec((None, tq, LANES), lambda b, hp, rb: (b, rb, hp)),
        out_shape=jax.ShapeDtypeStruct((nb, s, NA_WIDTH), BF16),
        compiler_params=_cparams(3),
        name="na_attention",
    )(q, k, v, kc, vc, bias)


def _na_ctx_kernel(q_ref, k_ref, v_ref, o_ref):
    q = q_ref[...]
    k = k_ref[...]
    v = v_ref[...]
    lane = lax.broadcasted_iota(jnp.int32, q.shape, 1)
    outs = []
    for hd in range(2):
        in_head = (lane >= NA_HEAD_DIM) if hd else (lane < NA_HEAD_DIM)
        qh = jnp.where(in_head, q, jnp.zeros_like(q))
        s = _dot_nt(qh, k) * NA_SCALE
        p = jnp.exp(s - jnp.max(s, axis=1, keepdims=True))
        outs.append(_dot(p.astype(BF16), v) / jnp.sum(p, axis=1, keepdims=True))
    o_ref[...] = jnp.where(lane < NA_HEAD_DIM, outs[0], outs[1]).astype(BF16)


def _na_ctx_attention(q, k, v):
    nb, c, _ = q.shape
    spec = pl.BlockSpec((None, c, LANES), lambda b, hp: (b, 0, hp))
    return pl.pallas_call(
        _na_ctx_kernel,
        grid=(nb, NA_HEADS // 2),
        in_specs=[spec, spec, spec],
        out_specs=spec,
        out_shape=jax.ShapeDtypeStruct((nb, c, NA_WIDTH), BF16),
        compiler_params=_cparams(2),
        name="na_ctx_attention",
    )(q, k, v)


def _na_bias_tables(rpb, grid_rows):
    qr = np.arange(NA_ROWS_PER_BLOCK)[:, None]
    kr = np.arange(NA_KEY_ROWS)[None, :]
    n_dr = 2 * NA_WIN_R - 1
    n_dc = 2 * NA_WIN_C - 1
    row_onehot = np.zeros((3, NA_ROWS_PER_BLOCK, NA_KEY_ROWS, n_dr), np.float32)
    row_valid = np.zeros((3, NA_ROWS_PER_BLOCK, NA_KEY_ROWS), bool)
    n_blocks = grid_rows // NA_ROWS_PER_BLOCK
    for vi, rb in enumerate((0, 1, n_blocks - 1)):
        r = rb * NA_ROWS_PER_BLOCK + qr
        start_row = min(max(rb * NA_ROWS_PER_BLOCK - NA_WIN_R // 2, 0), grid_rows - NA_KEY_ROWS)
        key_row = start_row + kr
        win = np.clip(r - NA_WIN_R // 2, 0, grid_rows - NA_WIN_R)
        valid = (key_row >= win) & (key_row < win + NA_WIN_R)
        dr = np.clip(key_row - r + (NA_WIN_R - 1), 0, n_dr - 1)
        row_valid[vi] = valid
        row_onehot[vi] = (dr[..., None] == np.arange(n_dr)) & valid[..., None]
    qc = np.arange(GRID_W)[:, None]
    kcol = np.arange(GRID_W)[None, :]
    wstart = np.clip(qc - NA_WIN_C // 2, 0, GRID_W - NA_WIN_C)
    col_valid = (kcol >= wstart) & (kcol < wstart + NA_WIN_C)
    dc = np.clip(kcol - qc, -(NA_WIN_C - 1), NA_WIN_C - 1) + (NA_WIN_C - 1)
    col_onehot = ((dc[..., None] == np.arange(n_dc)) & col_valid[..., None]).astype(np.float32)
    t = jnp.einsum("lhij,cdj->lhicd", rpb.astype(F32), jnp.asarray(col_onehot), precision=lax.Precision.HIGHEST)
    t = jnp.einsum("vqki,lhicd->vlhqckd", jnp.asarray(row_onehot), t, precision=lax.Precision.HIGHEST)
    valid = row_valid[:, None, None, :, None, :, None] & col_valid[None, None, None, None, :, None, :]
    t = jnp.where(jnp.asarray(valid), t, MASK_VALUE)
    n_layers, n_heads = rpb.shape[:2]
    return t.reshape(3, n_layers, n_heads, NA_ROWS_PER_BLOCK * GRID_W, NA_KEY_ROWS * GRID_W)


FT_N1 = 64


def _ft_stage1_kernel(w_ref, x_ref, o_ref):
    o_ref[...] = _dot(w_ref[...], x_ref[...])


def _ft_stage2_kernel(are_ref, aim_ref, tc_ref, ts_ref, w_ref, cd_ref, sd_ref, o_ref, *, scale):
    are = are_ref[...]
    aim = aim_ref[...]
    tc = jnp.concatenate([tc_ref[...]] * (FT_WIDTH // LANES), axis=1)
    ts = jnp.concatenate([ts_ref[...]] * (FT_WIDTH // LANES), axis=1)
    bre = are * tc + aim * ts
    bim = aim * tc - are * ts
    bb = jnp.concatenate([bre, bim], axis=0).astype(BF16)
    xk = _dot(w_ref[...], bb)
    n2 = are.shape[0]
    f = _dot(xk[:n2].astype(BF16), cd_ref[...]) + _dot(xk[n2:].astype(BF16), sd_ref[...])
    o_ref[...] = (f * scale).astype(BF16)


def _dft_mats(n):
    ang = 2.0 * np.pi * np.outer(np.arange(n), np.arange(n)) / n
    return np.cos(ang), np.sin(ang)


def _channel_dft():
    c, s = _dft_mats(FT_GROUP_DIM)
    eye = np.eye(FT_GROUPS)
    return jnp.asarray(np.kron(eye, c), BF16), jnp.asarray(np.kron(eye, s), BF16)


def _fourier_lat(u):
    nb, n, _ = u.shape
    n2 = n // FT_N1
    cols = n2 * FT_WIDTH
    c1, s1 = _dft_mats(FT_N1)
    w1 = jnp.asarray(np.concatenate([c1, -s1], axis=0), BF16)
    tcol = min(2048, cols)
    a = pl.pallas_call(
        _ft_stage1_kernel,
        grid=(nb, cols // tcol),
        in_specs=[
            pl.BlockSpec(w1.shape, lambda b, j: (0, 0)),
            pl.BlockSpec((None, FT_N1, tcol), lambda b, j: (b, 0, j)),
        ],
        out_specs=pl.BlockSpec((None, 2 * FT_N1, tcol), lambda b, j: (b, 0, j)),
        out_shape=jax.ShapeDtypeStruct((nb, 2 * FT_N1, cols), F32),
        compiler_params=_cparams(2),
        name="fourier_stage1",
    )(w1, u.reshape(nb, FT_N1, cols))
    a = a.reshape(nb, 2 * FT_N1, n2, FT_WIDTH)

    ang = 2.0 * np.pi * np.outer(np.arange(FT_N1), np.arange(n2)) / n
    tc = jnp.asarray(np.repeat(np.cos(ang)[:, :, None], LANES, axis=2), F32)
    ts = jnp.asarray(np.repeat(np.sin(ang)[:, :, None], LANES, axis=2), F32)
    c3, s3 = _dft_mats(n2)
    w3 = jnp.asarray(np.block([[c3, s3], [-s3, c3]]), BF16)
    cd, sd = _channel_dft()
    kern = functools.partial(_ft_stage2_kernel, scale=1.0 / math.sqrt(n * FT_GROUP_DIM))
    out = pl.pallas_call(
        kern,
        grid=(nb, FT_N1),
        in_specs=[
            pl.BlockSpec((None, None, n2, FT_WIDTH), lambda b, k1: (b, k1, 0, 0)),
            pl.BlockSpec((None, None, n2, FT_WIDTH), lambda b, k1: (b, FT_N1 + k1, 0, 0)),
            pl.BlockSpec((None, n2, LANES), lambda b, k1: (k1, 0, 0)),
            pl.BlockSpec((None, n2, LANES), lambda b, k1: (k1, 0, 0)),
            pl.BlockSpec(w3.shape, lambda b, k1: (0, 0)),
            pl.BlockSpec(cd.shape, lambda b, k1: (0, 0)),
            pl.BlockSpec(sd.shape, lambda b, k1: (0, 0)),
        ],
        out_specs=pl.BlockSpec((None, n2, FT_WIDTH), lambda b, k1: (b, 0, k1)),
        out_shape=jax.ShapeDtypeStruct((nb, n2, FT_N1 * FT_WIDTH), BF16),
        compiler_params=_cparams(2),
        name="fourier_stage2",
    )(a, a, tc, ts, w3, cd, sd)
    return out.reshape(nb, n, FT_WIDTH)


def _ft_ctx_kernel(u_ref, cn_ref, sn_ref, cd_ref, sd_ref, o_ref, *, scale):
    u = u_ref[...]
    y1 = _dot(u, cd_ref[...]).astype(BF16)
    y2 = _dot(u, sd_ref[...]).astype(BF16)
    f = _dot(cn_ref[...], y1) - _dot(sn_ref[...], y2)
    o_ref[...] = (f * scale).astype(BF16)


def _fourier_ctx(u):
    nb, n, _ = u.shape
    cn, sn = _dft_mats(n)
    cn, sn = jnp.asarray(cn, BF16), jnp.asarray(sn, BF16)
    cd, sd = _channel_dft()
    kern = functools.partial(_ft_ctx_kernel, scale=1.0 / math.sqrt(n * FT_GROUP_DIM))
    sq = pl.BlockSpec((n, n), lambda b: (0, 0))
    ch = pl.BlockSpec(cd.shape, lambda b: (0, 0))
    tok = pl.BlockSpec((None, n, FT_WIDTH), lambda b: (b, 0, 0))
    return pl.pallas_call(
        kern,
        grid=(nb,),
        in_specs=[tok, sq, sq, ch, ch],
        out_specs=tok,
        out_shape=jax.ShapeDtypeStruct((nb, n, FT_WIDTH), BF16),
        compiler_params=_cparams(1),
        name="fourier_ctx",
    )(u, cn, sn, cd, sd)


def _merge_kernel(x_ref, oa_ref, ob_ref, oc_ref, mod_ref, n1g_ref, n2g_ref, wgate_ref, bgate_ref,
                  wba_ref, wbb_ref, wbc_ref, wout_ref, x1_ref, h2_ref):
    d = x_ref.shape[-1]
    x = x_ref[...]
    mod = mod_ref[...]
    sh1, sc1, g1, sh2, sc2 = (mod[:, i * d:(i + 1) * d] for i in range(5))
    h = _rms(x, n1g_ref[...]) * (1.0 + sc1) + sh1
    gate = jax.nn.sigmoid(_dot(h.astype(BF16), wgate_ref[...]) + bgate_ref[...])
    m = gate[:, 0:d] * _dot(oa_ref[...], wba_ref[...])
    m += gate[:, d:2 * d] * _dot(ob_ref[...], wbb_ref[...])
    m += gate[:, 2 * d:3 * d] * _dot(oc_ref[...], wbc_ref[...])
    y = _dot(m.astype(BF16), wout_ref[...])
    x1 = x + g1 * y
    x1_ref[...] = x1
    h2_ref[...] = (_rms(x1, n2g_ref[...]) * (1.0 + sc2) + sh2).astype(BF16)


def _merge(x, o_a, o_b, o_c, mod, mod_row, lw):
    nb, n, d = x.shape
    tm = min(256, n)

    def full(a):
        return pl.BlockSpec(a.shape, lambda b, i: (0,) * a.ndim)

    def tok(width):
        return pl.BlockSpec((None, tm, width), lambda b, i: (b, i, 0))

    weights = [lw["norm1_g"], lw["norm2_g"], lw["w_gate"], lw["b_gate"], lw["w_br_mla"], lw["w_br_na"],
               lw["w_br_ft"], lw["w_out"]]
    return pl.pallas_call(
        _merge_kernel,
        grid=(nb, n // tm),
        in_specs=[tok(d), tok(o_a.shape[-1]), tok(o_b.shape[-1]), tok(o_c.shape[-1]),
                  pl.BlockSpec((None, 1, mod.shape[-1]), lambda b, i: (mod_row(b), 0, 0))]
        + [full(w) for w in weights],
        out_specs=[tok(d), tok(d)],
        out_shape=[jax.ShapeDtypeStruct((nb, n, d), F32), jax.ShapeDtypeStruct((nb, n, d), BF16)],
        compiler_params=_cparams(2),
        name="merge",
    )(x, o_a, o_b, o_c, mod, *weights)


def _route(logits, router_bias):
    lane = lax.broadcasted_iota(jnp.int32, logits.shape, 1)
    neg = -jnp.inf
    big = jnp.int32(LANES)
    scores = jax.nn.sigmoid(logits)
    sel = scores + router_bias

    def top2(vals):
        m1 = jnp.max(vals, axis=1, keepdims=True)
        i1 = jnp.min(jnp.where(vals == m1, lane, big), axis=1, keepdims=True)
        rest = jnp.where(lane == i1, neg, vals)
        m2 = jnp.max(rest, axis=1, keepdims=True)
        i2 = jnp.min(jnp.where(rest == m2, lane, big), axis=1, keepdims=True)
        return m1, i1, m2, i2

    grp_scores = []
    for g in range(N_GROUPS):
        in_grp = (lane >= g * EXPERTS_PER_GROUP) & (lane < (g + 1) * EXPERTS_PER_GROUP)
        m1, _, m2, _ = top2(jnp.where(in_grp, sel, neg))
        grp_scores.append(m1 + m2)
    best = functools.reduce(jnp.maximum, grp_scores)
    grp = jnp.full(best.shape, N_GROUPS - 1, jnp.int32)
    for g in range(N_GROUPS - 2, -1, -1):
        grp = jnp.where(grp_scores[g] == best, g, grp)
    in_sel = (lane >= grp * EXPERTS_PER_GROUP) & (lane < (grp + 1) * EXPERTS_PER_GROUP)
    _, i1, _, i2 = top2(jnp.where(in_sel, sel, neg))
    hit1 = lane == i1
    hit2 = lane == i2
    w1 = jnp.sum(jnp.where(hit1, scores, 0.0), axis=1, keepdims=True)
    w2 = jnp.sum(jnp.where(hit2, scores, 0.0), axis=1, keepdims=True)
    total = w1 + w2
    return jnp.where(hit1, w1 / total, 0.0) + jnp.where(hit2, w2 / total, 0.0)


def _moe_kernel(h_ref, x_ref, mod_ref, rw_ref, rb_ref, wg_ref, wu_ref, wd_ref, o_ref, gate_scr, acc_scr):
    d = x_ref.shape[-1]
    step = pl.program_id(2)
    n_steps = pl.num_programs(2)
    h = h_ref[...]

    @pl.when(step == 0)
    def _():
        gate = _route(_dot(h, rw_ref[...]), rb_ref[...])
        for c in range(N_EXPERTS // EXPERTS_PER_STEP):
            shift = (LANES - c * EXPERTS_PER_STEP) % LANES
            gate_scr[c] = pltpu.roll(gate, shift, axis=1) if shift else gate
        acc_scr[...] = jnp.zeros(acc_scr.shape, F32)

    gate = gate_scr[step]
    acc = acc_scr[...]
    for j in range(EXPERTS_PER_STEP):
        a = _dot(h, wg_ref[j])
        u = _dot(h, wu_ref[j])
        hid = (a * jax.nn.sigmoid(a)) * u * gate[:, j:j + 1]
        acc += _dot(hid.astype(BF16), wd_ref[j])
    acc_scr[...] = acc

    @pl.when(step == n_steps - 1)
    def _():
        g2 = mod_ref[:, 5 * d:6 * d]
        o_ref[...] = x_ref[...] + g2 * acc


def _moe(h2, x1, mod, mod_row, lw, router_w, router_b):
    nb, n, d = x1.shape
    tm = min(1024, n)
    n_steps = N_EXPERTS // EXPERTS_PER_STEP
    ff = lw["w_e_gate"].shape[-1]
    tok = pl.BlockSpec((None, tm, d), lambda b, i, c: (b, i, 0))
    return pl.pallas_call(
        _moe_kernel,
        grid=(nb, n // tm, n_steps),
        in_specs=[
            tok, tok,
            pl.BlockSpec((None, 1, mod.shape[-1]), lambda b, i, c: (mod_row(b), 0, 0)),
            pl.BlockSpec(router_w.shape, lambda b, i, c: (0, 0)),
            pl.BlockSpec(router_b.shape, lambda b, i, c: (0, 0)),
            pl.BlockSpec((EXPERTS_PER_STEP, d, ff), lambda b, i, c: (c, 0, 0)),
            pl.BlockSpec((EXPERTS_PER_STEP, d, ff), lambda b, i, c: (c, 0, 0)),
            pl.BlockSpec((EXPERTS_PER_STEP, ff, d), lambda b, i, c: (c, 0, 0)),
        ],
        out_specs=tok,
        out_shape=jax.ShapeDtypeStruct((nb, n, d), F32),
        scratch_shapes=[pltpu.VMEM((n_steps, tm, LANES), F32), pltpu.VMEM((tm, d), F32)],
        compiler_params=_cparams(3),
        name="moe",
    )(h2, x1, mod, router_w, router_b, lw["w_e_gate"], lw["w_e_up"], lw["w_e_down"])


def _final_norm_kernel(x_ref, g_ref, o_ref):
    o_ref[...] = _rms(x_ref[...], g_ref[...])


def _final_norm(x, g):
    nb, n, d = x.shape
    tm = min(512, n)
    tok = pl.BlockSpec((None, tm, d), lambda b, i: (b, i, 0))
    return pl.pallas_call(
        _final_norm_kernel,
        grid=(nb, n // tm),
        in_specs=[tok, pl.BlockSpec((1, d), lambda b, i: (0, 0))],
        out_specs=tok,
        out_shape=jax.ShapeDtypeStruct((nb, n, d), F32),
        compiler_params=_cparams(2),
        name="final_norm",
    )(x, g.reshape(1, d))


def _rotate_cols(w):
    half = w.shape[-1] // 2
    return jnp.concatenate([-w[..., half:], w[..., :half]], axis=-1)


def _prepare_weights(w_in, q_norm_g, w_q_up, kv_norm_g, w_kv_up, norm1_g, norm2_g, w_gate, b_gate, w_br_mla,
                     w_br_na, w_br_ft, w_out, w_e_gate, w_e_up, w_e_down):
    n_layers, d, _ = w_in.shape
    splits = np.cumsum([MLA_Q_LORA, MLA_KV_LORA, MLA_ROPE, NA_WIDTH, NA_WIDTH, NA_WIDTH])
    w_q, w_kv, w_kr, w_nq, w_nk, w_nv, w_ft = jnp.split(w_in, [int(s) for s in splits], axis=-1)

    def in_rope_slot(w):
        zeros_lo = jnp.zeros(w.shape[:-1] + (MLA_NOPE,), w.dtype)
        zeros_hi = jnp.zeros(w.shape[:-1] + (HEAD_PAD - MLA_NOPE - MLA_ROPE,), w.dtype)
        return jnp.concatenate([zeros_lo, w, zeros_hi], axis=-1)

    win = jnp.concatenate([w_q, w_kv, in_rope_slot(w_kr), in_rope_slot(_rotate_cols(w_kr)), w_nq, w_nk, w_nv, w_ft],
                          axis=-1).astype(BF16)
    wq = w_q_up.reshape(n_layers, MLA_Q_LORA, MLA_HEADS, MLA_NOPE + MLA_ROPE)
    wq_nope, wq_pe = wq[..., :MLA_NOPE], wq[..., MLA_NOPE:]
    pad = jnp.zeros(wq_pe.shape[:-1] + (HEAD_PAD - MLA_NOPE - MLA_ROPE,), wq.dtype)
    wq_main = jnp.concatenate([wq_nope, wq_pe, pad], axis=-1)
    wq_rot = jnp.concatenate([jnp.zeros_like(wq_nope), _rotate_cols(wq_pe), pad], axis=-1)
    wkv = w_kv_up.reshape(n_layers, MLA_KV_LORA, MLA_HEADS, MLA_NOPE + MLA_V)
    wk_nope, wv = wkv[..., :MLA_NOPE], wkv[..., MLA_NOPE:]
    wkv_k = jnp.concatenate([wk_nope, jnp.zeros(wk_nope.shape[:-1] + (HEAD_PAD - MLA_NOPE,), wkv.dtype)], axis=-1)
    wide = MLA_HEADS * HEAD_PAD
    return {
        "w_in": win,
        "wq_main": wq_main.reshape(n_layers, MLA_Q_LORA, wide).astype(BF16),
        "wq_rot": wq_rot.reshape(n_layers, MLA_Q_LORA, wide).astype(BF16),
        "wkv_k": wkv_k.reshape(n_layers, MLA_KV_LORA, wide).astype(BF16),
        "wkv_v": jnp.concatenate([wv, jnp.zeros(wv.shape[:-1] + (HEAD_PAD - MLA_V,), wv.dtype)], axis=-1)
        .reshape(n_layers, MLA_KV_LORA, wide).astype(BF16),
        "q_norm_g": q_norm_g.reshape(n_layers, 1, -1),
        "kv_norm_g": kv_norm_g.reshape(n_layers, 1, -1),
        "norm1_g": norm1_g.reshape(n_layers, 1, d),
        "norm2_g": norm2_g.reshape(n_layers, 1, d),
        "w_gate": w_gate.astype(BF16),
        "b_gate": b_gate.reshape(n_layers, 1, -1),
        "w_br_mla": jnp.pad(w_br_mla.reshape(n_layers, MLA_HEADS, MLA_V, d), ((0, 0), (0, 0), (0, HEAD_PAD - MLA_V), (0, 0)))
        .reshape(n_layers, wide, d).astype(BF16),
        "w_br_na": w_br_na.astype(BF16),
        "w_br_ft": w_br_ft.astype(BF16),
        "w_out": w_out.astype(BF16),
        "w_e_gate": w_e_gate.astype(BF16),
        "w_e_up": w_e_up.astype(BF16),
        "w_e_down": w_e_down.astype(BF16),
    }


def _rope_tables(s, c):
    n_freq = MLA_ROPE // 4
    inv_freq = ROPE_THETA ** (-jnp.arange(n_freq, dtype=F32) / n_freq)
    t = jnp.arange(s, dtype=jnp.int32)
    row = (t // GRID_W).astype(F32)
    col = (t % GRID_W).astype(F32)
    ang = jnp.concatenate([row[:, None] * inv_freq, col[:, None] * inv_freq], axis=-1)
    cos, sin = jnp.cos(ang), jnp.sin(ang)
    pad = HEAD_PAD - MLA_NOPE - MLA_ROPE
    cos_lat = jnp.concatenate([jnp.ones((s, MLA_NOPE), F32), cos, cos, jnp.zeros((s, pad), F32)], axis=-1)
    sin_lat = jnp.concatenate([jnp.zeros((s, MLA_NOPE), F32), sin, sin, jnp.zeros((s, pad), F32)], axis=-1)
    cos_ctx = jnp.concatenate([jnp.ones((c, MLA_NOPE + MLA_ROPE), F32), jnp.zeros((c, pad), F32)], axis=-1)
    sin_ctx = jnp.zeros((c, HEAD_PAD), F32)
    return cos_lat, sin_lat, cos_ctx, sin_ctx


@jax.jit
def _forward(x, c, ctx, c_ctx, w_ada, b_ada, norm1_g, norm2_g, w_in, q_norm_g, w_q_up, kv_norm_g, w_kv_up,
             na_rpb, w_gate, b_gate, w_br_mla, w_br_na, w_br_ft, w_out, router_w, router_bias, w_e_gate,
             w_e_up, w_e_down, final_norm_g):
    nb, s, d = x.shape
    n_ctx = ctx.shape[1]
    n_layers = w_ada.shape[0]
    grid_rows = s // GRID_W

    mod_rows = 8
    c_rows = jnp.zeros((mod_rows, d), F32).at[:nb].set(c).at[nb].set(c_ctx)
    mod_all = _modulation(c_rows, w_ada, b_ada).reshape(n_layers, mod_rows, 1, 6 * d)

    weights = _prepare_weights(w_in, q_norm_g, w_q_up, kv_norm_g, w_kv_up, norm1_g, norm2_g, w_gate, b_gate,
                               w_br_mla, w_br_na, w_br_ft, w_out, w_e_gate, w_e_up, w_e_down)
    bias_all = _na_bias_tables(na_rpb, grid_rows)
    cos_lat, sin_lat, cos_ctx, sin_ctx = _rope_tables(s, n_ctx)
    router_w_p = jnp.zeros((d, LANES), F32).at[:, :N_EXPERTS].set(router_w).astype(BF16)
    router_b_p = jnp.zeros((1, LANES), F32).at[0, :N_EXPERTS].set(router_bias)

    def lat_row(b):
        return b

    def ctx_row(b):
        return nb

    x_lat, x_ctx = x, ctx
    for l in range(n_layers):
        lw = {k: v[l] for k, v in weights.items()}
        mod = mod_all[l]
        need_ctx = l < n_layers - 1
        q_l, k_l, v_l, nq_l, nk_l, nv_l, ft_l = _project(x_lat, mod, lat_row, lw, cos_lat, sin_lat)
        q_c, k_c, v_c, nq_c, nk_c, nv_c, ft_c = _project(x_ctx, mod, ctx_row, lw, cos_ctx, sin_ctx)
        o_a = _mla_attention(q_l, jnp.concatenate([k_l, k_c], axis=1), jnp.concatenate([v_l, v_c], axis=1))
        o_b = _na_attention(nq_l, nk_l, nv_l, nk_c, nv_c, bias_all[:, l])
        o_c = _fourier_lat(ft_l)
        x1, h2 = _merge(x_lat, o_a, o_b, o_c, mod, lat_row, lw)
        x_lat = _moe(h2, x1, mod, lat_row, lw, router_w_p, router_b_p)
        if need_ctx:
            oc_a = _mla_ctx_attention(q_c, k_c, v_c)
            oc_b = _na_ctx_attention(nq_c, nk_c, nv_c)
            oc_c = _fourier_ctx(ft_c)
            x1c, h2c = _merge(x_ctx, oc_a, oc_b, oc_c, mod, ctx_row, lw)
            x_ctx = _moe(h2c, x1c, mod, ctx_row, lw, router_w_p, router_b_p)
    return _final_norm(x_lat, final_norm_g)


def kernel(x, c, ctx, c_ctx, w_ada, b_ada, norm1_g, norm2_g, w_in, q_norm_g, w_q_up, kv_norm_g, w_kv_up, na_rpb, w_gate, b_gate, w_br_mla, w_br_na, w_br_ft, w_out, router_w, router_bias, w_e_gate, w_e_up, w_e_down, final_norm_g):
    return _forward(x, c, ctx, c_ctx, w_ada, b_ada, norm1_g, norm2_g, w_in, q_norm_g, w_q_up, kv_norm_g,
                    w_kv_up, na_rpb, w_gate, b_gate, w_br_mla, w_br_na, w_br_ft, w_out, router_w, router_bias,
                    w_e_gate, w_e_up, w_e_down, final_norm_g)
```

```python
import functools
import math

import jax
import jax.numpy as jnp
import numpy as np
from jax import lax
from jax.experimental import pallas as pl
from jax.experimental.pallas import tpu as pltpu

F32 = jnp.float32
BF16 = jnp.bfloat16

GRID_W = 64
MLA_HEADS = 8
MLA_NOPE = 64
MLA_ROPE = 32
MLA_V = 64
MLA_Q_LORA = 256
MLA_KV_LORA = 128
MLA_SCALE = (MLA_NOPE + MLA_ROPE) ** -0.5
NA_HEADS = 4
NA_HEAD_DIM = 64
NA_WIN_R = 8
NA_WIN_C = 16
NA_SCALE = NA_HEAD_DIM ** -0.5
NA_WIDTH = NA_HEADS * NA_HEAD_DIM
FT_GROUPS = 4
FT_GROUP_DIM = 64
FT_WIDTH = FT_GROUPS * FT_GROUP_DIM
N_EXPERTS = 16
N_GROUPS = 4
EXPERTS_PER_GROUP = N_EXPERTS // N_GROUPS
ROPE_THETA = 10000.0
NORM_EPS = 1e-6
MASK_VALUE = -1e30
LOG2E = math.log2(math.e)

LANES = 128
V7X_VMEM_LIMIT_BYTES = 56 * 1024 * 1024

HEAD_PAD = LANES
NA_ROWS_PER_BLOCK = 8
NA_KEY_ROWS = 16
NA_BIAS_SHIFT = (NA_KEY_ROWS - NA_ROWS_PER_BLOCK) + (NA_ROWS_PER_BLOCK - 1) - (NA_WIN_R - 1)
NA_BIAS_TILES = (NA_WIN_R - 1) + NA_BIAS_SHIFT + (NA_KEY_ROWS - 2) + 1
FT_K1_PER_STEP = 8
EXPERTS_PER_STEP = 4
MLA_MAX_KV_CHUNK = 2816


def _cparams(n_axes):
    return pltpu.CompilerParams(
        dimension_semantics=("arbitrary",) * n_axes,
        vmem_limit_bytes=V7X_VMEM_LIMIT_BYTES,
    )


def _rms(x, g):
    return x * lax.rsqrt(jnp.mean(x * x, axis=-1, keepdims=True) + NORM_EPS) * g


def _dot(a, b):
    return jnp.dot(a, b, preferred_element_type=F32)


def _dot_nt(a, b):
    return lax.dot_general(a, b, (((1,), (1,)), ((), ())), preferred_element_type=F32)


def _mod_kernel(c_ref, w_ref, b_ref, o_ref):
    c = c_ref[...]
    o_ref[...] = _dot(c * jax.nn.sigmoid(c), w_ref[...]) + b_ref[...]


def _modulation(c_rows, w_ada, b_ada):
    n_layers, d, width = w_ada.shape
    rows = c_rows.shape[0]
    tn = 1536
    return pl.pallas_call(
        _mod_kernel,
        grid=(n_layers, width // tn),
        in_specs=[
            pl.BlockSpec((rows, d), lambda l, j: (0, 0)),
            pl.BlockSpec((None, d, tn), lambda l, j: (l, 0, j)),
            pl.BlockSpec((None, 1, tn), lambda l, j: (l, 0, j)),
        ],
        out_specs=pl.BlockSpec((None, rows, tn), lambda l, j: (l, 0, j)),
        out_shape=jax.ShapeDtypeStruct((n_layers, rows, width), F32),
        compiler_params=_cparams(2),
        name="modulation",
    )(c_rows, w_ada, b_ada.reshape(n_layers, 1, width))


def _proj_kernel(x_ref, mod_ref, n1g_ref, win_ref, qg_ref, wqm_ref, wqr_ref, kvg_ref, wkk_ref, wkv_ref,
                 cos_ref, sin_ref, q_ref, k_ref, v_ref, nq_ref, nk_ref, nv_ref, ft_ref):
    d = x_ref.shape[-1]
    x = x_ref[...]
    mod = mod_ref[...]
    sh1, sc1 = mod[:, 0:d], mod[:, d:2 * d]
    h = _rms(x, n1g_ref[...]) * (1.0 + sc1) + sh1
    p = _dot(h.astype(BF16), win_ref[...])
    cos = cos_ref[...]
    sin = sin_ref[...]
    o = 0
    q_lat = p[:, o:o + MLA_Q_LORA]
    o += MLA_Q_LORA
    kv_lat = p[:, o:o + MLA_KV_LORA]
    o += MLA_KV_LORA
    kr_a = p[:, o:o + HEAD_PAD]
    o += HEAD_PAD
    kr_b = p[:, o:o + HEAD_PAD]
    o += HEAD_PAD
    nq_ref[...] = p[:, o:o + NA_WIDTH].astype(BF16)
    o += NA_WIDTH
    nk_ref[...] = p[:, o:o + NA_WIDTH].astype(BF16)
    o += NA_WIDTH
    nv_ref[...] = p[:, o:o + NA_WIDTH].astype(BF16)
    o += NA_WIDTH
    ft_ref[...] = p[:, o:o + FT_WIDTH].astype(BF16)

    qn = _rms(q_lat, qg_ref[...]).astype(BF16)
    qm = _dot(qn, wqm_ref[...])
    qr = _dot(qn, wqr_ref[...])
    kvn = _rms(kv_lat, kvg_ref[...]).astype(BF16)
    kn = _dot(kvn, wkk_ref[...])
    v = _dot(kvn, wkv_ref[...])
    v_lane = lax.broadcasted_iota(jnp.int32, v.shape, 1) & (HEAD_PAD - 1)
    v_ref[...] = jnp.where(v_lane == MLA_V, 1.0, v).astype(BF16)
    kr = kr_a * cos + kr_b * sin
    q_scale = MLA_SCALE * LOG2E
    for hd in range(MLA_HEADS):
        sl = slice(hd * HEAD_PAD, (hd + 1) * HEAD_PAD)
        q_ref[:, sl] = ((qm[:, sl] * cos + qr[:, sl] * sin) * q_scale).astype(BF16)
        k_ref[:, sl] = (kn[:, sl] + kr).astype(BF16)


def _layer_spec(a, layer):
    return pl.BlockSpec((None,) + a.shape[1:], lambda *_: (layer,) + (0,) * (a.ndim - 1))


def _mod_spec(mod, layer, mod_row):
    return pl.BlockSpec((None, None, 1, mod.shape[-1]), lambda b, *_: (layer, mod_row(b), 0, 0))


def _project(x, mod, mod_row, lw, layer, cos_t, sin_t):
    nb, n, d = x.shape
    tm = min(512, n)
    win = lw["w_in"]
    wide = MLA_HEADS * HEAD_PAD

    def full(a):
        return _layer_spec(a, layer)

    def tok(width):
        return pl.BlockSpec((None, tm, width), lambda b, i: (b, i, 0))

    outs = [wide, wide, wide, NA_WIDTH, NA_WIDTH, NA_WIDTH, FT_WIDTH]
    return pl.pallas_call(
        _proj_kernel,
        grid=(nb, n // tm),
        in_specs=[
            tok(d),
            _mod_spec(mod, layer, mod_row),
            full(lw["norm1_g"]), full(win), full(lw["q_norm_g"]), full(lw["wq_main"]), full(lw["wq_rot"]),
            full(lw["kv_norm_g"]), full(lw["wkv_k"]), full(lw["wkv_v"]),
            pl.BlockSpec((tm, HEAD_PAD), lambda b, i: (i, 0)),
            pl.BlockSpec((tm, HEAD_PAD), lambda b, i: (i, 0)),
        ],
        out_specs=[tok(w) for w in outs],
        out_shape=[jax.ShapeDtypeStruct((nb, n, w), BF16) for w in outs],
        compiler_params=_cparams(2),
        name="project",
    )(x, mod, lw["norm1_g"], win, lw["q_norm_g"], lw["wq_main"], lw["wq_rot"], lw["kv_norm_g"],
      lw["wkv_k"], lw["wkv_v"], cos_t, sin_t)


def _kv_chunk(p):
    best = LANES
    for n in range(1, p // LANES + 1):
        tk = p // n
        if p % n == 0 and tk % LANES == 0 and tk <= MLA_MAX_KV_CHUNK:
            best = max(best, tk)
    return best


def _mla_kernel(q_ref, k_ref, v_ref, o_ref, s0_scr, s1_scr, p0_scr, p1_scr, mx_scr, m_scr, alpha_scr, acc_scr,
                *, tq, tk, n_q, n_c):
    def head(hd):
        return slice(hd * HEAD_PAD, (hd + 1) * HEAD_PAD)

    def q_rows(qt):
        return pl.ds(pl.multiple_of(qt * tq, tq), tq)

    def k_rows(c):
        return pl.ds(pl.multiple_of(c * tk, tk), tk)

    def scores(hd, qt, c, s_scr):
        s = _dot_nt(q_ref[q_rows(qt), head(hd)], k_ref[k_rows(c), head(hd)])
        s_scr[...] = s
        mx_scr[hd] = jnp.broadcast_to(jnp.max(s, axis=1, keepdims=True), mx_scr.shape[1:])

    def softmax(hd, s_scr, p_scr, first):
        m_prev = jnp.where(first, MASK_VALUE, m_scr[hd])
        m_new = jnp.maximum(m_prev, mx_scr[hd])
        alpha_scr[hd] = jnp.exp2(m_prev - m_new)
        m_scr[hd] = m_new
        p_scr[...] = jnp.exp2(s_scr[...] - jnp.tile(m_new, (1, tk // LANES))).astype(BF16)

    def accumulate(hd, p_scr, qt, c):
        acc = alpha_scr[hd] * acc_scr[hd] + _dot(p_scr[...], v_ref[k_rows(c), head(hd)])
        acc_scr[hd] = acc
        o_ref[q_rows(qt), head(hd)] = (acc / acc[:, MLA_V:MLA_V + 1]).astype(BF16)

    lane = lax.broadcasted_iota(jnp.int32, acc_scr.shape, 2)
    acc_scr[...] = jnp.where(lane == MLA_V, 1.0, 0.0)
    m_scr[...] = jnp.full(m_scr.shape, MASK_VALUE, F32)
    alpha_scr[...] = jnp.ones(alpha_scr.shape, F32)
    p1_scr[...] = jnp.zeros(p1_scr.shape, BF16)
    scores(0, 0, 0, s0_scr)

    def body(i, carry):
        qt, c = carry
        first = c == 0
        last = c == n_c - 1
        c_prev = jnp.where(first, n_c - 1, c - 1)
        qt_prev = jnp.maximum(jnp.where(first, qt - 1, qt), 0)
        c_next = jnp.where(last, 0, c + 1)
        qt_next = jnp.where(last, qt + 1, qt)
        scores(1, qt, c, s1_scr)
        softmax(0, s0_scr, p0_scr, first)
        accumulate(1, p1_scr, qt_prev, c_prev)
        scores(0, jnp.minimum(qt_next, n_q - 1), c_next, s0_scr)
        softmax(1, s1_scr, p1_scr, first)
        accumulate(0, p0_scr, qt, c)
        return qt_next, c_next

    lax.fori_loop(0, n_q * n_c, body, (jnp.int32(0), jnp.int32(0)))
    accumulate(1, p1_scr, n_q - 1, n_c - 1)


def _mla_attention(q, k_all, v_all):
    nb, s, _ = q.shape
    p = k_all.shape[1]
    tq = min(512, s)
    tk = _kv_chunk(p)
    pairs = MLA_HEADS // 2
    kern = functools.partial(_mla_kernel, tq=tq, tk=tk, n_q=s // tq, n_c=p // tk)
    return pl.pallas_call(
        kern,
        grid=(nb, pairs),
        in_specs=[
            pl.BlockSpec((None, s, 2 * HEAD_PAD), lambda b, hp: (b, 0, hp)),
            pl.BlockSpec((None, p, 2 * HEAD_PAD), lambda b, hp: (b, 0, hp)),
            pl.BlockSpec((None, p, 2 * HEAD_PAD), lambda b, hp: (b, 0, hp)),
        ],
        out_specs=pl.BlockSpec((None, s, 2 * HEAD_PAD), lambda b, hp: (b, 0, hp)),
        out_shape=jax.ShapeDtypeStruct((nb, s, MLA_HEADS * HEAD_PAD), BF16),
        scratch_shapes=[
            pltpu.VMEM((tq, tk), F32),
            pltpu.VMEM((tq, tk), F32),
            pltpu.VMEM((tq, tk), BF16),
            pltpu.VMEM((tq, tk), BF16),
            pltpu.VMEM((2, tq, LANES), F32),
            pltpu.VMEM((2, tq, LANES), F32),
            pltpu.VMEM((2, tq, LANES), F32),
            pltpu.VMEM((2, tq, HEAD_PAD), F32),
        ],
        compiler_params=_cparams(2),
        name="mla_attention",
    )(q, k_all, v_all)


def _mla_ctx_kernel(q_ref, k_ref, v_ref, o_ref):
    for hd in range(2):
        sl = slice(hd * HEAD_PAD, (hd + 1) * HEAD_PAD)
        s = _dot_nt(q_ref[:, sl], k_ref[:, sl])
        p = jnp.exp2(s - jnp.max(s, axis=1, keepdims=True))
        acc = _dot(p.astype(BF16), v_ref[:, sl])
        o_ref[:, sl] = (acc / acc[:, MLA_V:MLA_V + 1]).astype(BF16)


def _mla_ctx_attention(q, k, v):
    nb, c, _ = q.shape
    pairs = MLA_HEADS // 2
    spec = pl.BlockSpec((None, c, 2 * HEAD_PAD), lambda b, hp: (b, 0, hp))
    return pl.pallas_call(
        _mla_ctx_kernel,
        grid=(nb, pairs),
        in_specs=[spec, spec, spec],
        out_specs=spec,
        out_shape=jax.ShapeDtypeStruct((nb, c, MLA_HEADS * HEAD_PAD), BF16),
        compiler_params=_cparams(2),
        name="mla_ctx_attention",
    )(q, k, v)


def _na_kernel(q_ref, k_ref, v_ref, kc_ref, vc_ref, bias_ref, rowmask_ref, o_ref, *, grid_rows):
    rb = pl.program_id(2)
    start_row = jnp.clip(rb * NA_ROWS_PER_BLOCK - NA_WIN_R // 2, 0, grid_rows - NA_KEY_ROWS)
    start = pl.multiple_of(start_row * GRID_W, (NA_WIN_R // 2) * GRID_W)
    n_keys = NA_KEY_ROWS * GRID_W
    q = q_ref[...]
    kw = k_ref[pl.ds(start, n_keys), :]
    vw = v_ref[pl.ds(start, n_keys), :]
    kc = kc_ref[...]
    vc = vc_ref[...]
    lane = lax.broadcasted_iota(jnp.int32, q.shape, 1)
    tile0 = start_row - rb * NA_ROWS_PER_BLOCK + (NA_WIN_R - 1) + NA_BIAS_SHIFT
    row_mask = rowmask_ref[...]
    outs = []
    for hd in range(2):
        in_head = (lane >= NA_HEAD_DIM) if hd else (lane < NA_HEAD_DIM)
        qh = jnp.where(in_head, q, jnp.zeros_like(q))
        bias = jnp.concatenate(
            [jnp.concatenate([bias_ref[hd, tile0 + 2 * j - qr] for j in range(NA_KEY_ROWS // 2)], axis=1)
             for qr in range(NA_ROWS_PER_BLOCK)], axis=0)
        s = _dot_nt(qh, kw) * NA_SCALE + bias + row_mask
        sc = _dot_nt(qh, kc) * NA_SCALE
        m = jnp.maximum(jnp.max(s, axis=1, keepdims=True), jnp.max(sc, axis=1, keepdims=True))
        p = jnp.exp(s - m)
        pc = jnp.exp(sc - m)
        denom = jnp.sum(p, axis=1, keepdims=True) + jnp.sum(pc, axis=1, keepdims=True)
        outs.append((_dot(p.astype(BF16), vw) + _dot(pc.astype(BF16), vc)) / denom)
    o_ref[...] = jnp.where(lane < NA_HEAD_DIM, outs[0], outs[1]).astype(BF16)


def _na_attention(q, k, v, kc, vc, bias, row_mask, layer):
    nb, s, _ = q.shape
    c = kc.shape[1]
    grid_rows = s // GRID_W
    tq = NA_ROWS_PER_BLOCK * GRID_W
    n_blocks = grid_rows // NA_ROWS_PER_BLOCK
    n_keys = NA_KEY_ROWS * GRID_W
    pairs = NA_HEADS // 2

    def variant(rb):
        return jnp.where(rb == 0, 0, jnp.where(rb == n_blocks - 1, 2, 1))

    kern = functools.partial(_na_kernel, grid_rows=grid_rows)
    return pl.pallas_call(
        kern,
        grid=(nb, pairs, n_blocks),
        in_specs=[
            pl.BlockSpec((None, tq, LANES), lambda b, hp, rb: (b, rb, hp)),
            pl.BlockSpec((None, s, LANES), lambda b, hp, rb: (b, 0, hp)),
            pl.BlockSpec((None, s, LANES), lambda b, hp, rb: (b, 0, hp)),
            pl.BlockSpec((None, c, LANES), lambda b, hp, rb: (b, 0, hp)),
            pl.BlockSpec((None, c, LANES), lambda b, hp, rb: (b, 0, hp)),
            pl.BlockSpec((None, 2) + bias.shape[2:], lambda b, hp, rb: (layer, hp, 0, 0, 0)),
            pl.BlockSpec((None, tq, n_keys), lambda b, hp, rb: (variant(rb), 0, 0)),
        ],
        out_specs=pl.BlockSpec((None, tq, LANES), lambda b, hp, rb: (b, rb, hp)),
        out_shape=jax.ShapeDtypeStruct((nb, s, NA_WIDTH), BF16),
        compiler_params=_cparams(3),
        name="na_attention",
    )(q, k, v, kc, vc, bias, row_mask)


def _na_ctx_kernel(q_ref, k_ref, v_ref, o_ref):
    q = q_ref[...]
    k = k_ref[...]
    v = v_ref[...]
    lane = lax.broadcasted_iota(jnp.int32, q.shape, 1)
    outs = []
    for hd in range(2):
        in_head = (lane >= NA_HEAD_DIM) if hd else (lane < NA_HEAD_DIM)
        qh = jnp.where(in_head, q, jnp.zeros_like(q))
        s = _dot_nt(qh, k) * NA_SCALE
        p = jnp.exp(s - jnp.max(s, axis=1, keepdims=True))
        outs.append(_dot(p.astype(BF16), v) / jnp.sum(p, axis=1, keepdims=True))
    o_ref[...] = jnp.where(lane < NA_HEAD_DIM, outs[0], outs[1]).astype(BF16)


def _na_ctx_attention(q, k, v):
    nb, c, _ = q.shape
    spec = pl.BlockSpec((None, c, LANES), lambda b, hp: (b, 0, hp))
    return pl.pallas_call(
        _na_ctx_kernel,
        grid=(nb, NA_HEADS // 2),
        in_specs=[spec, spec, spec],
        out_specs=spec,
        out_shape=jax.ShapeDtypeStruct((nb, c, NA_WIDTH), BF16),
        compiler_params=_cparams(2),
        name="na_ctx_attention",
    )(q, k, v)


def _na_row_masks(grid_rows):
    qr = np.arange(NA_ROWS_PER_BLOCK)[:, None]
    kr = np.arange(NA_KEY_ROWS)[None, :]
    row_valid = np.zeros((3, NA_ROWS_PER_BLOCK, NA_KEY_ROWS), bool)
    n_blocks = grid_rows // NA_ROWS_PER_BLOCK
    for vi, rb in enumerate((0, 1, n_blocks - 1)):
        r = rb * NA_ROWS_PER_BLOCK + qr
        start_row = min(max(rb * NA_ROWS_PER_BLOCK - NA_WIN_R // 2, 0), grid_rows - NA_KEY_ROWS)
        key_row = start_row + kr
        win = np.clip(r - NA_WIN_R // 2, 0, grid_rows - NA_WIN_R)
        row_valid[vi] = (key_row >= win) & (key_row < win + NA_WIN_R)
    mask = np.where(row_valid, 0.0, MASK_VALUE).astype(np.float32)
    mask = np.broadcast_to(mask[:, :, None, :, None], (3, NA_ROWS_PER_BLOCK, GRID_W, NA_KEY_ROWS, GRID_W))
    return jnp.asarray(mask.reshape(3, NA_ROWS_PER_BLOCK * GRID_W, NA_KEY_ROWS * GRID_W))


def _na_bias_tables(rpb):
    n_dc = 2 * NA_WIN_C - 1
    qc = np.arange(GRID_W)[:, None]
    kcol = np.arange(GRID_W)[None, :]
    wstart = np.clip(qc - NA_WIN_C // 2, 0, GRID_W - NA_WIN_C)
    col_valid = (kcol >= wstart) & (kcol < wstart + NA_WIN_C)
    dc = np.clip(kcol - qc, -(NA_WIN_C - 1), NA_WIN_C - 1) + (NA_WIN_C - 1)
    col_onehot = ((dc[..., None] == np.arange(n_dc)) & col_valid[..., None]).astype(np.float32)
    t = jnp.einsum("lhij,cdj->lhicd", rpb.astype(F32), jnp.asarray(col_onehot), precision=lax.Precision.HIGHEST)
    t = jnp.where(jnp.asarray(col_valid), t, MASK_VALUE)
    n_dr = t.shape[2]
    t = jnp.pad(t, ((0, 0), (0, 0), (NA_BIAS_SHIFT, NA_BIAS_TILES + 1 - NA_BIAS_SHIFT - n_dr), (0, 0), (0, 0)),
                constant_values=MASK_VALUE)
    return jnp.concatenate([t[:, :, :NA_BIAS_TILES], t[:, :, 1:NA_BIAS_TILES + 1]], axis=-1)


FT_N1 = 64


def _ft_stage1_kernel(w_ref, x_ref, o_ref):
    o_ref[...] = _dot(w_ref[...], x_ref[...])


def _ft_stage2_kernel(are_ref, aim_ref, tc_ref, ts_ref, w_ref, cd_ref, sd_ref, o_ref, *, scale):
    n2 = are_ref.shape[1]
    for g in range(are_ref.shape[0]):
        are = are_ref[g]
        aim = aim_ref[g]
        tc = jnp.concatenate([tc_ref[g]] * (FT_WIDTH // LANES), axis=1)
        ts = jnp.concatenate([ts_ref[g]] * (FT_WIDTH // LANES), axis=1)
        bre = are * tc + aim * ts
        bim = aim * tc - are * ts
        bb = jnp.concatenate([bre, bim], axis=0).astype(BF16)
        xk = _dot(w_ref[...], bb)
        f = _dot(xk[:n2].astype(BF16), cd_ref[...]) + _dot(xk[n2:].astype(BF16), sd_ref[...])
        o_ref[:, g * FT_WIDTH:(g + 1) * FT_WIDTH] = (f * scale).astype(BF16)


def _dft_mats(n):
    ang = 2.0 * np.pi * np.outer(np.arange(n), np.arange(n)) / n
    return np.cos(ang), np.sin(ang)


def _channel_dft():
    c, s = _dft_mats(FT_GROUP_DIM)
    eye = np.eye(FT_GROUPS)
    return jnp.asarray(np.kron(eye, c), BF16), jnp.asarray(np.kron(eye, s), BF16)


def _fourier_lat(u):
    nb, n, _ = u.shape
    n2 = n // FT_N1
    cols = n2 * FT_WIDTH
    c1, s1 = _dft_mats(FT_N1)
    w1 = jnp.asarray(np.concatenate([c1, -s1], axis=0), BF16)
    tcol = min(2048, cols)
    a = pl.pallas_call(
        _ft_stage1_kernel,
        grid=(nb, cols // tcol),
        in_specs=[
            pl.BlockSpec(w1.shape, lambda b, j: (0, 0)),
            pl.BlockSpec((None, FT_N1, tcol), lambda b, j: (b, 0, j)),
        ],
        out_specs=pl.BlockSpec((None, 2 * FT_N1, tcol), lambda b, j: (b, 0, j)),
        out_shape=jax.ShapeDtypeStruct((nb, 2 * FT_N1, cols), F32),
        compiler_params=_cparams(2),
        name="fourier_stage1",
    )(w1, u.reshape(nb, FT_N1, cols))
    a = a.reshape(nb, 2 * FT_N1, n2, FT_WIDTH)

    ang = 2.0 * np.pi * np.outer(np.arange(FT_N1), np.arange(n2)) / n
    tc = jnp.asarray(np.repeat(np.cos(ang)[:, :, None], LANES, axis=2), F32)
    ts = jnp.asarray(np.repeat(np.sin(ang)[:, :, None], LANES, axis=2), F32)
    c3, s3 = _dft_mats(n2)
    w3 = jnp.asarray(np.block([[c3, s3], [-s3, c3]]), BF16)
    cd, sd = _channel_dft()
    kern = functools.partial(_ft_stage2_kernel, scale=1.0 / math.sqrt(n * FT_GROUP_DIM))
    g = FT_K1_PER_STEP
    n_steps = FT_N1 // g
    out = pl.pallas_call(
        kern,
        grid=(nb, n_steps),
        in_specs=[
            pl.BlockSpec((None, g, n2, FT_WIDTH), lambda b, k: (b, k, 0, 0)),
            pl.BlockSpec((None, g, n2, FT_WIDTH), lambda b, k: (b, n_steps + k, 0, 0)),
            pl.BlockSpec((g, n2, LANES), lambda b, k: (k, 0, 0)),
            pl.BlockSpec((g, n2, LANES), lambda b, k: (k, 0, 0)),
            pl.BlockSpec(w3.shape, lambda b, k: (0, 0)),
            pl.BlockSpec(cd.shape, lambda b, k: (0, 0)),
            pl.BlockSpec(sd.shape, lambda b, k: (0, 0)),
        ],
        out_specs=pl.BlockSpec((None, n2, g * FT_WIDTH), lambda b, k: (b, 0, k)),
        out_shape=jax.ShapeDtypeStruct((nb, n2, FT_N1 * FT_WIDTH), BF16),
        compiler_params=_cparams(2),
        name="fourier_stage2",
    )(a, a, tc, ts, w3, cd, sd)
    return out.reshape(nb, n, FT_WIDTH)


def _ft_ctx_kernel(u_ref, cn_ref, sn_ref, cd_ref, sd_ref, o_ref, *, scale):
    u = u_ref[...]
    y1 = _dot(u, cd_ref[...]).astype(BF16)
    y2 = _dot(u, sd_ref[...]).astype(BF16)
    f = _dot(cn_ref[...], y1) - _dot(sn_ref[...], y2)
    o_ref[...] = (f * scale).astype(BF16)


def _fourier_ctx(u):
    nb, n, _ = u.shape
    cn, sn = _dft_mats(n)
    cn, sn = jnp.asarray(cn, BF16), jnp.asarray(sn, BF16)
    cd, sd = _channel_dft()
    kern = functools.partial(_ft_ctx_kernel, scale=1.0 / math.sqrt(n * FT_GROUP_DIM))
    sq = pl.BlockSpec((n, n), lambda b: (0, 0))
    ch = pl.BlockSpec(cd.shape, lambda b: (0, 0))
    tok = pl.BlockSpec((None, n, FT_WIDTH), lambda b: (b, 0, 0))
    return pl.pallas_call(
        kern,
        grid=(nb,),
        in_specs=[tok, sq, sq, ch, ch],
        out_specs=tok,
        out_shape=jax.ShapeDtypeStruct((nb, n, FT_WIDTH), BF16),
        compiler_params=_cparams(1),
        name="fourier_ctx",
    )(u, cn, sn, cd, sd)


def _merge_kernel(x_ref, oa_ref, ob_ref, oc_ref, mod_ref, n1g_ref, n2g_ref, wgate_ref, bgate_ref,
                  wba_ref, wbb_ref, wbc_ref, wout_ref, x1_ref, h2_ref):
    d = x_ref.shape[-1]
    x = x_ref[...]
    mod = mod_ref[...]
    sh1, sc1, g1, sh2, sc2 = (mod[:, i * d:(i + 1) * d] for i in range(5))
    h = _rms(x, n1g_ref[...]) * (1.0 + sc1) + sh1
    gate = jax.nn.sigmoid(_dot(h.astype(BF16), wgate_ref[...]) + bgate_ref[...])
    m = gate[:, 0:d] * _dot(oa_ref[...], wba_ref[...])
    m += gate[:, d:2 * d] * _dot(ob_ref[...], wbb_ref[...])
    m += gate[:, 2 * d:3 * d] * _dot(oc_ref[...], wbc_ref[...])
    y = _dot(m.astype(BF16), wout_ref[...])
    x1 = x + g1 * y
    x1_ref[...] = x1
    h2_ref[...] = (_rms(x1, n2g_ref[...]) * (1.0 + sc2) + sh2).astype(BF16)


def _merge(x, o_a, o_b, o_c, mod, mod_row, lw, layer):
    nb, n, d = x.shape
    tm = min(256, n)

    def full(a):
        return _layer_spec(a, layer)

    def tok(width):
        return pl.BlockSpec((None, tm, width), lambda b, i: (b, i, 0))

    weights = [lw["norm1_g"], lw["norm2_g"], lw["w_gate"], lw["b_gate"], lw["w_br_mla"], lw["w_br_na"],
               lw["w_br_ft"], lw["w_out"]]
    return pl.pallas_call(
        _merge_kernel,
        grid=(nb, n // tm),
        in_specs=[tok(d), tok(o_a.shape[-1]), tok(o_b.shape[-1]), tok(o_c.shape[-1]), _mod_spec(mod, layer, mod_row)]
        + [full(w) for w in weights],
        out_specs=[tok(d), tok(d)],
        out_shape=[jax.ShapeDtypeStruct((nb, n, d), F32), jax.ShapeDtypeStruct((nb, n, d), BF16)],
        compiler_params=_cparams(2),
        name="merge",
    )(x, o_a, o_b, o_c, mod, *weights)


def _route(logits, router_bias):
    lane = lax.broadcasted_iota(jnp.int32, logits.shape, 1)
    neg = -jnp.inf
    big = jnp.int32(LANES)
    scores = jax.nn.sigmoid(logits)
    sel = scores + router_bias

    def top2(vals):
        m1 = jnp.max(vals, axis=1, keepdims=True)
        i1 = jnp.min(jnp.where(vals == m1, lane, big), axis=1, keepdims=True)
        rest = jnp.where(lane == i1, neg, vals)
        m2 = jnp.max(rest, axis=1, keepdims=True)
        i2 = jnp.min(jnp.where(rest == m2, lane, big), axis=1, keepdims=True)
        return m1, i1, m2, i2

    grp_scores = []
    for g in range(N_GROUPS):
        in_grp = (lane >= g * EXPERTS_PER_GROUP) & (lane < (g + 1) * EXPERTS_PER_GROUP)
        m1, _, m2, _ = top2(jnp.where(in_grp, sel, neg))
        grp_scores.append(m1 + m2)
    best = functools.reduce(jnp.maximum, grp_scores)
    grp = jnp.full(best.shape, N_GROUPS - 1, jnp.int32)
    for g in range(N_GROUPS - 2, -1, -1):
        grp = jnp.where(grp_scores[g] == best, g, grp)
    in_sel = (lane >= grp * EXPERTS_PER_GROUP) & (lane < (grp + 1) * EXPERTS_PER_GROUP)
    _, i1, _, i2 = top2(jnp.where(in_sel, sel, neg))
    hit1 = lane == i1
    hit2 = lane == i2
    w1 = jnp.sum(jnp.where(hit1, scores, 0.0), axis=1, keepdims=True)
    w2 = jnp.sum(jnp.where(hit2, scores, 0.0), axis=1, keepdims=True)
    total = w1 + w2
    return jnp.where(hit1, w1 / total, 0.0) + jnp.where(hit2, w2 / total, 0.0)


def _moe_kernel(h_ref, x_ref, mod_ref, rw_ref, rb_ref, fg_ref, wg_ref, wu_ref, wd_ref, o_ref, gate_scr, acc_scr,
                *, apply_final_norm):
    d = x_ref.shape[-1]
    step = pl.program_id(2)
    n_steps = pl.num_programs(2)
    h = h_ref[...]

    @pl.when(step == 0)
    def _():
        gate = _route(_dot(h, rw_ref[...]), rb_ref[...])
        for c in range(N_EXPERTS // EXPERTS_PER_STEP):
            shift = (LANES - c * EXPERTS_PER_STEP) % LANES
            gate_scr[c] = pltpu.roll(gate, shift, axis=1) if shift else gate
        acc_scr[...] = jnp.zeros(acc_scr.shape, F32)

    gate = gate_scr[step]
    acc = acc_scr[...]
    for j in range(EXPERTS_PER_STEP):
        a = _dot(h, wg_ref[j])
        u = _dot(h, wu_ref[j])
        hid = (a * jax.nn.sigmoid(a)) * u * gate[:, j:j + 1]
        acc += _dot(hid.astype(BF16), wd_ref[j])
    acc_scr[...] = acc

    @pl.when(step == n_steps - 1)
    def _():
        g2 = mod_ref[:, 5 * d:6 * d]
        x2 = x_ref[...] + g2 * acc
        o_ref[...] = _rms(x2, fg_ref[...]) if apply_final_norm else x2


def _moe(h2, x1, mod, mod_row, lw, layer, router_w, router_b, final_g, apply_final_norm):
    nb, n, d = x1.shape
    tm = min(1024, n)
    n_steps = N_EXPERTS // EXPERTS_PER_STEP
    ff = lw["w_e_gate"].shape[-1]
    tok = pl.BlockSpec((None, tm, d), lambda b, i, c: (b, i, 0))
    kern = functools.partial(_moe_kernel, apply_final_norm=apply_final_norm)
    return pl.pallas_call(
        kern,
        grid=(nb, n // tm, n_steps),
        in_specs=[
            tok, tok,
            _mod_spec(mod, layer, mod_row),
            pl.BlockSpec(router_w.shape, lambda b, i, c: (0, 0)),
            pl.BlockSpec(router_b.shape, lambda b, i, c: (0, 0)),
            pl.BlockSpec(final_g.shape, lambda b, i, c: (0, 0)),
            pl.BlockSpec((None, EXPERTS_PER_STEP, d, ff), lambda b, i, c: (layer, c, 0, 0)),
            pl.BlockSpec((None, EXPERTS_PER_STEP, d, ff), lambda b, i, c: (layer, c, 0, 0)),
            pl.BlockSpec((None, EXPERTS_PER_STEP, ff, d), lambda b, i, c: (layer, c, 0, 0)),
        ],
        out_specs=tok,
        out_shape=jax.ShapeDtypeStruct((nb, n, d), F32),
        scratch_shapes=[pltpu.VMEM((n_steps, tm, LANES), F32), pltpu.VMEM((tm, d), F32)],
        compiler_params=_cparams(3),
        name="moe",
    )(h2, x1, mod, router_w, router_b, final_g, lw["w_e_gate"], lw["w_e_up"], lw["w_e_down"])


def _rotate_cols(w):
    half = w.shape[-1] // 2
    return jnp.concatenate([-w[..., half:], w[..., :half]], axis=-1)


def _prepare_weights(w_in, q_norm_g, w_q_up, kv_norm_g, w_kv_up, norm1_g, norm2_g, w_gate, b_gate, w_br_mla,
                     w_br_na, w_br_ft, w_out, w_e_gate, w_e_up, w_e_down):
    n_layers, d, _ = w_in.shape
    splits = np.cumsum([MLA_Q_LORA, MLA_KV_LORA, MLA_ROPE, NA_WIDTH, NA_WIDTH, NA_WIDTH])
    w_q, w_kv, w_kr, w_nq, w_nk, w_nv, w_ft = jnp.split(w_in, [int(s) for s in splits], axis=-1)

    def in_rope_slot(w):
        zeros_lo = jnp.zeros(w.shape[:-1] + (MLA_NOPE,), w.dtype)
        zeros_hi = jnp.zeros(w.shape[:-1] + (HEAD_PAD - MLA_NOPE - MLA_ROPE,), w.dtype)
        return jnp.concatenate([zeros_lo, w, zeros_hi], axis=-1)

    win = jnp.concatenate([w_q, w_kv, in_rope_slot(w_kr), in_rope_slot(_rotate_cols(w_kr)), w_nq, w_nk, w_nv, w_ft],
                          axis=-1).astype(BF16)
    wq = w_q_up.reshape(n_layers, MLA_Q_LORA, MLA_HEADS, MLA_NOPE + MLA_ROPE)
    wq_nope, wq_pe = wq[..., :MLA_NOPE], wq[..., MLA_NOPE:]
    pad = jnp.zeros(wq_pe.shape[:-1] + (HEAD_PAD - MLA_NOPE - MLA_ROPE,), wq.dtype)
    wq_main = jnp.concatenate([wq_nope, wq_pe, pad], axis=-1)
    wq_rot = jnp.concatenate([jnp.zeros_like(wq_nope), _rotate_cols(wq_pe), pad], axis=-1)
    wkv = w_kv_up.reshape(n_layers, MLA_KV_LORA, MLA_HEADS, MLA_NOPE + MLA_V)
    wk_nope, wv = wkv[..., :MLA_NOPE], wkv[..., MLA_NOPE:]
    wkv_k = jnp.concatenate([wk_nope, jnp.zeros(wk_nope.shape[:-1] + (HEAD_PAD - MLA_NOPE,), wkv.dtype)], axis=-1)
    wide = MLA_HEADS * HEAD_PAD
    return {
        "w_in": win,
        "wq_main": wq_main.reshape(n_layers, MLA_Q_LORA, wide).astype(BF16),
        "wq_rot": wq_rot.reshape(n_layers, MLA_Q_LORA, wide).astype(BF16),
        "wkv_k": wkv_k.reshape(n_layers, MLA_KV_LORA, wide).astype(BF16),
        "wkv_v": jnp.concatenate([wv, jnp.zeros(wv.shape[:-1] + (HEAD_PAD - MLA_V,), wv.dtype)], axis=-1)
        .reshape(n_layers, MLA_KV_LORA, wide).astype(BF16),
        "q_norm_g": q_norm_g.reshape(n_layers, 1, -1),
        "kv_norm_g": kv_norm_g.reshape(n_layers, 1, -1),
        "norm1_g": norm1_g.reshape(n_layers, 1, d),
        "norm2_g": norm2_g.reshape(n_layers, 1, d),
        "w_gate": w_gate.astype(BF16),
        "b_gate": b_gate.reshape(n_layers, 1, -1),
        "w_br_mla": jnp.pad(w_br_mla.reshape(n_layers, MLA_HEADS, MLA_V, d), ((0, 0), (0, 0), (0, HEAD_PAD - MLA_V), (0, 0)))
        .reshape(n_layers, wide, d).astype(BF16),
        "w_br_na": w_br_na.astype(BF16),
        "w_br_ft": w_br_ft.astype(BF16),
        "w_out": w_out.astype(BF16),
        "w_e_gate": w_e_gate.astype(BF16),
        "w_e_up": w_e_up.astype(BF16),
        "w_e_down": w_e_down.astype(BF16),
    }


def _rope_tables(s, c):
    n_freq = MLA_ROPE // 4
    inv_freq = ROPE_THETA ** (-jnp.arange(n_freq, dtype=F32) / n_freq)
    t = jnp.arange(s, dtype=jnp.int32)
    row = (t // GRID_W).astype(F32)
    col = (t % GRID_W).astype(F32)
    ang = jnp.concatenate([row[:, None] * inv_freq, col[:, None] * inv_freq], axis=-1)
    cos, sin = jnp.cos(ang), jnp.sin(ang)
    pad = HEAD_PAD - MLA_NOPE - MLA_ROPE
    cos_lat = jnp.concatenate([jnp.ones((s, MLA_NOPE), F32), cos, cos, jnp.zeros((s, pad), F32)], axis=-1)
    sin_lat = jnp.concatenate([jnp.zeros((s, MLA_NOPE), F32), sin, sin, jnp.zeros((s, pad), F32)], axis=-1)
    cos_ctx = jnp.concatenate([jnp.ones((c, MLA_NOPE + MLA_ROPE), F32), jnp.zeros((c, pad), F32)], axis=-1)
    sin_ctx = jnp.zeros((c, HEAD_PAD), F32)
    return cos_lat, sin_lat, cos_ctx, sin_ctx


@jax.jit
def _forward(x, c, ctx, c_ctx, w_ada, b_ada, norm1_g, norm2_g, w_in, q_norm_g, w_q_up, kv_norm_g, w_kv_up,
             na_rpb, w_gate, b_gate, w_br_mla, w_br_na, w_br_ft, w_out, router_w, router_bias, w_e_gate,
             w_e_up, w_e_down, final_norm_g):
    nb, s, d = x.shape
    n_ctx = ctx.shape[1]
    n_layers = w_ada.shape[0]
    grid_rows = s // GRID_W

    mod_rows = 8
    c_rows = jnp.zeros((mod_rows, d), F32).at[:nb].set(c).at[nb].set(c_ctx)
    mod_all = _modulation(c_rows, w_ada, b_ada).reshape(n_layers, mod_rows, 1, 6 * d)

    weights = _prepare_weights(w_in, q_norm_g, w_q_up, kv_norm_g, w_kv_up, norm1_g, norm2_g, w_gate, b_gate,
                               w_br_mla, w_br_na, w_br_ft, w_out, w_e_gate, w_e_up, w_e_down)
    na_bias = _na_bias_tables(na_rpb)
    na_row_mask = _na_row_masks(grid_rows)
    cos_lat, sin_lat, cos_ctx, sin_ctx = _rope_tables(s, n_ctx)
    router_w_p = jnp.zeros((d, LANES), F32).at[:, :N_EXPERTS].set(router_w).astype(BF16)
    router_b_p = jnp.zeros((1, LANES), F32).at[0, :N_EXPERTS].set(router_bias)
    final_g = final_norm_g.reshape(1, d)

    def lat_row(b):
        return b

    def ctx_row(b):
        return nb

    x_lat, x_ctx = x, ctx
    for l in range(n_layers):
        last = l == n_layers - 1
        q_l, k_l, v_l, nq_l, nk_l, nv_l, ft_l = _project(x_lat, mod_all, lat_row, weights, l, cos_lat, sin_lat)
        q_c, k_c, v_c, nq_c, nk_c, nv_c, ft_c = _project(x_ctx, mod_all, ctx_row, weights, l, cos_ctx, sin_ctx)
        o_a = _mla_attention(q_l, jnp.concatenate([k_l, k_c], axis=1), jnp.concatenate([v_l, v_c], axis=1))
        o_b = _na_attention(nq_l, nk_l, nv_l, nk_c, nv_c, na_bias, na_row_mask, l)
        o_c = _fourier_lat(ft_l)
        x1, h2 = _merge(x_lat, o_a, o_b, o_c, mod_all, lat_row, weights, l)
        x_lat = _moe(h2, x1, mod_all, lat_row, weights, l, router_w_p, router_b_p, final_g, last)
        if not last:
            oc_a = _mla_ctx_attention(q_c, k_c, v_c)
            oc_b = _na_ctx_attention(nq_c, nk_c, nv_c)
            oc_c = _fourier_ctx(ft_c)
            x1c, h2c = _merge(x_ctx, oc_a, oc_b, oc_c, mod_all, ctx_row, weights, l)
            x_ctx = _moe(h2c, x1c, mod_all, ctx_row, weights, l, router_w_p, router_b_p, final_g, False)
    return x_lat


def kernel(x, c, ctx, c_ctx, w_ada, b_ada, norm1_g, norm2_g, w_in, q_norm_g, w_q_up, kv_norm_g, w_kv_up, na_rpb, w_gate, b_gate, w_br_mla, w_br_na, w_br_ft, w_out, router_w, router_bias, w_e_gate, w_e_up, w_e_down, final_norm_g):
    return _forward(x, c, ctx, c_ctx, w_ada, b_ada, norm1_g, norm2_g, w_in, q_norm_g, w_q_up, kv_norm_g,
                    w_kv_up, na_rpb, w_gate, b_gate, w_br_mla, w_br_na, w_br_ft, w_out, router_w, router_bias,
                    w_e_gate, w_e_up, w_e_down, final_norm_g)
```

```python
import functools
import math

import jax
import jax.numpy as jnp
import numpy as np
from jax import lax
from jax.experimental import pallas as pl
from jax.experimental.pallas import tpu as pltpu

F32 = jnp.float32
BF16 = jnp.bfloat16

GRID_W = 64
MLA_HEADS = 8
MLA_NOPE = 64
MLA_ROPE = 32
MLA_V = 64
MLA_Q_LORA = 256
MLA_KV_LORA = 128
MLA_SCALE = (MLA_NOPE + MLA_ROPE) ** -0.5
NA_HEADS = 4
NA_HEAD_DIM = 64
NA_WIN_R = 8
NA_WIN_C = 16
NA_SCALE = NA_HEAD_DIM ** -0.5
NA_WIDTH = NA_HEADS * NA_HEAD_DIM
FT_GROUPS = 4
FT_GROUP_DIM = 64
FT_WIDTH = FT_GROUPS * FT_GROUP_DIM
N_EXPERTS = 16
N_GROUPS = 4
EXPERTS_PER_GROUP = N_EXPERTS // N_GROUPS
ROPE_THETA = 10000.0
NORM_EPS = 1e-6
MASK_VALUE = -1e30
LOG2E = math.log2(math.e)

LANES = 128
V7X_VMEM_LIMIT_BYTES = 56 * 1024 * 1024

HEAD_PAD = LANES
NA_ROWS_PER_BLOCK = 8
NA_KEY_ROWS = 16
NA_BIAS_SHIFT = (NA_KEY_ROWS - NA_ROWS_PER_BLOCK) + (NA_ROWS_PER_BLOCK - 1) - (NA_WIN_R - 1)
NA_BIAS_TILES = (NA_WIN_R - 1) + NA_BIAS_SHIFT + (NA_KEY_ROWS - 2) + 1
FT_K1_PER_STEP = 8
EXPERTS_PER_STEP = 4
MOE_LAT_TILE = 1024
TOKEN_TILE = 256
MLA_MAX_KV_CHUNK = 2816


def _cparams(n_axes):
    return pltpu.CompilerParams(
        dimension_semantics=("arbitrary",) * n_axes,
        vmem_limit_bytes=V7X_VMEM_LIMIT_BYTES,
    )


def _rms(x, g):
    return x * lax.rsqrt(jnp.mean(x * x, axis=-1, keepdims=True) + NORM_EPS) * g


def _dot(a, b):
    return jnp.dot(a, b, preferred_element_type=F32)


def _dot_nt(a, b):
    return lax.dot_general(a, b, (((1,), (1,)), ((), ())), preferred_element_type=F32)


def _mod_kernel(c_ref, w_ref, b_ref, o_ref):
    c = c_ref[...]
    o_ref[...] = _dot(c * jax.nn.sigmoid(c), w_ref[...]) + b_ref[...]


def _modulation(c_rows, w_ada, b_ada):
    n_layers, d, width = w_ada.shape
    rows = c_rows.shape[0]
    tn = 1536
    return pl.pallas_call(
        _mod_kernel,
        grid=(n_layers, width // tn),
        in_specs=[
            pl.BlockSpec((rows, d), lambda l, j: (0, 0)),
            pl.BlockSpec((None, d, tn), lambda l, j: (l, 0, j)),
            pl.BlockSpec((None, 1, tn), lambda l, j: (l, 0, j)),
        ],
        out_specs=pl.BlockSpec((None, rows, tn), lambda l, j: (l, 0, j)),
        out_shape=jax.ShapeDtypeStruct((n_layers, rows, width), F32),
        compiler_params=_cparams(2),
        name="modulation",
    )(c_rows, w_ada, b_ada.reshape(n_layers, 1, width))


def _proj_kernel(x_ref, mod_ref, n1g_ref, win_ref, qg_ref, wqm_ref, wqr_ref, kvg_ref, wkk_ref, wkv_ref,
                 cos_ref, sin_ref, q_ref, k_ref, v_ref, nq_ref, nk_ref, nv_ref, ft_ref):
    d = x_ref.shape[-1]
    x = x_ref[...]
    mod = mod_ref[...]
    sh1, sc1 = mod[:, 0:d], mod[:, d:2 * d]
    h = _rms(x, n1g_ref[...]) * (1.0 + sc1) + sh1
    p = _dot(h.astype(BF16), win_ref[...])
    cos = cos_ref[...]
    sin = sin_ref[...]
    o = 0
    q_lat = p[:, o:o + MLA_Q_LORA]
    o += MLA_Q_LORA
    kv_lat = p[:, o:o + MLA_KV_LORA]
    o += MLA_KV_LORA
    kr_a = p[:, o:o + HEAD_PAD]
    o += HEAD_PAD
    kr_b = p[:, o:o + HEAD_PAD]
    o += HEAD_PAD
    nq_ref[...] = (p[:, o:o + NA_WIDTH] * NA_SCALE).astype(BF16)
    o += NA_WIDTH
    nk_ref[...] = p[:, o:o + NA_WIDTH].astype(BF16)
    o += NA_WIDTH
    nv_ref[...] = p[:, o:o + NA_WIDTH].astype(BF16)
    o += NA_WIDTH
    ft_ref[...] = p[:, o:o + FT_WIDTH].astype(BF16)

    qn = _rms(q_lat, qg_ref[...]).astype(BF16)
    qm = _dot(qn, wqm_ref[...])
    qr = _dot(qn, wqr_ref[...])
    kvn = _rms(kv_lat, kvg_ref[...]).astype(BF16)
    kn = _dot(kvn, wkk_ref[...])
    v = _dot(kvn, wkv_ref[...])
    v_lane = lax.broadcasted_iota(jnp.int32, v.shape, 1) & (HEAD_PAD - 1)
    v_ref[...] = jnp.where(v_lane == MLA_V, 1.0, v).astype(BF16)
    kr = kr_a * cos + kr_b * sin
    q_scale = MLA_SCALE * LOG2E
    for hd in range(MLA_HEADS):
        sl = slice(hd * HEAD_PAD, (hd + 1) * HEAD_PAD)
        q_ref[:, sl] = ((qm[:, sl] * cos + qr[:, sl] * sin) * q_scale).astype(BF16)
        k_ref[:, sl] = (kn[:, sl] + kr).astype(BF16)


def _layer_spec(a, layer):
    return pl.BlockSpec((None,) + a.shape[1:], lambda *_: (layer,) + (0,) * (a.ndim - 1))


def _mod_spec(mod, layer, mod_row):
    return pl.BlockSpec((None, None, 1, mod.shape[-1]), lambda b, i, *_: (layer, mod_row(b, i), 0, 0))


def _project(x, mod, mod_row, lw, layer, cos_t, sin_t):
    nb, n, d = x.shape
    tm = TOKEN_TILE
    win = lw["w_in"]
    wide = MLA_HEADS * HEAD_PAD

    def full(a):
        return _layer_spec(a, layer)

    def tok(width):
        return pl.BlockSpec((None, tm, width), lambda b, i: (b, i, 0))

    outs = [wide, wide, wide, NA_WIDTH, NA_WIDTH, NA_WIDTH, FT_WIDTH]
    return pl.pallas_call(
        _proj_kernel,
        grid=(nb, n // tm),
        in_specs=[
            tok(d),
            _mod_spec(mod, layer, mod_row),
            full(lw["norm1_g"]), full(win), full(lw["q_norm_g"]), full(lw["wq_main"]), full(lw["wq_rot"]),
            full(lw["kv_norm_g"]), full(lw["wkv_k"]), full(lw["wkv_v"]),
            pl.BlockSpec((tm, HEAD_PAD), lambda b, i: (i, 0)),
            pl.BlockSpec((tm, HEAD_PAD), lambda b, i: (i, 0)),
        ],
        out_specs=[tok(w) for w in outs],
        out_shape=[jax.ShapeDtypeStruct((nb, n, w), BF16) for w in outs],
        compiler_params=_cparams(2),
        name="project",
    )(x, mod, lw["norm1_g"], win, lw["q_norm_g"], lw["wq_main"], lw["wq_rot"], lw["kv_norm_g"],
      lw["wkv_k"], lw["wkv_v"], cos_t, sin_t)


def _kv_chunk(p):
    best = LANES
    for n in range(1, p // LANES + 1):
        tk = p // n
        if p % n == 0 and tk % LANES == 0 and tk <= MLA_MAX_KV_CHUNK:
            best = max(best, tk)
    return best


def _mla_kernel(q_ref, k_ref, v_ref, o_ref, s0_scr, s1_scr, p0_scr, p1_scr, mx_scr, m_scr, alpha_scr, acc_scr,
                *, tq, tk, n_q, n_c):
    def head(hd):
        return slice(hd * HEAD_PAD, (hd + 1) * HEAD_PAD)

    def q_rows(qt):
        return pl.ds(pl.multiple_of(qt * tq, tq), tq)

    def k_rows(c):
        return pl.ds(pl.multiple_of(c * tk, tk), tk)

    def scores(hd, qt, c, s_scr):
        s = _dot_nt(q_ref[q_rows(qt), head(hd)], k_ref[k_rows(c), head(hd)])
        s_scr[...] = s
        mx_scr[hd] = jnp.broadcast_to(jnp.max(s, axis=1, keepdims=True), mx_scr.shape[1:])

    def softmax(hd, s_scr, p_scr, first):
        m_prev = jnp.where(first, MASK_VALUE, m_scr[hd])
        m_new = jnp.maximum(m_prev, mx_scr[hd])
        alpha_scr[hd] = jnp.exp2(m_prev - m_new)
        m_scr[hd] = m_new
        p_scr[...] = jnp.exp2(s_scr[...] - jnp.tile(m_new, (1, tk // LANES))).astype(BF16)

    def accumulate(hd, p_scr, qt, c):
        acc = alpha_scr[hd] * acc_scr[hd] + _dot(p_scr[...], v_ref[k_rows(c), head(hd)])
        acc_scr[hd] = acc
        o_ref[q_rows(qt), head(hd)] = (acc / acc[:, MLA_V:MLA_V + 1]).astype(BF16)

    lane = lax.broadcasted_iota(jnp.int32, acc_scr.shape, 2)
    acc_scr[...] = jnp.where(lane == MLA_V, 1.0, 0.0)
    m_scr[...] = jnp.full(m_scr.shape, MASK_VALUE, F32)
    alpha_scr[...] = jnp.ones(alpha_scr.shape, F32)
    p1_scr[...] = jnp.zeros(p1_scr.shape, BF16)
    scores(0, 0, 0, s0_scr)

    def body(i, carry):
        qt, c = carry
        first = c == 0
        last = c == n_c - 1
        c_prev = jnp.where(first, n_c - 1, c - 1)
        qt_prev = jnp.maximum(jnp.where(first, qt - 1, qt), 0)
        c_next = jnp.where(last, 0, c + 1)
        qt_next = jnp.where(last, qt + 1, qt)
        scores(1, qt, c, s1_scr)
        softmax(0, s0_scr, p0_scr, first)
        accumulate(1, p1_scr, qt_prev, c_prev)
        scores(0, jnp.minimum(qt_next, n_q - 1), c_next, s0_scr)
        softmax(1, s1_scr, p1_scr, first)
        accumulate(0, p0_scr, qt, c)
        return qt_next, c_next

    lax.fori_loop(0, n_q * n_c, body, (jnp.int32(0), jnp.int32(0)))
    accumulate(1, p1_scr, n_q - 1, n_c - 1)


def _mla_attention(q, k_all, v_all, s):
    nb, p, _ = k_all.shape
    tq = min(512, s)
    tk = _kv_chunk(p)
    pairs = MLA_HEADS // 2
    kern = functools.partial(_mla_kernel, tq=tq, tk=tk, n_q=s // tq, n_c=p // tk)
    return pl.pallas_call(
        kern,
        grid=(nb, pairs),
        in_specs=[
            pl.BlockSpec((None, s, 2 * HEAD_PAD), lambda b, hp: (b, 0, hp)),
            pl.BlockSpec((None, p, 2 * HEAD_PAD), lambda b, hp: (b, 0, hp)),
            pl.BlockSpec((None, p, 2 * HEAD_PAD), lambda b, hp: (b, 0, hp)),
        ],
        out_specs=pl.BlockSpec((None, s, 2 * HEAD_PAD), lambda b, hp: (b, 0, hp)),
        out_shape=jax.ShapeDtypeStruct((nb, s, MLA_HEADS * HEAD_PAD), BF16),
        scratch_shapes=[
            pltpu.VMEM((tq, tk), F32),
            pltpu.VMEM((tq, tk), F32),
            pltpu.VMEM((tq, tk), BF16),
            pltpu.VMEM((tq, tk), BF16),
            pltpu.VMEM((2, tq, LANES), F32),
            pltpu.VMEM((2, tq, LANES), F32),
            pltpu.VMEM((2, tq, LANES), F32),
            pltpu.VMEM((2, tq, HEAD_PAD), F32),
        ],
        compiler_params=_cparams(2),
        name="mla_attention",
    )(q, k_all, v_all)


def _mla_ctx_kernel(q_ref, k_ref, v_ref, o_ref):
    for hd in range(2):
        sl = slice(hd * HEAD_PAD, (hd + 1) * HEAD_PAD)
        s = _dot_nt(q_ref[:, sl], k_ref[:, sl])
        p = jnp.exp2(s - jnp.max(s, axis=1, keepdims=True))
        acc = _dot(p.astype(BF16), v_ref[:, sl])
        o_ref[:, sl] = (acc / acc[:, MLA_V:MLA_V + 1]).astype(BF16)


def _mla_ctx_attention(q, k, v, s):
    nb, p, _ = q.shape
    c = p - s
    pairs = MLA_HEADS // 2
    spec = pl.BlockSpec((None, c, 2 * HEAD_PAD), lambda b, hp: (b, s // c, hp))
    return pl.pallas_call(
        _mla_ctx_kernel,
        grid=(nb, pairs),
        in_specs=[spec, spec, spec],
        out_specs=pl.BlockSpec((None, c, 2 * HEAD_PAD), lambda b, hp: (b, 0, hp)),
        out_shape=jax.ShapeDtypeStruct((nb, c, MLA_HEADS * HEAD_PAD), BF16),
        compiler_params=_cparams(2),
        name="mla_ctx_attention",
    )(q, k, v)


def _na_kernel(q_ref, k_ref, v_ref, kc_ref, vc_ref, bias_ref, rowmask_ref, o_ref, *, grid_rows):
    rb = pl.program_id(2)
    start_row = jnp.clip(rb * NA_ROWS_PER_BLOCK - NA_WIN_R // 2, 0, grid_rows - NA_KEY_ROWS)
    start = pl.multiple_of(start_row * GRID_W, (NA_WIN_R // 2) * GRID_W)
    n_keys = NA_KEY_ROWS * GRID_W
    q = q_ref[...]
    kw = k_ref[pl.ds(start, n_keys), :]
    vw = v_ref[pl.ds(start, n_keys), :]
    kc = kc_ref[...]
    vc = vc_ref[...]
    lane = lax.broadcasted_iota(jnp.int32, q.shape, 1)
    tile0 = start_row - rb * NA_ROWS_PER_BLOCK + (NA_WIN_R - 1) + NA_BIAS_SHIFT
    row_mask = rowmask_ref[...]
    outs = []
    for hd in range(2):
        in_head = (lane >= NA_HEAD_DIM) if hd else (lane < NA_HEAD_DIM)
        qh = jnp.where(in_head, q, jnp.zeros_like(q))
        bias = jnp.concatenate(
            [jnp.concatenate([bias_ref[hd, tile0 + 2 * j - qr] for j in range(NA_KEY_ROWS // 2)], axis=1)
             for qr in range(NA_ROWS_PER_BLOCK)], axis=0)
        s = _dot_nt(qh, kw) + bias + row_mask
        sc = _dot_nt(qh, kc)
        m = jnp.maximum(jnp.max(s, axis=1, keepdims=True), jnp.max(sc, axis=1, keepdims=True))
        p = jnp.exp(s - m)
        pc = jnp.exp(sc - m)
        denom = jnp.sum(p, axis=1, keepdims=True) + jnp.sum(pc, axis=1, keepdims=True)
        outs.append((_dot(p.astype(BF16), vw) + _dot(pc.astype(BF16), vc)) / denom)
    o_ref[...] = jnp.where(lane < NA_HEAD_DIM, outs[0], outs[1]).astype(BF16)


def _na_attention(q, k, v, bias, row_mask, layer, s):
    nb, p, _ = q.shape
    c = p - s
    grid_rows = s // GRID_W
    tq = NA_ROWS_PER_BLOCK * GRID_W
    n_blocks = grid_rows // NA_ROWS_PER_BLOCK
    n_keys = NA_KEY_ROWS * GRID_W
    pairs = NA_HEADS // 2

    def variant(rb):
        return jnp.where(rb == 0, 0, jnp.where(rb == n_blocks - 1, 2, 1))

    kern = functools.partial(_na_kernel, grid_rows=grid_rows)
    return pl.pallas_call(
        kern,
        grid=(nb, pairs, n_blocks),
        in_specs=[
            pl.BlockSpec((None, tq, LANES), lambda b, hp, rb: (b, rb, hp)),
            pl.BlockSpec((None, s, LANES), lambda b, hp, rb: (b, 0, hp)),
            pl.BlockSpec((None, s, LANES), lambda b, hp, rb: (b, 0, hp)),
            pl.BlockSpec((None, c, LANES), lambda b, hp, rb: (b, s // c, hp)),
            pl.BlockSpec((None, c, LANES), lambda b, hp, rb: (b, s // c, hp)),
            pl.BlockSpec((None, 2) + bias.shape[2:], lambda b, hp, rb: (layer, hp, 0, 0, 0)),
            pl.BlockSpec((None, tq, n_keys), lambda b, hp, rb: (variant(rb), 0, 0)),
        ],
        out_specs=pl.BlockSpec((None, tq, LANES), lambda b, hp, rb: (b, rb, hp)),
        out_shape=jax.ShapeDtypeStruct((nb, s, NA_WIDTH), BF16),
        compiler_params=_cparams(3),
        name="na_attention",
    )(q, k, v, k, v, bias, row_mask)


def _na_ctx_kernel(q_ref, k_ref, v_ref, o_ref):
    q = q_ref[...]
    k = k_ref[...]
    v = v_ref[...]
    lane = lax.broadcasted_iota(jnp.int32, q.shape, 1)
    outs = []
    for hd in range(2):
        in_head = (lane >= NA_HEAD_DIM) if hd else (lane < NA_HEAD_DIM)
        qh = jnp.where(in_head, q, jnp.zeros_like(q))
        s = _dot_nt(qh, k)
        p = jnp.exp(s - jnp.max(s, axis=1, keepdims=True))
        outs.append(_dot(p.astype(BF16), v) / jnp.sum(p, axis=1, keepdims=True))
    o_ref[...] = jnp.where(lane < NA_HEAD_DIM, outs[0], outs[1]).astype(BF16)


def _na_ctx_attention(q, k, v, s):
    nb, p, _ = q.shape
    c = p - s
    spec = pl.BlockSpec((None, c, LANES), lambda b, hp: (b, s // c, hp))
    return pl.pallas_call(
        _na_ctx_kernel,
        grid=(nb, NA_HEADS // 2),
        in_specs=[spec, spec, spec],
        out_specs=pl.BlockSpec((None, c, LANES), lambda b, hp: (b, 0, hp)),
        out_shape=jax.ShapeDtypeStruct((nb, c, NA_WIDTH), BF16),
        compiler_params=_cparams(2),
        name="na_ctx_attention",
    )(q, k, v)


def _na_row_masks(grid_rows):
    qr = np.arange(NA_ROWS_PER_BLOCK)[:, None]
    kr = np.arange(NA_KEY_ROWS)[None, :]
    row_valid = np.zeros((3, NA_ROWS_PER_BLOCK, NA_KEY_ROWS), bool)
    n_blocks = grid_rows // NA_ROWS_PER_BLOCK
    for vi, rb in enumerate((0, 1, n_blocks - 1)):
        r = rb * NA_ROWS_PER_BLOCK + qr
        start_row = min(max(rb * NA_ROWS_PER_BLOCK - NA_WIN_R // 2, 0), grid_rows - NA_KEY_ROWS)
        key_row = start_row + kr
        win = np.clip(r - NA_WIN_R // 2, 0, grid_rows - NA_WIN_R)
        row_valid[vi] = (key_row >= win) & (key_row < win + NA_WIN_R)
    mask = np.where(row_valid, 0.0, MASK_VALUE).astype(np.float32)
    mask = np.broadcast_to(mask[:, :, None, :, None], (3, NA_ROWS_PER_BLOCK, GRID_W, NA_KEY_ROWS, GRID_W))
    return jnp.asarray(mask.reshape(3, NA_ROWS_PER_BLOCK * GRID_W, NA_KEY_ROWS * GRID_W))


def _na_bias_tables(rpb):
    n_dc = 2 * NA_WIN_C - 1
    qc = np.arange(GRID_W)[:, None]
    kcol = np.arange(GRID_W)[None, :]
    wstart = np.clip(qc - NA_WIN_C // 2, 0, GRID_W - NA_WIN_C)
    col_valid = (kcol >= wstart) & (kcol < wstart + NA_WIN_C)
    dc = np.clip(kcol - qc, -(NA_WIN_C - 1), NA_WIN_C - 1) + (NA_WIN_C - 1)
    col_onehot = ((dc[..., None] == np.arange(n_dc)) & col_valid[..., None]).astype(np.float32)
    t = jnp.einsum("lhij,cdj->lhicd", rpb.astype(F32), jnp.asarray(col_onehot), precision=lax.Precision.HIGHEST)
    t = jnp.where(jnp.asarray(col_valid), t, MASK_VALUE)
    n_dr = t.shape[2]
    t = jnp.pad(t, ((0, 0), (0, 0), (NA_BIAS_SHIFT, NA_BIAS_TILES + 1 - NA_BIAS_SHIFT - n_dr), (0, 0), (0, 0)),
                constant_values=MASK_VALUE)
    return jnp.concatenate([t[:, :, :NA_BIAS_TILES], t[:, :, 1:NA_BIAS_TILES + 1]], axis=-1)


FT_N1 = 64


def _ft_stage1_kernel(w_ref, x_ref, o_ref):
    o_ref[...] = _dot(w_ref[...], x_ref[...])


def _ft_stage2_kernel(are_ref, aim_ref, tc_ref, ts_ref, w_ref, cd_ref, sd_ref, o_ref, *, scale):
    n2 = are_ref.shape[1]
    for g in range(are_ref.shape[0]):
        are = are_ref[g]
        aim = aim_ref[g]
        tc = jnp.concatenate([tc_ref[g]] * (FT_WIDTH // LANES), axis=1)
        ts = jnp.concatenate([ts_ref[g]] * (FT_WIDTH // LANES), axis=1)
        bre = are * tc + aim * ts
        bim = aim * tc - are * ts
        bb = jnp.concatenate([bre, bim], axis=0).astype(BF16)
        xk = _dot(w_ref[...], bb)
        f = _dot(xk[:n2].astype(BF16), cd_ref[...]) + _dot(xk[n2:].astype(BF16), sd_ref[...])
        o_ref[:, g * FT_WIDTH:(g + 1) * FT_WIDTH] = (f * scale).astype(BF16)


def _dft_mats(n):
    ang = 2.0 * np.pi * np.outer(np.arange(n), np.arange(n)) / n
    return np.cos(ang), np.sin(ang)


def _channel_dft():
    c, s = _dft_mats(FT_GROUP_DIM)
    eye = np.eye(FT_GROUPS)
    return jnp.asarray(np.kron(eye, c), BF16), jnp.asarray(np.kron(eye, s), BF16)


def _fourier_lat(u, n):
    nb, p, _ = u.shape
    n2 = n // FT_N1
    cols = n2 * FT_WIDTH
    c1, s1 = _dft_mats(FT_N1)
    w1 = jnp.asarray(np.concatenate([c1, -s1], axis=0), BF16)
    tcol = min(2048, cols)
    a = pl.pallas_call(
        _ft_stage1_kernel,
        grid=(nb, cols // tcol),
        in_specs=[
            pl.BlockSpec(w1.shape, lambda b, j: (0, 0)),
            pl.BlockSpec((None, FT_N1, tcol), lambda b, j: (b, 0, j)),
        ],
        out_specs=pl.BlockSpec((None, 2 * FT_N1, tcol), lambda b, j: (b, 0, j)),
        out_shape=jax.ShapeDtypeStruct((nb, 2 * FT_N1, cols), F32),
        compiler_params=_cparams(2),
        name="fourier_stage1",
    )(w1, u.reshape(nb, p // n2, cols))
    a = a.reshape(nb, 2 * FT_N1, n2, FT_WIDTH)

    ang = 2.0 * np.pi * np.outer(np.arange(FT_N1), np.arange(n2)) / n
    tc = jnp.asarray(np.repeat(np.cos(ang)[:, :, None], LANES, axis=2), F32)
    ts = jnp.asarray(np.repeat(np.sin(ang)[:, :, None], LANES, axis=2), F32)
    c3, s3 = _dft_mats(n2)
    w3 = jnp.asarray(np.block([[c3, s3], [-s3, c3]]), BF16)
    cd, sd = _channel_dft()
    kern = functools.partial(_ft_stage2_kernel, scale=1.0 / math.sqrt(n * FT_GROUP_DIM))
    g = FT_K1_PER_STEP
    n_steps = FT_N1 // g
    out = pl.pallas_call(
        kern,
        grid=(nb, n_steps),
        in_specs=[
            pl.BlockSpec((None, g, n2, FT_WIDTH), lambda b, k: (b, k, 0, 0)),
            pl.BlockSpec((None, g, n2, FT_WIDTH), lambda b, k: (b, n_steps + k, 0, 0)),
            pl.BlockSpec((g, n2, LANES), lambda b, k: (k, 0, 0)),
            pl.BlockSpec((g, n2, LANES), lambda b, k: (k, 0, 0)),
            pl.BlockSpec(w3.shape, lambda b, k: (0, 0)),
            pl.BlockSpec(cd.shape, lambda b, k: (0, 0)),
            pl.BlockSpec(sd.shape, lambda b, k: (0, 0)),
        ],
        out_specs=pl.BlockSpec((None, n2, g * FT_WIDTH), lambda b, k: (b, 0, k)),
        out_shape=jax.ShapeDtypeStruct((nb, n2, FT_N1 * FT_WIDTH), BF16),
        compiler_params=_cparams(2),
        name="fourier_stage2",
    )(a, a, tc, ts, w3, cd, sd)
    return out.reshape(nb, n, FT_WIDTH)


def _ft_ctx_kernel(u_ref, cn_ref, sn_ref, cd_ref, sd_ref, o_ref, *, scale):
    u = u_ref[...]
    y1 = _dot(u, cd_ref[...]).astype(BF16)
    y2 = _dot(u, sd_ref[...]).astype(BF16)
    f = _dot(cn_ref[...], y1) - _dot(sn_ref[...], y2)
    o_ref[...] = (f * scale).astype(BF16)


def _fourier_ctx(u, s):
    nb, p, _ = u.shape
    n = p - s
    cn, sn = _dft_mats(n)
    cn, sn = jnp.asarray(cn, BF16), jnp.asarray(sn, BF16)
    cd, sd = _channel_dft()
    kern = functools.partial(_ft_ctx_kernel, scale=1.0 / math.sqrt(n * FT_GROUP_DIM))
    sq = pl.BlockSpec((n, n), lambda b: (0, 0))
    ch = pl.BlockSpec(cd.shape, lambda b: (0, 0))
    return pl.pallas_call(
        kern,
        grid=(nb,),
        in_specs=[pl.BlockSpec((None, n, FT_WIDTH), lambda b: (b, s // n, 0)), sq, sq, ch, ch],
        out_specs=pl.BlockSpec((None, n, FT_WIDTH), lambda b: (b, 0, 0)),
        out_shape=jax.ShapeDtypeStruct((nb, n, FT_WIDTH), BF16),
        compiler_params=_cparams(1),
        name="fourier_ctx",
    )(u, cn, sn, cd, sd)


def _merge_kernel(x_ref, oa_ref, ob_ref, oc_ref, ca_ref, cb_ref, cc_ref, mod_ref, n1g_ref, n2g_ref, wgate_ref,
                  bgate_ref, wba_ref, wbb_ref, wbc_ref, wout_ref, x1_ref, h2_ref, *, n_lat_tiles):
    d = x_ref.shape[-1]
    x = x_ref[...]
    is_ctx = pl.program_id(1) >= n_lat_tiles
    o_a = jnp.where(is_ctx, ca_ref[...], oa_ref[...])
    o_b = jnp.where(is_ctx, cb_ref[...], ob_ref[...])
    o_c = jnp.where(is_ctx, cc_ref[...], oc_ref[...])
    mod = mod_ref[...]
    sh1, sc1, g1, sh2, sc2 = (mod[:, i * d:(i + 1) * d] for i in range(5))
    h = _rms(x, n1g_ref[...]) * (1.0 + sc1) + sh1
    gate = jax.nn.sigmoid(_dot(h.astype(BF16), wgate_ref[...]) + bgate_ref[...])
    m = gate[:, 0:d] * _dot(o_a, wba_ref[...])
    m += gate[:, d:2 * d] * _dot(o_b, wbb_ref[...])
    m += gate[:, 2 * d:3 * d] * _dot(o_c, wbc_ref[...])
    y = _dot(m.astype(BF16), wout_ref[...])
    x1 = x + g1 * y
    x1_ref[...] = x1
    h2_ref[...] = (_rms(x1, n2g_ref[...]) * (1.0 + sc2) + sh2).astype(BF16)


def _merge(x, lat_outs, ctx_outs, mod, mod_row, lw, layer):
    nb, n, d = x.shape
    tm = TOKEN_TILE
    n_lat_tiles = lat_outs[0].shape[1] // tm

    def full(a):
        return _layer_spec(a, layer)

    def tok(width):
        return pl.BlockSpec((None, tm, width), lambda b, i: (b, i, 0))

    def lat(a):
        return pl.BlockSpec((None, tm, a.shape[-1]), lambda b, i: (b, jnp.minimum(i, n_lat_tiles - 1), 0))

    def ctx(a):
        return pl.BlockSpec((None, tm, a.shape[-1]), lambda b, i: (b, jnp.maximum(i - n_lat_tiles, 0), 0))

    weights = [lw["norm1_g"], lw["norm2_g"], lw["w_gate"], lw["b_gate"], lw["w_br_mla"], lw["w_br_na"],
               lw["w_br_ft"], lw["w_out"]]
    return pl.pallas_call(
        functools.partial(_merge_kernel, n_lat_tiles=n_lat_tiles),
        grid=(nb, n // tm),
        in_specs=[tok(d)] + [lat(a) for a in lat_outs] + [ctx(a) for a in ctx_outs] + [_mod_spec(mod, layer, mod_row)]
        + [full(w) for w in weights],
        out_specs=[tok(d), tok(d)],
        out_shape=[jax.ShapeDtypeStruct((nb, n, d), F32), jax.ShapeDtypeStruct((nb, n, d), BF16)],
        compiler_params=_cparams(2),
        name="merge",
    )(x, *lat_outs, *ctx_outs, mod, *weights)


def _route(logits, router_bias):
    row = lax.broadcasted_iota(jnp.int32, logits.shape, 0)
    neg = -jnp.inf
    big = jnp.int32(N_EXPERTS)
    scores = jax.nn.sigmoid(logits)
    sel = scores + router_bias

    def top2(vals):
        m1 = jnp.max(vals, axis=0, keepdims=True)
        i1 = jnp.min(jnp.where(vals == m1, row, big), axis=0, keepdims=True)
        rest = jnp.where(row == i1, neg, vals)
        m2 = jnp.max(rest, axis=0, keepdims=True)
        i2 = jnp.min(jnp.where(rest == m2, row, big), axis=0, keepdims=True)
        return m1, i1, m2, i2

    grp_scores = []
    for g in range(N_GROUPS):
        in_grp = (row >= g * EXPERTS_PER_GROUP) & (row < (g + 1) * EXPERTS_PER_GROUP)
        m1, _, m2, _ = top2(jnp.where(in_grp, sel, neg))
        grp_scores.append(m1 + m2)
    best = functools.reduce(jnp.maximum, grp_scores)
    grp = jnp.full(best.shape, N_GROUPS - 1, jnp.int32)
    for g in range(N_GROUPS - 2, -1, -1):
        grp = jnp.where(grp_scores[g] == best, g, grp)
    in_sel = (row >= grp * EXPERTS_PER_GROUP) & (row < (grp + 1) * EXPERTS_PER_GROUP)
    _, i1, _, i2 = top2(jnp.where(in_sel, sel, neg))
    hit1 = row == i1
    hit2 = row == i2
    w1 = jnp.sum(jnp.where(hit1, scores, 0.0), axis=0, keepdims=True)
    w2 = jnp.sum(jnp.where(hit2, scores, 0.0), axis=0, keepdims=True)
    total = w1 + w2
    return jnp.where(hit1, w1 / total, 0.0) + jnp.where(hit2, w2 / total, 0.0)


def _moe_kernel(h_ref, x_ref, mod_ref, cmod_ref, rw_ref, rb_ref, fg_ref, wg_ref, wu_ref, wd_ref, o_ref, gate_scr,
                acc_scr, *, apply_final_norm, n_lat):
    d = x_ref.shape[-1]
    tm = x_ref.shape[0]
    step = pl.program_id(2)
    n_steps = pl.num_programs(2)
    h = h_ref[...]

    @pl.when(step == 0)
    def _():
        gate_t = _route(_dot_nt(rw_ref[...], h), rb_ref[...])
        pad_rows = jnp.zeros((LANES - N_EXPERTS, gate_t.shape[1]), F32)
        gate = jnp.concatenate([gate_t, pad_rows], axis=0).T
        for c in range(N_EXPERTS // EXPERTS_PER_STEP):
            shift = (LANES - c * EXPERTS_PER_STEP) % LANES
            gate_scr[c] = pltpu.roll(gate, shift, axis=1) if shift else gate
        acc_scr[...] = jnp.zeros(acc_scr.shape, F32)

    gate = gate_scr[step]
    acc = acc_scr[...]
    for j in range(EXPERTS_PER_STEP):
        a = _dot(h, wg_ref[j])
        u = _dot(h, wu_ref[j])
        hid = (a * jax.nn.sigmoid(a)) * u * gate[:, j:j + 1]
        acc += _dot(hid.astype(BF16), wd_ref[j])
    acc_scr[...] = acc

    @pl.when(step == n_steps - 1)
    def _():
        tok_row = pl.program_id(1) * tm + lax.broadcasted_iota(jnp.int32, (tm, 1), 0)
        g2 = jnp.where(tok_row >= n_lat, cmod_ref[:, 5 * d:6 * d], mod_ref[:, 5 * d:6 * d])
        x2 = x_ref[...] + g2 * acc
        o_ref[...] = _rms(x2, fg_ref[...]) if apply_final_norm else x2


def _moe_tile(p):
    return max(t for t in range(TOKEN_TILE, MOE_LAT_TILE + 1, TOKEN_TILE) if p % t == 0)


def _moe(h2, x1, mod, ctx_row, lw, layer, router_w, router_b, final_g, n_lat, n_rows, tm, apply_final_norm):
    nb, _, d = x1.shape
    n_steps = N_EXPERTS // EXPERTS_PER_STEP
    ff = lw["w_e_gate"].shape[-1]
    tok = pl.BlockSpec((None, tm, d), lambda b, i, c: (b, i, 0))
    kern = functools.partial(_moe_kernel, apply_final_norm=apply_final_norm, n_lat=n_lat)
    return pl.pallas_call(
        kern,
        grid=(nb, n_rows // tm, n_steps),
        in_specs=[
            tok, tok,
            _mod_spec(mod, layer, lambda b, i: b),
            _mod_spec(mod, layer, lambda b, i: ctx_row),
            pl.BlockSpec(router_w.shape, lambda b, i, c: (0, 0)),
            pl.BlockSpec(router_b.shape, lambda b, i, c: (0, 0)),
            pl.BlockSpec(final_g.shape, lambda b, i, c: (0, 0)),
            pl.BlockSpec((None, EXPERTS_PER_STEP, d, ff), lambda b, i, c: (layer, c, 0, 0)),
            pl.BlockSpec((None, EXPERTS_PER_STEP, d, ff), lambda b, i, c: (layer, c, 0, 0)),
            pl.BlockSpec((None, EXPERTS_PER_STEP, ff, d), lambda b, i, c: (layer, c, 0, 0)),
        ],
        out_specs=tok,
        out_shape=jax.ShapeDtypeStruct((nb, n_rows, d), F32),
        scratch_shapes=[pltpu.VMEM((n_steps, tm, LANES), F32), pltpu.VMEM((tm, d), F32)],
        compiler_params=_cparams(3),
        name="moe",
    )(h2, x1, mod, mod, router_w, router_b, final_g, lw["w_e_gate"], lw["w_e_up"], lw["w_e_down"])


def _rotate_cols(w):
    half = w.shape[-1] // 2
    return jnp.concatenate([-w[..., half:], w[..., :half]], axis=-1)


def _prepare_weights(w_in, q_norm_g, w_q_up, kv_norm_g, w_kv_up, norm1_g, norm2_g, w_gate, b_gate, w_br_mla,
                     w_br_na, w_br_ft, w_out, w_e_gate, w_e_up, w_e_down):
    n_layers, d, _ = w_in.shape
    splits = np.cumsum([MLA_Q_LORA, MLA_KV_LORA, MLA_ROPE, NA_WIDTH, NA_WIDTH, NA_WIDTH])
    w_q, w_kv, w_kr, w_nq, w_nk, w_nv, w_ft = jnp.split(w_in, [int(s) for s in splits], axis=-1)

    def in_rope_slot(w):
        zeros_lo = jnp.zeros(w.shape[:-1] + (MLA_NOPE,), w.dtype)
        zeros_hi = jnp.zeros(w.shape[:-1] + (HEAD_PAD - MLA_NOPE - MLA_ROPE,), w.dtype)
        return jnp.concatenate([zeros_lo, w, zeros_hi], axis=-1)

    win = jnp.concatenate([w_q, w_kv, in_rope_slot(w_kr), in_rope_slot(_rotate_cols(w_kr)), w_nq, w_nk, w_nv, w_ft],
                          axis=-1).astype(BF16)
    wq = w_q_up.reshape(n_layers, MLA_Q_LORA, MLA_HEADS, MLA_NOPE + MLA_ROPE)
    wq_nope, wq_pe = wq[..., :MLA_NOPE], wq[..., MLA_NOPE:]
    pad = jnp.zeros(wq_pe.shape[:-1] + (HEAD_PAD - MLA_NOPE - MLA_ROPE,), wq.dtype)
    wq_main = jnp.concatenate([wq_nope, wq_pe, pad], axis=-1)
    wq_rot = jnp.concatenate([jnp.zeros_like(wq_nope), _rotate_cols(wq_pe), pad], axis=-1)
    wkv = w_kv_up.reshape(n_layers, MLA_KV_LORA, MLA_HEADS, MLA_NOPE + MLA_V)
    wk_nope, wv = wkv[..., :MLA_NOPE], wkv[..., MLA_NOPE:]
    wkv_k = jnp.concatenate([wk_nope, jnp.zeros(wk_nope.shape[:-1] + (HEAD_PAD - MLA_NOPE,), wkv.dtype)], axis=-1)
    wide = MLA_HEADS * HEAD_PAD
    return {
        "w_in": win,
        "wq_main": wq_main.reshape(n_layers, MLA_Q_LORA, wide).astype(BF16),
        "wq_rot": wq_rot.reshape(n_layers, MLA_Q_LORA, wide).astype(BF16),
        "wkv_k": wkv_k.reshape(n_layers, MLA_KV_LORA, wide).astype(BF16),
        "wkv_v": jnp.concatenate([wv, jnp.zeros(wv.shape[:-1] + (HEAD_PAD - MLA_V,), wv.dtype)], axis=-1)
        .reshape(n_layers, MLA_KV_LORA, wide).astype(BF16),
        "q_norm_g": q_norm_g.reshape(n_layers, 1, -1),
        "kv_norm_g": kv_norm_g.reshape(n_layers, 1, -1),
        "norm1_g": norm1_g.reshape(n_layers, 1, d),
        "norm2_g": norm2_g.reshape(n_layers, 1, d),
        "w_gate": w_gate.astype(BF16),
        "b_gate": b_gate.reshape(n_layers, 1, -1),
        "w_br_mla": jnp.pad(w_br_mla.reshape(n_layers, MLA_HEADS, MLA_V, d), ((0, 0), (0, 0), (0, HEAD_PAD - MLA_V), (0, 0)))
        .reshape(n_layers, wide, d).astype(BF16),
        "w_br_na": w_br_na.astype(BF16),
        "w_br_ft": w_br_ft.astype(BF16),
        "w_out": w_out.astype(BF16),
        "w_e_gate": w_e_gate.astype(BF16),
        "w_e_up": w_e_up.astype(BF16),
        "w_e_down": w_e_down.astype(BF16),
    }


def _rope_tables(s, c):
    n_freq = MLA_ROPE // 4
    inv_freq = ROPE_THETA ** (-jnp.arange(n_freq, dtype=F32) / n_freq)
    t = jnp.arange(s, dtype=jnp.int32)
    row = (t // GRID_W).astype(F32)
    col = (t % GRID_W).astype(F32)
    ang = jnp.concatenate([row[:, None] * inv_freq, col[:, None] * inv_freq], axis=-1)
    cos, sin = jnp.cos(ang), jnp.sin(ang)
    pad = HEAD_PAD - MLA_NOPE - MLA_ROPE
    cos_lat = jnp.concatenate([jnp.ones((s, MLA_NOPE), F32), cos, cos, jnp.zeros((s, pad), F32)], axis=-1)
    sin_lat = jnp.concatenate([jnp.zeros((s, MLA_NOPE), F32), sin, sin, jnp.zeros((s, pad), F32)], axis=-1)
    cos_ctx = jnp.concatenate([jnp.ones((c, MLA_NOPE + MLA_ROPE), F32), jnp.zeros((c, pad), F32)], axis=-1)
    sin_ctx = jnp.zeros((c, HEAD_PAD), F32)
    return cos_lat, sin_lat, cos_ctx, sin_ctx


@jax.jit
def _forward(x, c, ctx, c_ctx, w_ada, b_ada, norm1_g, norm2_g, w_in, q_norm_g, w_q_up, kv_norm_g, w_kv_up,
             na_rpb, w_gate, b_gate, w_br_mla, w_br_na, w_br_ft, w_out, router_w, router_bias, w_e_gate,
             w_e_up, w_e_down, final_norm_g):
    nb, s, d = x.shape
    n_ctx = ctx.shape[1]
    n_layers = w_ada.shape[0]
    grid_rows = s // GRID_W

    mod_rows = 8
    c_rows = jnp.zeros((mod_rows, d), F32).at[:nb].set(c).at[nb].set(c_ctx)
    mod_all = _modulation(c_rows, w_ada, b_ada).reshape(n_layers, mod_rows, 1, 6 * d)

    weights = _prepare_weights(w_in, q_norm_g, w_q_up, kv_norm_g, w_kv_up, norm1_g, norm2_g, w_gate, b_gate,
                               w_br_mla, w_br_na, w_br_ft, w_out, w_e_gate, w_e_up, w_e_down)
    na_bias = _na_bias_tables(na_rpb)
    na_row_mask = _na_row_masks(grid_rows)
    cos_lat, sin_lat, cos_ctx, sin_ctx = _rope_tables(s, n_ctx)
    cos_t = jnp.concatenate([cos_lat, cos_ctx], axis=0)
    sin_t = jnp.concatenate([sin_lat, sin_ctx], axis=0)
    router_w_p = router_w.T.astype(BF16)
    router_b_p = router_bias.reshape(N_EXPERTS, 1).astype(F32)
    final_g = final_norm_g.reshape(1, d)

    p = s + n_ctx
    n_lat_tiles = s // TOKEN_TILE
    ctx_row = nb

    def tile_row(b, i):
        return jnp.where(i >= n_lat_tiles, ctx_row, b)

    x_all = jnp.concatenate([x, ctx], axis=1)
    zero_ctx = [jnp.zeros((nb, n_ctx, w), BF16) for w in (MLA_HEADS * HEAD_PAD, NA_WIDTH, FT_WIDTH)]
    for l in range(n_layers):
        last = l == n_layers - 1
        q, k, v, nq, nk, nv, ft = _project(x_all, mod_all, tile_row, weights, l, cos_t, sin_t)
        lat_outs = [_mla_attention(q, k, v, s), _na_attention(nq, nk, nv, na_bias, na_row_mask, l, s),
                    _fourier_lat(ft, s)]
        if last:
            ctx_outs = zero_ctx
        else:
            ctx_outs = [_mla_ctx_attention(q, k, v, s), _na_ctx_attention(nq, nk, nv, s), _fourier_ctx(ft, s)]
        x1, h2 = _merge(x_all, lat_outs, ctx_outs, mod_all, tile_row, weights, l)
        if last:
            return _moe(h2, x1, mod_all, ctx_row, weights, l, router_w_p, router_b_p, final_g, s, s,
                        min(MOE_LAT_TILE, s), True)
        x_all = _moe(h2, x1, mod_all, ctx_row, weights, l, router_w_p, router_b_p, final_g, s, p,
                     _moe_tile(p), False)


def kernel(x, c, ctx, c_ctx, w_ada, b_ada, norm1_g, norm2_g, w_in, q_norm_g, w_q_up, kv_norm_g, w_kv_up, na_rpb, w_gate, b_gate, w_br_mla, w_br_na, w_br_ft, w_out, router_w, router_bias, w_e_gate, w_e_up, w_e_down, final_norm_g):
    return _forward(x, c, ctx, c_ctx, w_ada, b_ada, norm1_g, norm2_g, w_in, q_norm_g, w_q_up, kv_norm_g,
                    w_kv_up, na_rpb, w_gate, b_gate, w_br_mla, w_br_na, w_br_ft, w_out, router_w, router_bias,
                    w_e_gate, w_e_up, w_e_down, final_norm_g)
```

```python
import functools
import math

import jax
import jax.numpy as jnp
import numpy as np
from jax import lax
from jax.experimental import pallas as pl
from jax.experimental.pallas import tpu as pltpu

F32 = jnp.float32
BF16 = jnp.bfloat16

GRID_W = 64
MLA_HEADS = 8
MLA_NOPE = 64
MLA_ROPE = 32
MLA_V = 64
MLA_Q_LORA = 256
MLA_KV_LORA = 128
MLA_SCALE = (MLA_NOPE + MLA_ROPE) ** -0.5
NA_HEADS = 4
NA_HEAD_DIM = 64
NA_WIN_R = 8
NA_WIN_C = 16
NA_SCALE = NA_HEAD_DIM ** -0.5
NA_WIDTH = NA_HEADS * NA_HEAD_DIM
FT_GROUPS = 4
FT_GROUP_DIM = 64
FT_WIDTH = FT_GROUPS * FT_GROUP_DIM
N_EXPERTS = 16
N_GROUPS = 4
EXPERTS_PER_GROUP = N_EXPERTS // N_GROUPS
ROPE_THETA = 10000.0
NORM_EPS = 1e-6
MASK_VALUE = -1e30
LOG2E = math.log2(math.e)

LANES = 128
V7X_VMEM_LIMIT_BYTES = 56 * 1024 * 1024

HEAD_PAD = LANES
NA_ROWS_PER_BLOCK = 8
NA_KEY_ROWS = 16
NA_BIAS_SHIFT = (NA_KEY_ROWS - NA_ROWS_PER_BLOCK) + (NA_ROWS_PER_BLOCK - 1) - (NA_WIN_R - 1)
NA_BIAS_TILES = (NA_WIN_R - 1) + NA_BIAS_SHIFT + (NA_KEY_ROWS - 2) + 1
FT_K1_PER_STEP = 8
EXPERTS_PER_STEP = 4
MOE_LAT_TILE = 1024
TOKEN_TILE = 256
MLA_MAX_KV_CHUNK = 2816


def _cparams(n_axes):
    return pltpu.CompilerParams(
        dimension_semantics=("arbitrary",) * n_axes,
        vmem_limit_bytes=V7X_VMEM_LIMIT_BYTES,
    )


def _rms(x, g):
    return x * lax.rsqrt(jnp.mean(x * x, axis=-1, keepdims=True) + NORM_EPS) * g


def _dot(a, b):
    return jnp.dot(a, b, preferred_element_type=F32)


def _dot_nt(a, b):
    return lax.dot_general(a, b, (((1,), (1,)), ((), ())), preferred_element_type=F32)


def _mod_kernel(c_ref, w_ref, b_ref, o_ref):
    c = c_ref[...]
    o_ref[...] = _dot(c * jax.nn.sigmoid(c), w_ref[...]) + b_ref[...]


def _modulation(c_rows, w_ada, b_ada):
    n_layers, d, width = w_ada.shape
    rows = c_rows.shape[0]
    tn = 1536
    return pl.pallas_call(
        _mod_kernel,
        grid=(n_layers, width // tn),
        in_specs=[
            pl.BlockSpec((rows, d), lambda l, j: (0, 0)),
            pl.BlockSpec((None, d, tn), lambda l, j: (l, 0, j)),
            pl.BlockSpec((None, 1, tn), lambda l, j: (l, 0, j)),
        ],
        out_specs=pl.BlockSpec((None, rows, tn), lambda l, j: (l, 0, j)),
        out_shape=jax.ShapeDtypeStruct((n_layers, rows, width), F32),
        compiler_params=_cparams(2),
        name="modulation",
    )(c_rows, w_ada, b_ada.reshape(n_layers, 1, width))


def _proj_kernel(x_ref, mod_ref, n1g_ref, win_ref, qg_ref, wqm_ref, wqr_ref, kvg_ref, wkk_ref, wkv_ref,
                 cos_ref, sin_ref, q_ref, k_ref, v_ref, nq_ref, nk_ref, nv_ref, ft_ref):
    d = x_ref.shape[-1]
    x = x_ref[...]
    mod = mod_ref[...]
    sh1, sc1 = mod[:, 0:d], mod[:, d:2 * d]
    h = _rms(x, n1g_ref[...]) * (1.0 + sc1) + sh1
    p = _dot(h.astype(BF16), win_ref[...])
    cos = cos_ref[...]
    sin = sin_ref[...]
    o = 0
    q_lat = p[:, o:o + MLA_Q_LORA]
    o += MLA_Q_LORA
    kv_lat = p[:, o:o + MLA_KV_LORA]
    o += MLA_KV_LORA
    kr_a = p[:, o:o + HEAD_PAD]
    o += HEAD_PAD
    kr_b = p[:, o:o + HEAD_PAD]
    o += HEAD_PAD
    nq_ref[...] = (p[:, o:o + NA_WIDTH] * NA_SCALE).astype(BF16)
    o += NA_WIDTH
    nk_ref[...] = p[:, o:o + NA_WIDTH].astype(BF16)
    o += NA_WIDTH
    nv_ref[...] = p[:, o:o + NA_WIDTH].astype(BF16)
    o += NA_WIDTH
    ft_ref[...] = p[:, o:o + FT_WIDTH].astype(BF16)

    qn = _rms(q_lat, qg_ref[...]).astype(BF16)
    qm = _dot(qn, wqm_ref[...])
    qr = _dot(qn, wqr_ref[...])
    kvn = _rms(kv_lat, kvg_ref[...]).astype(BF16)
    kn = _dot(kvn, wkk_ref[...])
    v = _dot(kvn, wkv_ref[...])
    v_lane = lax.broadcasted_iota(jnp.int32, v.shape, 1) & (HEAD_PAD - 1)
    v_ref[...] = jnp.where(v_lane == MLA_V, 1.0, v).astype(BF16)
    kr = kr_a * cos + kr_b * sin
    q_scale = MLA_SCALE * LOG2E
    for hd in range(MLA_HEADS):
        sl = slice(hd * HEAD_PAD, (hd + 1) * HEAD_PAD)
        q_ref[:, sl] = ((qm[:, sl] * cos + qr[:, sl] * sin) * q_scale).astype(BF16)
        k_ref[:, sl] = (kn[:, sl] + kr).astype(BF16)


def _layer_spec(a, layer):
    return pl.BlockSpec((None,) + a.shape[1:], lambda *_: (layer,) + (0,) * (a.ndim - 1))


def _mod_spec(mod, layer, mod_row):
    return pl.BlockSpec((None, None, 1, mod.shape[-1]), lambda b, i, *_: (layer, mod_row(b, i), 0, 0))


def _project(x, mod, mod_row, lw, layer, cos_t, sin_t):
    nb, n, d = x.shape
    tm = TOKEN_TILE
    win = lw["w_in"]
    wide = MLA_HEADS * HEAD_PAD

    def full(a):
        return _layer_spec(a, layer)

    def tok(width):
        return pl.BlockSpec((None, tm, width), lambda b, i: (b, i, 0))

    outs = [wide, wide, wide, NA_WIDTH, NA_WIDTH, NA_WIDTH, FT_WIDTH]
    return pl.pallas_call(
        _proj_kernel,
        grid=(nb, n // tm),
        in_specs=[
            tok(d),
            _mod_spec(mod, layer, mod_row),
            full(lw["norm1_g"]), full(win), full(lw["q_norm_g"]), full(lw["wq_main"]), full(lw["wq_rot"]),
            full(lw["kv_norm_g"]), full(lw["wkv_k"]), full(lw["wkv_v"]),
            pl.BlockSpec((tm, HEAD_PAD), lambda b, i: (i, 0)),
            pl.BlockSpec((tm, HEAD_PAD), lambda b, i: (i, 0)),
        ],
        out_specs=[tok(w) for w in outs],
        out_shape=[jax.ShapeDtypeStruct((nb, n, w), BF16) for w in outs],
        compiler_params=_cparams(2),
        name="project",
    )(x, mod, lw["norm1_g"], win, lw["q_norm_g"], lw["wq_main"], lw["wq_rot"], lw["kv_norm_g"],
      lw["wkv_k"], lw["wkv_v"], cos_t, sin_t)


def _kv_chunk(p):
    best = LANES
    for n in range(1, p // LANES + 1):
        tk = p // n
        if p % n == 0 and tk % LANES == 0 and tk <= MLA_MAX_KV_CHUNK:
            best = max(best, tk)
    return best


def _mla_kernel(q_ref, k_ref, v_ref, o_ref, s0_scr, s1_scr, p0_scr, p1_scr, mx_scr, m_scr, alpha_scr, acc_scr,
                *, tq, tk, n_q, n_c):
    def head(hd):
        return slice(hd * HEAD_PAD, (hd + 1) * HEAD_PAD)

    def q_rows(qt):
        return pl.ds(pl.multiple_of(qt * tq, tq), tq)

    def k_rows(c):
        return pl.ds(pl.multiple_of(c * tk, tk), tk)

    def scores(hd, qt, c, s_scr):
        s = _dot_nt(q_ref[q_rows(qt), head(hd)], k_ref[k_rows(c), head(hd)])
        s_scr[...] = s
        mx_scr[hd] = jnp.broadcast_to(jnp.max(s, axis=1, keepdims=True), mx_scr.shape[1:])

    def softmax(hd, s_scr, p_scr, first):
        m_prev = jnp.where(first, MASK_VALUE, m_scr[hd])
        m_new = jnp.maximum(m_prev, mx_scr[hd])
        alpha_scr[hd] = jnp.exp2(m_prev - m_new)
        m_scr[hd] = m_new
        p_scr[...] = jnp.exp2(s_scr[...] - jnp.tile(m_new, (1, tk // LANES))).astype(BF16)

    def accumulate(hd, p_scr, qt, c):
        acc = alpha_scr[hd] * acc_scr[hd] + _dot(p_scr[...], v_ref[k_rows(c), head(hd)])
        acc_scr[hd] = acc
        o_ref[q_rows(qt), head(hd)] = (acc / acc[:, MLA_V:MLA_V + 1]).astype(BF16)

    lane = lax.broadcasted_iota(jnp.int32, acc_scr.shape, 2)
    acc_scr[...] = jnp.where(lane == MLA_V, 1.0, 0.0)
    m_scr[...] = jnp.full(m_scr.shape, MASK_VALUE, F32)
    alpha_scr[...] = jnp.ones(alpha_scr.shape, F32)
    p1_scr[...] = jnp.zeros(p1_scr.shape, BF16)
    scores(0, 0, 0, s0_scr)

    def body(i, carry):
        qt, c = carry
        first = c == 0
        last = c == n_c - 1
        c_prev = jnp.where(first, n_c - 1, c - 1)
        qt_prev = jnp.maximum(jnp.where(first, qt - 1, qt), 0)
        c_next = jnp.where(last, 0, c + 1)
        qt_next = jnp.where(last, qt + 1, qt)
        scores(1, qt, c, s1_scr)
        softmax(0, s0_scr, p0_scr, first)
        accumulate(1, p1_scr, qt_prev, c_prev)
        scores(0, jnp.minimum(qt_next, n_q - 1), c_next, s0_scr)
        softmax(1, s1_scr, p1_scr, first)
        accumulate(0, p0_scr, qt, c)
        return qt_next, c_next

    lax.fori_loop(0, n_q * n_c, body, (jnp.int32(0), jnp.int32(0)))
    accumulate(1, p1_scr, n_q - 1, n_c - 1)


def _mla_attention(q, k_all, v_all, s):
    nb, p, _ = k_all.shape
    tq = min(512, s)
    tk = _kv_chunk(p)
    pairs = MLA_HEADS // 2
    kern = functools.partial(_mla_kernel, tq=tq, tk=tk, n_q=s // tq, n_c=p // tk)
    return pl.pallas_call(
        kern,
        grid=(nb, pairs),
        in_specs=[
            pl.BlockSpec((None, s, 2 * HEAD_PAD), lambda b, hp: (b, 0, hp)),
            pl.BlockSpec((None, p, 2 * HEAD_PAD), lambda b, hp: (b, 0, hp)),
            pl.BlockSpec((None, p, 2 * HEAD_PAD), lambda b, hp: (b, 0, hp)),
        ],
        out_specs=pl.BlockSpec((None, s, 2 * HEAD_PAD), lambda b, hp: (b, 0, hp)),
        out_shape=jax.ShapeDtypeStruct((nb, s, MLA_HEADS * HEAD_PAD), BF16),
        scratch_shapes=[
            pltpu.VMEM((tq, tk), F32),
            pltpu.VMEM((tq, tk), F32),
            pltpu.VMEM((tq, tk), BF16),
            pltpu.VMEM((tq, tk), BF16),
            pltpu.VMEM((2, tq, LANES), F32),
            pltpu.VMEM((2, tq, LANES), F32),
            pltpu.VMEM((2, tq, LANES), F32),
            pltpu.VMEM((2, tq, HEAD_PAD), F32),
        ],
        compiler_params=_cparams(2),
        name="mla_attention",
    )(q, k_all, v_all)


def _mla_ctx_kernel(q_ref, k_ref, v_ref, o_ref):
    for hd in range(2):
        sl = slice(hd * HEAD_PAD, (hd + 1) * HEAD_PAD)
        s = _dot_nt(q_ref[:, sl], k_ref[:, sl])
        p = jnp.exp2(s - jnp.max(s, axis=1, keepdims=True))
        acc = _dot(p.astype(BF16), v_ref[:, sl])
        o_ref[:, sl] = (acc / acc[:, MLA_V:MLA_V + 1]).astype(BF16)


def _mla_ctx_attention(q, k, v, s):
    nb, p, _ = q.shape
    c = p - s
    pairs = MLA_HEADS // 2
    spec = pl.BlockSpec((None, c, 2 * HEAD_PAD), lambda b, hp: (b, s // c, hp))
    return pl.pallas_call(
        _mla_ctx_kernel,
        grid=(nb, pairs),
        in_specs=[spec, spec, spec],
        out_specs=pl.BlockSpec((None, c, 2 * HEAD_PAD), lambda b, hp: (b, 0, hp)),
        out_shape=jax.ShapeDtypeStruct((nb, c, MLA_HEADS * HEAD_PAD), BF16),
        compiler_params=_cparams(2),
        name="mla_ctx_attention",
    )(q, k, v)


def _na_kernel(q_ref, k_ref, v_ref, kc_ref, vc_ref, bias_ref, rowmask_ref, o_ref, comb_scr, s0_scr, s1_scr, p0_scr,
               p1_scr, mx_scr, l_scr, *, grid_rows, n_blocks):
    tq = NA_ROWS_PER_BLOCK * GRID_W
    n_keys = NA_KEY_ROWS * GRID_W

    for hd in range(2):
        for var, row_off in enumerate((0, -(NA_WIN_R // 2), -(NA_KEY_ROWS - NA_ROWS_PER_BLOCK))):
            tile0 = row_off + (NA_WIN_R - 1) + NA_BIAS_SHIFT
            bias = jnp.concatenate(
                [jnp.concatenate([bias_ref[hd, tile0 + 2 * j - qr] for j in range(NA_KEY_ROWS // 2)], axis=1)
                 for qr in range(NA_ROWS_PER_BLOCK)], axis=0)
            comb_scr[hd, var] = bias + rowmask_ref[var].astype(F32)

    lane = lax.broadcasted_iota(jnp.int32, (tq, LANES), 1)

    def q_rows(rb):
        return pl.ds(pl.multiple_of(rb * tq, tq), tq)

    def key_rows(rb):
        start_row = jnp.clip(rb * NA_ROWS_PER_BLOCK - NA_WIN_R // 2, 0, grid_rows - NA_KEY_ROWS)
        return pl.ds(pl.multiple_of(start_row * GRID_W, (NA_WIN_R // 2) * GRID_W), n_keys)

    def scores(hd, rb, s_scr):
        var = jnp.where(rb == 0, 0, jnp.where(rb == n_blocks - 1, 2, 1))
        q = q_ref[q_rows(rb), :]
        in_head = (lane >= NA_HEAD_DIM) if hd else (lane < NA_HEAD_DIM)
        qh = jnp.where(in_head, q, jnp.zeros_like(q))
        s = _dot_nt(qh, k_ref[key_rows(rb), :]) + comb_scr[hd, var]
        sc = _dot_nt(qh, kc_ref[...])
        s_scr[:, :n_keys] = s
        s_scr[:, n_keys:] = sc
        m = jnp.maximum(jnp.max(s, axis=1, keepdims=True), jnp.max(sc, axis=1, keepdims=True))
        mx_scr[hd] = jnp.broadcast_to(m, mx_scr.shape[1:])

    def softmax(hd, s_scr, p_scr):
        p = jnp.exp(s_scr[...] - jnp.tile(mx_scr[hd], (1, s_scr.shape[1] // LANES)))
        l_scr[hd] = jnp.broadcast_to(jnp.sum(p, axis=1, keepdims=True), l_scr.shape[1:])
        p_scr[...] = p.astype(BF16)

    def accumulate(hd, p_scr, rb):
        acc = _dot(p_scr[:, :n_keys], v_ref[key_rows(rb), :]) + _dot(p_scr[:, n_keys:], vc_ref[...])
        out = (acc / l_scr[hd]).astype(BF16)
        sl = slice(hd * NA_HEAD_DIM, (hd + 1) * NA_HEAD_DIM)
        o_ref[q_rows(rb), sl] = out[:, sl]

    p1_scr[...] = jnp.zeros(p1_scr.shape, BF16)
    l_scr[...] = jnp.ones(l_scr.shape, F32)
    scores(0, 0, s0_scr)

    def body(rb, carry):
        scores(1, rb, s1_scr)
        softmax(0, s0_scr, p0_scr)
        accumulate(1, p1_scr, jnp.maximum(rb - 1, 0))
        scores(0, jnp.minimum(rb + 1, n_blocks - 1), s0_scr)
        softmax(1, s1_scr, p1_scr)
        accumulate(0, p0_scr, rb)
        return carry

    lax.fori_loop(0, n_blocks, body, 0)
    accumulate(1, p1_scr, n_blocks - 1)


def _na_attention(q, k, v, bias, row_mask, layer, s):
    nb, p, _ = q.shape
    c = p - s
    grid_rows = s // GRID_W
    tq = NA_ROWS_PER_BLOCK * GRID_W
    n_blocks = grid_rows // NA_ROWS_PER_BLOCK
    n_keys = NA_KEY_ROWS * GRID_W
    pairs = NA_HEADS // 2
    kern = functools.partial(_na_kernel, grid_rows=grid_rows, n_blocks=n_blocks)
    lat = pl.BlockSpec((None, s, LANES), lambda b, hp: (b, 0, hp))
    ctx = pl.BlockSpec((None, c, LANES), lambda b, hp: (b, s // c, hp))
    return pl.pallas_call(
        kern,
        grid=(nb, pairs),
        in_specs=[
            lat, lat, lat, ctx, ctx,
            pl.BlockSpec((None, 2) + bias.shape[2:], lambda b, hp: (layer, hp, 0, 0, 0)),
            pl.BlockSpec(row_mask.shape, lambda b, hp: (0, 0, 0)),
        ],
        out_specs=pl.BlockSpec((None, s, LANES), lambda b, hp: (b, 0, hp)),
        out_shape=jax.ShapeDtypeStruct((nb, s, NA_WIDTH), BF16),
        scratch_shapes=[
            pltpu.VMEM((2, 3, tq, n_keys), F32),
            pltpu.VMEM((tq, n_keys + c), F32),
            pltpu.VMEM((tq, n_keys + c), F32),
            pltpu.VMEM((tq, n_keys + c), BF16),
            pltpu.VMEM((tq, n_keys + c), BF16),
            pltpu.VMEM((2, tq, LANES), F32),
            pltpu.VMEM((2, tq, LANES), F32),
        ],
        compiler_params=_cparams(2),
        name="na_attention",
    )(q, k, v, k, v, bias, row_mask)


def _na_ctx_kernel(q_ref, k_ref, v_ref, o_ref):
    q = q_ref[...]
    k = k_ref[...]
    v = v_ref[...]
    lane = lax.broadcasted_iota(jnp.int32, q.shape, 1)
    outs = []
    for hd in range(2):
        in_head = (lane >= NA_HEAD_DIM) if hd else (lane < NA_HEAD_DIM)
        qh = jnp.where(in_head, q, jnp.zeros_like(q))
        s = _dot_nt(qh, k)
        p = jnp.exp(s - jnp.max(s, axis=1, keepdims=True))
        outs.append(_dot(p.astype(BF16), v) / jnp.sum(p, axis=1, keepdims=True))
    o_ref[...] = jnp.where(lane < NA_HEAD_DIM, outs[0], outs[1]).astype(BF16)


def _na_ctx_attention(q, k, v, s):
    nb, p, _ = q.shape
    c = p - s
    spec = pl.BlockSpec((None, c, LANES), lambda b, hp: (b, s // c, hp))
    return pl.pallas_call(
        _na_ctx_kernel,
        grid=(nb, NA_HEADS // 2),
        in_specs=[spec, spec, spec],
        out_specs=pl.BlockSpec((None, c, LANES), lambda b, hp: (b, 0, hp)),
        out_shape=jax.ShapeDtypeStruct((nb, c, NA_WIDTH), BF16),
        compiler_params=_cparams(2),
        name="na_ctx_attention",
    )(q, k, v)


def _na_row_masks(grid_rows):
    qr = np.arange(NA_ROWS_PER_BLOCK)[:, None]
    kr = np.arange(NA_KEY_ROWS)[None, :]
    row_valid = np.zeros((3, NA_ROWS_PER_BLOCK, NA_KEY_ROWS), bool)
    n_blocks = grid_rows // NA_ROWS_PER_BLOCK
    for vi, rb in enumerate((0, 1, n_blocks - 1)):
        r = rb * NA_ROWS_PER_BLOCK + qr
        start_row = min(max(rb * NA_ROWS_PER_BLOCK - NA_WIN_R // 2, 0), grid_rows - NA_KEY_ROWS)
        key_row = start_row + kr
        win = np.clip(r - NA_WIN_R // 2, 0, grid_rows - NA_WIN_R)
        row_valid[vi] = (key_row >= win) & (key_row < win + NA_WIN_R)
    mask = np.where(row_valid, 0.0, MASK_VALUE).astype(np.float32)
    mask = np.broadcast_to(mask[:, :, None, :, None], (3, NA_ROWS_PER_BLOCK, GRID_W, NA_KEY_ROWS, GRID_W))
    return jnp.asarray(mask.reshape(3, NA_ROWS_PER_BLOCK * GRID_W, NA_KEY_ROWS * GRID_W), BF16)


def _na_bias_tables(rpb):
    n_dc = 2 * NA_WIN_C - 1
    qc = np.arange(GRID_W)[:, None]
    kcol = np.arange(GRID_W)[None, :]
    wstart = np.clip(qc - NA_WIN_C // 2, 0, GRID_W - NA_WIN_C)
    col_valid = (kcol >= wstart) & (kcol < wstart + NA_WIN_C)
    dc = np.clip(kcol - qc, -(NA_WIN_C - 1), NA_WIN_C - 1) + (NA_WIN_C - 1)
    col_onehot = ((dc[..., None] == np.arange(n_dc)) & col_valid[..., None]).astype(np.float32)
    t = jnp.einsum("lhij,cdj->lhicd", rpb.astype(F32), jnp.asarray(col_onehot), precision=lax.Precision.HIGHEST)
    t = jnp.where(jnp.asarray(col_valid), t, MASK_VALUE)
    n_dr = t.shape[2]
    t = jnp.pad(t, ((0, 0), (0, 0), (NA_BIAS_SHIFT, NA_BIAS_TILES + 1 - NA_BIAS_SHIFT - n_dr), (0, 0), (0, 0)),
                constant_values=MASK_VALUE)
    return jnp.concatenate([t[:, :, :NA_BIAS_TILES], t[:, :, 1:NA_BIAS_TILES + 1]], axis=-1)


FT_N1 = 64


def _ft_stage1_kernel(w_ref, x_ref, o_ref):
    o_ref[...] = _dot(w_ref[...], x_ref[...])


def _ft_stage2_kernel(are_ref, aim_ref, tc_ref, ts_ref, w_ref, cd_ref, sd_ref, o_ref, *, scale):
    n2 = are_ref.shape[1]
    for g in range(are_ref.shape[0]):
        are = are_ref[g]
        aim = aim_ref[g]
        tc = jnp.concatenate([tc_ref[g]] * (FT_WIDTH // LANES), axis=1)
        ts = jnp.concatenate([ts_ref[g]] * (FT_WIDTH // LANES), axis=1)
        bre = are * tc + aim * ts
        bim = aim * tc - are * ts
        bb = jnp.concatenate([bre, bim], axis=0).astype(BF16)
        xk = _dot(w_ref[...], bb)
        f = _dot(xk[:n2].astype(BF16), cd_ref[...]) + _dot(xk[n2:].astype(BF16), sd_ref[...])
        o_ref[:, g * FT_WIDTH:(g + 1) * FT_WIDTH] = (f * scale).astype(BF16)


def _dft_mats(n):
    ang = 2.0 * np.pi * np.outer(np.arange(n), np.arange(n)) / n
    return np.cos(ang), np.sin(ang)


def _channel_dft():
    c, s = _dft_mats(FT_GROUP_DIM)
    eye = np.eye(FT_GROUPS)
    return jnp.asarray(np.kron(eye, c), BF16), jnp.asarray(np.kron(eye, s), BF16)


def _fourier_lat(u, n):
    nb, p, _ = u.shape
    n2 = n // FT_N1
    cols = n2 * FT_WIDTH
    c1, s1 = _dft_mats(FT_N1)
    w1 = jnp.asarray(np.concatenate([c1, -s1], axis=0), BF16)
    tcol = min(2048, cols)
    a = pl.pallas_call(
        _ft_stage1_kernel,
        grid=(nb, cols // tcol),
        in_specs=[
            pl.BlockSpec(w1.shape, lambda b, j: (0, 0)),
            pl.BlockSpec((None, FT_N1, tcol), lambda b, j: (b, 0, j)),
        ],
        out_specs=pl.BlockSpec((None, 2 * FT_N1, tcol), lambda b, j: (b, 0, j)),
        out_shape=jax.ShapeDtypeStruct((nb, 2 * FT_N1, cols), F32),
        compiler_params=_cparams(2),
        name="fourier_stage1",
    )(w1, u.reshape(nb, p // n2, cols))
    a = a.reshape(nb, 2 * FT_N1, n2, FT_WIDTH)

    ang = 2.0 * np.pi * np.outer(np.arange(FT_N1), np.arange(n2)) / n
    tc = jnp.asarray(np.repeat(np.cos(ang)[:, :, None], LANES, axis=2), F32)
    ts = jnp.asarray(np.repeat(np.sin(ang)[:, :, None], LANES, axis=2), F32)
    c3, s3 = _dft_mats(n2)
    w3 = jnp.asarray(np.block([[c3, s3], [-s3, c3]]), BF16)
    cd, sd = _channel_dft()
    kern = functools.partial(_ft_stage2_kernel, scale=1.0 / math.sqrt(n * FT_GROUP_DIM))
    g = FT_K1_PER_STEP
    n_steps = FT_N1 // g
    out = pl.pallas_call(
        kern,
        grid=(nb, n_steps),
        in_specs=[
            pl.BlockSpec((None, g, n2, FT_WIDTH), lambda b, k: (b, k, 0, 0)),
            pl.BlockSpec((None, g, n2, FT_WIDTH), lambda b, k: (b, n_steps + k, 0, 0)),
            pl.BlockSpec((g, n2, LANES), lambda b, k: (k, 0, 0)),
            pl.BlockSpec((g, n2, LANES), lambda b, k: (k, 0, 0)),
            pl.BlockSpec(w3.shape, lambda b, k: (0, 0)),
            pl.BlockSpec(cd.shape, lambda b, k: (0, 0)),
            pl.BlockSpec(sd.shape, lambda b, k: (0, 0)),
        ],
        out_specs=pl.BlockSpec((None, n2, g * FT_WIDTH), lambda b, k: (b, 0, k)),
        out_shape=jax.ShapeDtypeStruct((nb, n2, FT_N1 * FT_WIDTH), BF16),
        compiler_params=_cparams(2),
        name="fourier_stage2",
    )(a, a, tc, ts, w3, cd, sd)
    return out.reshape(nb, n, FT_WIDTH)


def _ft_ctx_kernel(u_ref, cn_ref, sn_ref, cd_ref, sd_ref, o_ref, *, scale):
    u = u_ref[...]
    y1 = _dot(u, cd_ref[...]).astype(BF16)
    y2 = _dot(u, sd_ref[...]).astype(BF16)
    f = _dot(cn_ref[...], y1) - _dot(sn_ref[...], y2)
    o_ref[...] = (f * scale).astype(BF16)


def _fourier_ctx(u, s):
    nb, p, _ = u.shape
    n = p - s
    cn, sn = _dft_mats(n)
    cn, sn = jnp.asarray(cn, BF16), jnp.asarray(sn, BF16)
    cd, sd = _channel_dft()
    kern = functools.partial(_ft_ctx_kernel, scale=1.0 / math.sqrt(n * FT_GROUP_DIM))
    sq = pl.BlockSpec((n, n), lambda b: (0, 0))
    ch = pl.BlockSpec(cd.shape, lambda b: (0, 0))
    return pl.pallas_call(
        kern,
        grid=(nb,),
        in_specs=[pl.BlockSpec((None, n, FT_WIDTH), lambda b: (b, s // n, 0)), sq, sq, ch, ch],
        out_specs=pl.BlockSpec((None, n, FT_WIDTH), lambda b: (b, 0, 0)),
        out_shape=jax.ShapeDtypeStruct((nb, n, FT_WIDTH), BF16),
        compiler_params=_cparams(1),
        name="fourier_ctx",
    )(u, cn, sn, cd, sd)


def _merge_kernel(x_ref, oa_ref, ob_ref, oc_ref, ca_ref, cb_ref, cc_ref, mod_ref, n1g_ref, n2g_ref, wgate_ref,
                  bgate_ref, wba_ref, wbb_ref, wbc_ref, wout_ref, x1_ref, h2_ref, *, n_lat_tiles):
    d = x_ref.shape[-1]
    x = x_ref[...]
    is_ctx = pl.program_id(1) >= n_lat_tiles
    o_a = jnp.where(is_ctx, ca_ref[...], oa_ref[...])
    o_b = jnp.where(is_ctx, cb_ref[...], ob_ref[...])
    o_c = jnp.where(is_ctx, cc_ref[...], oc_ref[...])
    mod = mod_ref[...]
    sh1, sc1, g1, sh2, sc2 = (mod[:, i * d:(i + 1) * d] for i in range(5))
    h = _rms(x, n1g_ref[...]) * (1.0 + sc1) + sh1
    gate = jax.nn.sigmoid(_dot(h.astype(BF16), wgate_ref[...]) + bgate_ref[...])
    m = gate[:, 0:d] * _dot(o_a, wba_ref[...])
    m += gate[:, d:2 * d] * _dot(o_b, wbb_ref[...])
    m += gate[:, 2 * d:3 * d] * _dot(o_c, wbc_ref[...])
    y = _dot(m.astype(BF16), wout_ref[...])
    x1 = x + g1 * y
    x1_ref[...] = x1
    h2_ref[...] = (_rms(x1, n2g_ref[...]) * (1.0 + sc2) + sh2).astype(BF16)


def _merge(x, lat_outs, ctx_outs, mod, mod_row, lw, layer):
    nb, n, d = x.shape
    tm = TOKEN_TILE
    n_lat_tiles = lat_outs[0].shape[1] // tm

    def full(a):
        return _layer_spec(a, layer)

    def tok(width):
        return pl.BlockSpec((None, tm, width), lambda b, i: (b, i, 0))

    def lat(a):
        return pl.BlockSpec((None, tm, a.shape[-1]), lambda b, i: (b, jnp.minimum(i, n_lat_tiles - 1), 0))

    def ctx(a):
        return pl.BlockSpec((None, tm, a.shape[-1]), lambda b, i: (b, jnp.maximum(i - n_lat_tiles, 0), 0))

    weights = [lw["norm1_g"], lw["norm2_g"], lw["w_gate"], lw["b_gate"], lw["w_br_mla"], lw["w_br_na"],
               lw["w_br_ft"], lw["w_out"]]
    return pl.pallas_call(
        functools.partial(_merge_kernel, n_lat_tiles=n_lat_tiles),
        grid=(nb, n // tm),
        in_specs=[tok(d)] + [lat(a) for a in lat_outs] + [ctx(a) for a in ctx_outs] + [_mod_spec(mod, layer, mod_row)]
        + [full(w) for w in weights],
        out_specs=[tok(d), tok(d)],
        out_shape=[jax.ShapeDtypeStruct((nb, n, d), F32), jax.ShapeDtypeStruct((nb, n, d), BF16)],
        compiler_params=_cparams(2),
        name="merge",
    )(x, *lat_outs, *ctx_outs, mod, *weights)


def _route(logits, router_bias):
    row = lax.broadcasted_iota(jnp.int32, logits.shape, 0)
    neg = -jnp.inf
    big = jnp.int32(N_EXPERTS)
    scores = jax.nn.sigmoid(logits)
    sel = scores + router_bias

    def top2(vals):
        m1 = jnp.max(vals, axis=0, keepdims=True)
        i1 = jnp.min(jnp.where(vals == m1, row, big), axis=0, keepdims=True)
        rest = jnp.where(row == i1, neg, vals)
        m2 = jnp.max(rest, axis=0, keepdims=True)
        i2 = jnp.min(jnp.where(rest == m2, row, big), axis=0, keepdims=True)
        return m1, i1, m2, i2

    grp_scores = []
    for g in range(N_GROUPS):
        in_grp = (row >= g * EXPERTS_PER_GROUP) & (row < (g + 1) * EXPERTS_PER_GROUP)
        m1, _, m2, _ = top2(jnp.where(in_grp, sel, neg))
        grp_scores.append(m1 + m2)
    best = functools.reduce(jnp.maximum, grp_scores)
    grp = jnp.full(best.shape, N_GROUPS - 1, jnp.int32)
    for g in range(N_GROUPS - 2, -1, -1):
        grp = jnp.where(grp_scores[g] == best, g, grp)
    in_sel = (row >= grp * EXPERTS_PER_GROUP) & (row < (grp + 1) * EXPERTS_PER_GROUP)
    _, i1, _, i2 = top2(jnp.where(in_sel, sel, neg))
    hit1 = row == i1
    hit2 = row == i2
    w1 = jnp.sum(jnp.where(hit1, scores, 0.0), axis=0, keepdims=True)
    w2 = jnp.sum(jnp.where(hit2, scores, 0.0), axis=0, keepdims=True)
    total = w1 + w2
    return jnp.where(hit1, w1 / total, 0.0) + jnp.where(hit2, w2 / total, 0.0)


def _moe_kernel(h_ref, x_ref, mod_ref, cmod_ref, rw_ref, rb_ref, fg_ref, wg_ref, wu_ref, wd_ref, o_ref, gate_scr,
                acc_scr, *, apply_final_norm, n_lat):
    d = x_ref.shape[-1]
    tm = x_ref.shape[0]
    step = pl.program_id(2)
    n_steps = pl.num_programs(2)
    h = h_ref[...]

    @pl.when(step == 0)
    def _():
        gate_t = _route(_dot_nt(rw_ref[...], h), rb_ref[...])
        pad_rows = jnp.zeros((LANES - N_EXPERTS, gate_t.shape[1]), F32)
        gate = jnp.concatenate([gate_t, pad_rows], axis=0).T
        for c in range(N_EXPERTS // EXPERTS_PER_STEP):
            shift = (LANES - c * EXPERTS_PER_STEP) % LANES
            gate_scr[c] = pltpu.roll(gate, shift, axis=1) if shift else gate
        acc_scr[...] = jnp.zeros(acc_scr.shape, F32)

    gate = gate_scr[step]
    acc = acc_scr[...]
    for j in range(EXPERTS_PER_STEP):
        a = _dot(h, wg_ref[j])
        u = _dot(h, wu_ref[j])
        hid = (a * jax.nn.sigmoid(a)) * u * gate[:, j:j + 1]
        acc += _dot(hid.astype(BF16), wd_ref[j])
    acc_scr[...] = acc

    @pl.when(step == n_steps - 1)
    def _():
        tok_row = pl.program_id(1) * tm + lax.broadcasted_iota(jnp.int32, (tm, 1), 0)
        g2 = jnp.where(tok_row >= n_lat, cmod_ref[:, 5 * d:6 * d], mod_ref[:, 5 * d:6 * d])
        x2 = x_ref[...] + g2 * acc
        o_ref[...] = _rms(x2, fg_ref[...]) if apply_final_norm else x2


def _moe_tile(p):
    return max(t for t in range(TOKEN_TILE, MOE_LAT_TILE + 1, TOKEN_TILE) if p % t == 0)


def _moe(h2, x1, mod, ctx_row, lw, layer, router_w, router_b, final_g, n_lat, n_rows, tm, apply_final_norm):
    nb, _, d = x1.shape
    n_steps = N_EXPERTS // EXPERTS_PER_STEP
    ff = lw["w_e_gate"].shape[-1]
    tok = pl.BlockSpec((None, tm, d), lambda b, i, c: (b, i, 0))
    kern = functools.partial(_moe_kernel, apply_final_norm=apply_final_norm, n_lat=n_lat)
    return pl.pallas_call(
        kern,
        grid=(nb, n_rows // tm, n_steps),
        in_specs=[
            tok, tok,
            _mod_spec(mod, layer, lambda b, i: b),
            _mod_spec(mod, layer, lambda b, i: ctx_row),
            pl.BlockSpec(router_w.shape, lambda b, i, c: (0, 0)),
            pl.BlockSpec(router_b.shape, lambda b, i, c: (0, 0)),
            pl.BlockSpec(final_g.shape, lambda b, i, c: (0, 0)),
            pl.BlockSpec((None, EXPERTS_PER_STEP, d, ff), lambda b, i, c: (layer, c, 0, 0)),
            pl.BlockSpec((None, EXPERTS_PER_STEP, d, ff), lambda b, i, c: (layer, c, 0, 0)),
            pl.BlockSpec((None, EXPERTS_PER_STEP, ff, d), lambda b, i, c: (layer, c, 0, 0)),
        ],
        out_specs=tok,
        out_shape=jax.ShapeDtypeStruct((nb, n_rows, d), F32),
        scratch_shapes=[pltpu.VMEM((n_steps, tm, LANES), F32), pltpu.VMEM((tm, d), F32)],
        compiler_params=_cparams(3),
        name="moe",
    )(h2, x1, mod, mod, router_w, router_b, final_g, lw["w_e_gate"], lw["w_e_up"], lw["w_e_down"])


def _rotate_cols(w):
    half = w.shape[-1] // 2
    return jnp.concatenate([-w[..., half:], w[..., :half]], axis=-1)


def _prepare_weights(w_in, q_norm_g, w_q_up, kv_norm_g, w_kv_up, norm1_g, norm2_g, w_gate, b_gate, w_br_mla,
                     w_br_na, w_br_ft, w_out, w_e_gate, w_e_up, w_e_down):
    n_layers, d, _ = w_in.shape
    splits = np.cumsum([MLA_Q_LORA, MLA_KV_LORA, MLA_ROPE, NA_WIDTH, NA_WIDTH, NA_WIDTH])
    w_q, w_kv, w_kr, w_nq, w_nk, w_nv, w_ft = jnp.split(w_in, [int(s) for s in splits], axis=-1)

    def in_rope_slot(w):
        zeros_lo = jnp.zeros(w.shape[:-1] + (MLA_NOPE,), w.dtype)
        zeros_hi = jnp.zeros(w.shape[:-1] + (HEAD_PAD - MLA_NOPE - MLA_ROPE,), w.dtype)
        return jnp.concatenate([zeros_lo, w, zeros_hi], axis=-1)

    win = jnp.concatenate([w_q, w_kv, in_rope_slot(w_kr), in_rope_slot(_rotate_cols(w_kr)), w_nq, w_nk, w_nv, w_ft],
                          axis=-1).astype(BF16)
    wq = w_q_up.reshape(n_layers, MLA_Q_LORA, MLA_HEADS, MLA_NOPE + MLA_ROPE)
    wq_nope, wq_pe = wq[..., :MLA_NOPE], wq[..., MLA_NOPE:]
    pad = jnp.zeros(wq_pe.shape[:-1] + (HEAD_PAD - MLA_NOPE - MLA_ROPE,), wq.dtype)
    wq_main = jnp.concatenate([wq_nope, wq_pe, pad], axis=-1)
    wq_rot = jnp.concatenate([jnp.zeros_like(wq_nope), _rotate_cols(wq_pe), pad], axis=-1)
    wkv = w_kv_up.reshape(n_layers, MLA_KV_LORA, MLA_HEADS, MLA_NOPE + MLA_V)
    wk_nope, wv = wkv[..., :MLA_NOPE], wkv[..., MLA_NOPE:]
    wkv_k = jnp.concatenate([wk_nope, jnp.zeros(wk_nope.shape[:-1] + (HEAD_PAD - MLA_NOPE,), wkv.dtype)], axis=-1)
    wide = MLA_HEADS * HEAD_PAD
    return {
        "w_in": win,
        "wq_main": wq_main.reshape(n_layers, MLA_Q_LORA, wide).astype(BF16),
        "wq_rot": wq_rot.reshape(n_layers, MLA_Q_LORA, wide).astype(BF16),
        "wkv_k": wkv_k.reshape(n_layers, MLA_KV_LORA, wide).astype(BF16),
        "wkv_v": jnp.concatenate([wv, jnp.zeros(wv.shape[:-1] + (HEAD_PAD - MLA_V,), wv.dtype)], axis=-1)
        .reshape(n_layers, MLA_KV_LORA, wide).astype(BF16),
        "q_norm_g": q_norm_g.reshape(n_layers, 1, -1),
        "kv_norm_g": kv_norm_g.reshape(n_layers, 1, -1),
        "norm1_g": norm1_g.reshape(n_layers, 1, d),
        "norm2_g": norm2_g.reshape(n_layers, 1, d),
        "w_gate": w_gate.astype(BF16),
        "b_gate": b_gate.reshape(n_layers, 1, -1),
        "w_br_mla": jnp.pad(w_br_mla.reshape(n_layers, MLA_HEADS, MLA_V, d), ((0, 0), (0, 0), (0, HEAD_PAD - MLA_V), (0, 0)))
        .reshape(n_layers, wide, d).astype(BF16),
        "w_br_na": w_br_na.astype(BF16),
        "w_br_ft": w_br_ft.astype(BF16),
        "w_out": w_out.astype(BF16),
        "w_e_gate": w_e_gate.astype(BF16),
        "w_e_up": w_e_up.astype(BF16),
        "w_e_down": w_e_down.astype(BF16),
    }


def _rope_tables(s, c):
    n_freq = MLA_ROPE // 4
    inv_freq = ROPE_THETA ** (-jnp.arange(n_freq, dtype=F32) / n_freq)
    t = jnp.arange(s, dtype=jnp.int32)
    row = (t // GRID_W).astype(F32)
    col = (t % GRID_W).astype(F32)
    ang = jnp.concatenate([row[:, None] * inv_freq, col[:, None] * inv_freq], axis=-1)
    cos, sin = jnp.cos(ang), jnp.sin(ang)
    pad = HEAD_PAD - MLA_NOPE - MLA_ROPE
    cos_lat = jnp.concatenate([jnp.ones((s, MLA_NOPE), F32), cos, cos, jnp.zeros((s, pad), F32)], axis=-1)
    sin_lat = jnp.concatenate([jnp.zeros((s, MLA_NOPE), F32), sin, sin, jnp.zeros((s, pad), F32)], axis=-1)
    cos_ctx = jnp.concatenate([jnp.ones((c, MLA_NOPE + MLA_ROPE), F32), jnp.zeros((c, pad), F32)], axis=-1)
    sin_ctx = jnp.zeros((c, HEAD_PAD), F32)
    return cos_lat, sin_lat, cos_ctx, sin_ctx


@jax.jit
def _forward(x, c, ctx, c_ctx, w_ada, b_ada, norm1_g, norm2_g, w_in, q_norm_g, w_q_up, kv_norm_g, w_kv_up,
             na_rpb, w_gate, b_gate, w_br_mla, w_br_na, w_br_ft, w_out, router_w, router_bias, w_e_gate,
             w_e_up, w_e_down, final_norm_g):
    nb, s, d = x.shape
    n_ctx = ctx.shape[1]
    n_layers = w_ada.shape[0]
    grid_rows = s // GRID_W

    mod_rows = 8
    c_rows = jnp.zeros((mod_rows, d), F32).at[:nb].set(c).at[nb].set(c_ctx)
    mod_all = _modulation(c_rows, w_ada, b_ada).reshape(n_layers, mod_rows, 1, 6 * d)

    weights = _prepare_weights(w_in, q_norm_g, w_q_up, kv_norm_g, w_kv_up, norm1_g, norm2_g, w_gate, b_gate,
                               w_br_mla, w_br_na, w_br_ft, w_out, w_e_gate, w_e_up, w_e_down)
    na_bias = _na_bias_tables(na_rpb)
    na_row_mask = _na_row_masks(grid_rows)
    cos_lat, sin_lat, cos_ctx, sin_ctx = _rope_tables(s, n_ctx)
    cos_t = jnp.concatenate([cos_lat, cos_ctx], axis=0)
    sin_t = jnp.concatenate([sin_lat, sin_ctx], axis=0)
    router_w_p = router_w.T.astype(BF16)
    router_b_p = router_bias.reshape(N_EXPERTS, 1).astype(F32)
    final_g = final_norm_g.reshape(1, d)

    p = s + n_ctx
    n_lat_tiles = s // TOKEN_TILE
    ctx_row = nb

    def tile_row(b, i):
        return jnp.where(i >= n_lat_tiles, ctx_row, b)

    x_all = jnp.concatenate([x, ctx], axis=1)
    zero_ctx = [jnp.zeros((nb, n_ctx, w), BF16) for w in (MLA_HEADS * HEAD_PAD, NA_WIDTH, FT_WIDTH)]
    for l in range(n_layers):
        last = l == n_layers - 1
        q, k, v, nq, nk, nv, ft = _project(x_all, mod_all, tile_row, weights, l, cos_t, sin_t)
        lat_outs = [_mla_attention(q, k, v, s), _na_attention(nq, nk, nv, na_bias, na_row_mask, l, s),
                    _fourier_lat(ft, s)]
        if last:
            ctx_outs = zero_ctx
        else:
            ctx_outs = [_mla_ctx_attention(q, k, v, s), _na_ctx_attention(nq, nk, nv, s), _fourier_ctx(ft, s)]
        x1, h2 = _merge(x_all, lat_outs, ctx_outs, mod_all, tile_row, weights, l)
        if last:
            return _moe(h2, x1, mod_all, ctx_row, weights, l, router_w_p, router_b_p, final_g, s, s,
                        min(MOE_LAT_TILE, s), True)
        x_all = _moe(h2, x1, mod_all, ctx_row, weights, l, router_w_p, router_b_p, final_g, s, p,
                     _moe_tile(p), False)


def kernel(x, c, ctx, c_ctx, w_ada, b_ada, norm1_g, norm2_g, w_in, q_norm_g, w_q_up, kv_norm_g, w_kv_up, na_rpb, w_gate, b_gate, w_br_mla, w_br_na, w_br_ft, w_out, router_w, router_bias, w_e_gate, w_e_up, w_e_down, final_norm_g):
    return _forward(x, c, ctx, c_ctx, w_ada, b_ada, norm1_g, norm2_g, w_in, q_norm_g, w_q_up, kv_norm_g,
                    w_kv_up, na_rpb, w_gate, b_gate, w_br_mla, w_br_na, w_br_ft, w_out, router_w, router_bias,
                    w_e_gate, w_e_up, w_e_down, final_norm_g)
```

```python
import functools
import math

import jax
import jax.numpy as jnp
import numpy as np
from jax import lax
from jax.experimental import pallas as pl
from jax.experimental.pallas import tpu as pltpu

F32 = jnp.float32
BF16 = jnp.bfloat16

GRID_W = 64
MLA_HEADS = 8
MLA_NOPE = 64
MLA_ROPE = 32
MLA_V = 64
MLA_Q_LORA = 256
MLA_KV_LORA = 128
MLA_SCALE = (MLA_NOPE + MLA_ROPE) ** -0.5
NA_HEADS = 4
NA_HEAD_DIM = 64
NA_WIN_R = 8
NA_WIN_C = 16
NA_SCALE = NA_HEAD_DIM ** -0.5
NA_WIDTH = NA_HEADS * NA_HEAD_DIM
FT_GROUPS = 4
FT_GROUP_DIM = 64
FT_WIDTH = FT_GROUPS * FT_GROUP_DIM
N_EXPERTS = 16
N_GROUPS = 4
EXPERTS_PER_GROUP = N_EXPERTS // N_GROUPS
ROPE_THETA = 10000.0
NORM_EPS = 1e-6
MASK_VALUE = -1e30
LOG2E = math.log2(math.e)

LANES = 128
V7X_VMEM_LIMIT_BYTES = 56 * 1024 * 1024

HEAD_PAD = LANES
NA_ROWS_PER_BLOCK = 8
NA_KEY_ROWS = 16
NA_BIAS_SHIFT = (NA_KEY_ROWS - NA_ROWS_PER_BLOCK) + (NA_ROWS_PER_BLOCK - 1) - (NA_WIN_R - 1)
NA_BIAS_TILES = (NA_WIN_R - 1) + NA_BIAS_SHIFT + (NA_KEY_ROWS - 2) + 1
FT_K1_PER_STEP = 8
EXPERTS_PER_STEP = 4
MOE_LAT_TILE = 1024
TOKEN_TILE = 256
MLA_MAX_KV_CHUNK = 2816


def _cparams(n_axes):
    return pltpu.CompilerParams(
        dimension_semantics=("arbitrary",) * n_axes,
        vmem_limit_bytes=V7X_VMEM_LIMIT_BYTES,
    )


def _rms(x, g):
    return x * lax.rsqrt(jnp.mean(x * x, axis=-1, keepdims=True) + NORM_EPS) * g


def _dot(a, b):
    return jnp.dot(a, b, preferred_element_type=F32)


def _dot_nt(a, b):
    return lax.dot_general(a, b, (((1,), (1,)), ((), ())), preferred_element_type=F32)


def _mod_kernel(c_ref, w_ref, b_ref, o_ref):
    c = c_ref[...]
    o_ref[...] = _dot(c * jax.nn.sigmoid(c), w_ref[...]) + b_ref[...]


def _modulation(c_rows, w_ada, b_ada):
    n_layers, d, width = w_ada.shape
    rows = c_rows.shape[0]
    tn = 1536
    return pl.pallas_call(
        _mod_kernel,
        grid=(n_layers, width // tn),
        in_specs=[
            pl.BlockSpec((rows, d), lambda l, j: (0, 0)),
            pl.BlockSpec((None, d, tn), lambda l, j: (l, 0, j)),
            pl.BlockSpec((None, 1, tn), lambda l, j: (l, 0, j)),
        ],
        out_specs=pl.BlockSpec((None, rows, tn), lambda l, j: (l, 0, j)),
        out_shape=jax.ShapeDtypeStruct((n_layers, rows, width), F32),
        compiler_params=_cparams(2),
        name="modulation",
    )(c_rows, w_ada, b_ada.reshape(n_layers, 1, width))


def _proj_kernel(x_ref, mod_ref, n1g_ref, win_ref, qg_ref, wqm_ref, wqr_ref, kvg_ref, wkk_ref, wkv_ref,
                 cos_ref, sin_ref, q_ref, k_ref, v_ref, nq_ref, nk_ref, nv_ref, ft_ref):
    d = x_ref.shape[-1]
    x = x_ref[...]
    mod = mod_ref[...]
    sh1, sc1 = mod[:, 0:d], mod[:, d:2 * d]
    h = _rms(x, n1g_ref[...]) * (1.0 + sc1) + sh1
    p = _dot(h.astype(BF16), win_ref[...])
    cos = cos_ref[...]
    sin = sin_ref[...]
    o = 0
    q_lat = p[:, o:o + MLA_Q_LORA]
    o += MLA_Q_LORA
    kv_lat = p[:, o:o + MLA_KV_LORA]
    o += MLA_KV_LORA
    kr_a = p[:, o:o + HEAD_PAD]
    o += HEAD_PAD
    kr_b = p[:, o:o + HEAD_PAD]
    o += HEAD_PAD
    nq_ref[...] = (p[:, o:o + NA_WIDTH] * NA_SCALE).astype(BF16)
    o += NA_WIDTH
    nk_ref[...] = p[:, o:o + NA_WIDTH].astype(BF16)
    o += NA_WIDTH
    nv_ref[...] = p[:, o:o + NA_WIDTH].astype(BF16)
    o += NA_WIDTH
    ft_ref[...] = p[:, o:o + FT_WIDTH].astype(BF16)

    qn = _rms(q_lat, qg_ref[...]).astype(BF16)
    qm = _dot(qn, wqm_ref[...])
    qr = _dot(qn, wqr_ref[...])
    kvn = _rms(kv_lat, kvg_ref[...]).astype(BF16)
    kn = _dot(kvn, wkk_ref[...])
    v = _dot(kvn, wkv_ref[...])
    v_lane = lax.broadcasted_iota(jnp.int32, v.shape, 1) & (HEAD_PAD - 1)
    v_ref[...] = jnp.where(v_lane == MLA_V, 1.0, v).astype(BF16)
    kr = kr_a * cos + kr_b * sin
    q_scale = MLA_SCALE * LOG2E
    for hd in range(MLA_HEADS):
        sl = slice(hd * HEAD_PAD, (hd + 1) * HEAD_PAD)
        q_ref[:, sl] = ((qm[:, sl] * cos + qr[:, sl] * sin) * q_scale).astype(BF16)
        k_ref[:, sl] = (kn[:, sl] + kr).astype(BF16)


def _layer_spec(a, layer):
    return pl.BlockSpec((None,) + a.shape[1:], lambda *_: (layer,) + (0,) * (a.ndim - 1))


def _mod_spec(mod, layer, mod_row):
    return pl.BlockSpec((None, None, 1, mod.shape[-1]), lambda b, i, *_: (layer, mod_row(b, i), 0, 0))


def _project(x, mod, mod_row, lw, layer, cos_t, sin_t):
    nb, n, d = x.shape
    tm = TOKEN_TILE
    win = lw["w_in"]
    wide = MLA_HEADS * HEAD_PAD

    def full(a):
        return _layer_spec(a, layer)

    def tok(width):
        return pl.BlockSpec((None, tm, width), lambda b, i: (b, i, 0))

    outs = [wide, wide, wide, NA_WIDTH, NA_WIDTH, NA_WIDTH, FT_WIDTH]
    return pl.pallas_call(
        _proj_kernel,
        grid=(nb, n // tm),
        in_specs=[
            tok(d),
            _mod_spec(mod, layer, mod_row),
            full(lw["norm1_g"]), full(win), full(lw["q_norm_g"]), full(lw["wq_main"]), full(lw["wq_rot"]),
            full(lw["kv_norm_g"]), full(lw["wkv_k"]), full(lw["wkv_v"]),
            pl.BlockSpec((tm, HEAD_PAD), lambda b, i: (i, 0)),
            pl.BlockSpec((tm, HEAD_PAD), lambda b, i: (i, 0)),
        ],
        out_specs=[tok(w) for w in outs],
        out_shape=[jax.ShapeDtypeStruct((nb, n, w), BF16) for w in outs],
        compiler_params=_cparams(2),
        name="project",
    )(x, mod, lw["norm1_g"], win, lw["q_norm_g"], lw["wq_main"], lw["wq_rot"], lw["kv_norm_g"],
      lw["wkv_k"], lw["wkv_v"], cos_t, sin_t)


def _kv_chunk(p):
    best = LANES
    for n in range(1, p // LANES + 1):
        tk = p // n
        if p % n == 0 and tk % LANES == 0 and tk <= MLA_MAX_KV_CHUNK:
            best = max(best, tk)
    return best


def _store_head_values(o_ref, rows, hd, out):
    if hd:
        out = pltpu.roll(out, MLA_V, axis=1)
    sl = slice(hd * MLA_V, (hd + 1) * MLA_V)
    o_ref[rows, sl] = out[:, sl].astype(BF16)


def _mla_kernel(q_ref, k_ref, v_ref, o_ref, s0_scr, s1_scr, p0_scr, p1_scr, mx_scr, m_scr, alpha_scr, acc_scr,
                *, tq, tk, n_q, n_c):
    def head(hd):
        return slice(hd * HEAD_PAD, (hd + 1) * HEAD_PAD)

    def q_rows(qt):
        return pl.ds(pl.multiple_of(qt * tq, tq), tq)

    def k_rows(c):
        return pl.ds(pl.multiple_of(c * tk, tk), tk)

    def scores(hd, qt, c, s_scr):
        s = _dot_nt(q_ref[q_rows(qt), head(hd)], k_ref[k_rows(c), head(hd)])
        s_scr[...] = s
        mx_scr[hd] = jnp.broadcast_to(jnp.max(s, axis=1, keepdims=True), mx_scr.shape[1:])

    def softmax(hd, s_scr, p_scr, first):
        m_prev = jnp.where(first, MASK_VALUE, m_scr[hd])
        m_new = jnp.maximum(m_prev, mx_scr[hd])
        alpha_scr[hd] = jnp.exp2(m_prev - m_new)
        m_scr[hd] = m_new
        p_scr[...] = jnp.exp2(s_scr[...] - jnp.tile(m_new, (1, tk // LANES))).astype(BF16)

    def accumulate(hd, p_scr, qt, c):
        acc = alpha_scr[hd] * acc_scr[hd] + _dot(p_scr[...], v_ref[k_rows(c), head(hd)])
        acc_scr[hd] = acc
        _store_head_values(o_ref, q_rows(qt), hd, acc / acc[:, MLA_V:MLA_V + 1])

    lane = lax.broadcasted_iota(jnp.int32, acc_scr.shape, 2)
    acc_scr[...] = jnp.where(lane == MLA_V, 1.0, 0.0)
    m_scr[...] = jnp.full(m_scr.shape, MASK_VALUE, F32)
    alpha_scr[...] = jnp.ones(alpha_scr.shape, F32)
    p1_scr[...] = jnp.zeros(p1_scr.shape, BF16)
    scores(0, 0, 0, s0_scr)

    def body(i, carry):
        qt, c = carry
        first = c == 0
        last = c == n_c - 1
        c_prev = jnp.where(first, n_c - 1, c - 1)
        qt_prev = jnp.maximum(jnp.where(first, qt - 1, qt), 0)
        c_next = jnp.where(last, 0, c + 1)
        qt_next = jnp.where(last, qt + 1, qt)
        scores(1, qt, c, s1_scr)
        softmax(0, s0_scr, p0_scr, first)
        accumulate(1, p1_scr, qt_prev, c_prev)
        scores(0, jnp.minimum(qt_next, n_q - 1), c_next, s0_scr)
        softmax(1, s1_scr, p1_scr, first)
        accumulate(0, p0_scr, qt, c)
        return qt_next, c_next

    lax.fori_loop(0, n_q * n_c, body, (jnp.int32(0), jnp.int32(0)))
    accumulate(1, p1_scr, n_q - 1, n_c - 1)


def _mla_attention(q, k_all, v_all, s):
    nb, p, _ = k_all.shape
    tq = min(512, s)
    tk = _kv_chunk(p)
    pairs = MLA_HEADS // 2
    kern = functools.partial(_mla_kernel, tq=tq, tk=tk, n_q=s // tq, n_c=p // tk)
    return pl.pallas_call(
        kern,
        grid=(nb, pairs),
        in_specs=[
            pl.BlockSpec((None, s, 2 * HEAD_PAD), lambda b, hp: (b, 0, hp)),
            pl.BlockSpec((None, p, 2 * HEAD_PAD), lambda b, hp: (b, 0, hp)),
            pl.BlockSpec((None, p, 2 * HEAD_PAD), lambda b, hp: (b, 0, hp)),
        ],
        out_specs=pl.BlockSpec((None, s, 2 * MLA_V), lambda b, hp: (b, 0, hp)),
        out_shape=jax.ShapeDtypeStruct((nb, s, MLA_HEADS * MLA_V), BF16),
        scratch_shapes=[
            pltpu.VMEM((tq, tk), F32),
            pltpu.VMEM((tq, tk), F32),
            pltpu.VMEM((tq, tk), BF16),
            pltpu.VMEM((tq, tk), BF16),
            pltpu.VMEM((2, tq, LANES), F32),
            pltpu.VMEM((2, tq, LANES), F32),
            pltpu.VMEM((2, tq, LANES), F32),
            pltpu.VMEM((2, tq, HEAD_PAD), F32),
        ],
        compiler_params=_cparams(2),
        name="mla_attention",
    )(q, k_all, v_all)


def _mla_ctx_kernel(q_ref, k_ref, v_ref, o_ref):
    for hd in range(2):
        sl = slice(hd * HEAD_PAD, (hd + 1) * HEAD_PAD)
        s = _dot_nt(q_ref[:, sl], k_ref[:, sl])
        p = jnp.exp2(s - jnp.max(s, axis=1, keepdims=True))
        acc = _dot(p.astype(BF16), v_ref[:, sl])
        _store_head_values(o_ref, slice(None), hd, acc / acc[:, MLA_V:MLA_V + 1])


def _mla_ctx_attention(q, k, v, s):
    nb, p, _ = q.shape
    c = p - s
    pairs = MLA_HEADS // 2
    spec = pl.BlockSpec((None, c, 2 * HEAD_PAD), lambda b, hp: (b, s // c, hp))
    return pl.pallas_call(
        _mla_ctx_kernel,
        grid=(nb, pairs),
        in_specs=[spec, spec, spec],
        out_specs=pl.BlockSpec((None, c, 2 * MLA_V), lambda b, hp: (b, 0, hp)),
        out_shape=jax.ShapeDtypeStruct((nb, c, MLA_HEADS * MLA_V), BF16),
        compiler_params=_cparams(2),
        name="mla_ctx_attention",
    )(q, k, v)


def _na_kernel(q_ref, k_ref, v_ref, kc_ref, vc_ref, bias_ref, rowmask_ref, o_ref, comb_scr, s0_scr, s1_scr, p0_scr,
               p1_scr, mx_scr, l_scr, *, grid_rows, n_blocks):
    tq = NA_ROWS_PER_BLOCK * GRID_W
    n_keys = NA_KEY_ROWS * GRID_W

    for hd in range(2):
        for var, row_off in enumerate((0, -(NA_WIN_R // 2), -(NA_KEY_ROWS - NA_ROWS_PER_BLOCK))):
            tile0 = row_off + (NA_WIN_R - 1) + NA_BIAS_SHIFT
            bias = jnp.concatenate(
                [jnp.concatenate([bias_ref[hd, tile0 + 2 * j - qr] for j in range(NA_KEY_ROWS // 2)], axis=1)
                 for qr in range(NA_ROWS_PER_BLOCK)], axis=0)
            comb_scr[hd, var] = bias + rowmask_ref[var].astype(F32)

    lane = lax.broadcasted_iota(jnp.int32, (tq, LANES), 1)

    def q_rows(rb):
        return pl.ds(pl.multiple_of(rb * tq, tq), tq)

    def key_rows(rb):
        start_row = jnp.clip(rb * NA_ROWS_PER_BLOCK - NA_WIN_R // 2, 0, grid_rows - NA_KEY_ROWS)
        return pl.ds(pl.multiple_of(start_row * GRID_W, (NA_WIN_R // 2) * GRID_W), n_keys)

    def scores(hd, rb, s_scr):
        var = jnp.where(rb == 0, 0, jnp.where(rb == n_blocks - 1, 2, 1))
        q = q_ref[q_rows(rb), :]
        in_head = (lane >= NA_HEAD_DIM) if hd else (lane < NA_HEAD_DIM)
        qh = jnp.where(in_head, q, jnp.zeros_like(q))
        s = _dot_nt(qh, k_ref[key_rows(rb), :]) + comb_scr[hd, var]
        sc = _dot_nt(qh, kc_ref[...])
        s_scr[:, :n_keys] = s
        s_scr[:, n_keys:] = sc
        m = jnp.maximum(jnp.max(s, axis=1, keepdims=True), jnp.max(sc, axis=1, keepdims=True))
        mx_scr[hd] = jnp.broadcast_to(m, mx_scr.shape[1:])

    def softmax(hd, s_scr, p_scr):
        p = jnp.exp(s_scr[...] - jnp.tile(mx_scr[hd], (1, s_scr.shape[1] // LANES)))
        l_scr[hd] = jnp.broadcast_to(jnp.sum(p, axis=1, keepdims=True), l_scr.shape[1:])
        p_scr[...] = p.astype(BF16)

    def accumulate(hd, p_scr, rb):
        acc = _dot(p_scr[:, :n_keys], v_ref[key_rows(rb), :]) + _dot(p_scr[:, n_keys:], vc_ref[...])
        out = (acc / l_scr[hd]).astype(BF16)
        sl = slice(hd * NA_HEAD_DIM, (hd + 1) * NA_HEAD_DIM)
        o_ref[q_rows(rb), sl] = out[:, sl]

    p1_scr[...] = jnp.zeros(p1_scr.shape, BF16)
    l_scr[...] = jnp.ones(l_scr.shape, F32)
    scores(0, 0, s0_scr)

    def body(rb, carry):
        scores(1, rb, s1_scr)
        softmax(0, s0_scr, p0_scr)
        accumulate(1, p1_scr, jnp.maximum(rb - 1, 0))
        scores(0, jnp.minimum(rb + 1, n_blocks - 1), s0_scr)
        softmax(1, s1_scr, p1_scr)
        accumulate(0, p0_scr, rb)
        return carry

    lax.fori_loop(0, n_blocks, body, 0)
    accumulate(1, p1_scr, n_blocks - 1)


def _na_attention(q, k, v, bias, row_mask, layer, s):
    nb, p, _ = q.shape
    c = p - s
    grid_rows = s // GRID_W
    tq = NA_ROWS_PER_BLOCK * GRID_W
    n_blocks = grid_rows // NA_ROWS_PER_BLOCK
    n_keys = NA_KEY_ROWS * GRID_W
    pairs = NA_HEADS // 2
    kern = functools.partial(_na_kernel, grid_rows=grid_rows, n_blocks=n_blocks)
    lat = pl.BlockSpec((None, s, LANES), lambda b, hp: (b, 0, hp))
    ctx = pl.BlockSpec((None, c, LANES), lambda b, hp: (b, s // c, hp))
    return pl.pallas_call(
        kern,
        grid=(nb, pairs),
        in_specs=[
            lat, lat, lat, ctx, ctx,
            pl.BlockSpec((None, 2) + bias.shape[2:], lambda b, hp: (layer, hp, 0, 0, 0)),
            pl.BlockSpec(row_mask.shape, lambda b, hp: (0, 0, 0)),
        ],
        out_specs=pl.BlockSpec((None, s, LANES), lambda b, hp: (b, 0, hp)),
        out_shape=jax.ShapeDtypeStruct((nb, s, NA_WIDTH), BF16),
        scratch_shapes=[
            pltpu.VMEM((2, 3, tq, n_keys), F32),
            pltpu.VMEM((tq, n_keys + c), F32),
            pltpu.VMEM((tq, n_keys + c), F32),
            pltpu.VMEM((tq, n_keys + c), BF16),
            pltpu.VMEM((tq, n_keys + c), BF16),
            pltpu.VMEM((2, tq, LANES), F32),
            pltpu.VMEM((2, tq, LANES), F32),
        ],
        compiler_params=_cparams(2),
        name="na_attention",
    )(q, k, v, k, v, bias, row_mask)


def _na_ctx_kernel(q_ref, k_ref, v_ref, o_ref):
    q = q_ref[...]
    k = k_ref[...]
    v = v_ref[...]
    lane = lax.broadcasted_iota(jnp.int32, q.shape, 1)
    outs = []
    for hd in range(2):
        in_head = (lane >= NA_HEAD_DIM) if hd else (lane < NA_HEAD_DIM)
        qh = jnp.where(in_head, q, jnp.zeros_like(q))
        s = _dot_nt(qh, k)
        p = jnp.exp(s - jnp.max(s, axis=1, keepdims=True))
        outs.append(_dot(p.astype(BF16), v) / jnp.sum(p, axis=1, keepdims=True))
    o_ref[...] = jnp.where(lane < NA_HEAD_DIM, outs[0], outs[1]).astype(BF16)


def _na_ctx_attention(q, k, v, s):
    nb, p, _ = q.shape
    c = p - s
    spec = pl.BlockSpec((None, c, LANES), lambda b, hp: (b, s // c, hp))
    return pl.pallas_call(
        _na_ctx_kernel,
        grid=(nb, NA_HEADS // 2),
        in_specs=[spec, spec, spec],
        out_specs=pl.BlockSpec((None, c, LANES), lambda b, hp: (b, 0, hp)),
        out_shape=jax.ShapeDtypeStruct((nb, c, NA_WIDTH), BF16),
        compiler_params=_cparams(2),
        name="na_ctx_attention",
    )(q, k, v)


def _na_row_masks(grid_rows):
    qr = np.arange(NA_ROWS_PER_BLOCK)[:, None]
    kr = np.arange(NA_KEY_ROWS)[None, :]
    row_valid = np.zeros((3, NA_ROWS_PER_BLOCK, NA_KEY_ROWS), bool)
    n_blocks = grid_rows // NA_ROWS_PER_BLOCK
    for vi, rb in enumerate((0, 1, n_blocks - 1)):
        r = rb * NA_ROWS_PER_BLOCK + qr
        start_row = min(max(rb * NA_ROWS_PER_BLOCK - NA_WIN_R // 2, 0), grid_rows - NA_KEY_ROWS)
        key_row = start_row + kr
        win = np.clip(r - NA_WIN_R // 2, 0, grid_rows - NA_WIN_R)
        row_valid[vi] = (key_row >= win) & (key_row < win + NA_WIN_R)
    mask = np.where(row_valid, 0.0, MASK_VALUE).astype(np.float32)
    mask = np.broadcast_to(mask[:, :, None, :, None], (3, NA_ROWS_PER_BLOCK, GRID_W, NA_KEY_ROWS, GRID_W))
    return jnp.asarray(mask.reshape(3, NA_ROWS_PER_BLOCK * GRID_W, NA_KEY_ROWS * GRID_W), BF16)


def _na_bias_tables(rpb):
    n_dc = 2 * NA_WIN_C - 1
    qc = np.arange(GRID_W)[:, None]
    kcol = np.arange(GRID_W)[None, :]
    wstart = np.clip(qc - NA_WIN_C // 2, 0, GRID_W - NA_WIN_C)
    col_valid = (kcol >= wstart) & (kcol < wstart + NA_WIN_C)
    dc = np.clip(kcol - qc, -(NA_WIN_C - 1), NA_WIN_C - 1) + (NA_WIN_C - 1)
    col_onehot = ((dc[..., None] == np.arange(n_dc)) & col_valid[..., None]).astype(np.float32)
    t = jnp.einsum("lhij,cdj->lhicd", rpb.astype(F32), jnp.asarray(col_onehot), precision=lax.Precision.HIGHEST)
    t = jnp.where(jnp.asarray(col_valid), t, MASK_VALUE)
    n_dr = t.shape[2]
    t = jnp.pad(t, ((0, 0), (0, 0), (NA_BIAS_SHIFT, NA_BIAS_TILES + 1 - NA_BIAS_SHIFT - n_dr), (0, 0), (0, 0)),
                constant_values=MASK_VALUE)
    return jnp.concatenate([t[:, :, :NA_BIAS_TILES], t[:, :, 1:NA_BIAS_TILES + 1]], axis=-1)


FT_N1 = 64


def _ft_stage1_kernel(w_ref, x_ref, o_ref):
    o_ref[...] = _dot(w_ref[...], x_ref[...])


def _ft_stage2_kernel(are_ref, aim_ref, tc_ref, ts_ref, w_ref, cd_ref, sd_ref, o_ref, *, scale):
    n2 = are_ref.shape[1]
    for g in range(are_ref.shape[0]):
        are = are_ref[g]
        aim = aim_ref[g]
        tc = jnp.concatenate([tc_ref[g]] * (FT_WIDTH // LANES), axis=1)
        ts = jnp.concatenate([ts_ref[g]] * (FT_WIDTH // LANES), axis=1)
        bre = are * tc + aim * ts
        bim = aim * tc - are * ts
        bb = jnp.concatenate([bre, bim], axis=0).astype(BF16)
        xk = _dot(w_ref[...], bb)
        f = _dot(xk[:n2].astype(BF16), cd_ref[...]) + _dot(xk[n2:].astype(BF16), sd_ref[...])
        o_ref[:, g * FT_WIDTH:(g + 1) * FT_WIDTH] = (f * scale).astype(BF16)


def _dft_mats(n):
    ang = 2.0 * np.pi * np.outer(np.arange(n), np.arange(n)) / n
    return np.cos(ang), np.sin(ang)


def _channel_dft():
    c, s = _dft_mats(FT_GROUP_DIM)
    eye = np.eye(FT_GROUPS)
    return jnp.asarray(np.kron(eye, c), BF16), jnp.asarray(np.kron(eye, s), BF16)


def _fourier_lat(u, n):
    nb, p, _ = u.shape
    n2 = n // FT_N1
    cols = n2 * FT_WIDTH
    c1, s1 = _dft_mats(FT_N1)
    w1 = jnp.asarray(np.concatenate([c1, -s1], axis=0), BF16)
    tcol = min(2048, cols)
    a = pl.pallas_call(
        _ft_stage1_kernel,
        grid=(nb, cols // tcol),
        in_specs=[
            pl.BlockSpec(w1.shape, lambda b, j: (0, 0)),
            pl.BlockSpec((None, FT_N1, tcol), lambda b, j: (b, 0, j)),
        ],
        out_specs=pl.BlockSpec((None, 2 * FT_N1, tcol), lambda b, j: (b, 0, j)),
        out_shape=jax.ShapeDtypeStruct((nb, 2 * FT_N1, cols), F32),
        compiler_params=_cparams(2),
        name="fourier_stage1",
    )(w1, u.reshape(nb, p // n2, cols))
    a = a.reshape(nb, 2 * FT_N1, n2, FT_WIDTH)

    ang = 2.0 * np.pi * np.outer(np.arange(FT_N1), np.arange(n2)) / n
    tc = jnp.asarray(np.repeat(np.cos(ang)[:, :, None], LANES, axis=2), F32)
    ts = jnp.asarray(np.repeat(np.sin(ang)[:, :, None], LANES, axis=2), F32)
    c3, s3 = _dft_mats(n2)
    w3 = jnp.asarray(np.block([[c3, s3], [-s3, c3]]), BF16)
    cd, sd = _channel_dft()
    kern = functools.partial(_ft_stage2_kernel, scale=1.0 / math.sqrt(n * FT_GROUP_DIM))
    g = FT_K1_PER_STEP
    n_steps = FT_N1 // g
    out = pl.pallas_call(
        kern,
        grid=(nb, n_steps),
        in_specs=[
            pl.BlockSpec((None, g, n2, FT_WIDTH), lambda b, k: (b, k, 0, 0)),
            pl.BlockSpec((None, g, n2, FT_WIDTH), lambda b, k: (b, n_steps + k, 0, 0)),
            pl.BlockSpec((g, n2, LANES), lambda b, k: (k, 0, 0)),
            pl.BlockSpec((g, n2, LANES), lambda b, k: (k, 0, 0)),
            pl.BlockSpec(w3.shape, lambda b, k: (0, 0)),
            pl.BlockSpec(cd.shape, lambda b, k: (0, 0)),
            pl.BlockSpec(sd.shape, lambda b, k: (0, 0)),
        ],
        out_specs=pl.BlockSpec((None, n2, g * FT_WIDTH), lambda b, k: (b, 0, k)),
        out_shape=jax.ShapeDtypeStruct((nb, n2, FT_N1 * FT_WIDTH), BF16),
        compiler_params=_cparams(2),
        name="fourier_stage2",
    )(a, a, tc, ts, w3, cd, sd)
    return out.reshape(nb, n, FT_WIDTH)


def _ft_ctx_kernel(u_ref, cn_ref, sn_ref, cd_ref, sd_ref, o_ref, *, scale):
    u = u_ref[...]
    y1 = _dot(u, cd_ref[...]).astype(BF16)
    y2 = _dot(u, sd_ref[...]).astype(BF16)
    f = _dot(cn_ref[...], y1) - _dot(sn_ref[...], y2)
    o_ref[...] = (f * scale).astype(BF16)


def _fourier_ctx(u, s):
    nb, p, _ = u.shape
    n = p - s
    cn, sn = _dft_mats(n)
    cn, sn = jnp.asarray(cn, BF16), jnp.asarray(sn, BF16)
    cd, sd = _channel_dft()
    kern = functools.partial(_ft_ctx_kernel, scale=1.0 / math.sqrt(n * FT_GROUP_DIM))
    sq = pl.BlockSpec((n, n), lambda b: (0, 0))
    ch = pl.BlockSpec(cd.shape, lambda b: (0, 0))
    return pl.pallas_call(
        kern,
        grid=(nb,),
        in_specs=[pl.BlockSpec((None, n, FT_WIDTH), lambda b: (b, s // n, 0)), sq, sq, ch, ch],
        out_specs=pl.BlockSpec((None, n, FT_WIDTH), lambda b: (b, 0, 0)),
        out_shape=jax.ShapeDtypeStruct((nb, n, FT_WIDTH), BF16),
        compiler_params=_cparams(1),
        name="fourier_ctx",
    )(u, cn, sn, cd, sd)


def _merge_kernel(x_ref, oa_ref, ob_ref, oc_ref, ca_ref, cb_ref, cc_ref, mod_ref, n1g_ref, n2g_ref, wgate_ref,
                  bgate_ref, wba_ref, wbb_ref, wbc_ref, wout_ref, x1_ref, h2_ref, *, n_lat_tiles):
    d = x_ref.shape[-1]
    x = x_ref[...]
    is_ctx = pl.program_id(1) >= n_lat_tiles
    o_a = jnp.where(is_ctx, ca_ref[...], oa_ref[...])
    o_b = jnp.where(is_ctx, cb_ref[...], ob_ref[...])
    o_c = jnp.where(is_ctx, cc_ref[...], oc_ref[...])
    mod = mod_ref[...]
    sh1, sc1, g1, sh2, sc2 = (mod[:, i * d:(i + 1) * d] for i in range(5))
    h = _rms(x, n1g_ref[...]) * (1.0 + sc1) + sh1
    gate = jax.nn.sigmoid(_dot(h.astype(BF16), wgate_ref[...]) + bgate_ref[...])
    m = gate[:, 0:d] * _dot(o_a, wba_ref[...])
    m += gate[:, d:2 * d] * _dot(o_b, wbb_ref[...])
    m += gate[:, 2 * d:3 * d] * _dot(o_c, wbc_ref[...])
    y = _dot(m.astype(BF16), wout_ref[...])
    x1 = x + g1 * y
    x1_ref[...] = x1
    h2_ref[...] = (_rms(x1, n2g_ref[...]) * (1.0 + sc2) + sh2).astype(BF16)


def _merge(x, lat_outs, ctx_outs, mod, mod_row, lw, layer):
    nb, n, d = x.shape
    tm = TOKEN_TILE
    n_lat_tiles = lat_outs[0].shape[1] // tm

    def full(a):
        return _layer_spec(a, layer)

    def tok(width):
        return pl.BlockSpec((None, tm, width), lambda b, i: (b, i, 0))

    def lat(a):
        return pl.BlockSpec((None, tm, a.shape[-1]), lambda b, i: (b, jnp.minimum(i, n_lat_tiles - 1), 0))

    def ctx(a):
        return pl.BlockSpec((None, tm, a.shape[-1]), lambda b, i: (b, jnp.maximum(i - n_lat_tiles, 0), 0))

    weights = [lw["norm1_g"], lw["norm2_g"], lw["w_gate"], lw["b_gate"], lw["w_br_mla"], lw["w_br_na"],
               lw["w_br_ft"], lw["w_out"]]
    return pl.pallas_call(
        functools.partial(_merge_kernel, n_lat_tiles=n_lat_tiles),
        grid=(nb, n // tm),
        in_specs=[tok(d)] + [lat(a) for a in lat_outs] + [ctx(a) for a in ctx_outs] + [_mod_spec(mod, layer, mod_row)]
        + [full(w) for w in weights],
        out_specs=[tok(d), tok(d)],
        out_shape=[jax.ShapeDtypeStruct((nb, n, d), F32), jax.ShapeDtypeStruct((nb, n, d), BF16)],
        compiler_params=_cparams(2),
        name="merge",
    )(x, *lat_outs, *ctx_outs, mod, *weights)


def _route(logits, router_bias):
    row = lax.broadcasted_iota(jnp.int32, logits.shape, 0)
    neg = -jnp.inf
    big = jnp.int32(N_EXPERTS)
    scores = jax.nn.sigmoid(logits)
    sel = scores + router_bias

    def top2(vals):
        m1 = jnp.max(vals, axis=0, keepdims=True)
        i1 = jnp.min(jnp.where(vals == m1, row, big), axis=0, keepdims=True)
        rest = jnp.where(row == i1, neg, vals)
        m2 = jnp.max(rest, axis=0, keepdims=True)
        i2 = jnp.min(jnp.where(rest == m2, row, big), axis=0, keepdims=True)
        return m1, i1, m2, i2

    grp_scores = []
    for g in range(N_GROUPS):
        in_grp = (row >= g * EXPERTS_PER_GROUP) & (row < (g + 1) * EXPERTS_PER_GROUP)
        m1, _, m2, _ = top2(jnp.where(in_grp, sel, neg))
        grp_scores.append(m1 + m2)
    best = functools.reduce(jnp.maximum, grp_scores)
    grp = jnp.full(best.shape, N_GROUPS - 1, jnp.int32)
    for g in range(N_GROUPS - 2, -1, -1):
        grp = jnp.where(grp_scores[g] == best, g, grp)
    in_sel = (row >= grp * EXPERTS_PER_GROUP) & (row < (grp + 1) * EXPERTS_PER_GROUP)
    _, i1, _, i2 = top2(jnp.where(in_sel, sel, neg))
    hit1 = row == i1
    hit2 = row == i2
    w1 = jnp.sum(jnp.where(hit1, scores, 0.0), axis=0, keepdims=True)
    w2 = jnp.sum(jnp.where(hit2, scores, 0.0), axis=0, keepdims=True)
    total = w1 + w2
    return jnp.where(hit1, w1 / total, 0.0) + jnp.where(hit2, w2 / total, 0.0)


def _moe_kernel(h_ref, x_ref, mod_ref, cmod_ref, rw_ref, rb_ref, fg_ref, wg_ref, wu_ref, wd_ref, o_ref, gate_scr,
                acc_scr, *, apply_final_norm, n_lat):
    d = x_ref.shape[-1]
    tm = x_ref.shape[0]
    step = pl.program_id(2)
    n_steps = pl.num_programs(2)
    h = h_ref[...]

    @pl.when(step == 0)
    def _():
        gate_t = _route(_dot_nt(rw_ref[...], h), rb_ref[...])
        pad_rows = jnp.zeros((LANES - N_EXPERTS, gate_t.shape[1]), F32)
        gate = jnp.concatenate([gate_t, pad_rows], axis=0).T
        for c in range(N_EXPERTS // EXPERTS_PER_STEP):
            shift = (LANES - c * EXPERTS_PER_STEP) % LANES
            gate_scr[c] = pltpu.roll(gate, shift, axis=1) if shift else gate
        acc_scr[...] = jnp.zeros(acc_scr.shape, F32)

    gate = gate_scr[step]
    acc = acc_scr[...]
    for j in range(EXPERTS_PER_STEP):
        a = _dot(h, wg_ref[j])
        u = _dot(h, wu_ref[j])
        hid = (a * jax.nn.sigmoid(a)) * u * gate[:, j:j + 1]
        acc += _dot(hid.astype(BF16), wd_ref[j])
    acc_scr[...] = acc

    @pl.when(step == n_steps - 1)
    def _():
        tok_row = pl.program_id(1) * tm + lax.broadcasted_iota(jnp.int32, (tm, 1), 0)
        g2 = jnp.where(tok_row >= n_lat, cmod_ref[:, 5 * d:6 * d], mod_ref[:, 5 * d:6 * d])
        x2 = x_ref[...] + g2 * acc
        o_ref[...] = _rms(x2, fg_ref[...]) if apply_final_norm else x2


def _moe_tile(p):
    return max(t for t in range(TOKEN_TILE, MOE_LAT_TILE + 1, TOKEN_TILE) if p % t == 0)


def _moe(h2, x1, mod, ctx_row, lw, layer, router_w, router_b, final_g, n_lat, n_rows, tm, apply_final_norm):
    nb, _, d = x1.shape
    n_steps = N_EXPERTS // EXPERTS_PER_STEP
    ff = lw["w_e_gate"].shape[-1]
    tok = pl.BlockSpec((None, tm, d), lambda b, i, c: (b, i, 0))
    kern = functools.partial(_moe_kernel, apply_final_norm=apply_final_norm, n_lat=n_lat)
    return pl.pallas_call(
        kern,
        grid=(nb, n_rows // tm, n_steps),
        in_specs=[
            tok, tok,
            _mod_spec(mod, layer, lambda b, i: b),
            _mod_spec(mod, layer, lambda b, i: ctx_row),
            pl.BlockSpec(router_w.shape, lambda b, i, c: (0, 0)),
            pl.BlockSpec(router_b.shape, lambda b, i, c: (0, 0)),
            pl.BlockSpec(final_g.shape, lambda b, i, c: (0, 0)),
            pl.BlockSpec((None, EXPERTS_PER_STEP, d, ff), lambda b, i, c: (layer, c, 0, 0)),
            pl.BlockSpec((None, EXPERTS_PER_STEP, d, ff), lambda b, i, c: (layer, c, 0, 0)),
            pl.BlockSpec((None, EXPERTS_PER_STEP, ff, d), lambda b, i, c: (layer, c, 0, 0)),
        ],
        out_specs=tok,
        out_shape=jax.ShapeDtypeStruct((nb, n_rows, d), F32),
        scratch_shapes=[pltpu.VMEM((n_steps, tm, LANES), F32), pltpu.VMEM((tm, d), F32)],
        compiler_params=_cparams(3),
        name="moe",
    )(h2, x1, mod, mod, router_w, router_b, final_g, lw["w_e_gate"], lw["w_e_up"], lw["w_e_down"])


def _rotate_cols(w):
    half = w.shape[-1] // 2
    return jnp.concatenate([-w[..., half:], w[..., :half]], axis=-1)


def _prepare_weights(w_in, q_norm_g, w_q_up, kv_norm_g, w_kv_up, norm1_g, norm2_g, w_gate, b_gate, w_br_mla,
                     w_br_na, w_br_ft, w_out, w_e_gate, w_e_up, w_e_down):
    n_layers, d, _ = w_in.shape
    splits = np.cumsum([MLA_Q_LORA, MLA_KV_LORA, MLA_ROPE, NA_WIDTH, NA_WIDTH, NA_WIDTH])
    w_q, w_kv, w_kr, w_nq, w_nk, w_nv, w_ft = jnp.split(w_in, [int(s) for s in splits], axis=-1)

    def in_rope_slot(w):
        zeros_lo = jnp.zeros(w.shape[:-1] + (MLA_NOPE,), w.dtype)
        zeros_hi = jnp.zeros(w.shape[:-1] + (HEAD_PAD - MLA_NOPE - MLA_ROPE,), w.dtype)
        return jnp.concatenate([zeros_lo, w, zeros_hi], axis=-1)

    win = jnp.concatenate([w_q, w_kv, in_rope_slot(w_kr), in_rope_slot(_rotate_cols(w_kr)), w_nq, w_nk, w_nv, w_ft],
                          axis=-1).astype(BF16)
    wq = w_q_up.reshape(n_layers, MLA_Q_LORA, MLA_HEADS, MLA_NOPE + MLA_ROPE)
    wq_nope, wq_pe = wq[..., :MLA_NOPE], wq[..., MLA_NOPE:]
    pad = jnp.zeros(wq_pe.shape[:-1] + (HEAD_PAD - MLA_NOPE - MLA_ROPE,), wq.dtype)
    wq_main = jnp.concatenate([wq_nope, wq_pe, pad], axis=-1)
    wq_rot = jnp.concatenate([jnp.zeros_like(wq_nope), _rotate_cols(wq_pe), pad], axis=-1)
    wkv = w_kv_up.reshape(n_layers, MLA_KV_LORA, MLA_HEADS, MLA_NOPE + MLA_V)
    wk_nope, wv = wkv[..., :MLA_NOPE], wkv[..., MLA_NOPE:]
    wkv_k = jnp.concatenate([wk_nope, jnp.zeros(wk_nope.shape[:-1] + (HEAD_PAD - MLA_NOPE,), wkv.dtype)], axis=-1)
    wide = MLA_HEADS * HEAD_PAD
    return {
        "w_in": win,
        "wq_main": wq_main.reshape(n_layers, MLA_Q_LORA, wide).astype(BF16),
        "wq_rot": wq_rot.reshape(n_layers, MLA_Q_LORA, wide).astype(BF16),
        "wkv_k": wkv_k.reshape(n_layers, MLA_KV_LORA, wide).astype(BF16),
        "wkv_v": jnp.concatenate([wv, jnp.zeros(wv.shape[:-1] + (HEAD_PAD - MLA_V,), wv.dtype)], axis=-1)
        .reshape(n_layers, MLA_KV_LORA, wide).astype(BF16),
        "q_norm_g": q_norm_g.reshape(n_layers, 1, -1),
        "kv_norm_g": kv_norm_g.reshape(n_layers, 1, -1),
        "norm1_g": norm1_g.reshape(n_layers, 1, d),
        "norm2_g": norm2_g.reshape(n_layers, 1, d),
        "w_gate": w_gate.astype(BF16),
        "b_gate": b_gate.reshape(n_layers, 1, -1),
        "w_br_mla": w_br_mla.astype(BF16),
        "w_br_na": w_br_na.astype(BF16),
        "w_br_ft": w_br_ft.astype(BF16),
        "w_out": w_out.astype(BF16),
        "w_e_gate": w_e_gate.astype(BF16),
        "w_e_up": w_e_up.astype(BF16),
        "w_e_down": w_e_down.astype(BF16),
    }


def _rope_tables(s, c):
    n_freq = MLA_ROPE // 4
    inv_freq = ROPE_THETA ** (-jnp.arange(n_freq, dtype=F32) / n_freq)
    t = jnp.arange(s, dtype=jnp.int32)
    row = (t // GRID_W).astype(F32)
    col = (t % GRID_W).astype(F32)
    ang = jnp.concatenate([row[:, None] * inv_freq, col[:, None] * inv_freq], axis=-1)
    cos, sin = jnp.cos(ang), jnp.sin(ang)
    pad = HEAD_PAD - MLA_NOPE - MLA_ROPE
    cos_lat = jnp.concatenate([jnp.ones((s, MLA_NOPE), F32), cos, cos, jnp.zeros((s, pad), F32)], axis=-1)
    sin_lat = jnp.concatenate([jnp.zeros((s, MLA_NOPE), F32), sin, sin, jnp.zeros((s, pad), F32)], axis=-1)
    cos_ctx = jnp.concatenate([jnp.ones((c, MLA_NOPE + MLA_ROPE), F32), jnp.zeros((c, pad), F32)], axis=-1)
    sin_ctx = jnp.zeros((c, HEAD_PAD), F32)
    return cos_lat, sin_lat, cos_ctx, sin_ctx


@jax.jit
def _forward(x, c, ctx, c_ctx, w_ada, b_ada, norm1_g, norm2_g, w_in, q_norm_g, w_q_up, kv_norm_g, w_kv_up,
             na_rpb, w_gate, b_gate, w_br_mla, w_br_na, w_br_ft, w_out, router_w, router_bias, w_e_gate,
             w_e_up, w_e_down, final_norm_g):
    nb, s, d = x.shape
    n_ctx = ctx.shape[1]
    n_layers = w_ada.shape[0]
    grid_rows = s // GRID_W

    mod_rows = 8
    c_rows = jnp.zeros((mod_rows, d), F32).at[:nb].set(c).at[nb].set(c_ctx)
    mod_all = _modulation(c_rows, w_ada, b_ada).reshape(n_layers, mod_rows, 1, 6 * d)

    weights = _prepare_weights(w_in, q_norm_g, w_q_up, kv_norm_g, w_kv_up, norm1_g, norm2_g, w_gate, b_gate,
                               w_br_mla, w_br_na, w_br_ft, w_out, w_e_gate, w_e_up, w_e_down)
    na_bias = _na_bias_tables(na_rpb)
    na_row_mask = _na_row_masks(grid_rows)
    cos_lat, sin_lat, cos_ctx, sin_ctx = _rope_tables(s, n_ctx)
    cos_t = jnp.concatenate([cos_lat, cos_ctx], axis=0)
    sin_t = jnp.concatenate([sin_lat, sin_ctx], axis=0)
    router_w_p = router_w.T.astype(BF16)
    router_b_p = router_bias.reshape(N_EXPERTS, 1).astype(F32)
    final_g = final_norm_g.reshape(1, d)

    p = s + n_ctx
    n_lat_tiles = s // TOKEN_TILE
    ctx_row = nb

    def tile_row(b, i):
        return jnp.where(i >= n_lat_tiles, ctx_row, b)

    x_all = jnp.concatenate([x, ctx], axis=1)
    zero_ctx = [jnp.zeros((nb, n_ctx, w), BF16) for w in (MLA_HEADS * MLA_V, NA_WIDTH, FT_WIDTH)]
    for l in range(n_layers):
        last = l == n_layers - 1
        q, k, v, nq, nk, nv, ft = _project(x_all, mod_all, tile_row, weights, l, cos_t, sin_t)
        lat_outs = [_mla_attention(q, k, v, s), _na_attention(nq, nk, nv, na_bias, na_row_mask, l, s),
                    _fourier_lat(ft, s)]
        if last:
            ctx_outs = zero_ctx
        else:
            ctx_outs = [_mla_ctx_attention(q, k, v, s), _na_ctx_attention(nq, nk, nv, s), _fourier_ctx(ft, s)]
        x1, h2 = _merge(x_all, lat_outs, ctx_outs, mod_all, tile_row, weights, l)
        if last:
            return _moe(h2, x1, mod_all, ctx_row, weights, l, router_w_p, router_b_p, final_g, s, s,
                        min(MOE_LAT_TILE, s), True)
        x_all = _moe(h2, x1, mod_all, ctx_row, weights, l, router_w_p, router_b_p, final_g, s, p,
                     _moe_tile(p), False)


def kernel(x, c, ctx, c_ctx, w_ada, b_ada, norm1_g, norm2_g, w_in, q_norm_g, w_q_up, kv_norm_g, w_kv_up, na_rpb, w_gate, b_gate, w_br_mla, w_br_na, w_br_ft, w_out, router_w, router_bias, w_e_gate, w_e_up, w_e_down, final_norm_g):
    return _forward(x, c, ctx, c_ctx, w_ada, b_ada, norm1_g, norm2_g, w_in, q_norm_g, w_q_up, kv_norm_g,
                    w_kv_up, na_rpb, w_gate, b_gate, w_br_mla, w_br_na, w_br_ft, w_out, router_w, router_bias,
                    w_e_gate, w_e_up, w_e_down, final_norm_g)
```

```python
import functools
import math

import jax
import jax.numpy as jnp
import numpy as np
from jax import lax
from jax.experimental import pallas as pl
from jax.experimental.pallas import tpu as pltpu

F32 = jnp.float32
BF16 = jnp.bfloat16

GRID_W = 64
MLA_HEADS = 8
MLA_NOPE = 64
MLA_ROPE = 32
MLA_V = 64
MLA_Q_LORA = 256
MLA_KV_LORA = 128
MLA_SCALE = (MLA_NOPE + MLA_ROPE) ** -0.5
NA_HEADS = 4
NA_HEAD_DIM = 64
NA_WIN_R = 8
NA_WIN_C = 16
NA_SCALE = NA_HEAD_DIM ** -0.5
NA_WIDTH = NA_HEADS * NA_HEAD_DIM
FT_GROUPS = 4
FT_GROUP_DIM = 64
FT_WIDTH = FT_GROUPS * FT_GROUP_DIM
N_EXPERTS = 16
N_GROUPS = 4
EXPERTS_PER_GROUP = N_EXPERTS // N_GROUPS
ROPE_THETA = 10000.0
NORM_EPS = 1e-6
MASK_VALUE = -1e30
LOG2E = math.log2(math.e)

LANES = 128
SUBLANES = 8
V7X_VMEM_LIMIT_BYTES = 56 * 1024 * 1024

HEAD_PAD = LANES
NA_ROWS_PER_BLOCK = 8
NA_KEY_ROWS = 16
NA_BIAS_SHIFT = (NA_KEY_ROWS - NA_ROWS_PER_BLOCK) + (NA_ROWS_PER_BLOCK - 1) - (NA_WIN_R - 1)
NA_BIAS_TILES = (NA_WIN_R - 1) + NA_BIAS_SHIFT + (NA_KEY_ROWS - 2) + 1
FT_K1_PER_STEP = 8
EXPERTS_PER_STEP = 4
MOE_LAT_TILE = 1024
TOKEN_TILE = 256
MLA_MAX_KV_CHUNK = 2816


def _cparams(n_axes):
    return pltpu.CompilerParams(
        dimension_semantics=("arbitrary",) * n_axes,
        vmem_limit_bytes=V7X_VMEM_LIMIT_BYTES,
    )


def _rms(x, g):
    return x * lax.rsqrt(jnp.mean(x * x, axis=-1, keepdims=True) + NORM_EPS) * g


def _dot(a, b):
    return jnp.dot(a, b, preferred_element_type=F32)


def _dot_nt(a, b):
    return lax.dot_general(a, b, (((1,), (1,)), ((), ())), preferred_element_type=F32)


def _mod_kernel(c_ref, w_ref, b_ref, o_ref):
    c = c_ref[...]
    o_ref[...] = _dot(c * jax.nn.sigmoid(c), w_ref[...]) + b_ref[...]


def _modulation(c_rows, w_ada, b_ada):
    n_layers, d, width = w_ada.shape
    rows = c_rows.shape[0]
    tn = 1536
    return pl.pallas_call(
        _mod_kernel,
        grid=(n_layers, width // tn),
        in_specs=[
            pl.BlockSpec((rows, d), lambda l, j: (0, 0)),
            pl.BlockSpec((None, d, tn), lambda l, j: (l, 0, j)),
            pl.BlockSpec((None, 1, tn), lambda l, j: (l, 0, j)),
        ],
        out_specs=pl.BlockSpec((None, rows, tn), lambda l, j: (l, 0, j)),
        out_shape=jax.ShapeDtypeStruct((n_layers, rows, width), F32),
        compiler_params=_cparams(2),
        name="modulation",
    )(c_rows, w_ada, b_ada.reshape(n_layers, 1, width))


def _proj_kernel(x_ref, mod_ref, n1g_ref, win_ref, qg_ref, wqm_ref, wqr_ref, kvg_ref, wkk_ref, wkv_ref,
                 cos_ref, sin_ref, q_ref, k_ref, v_ref, nq_ref, nk_ref, nv_ref, ft_ref):
    d = x_ref.shape[-1]
    x = x_ref[...]
    mod = mod_ref[...]
    sh1, sc1 = mod[:, 0:d], mod[:, d:2 * d]
    h = _rms(x, n1g_ref[...]) * (1.0 + sc1) + sh1
    p = _dot(h.astype(BF16), win_ref[...])
    cos = cos_ref[...]
    sin = sin_ref[...]
    o = 0
    q_lat = p[:, o:o + MLA_Q_LORA]
    o += MLA_Q_LORA
    kv_lat = p[:, o:o + MLA_KV_LORA]
    o += MLA_KV_LORA
    kr_a = p[:, o:o + HEAD_PAD]
    o += HEAD_PAD
    kr_b = p[:, o:o + HEAD_PAD]
    o += HEAD_PAD
    nq_ref[...] = (p[:, o:o + NA_WIDTH] * NA_SCALE).astype(BF16)
    o += NA_WIDTH
    nk_ref[...] = p[:, o:o + NA_WIDTH].astype(BF16)
    o += NA_WIDTH
    nv_ref[...] = p[:, o:o + NA_WIDTH].astype(BF16)
    o += NA_WIDTH
    ft_ref[...] = p[:, o:o + FT_WIDTH].astype(BF16)

    qn = _rms(q_lat, qg_ref[...]).astype(BF16)
    qm = _dot(qn, wqm_ref[...])
    qr = _dot(qn, wqr_ref[...])
    kvn = _rms(kv_lat, kvg_ref[...]).astype(BF16)
    kn = _dot(kvn, wkk_ref[...])
    v = _dot(kvn, wkv_ref[...])
    v_lane = lax.broadcasted_iota(jnp.int32, v.shape, 1) & (HEAD_PAD - 1)
    v_ref[...] = jnp.where(v_lane == MLA_V, 1.0, v).astype(BF16)
    kr = kr_a * cos + kr_b * sin
    q_scale = MLA_SCALE * LOG2E
    for hd in range(MLA_HEADS):
        sl = slice(hd * HEAD_PAD, (hd + 1) * HEAD_PAD)
        q_ref[:, sl] = ((qm[:, sl] * cos + qr[:, sl] * sin) * q_scale).astype(BF16)
        k_ref[:, sl] = (kn[:, sl] + kr).astype(BF16)


def _layer_spec(a, layer):
    return pl.BlockSpec((None,) + a.shape[1:], lambda *_: (layer,) + (0,) * (a.ndim - 1))


def _mod_spec(mod, layer, mod_row):
    return pl.BlockSpec((None, None, 1, mod.shape[-1]), lambda b, i, *_: (layer, mod_row(b, i), 0, 0))


def _project(x, mod, mod_row, lw, layer, cos_t, sin_t):
    nb, n, d = x.shape
    tm = TOKEN_TILE
    win = lw["w_in"]
    wide = MLA_HEADS * HEAD_PAD

    def full(a):
        return _layer_spec(a, layer)

    def tok(width):
        return pl.BlockSpec((None, tm, width), lambda b, i: (b, i, 0))

    outs = [wide, wide, wide, NA_WIDTH, NA_WIDTH, NA_WIDTH, FT_WIDTH]
    return pl.pallas_call(
        _proj_kernel,
        grid=(nb, n // tm),
        in_specs=[
            tok(d),
            _mod_spec(mod, layer, mod_row),
            full(lw["norm1_g"]), full(win), full(lw["q_norm_g"]), full(lw["wq_main"]), full(lw["wq_rot"]),
            full(lw["kv_norm_g"]), full(lw["wkv_k"]), full(lw["wkv_v"]),
            pl.BlockSpec((tm, HEAD_PAD), lambda b, i: (i, 0)),
            pl.BlockSpec((tm, HEAD_PAD), lambda b, i: (i, 0)),
        ],
        out_specs=[tok(w) for w in outs],
        out_shape=[jax.ShapeDtypeStruct((nb, n, w), BF16) for w in outs],
        compiler_params=_cparams(2),
        name="project",
    )(x, mod, lw["norm1_g"], win, lw["q_norm_g"], lw["wq_main"], lw["wq_rot"], lw["kv_norm_g"],
      lw["wkv_k"], lw["wkv_v"], cos_t, sin_t)


def _kv_chunk(p):
    best = LANES
    for n in range(1, p // LANES + 1):
        tk = p // n
        if p % n == 0 and tk % LANES == 0 and tk <= MLA_MAX_KV_CHUNK:
            best = max(best, tk)
    return best


def _store_head_values(o_ref, rows, hd, out):
    if hd:
        out = pltpu.roll(out, MLA_V, axis=1)
    sl = slice(hd * MLA_V, (hd + 1) * MLA_V)
    o_ref[rows, sl] = out[:, sl].astype(BF16)


def _mla_kernel(q_ref, k_ref, v_ref, o_ref, s0_scr, s1_scr, p0_scr, p1_scr, mx_scr, m_scr, alpha_scr, acc_scr,
                *, tq, tk, n_q, n_c):
    def head(hd):
        return slice(hd * HEAD_PAD, (hd + 1) * HEAD_PAD)

    def q_rows(qt):
        return pl.ds(pl.multiple_of(qt * tq, tq), tq)

    def k_rows(c):
        return pl.ds(pl.multiple_of(c * tk, tk), tk)

    def scores(hd, qt, c, s_scr):
        s = _dot_nt(q_ref[q_rows(qt), head(hd)], k_ref[k_rows(c), head(hd)])
        s_scr[...] = s
        mx_scr[hd] = jnp.broadcast_to(jnp.max(s, axis=1, keepdims=True), mx_scr.shape[1:])

    def softmax(hd, s_scr, p_scr, first):
        m_prev = jnp.where(first, MASK_VALUE, m_scr[hd])
        m_new = jnp.maximum(m_prev, mx_scr[hd])
        alpha_scr[hd] = jnp.exp2(m_prev - m_new)
        m_scr[hd] = m_new
        p_scr[...] = jnp.exp2(s_scr[...] - jnp.tile(m_new, (1, tk // LANES))).astype(BF16)

    def accumulate(hd, p_scr, qt, c):
        acc = alpha_scr[hd] * acc_scr[hd] + _dot(p_scr[...], v_ref[k_rows(c), head(hd)])
        acc_scr[hd] = acc
        _store_head_values(o_ref, q_rows(qt), hd, acc / acc[:, MLA_V:MLA_V + 1])

    lane = lax.broadcasted_iota(jnp.int32, acc_scr.shape, 2)
    acc_scr[...] = jnp.where(lane == MLA_V, 1.0, 0.0)
    m_scr[...] = jnp.full(m_scr.shape, MASK_VALUE, F32)
    alpha_scr[...] = jnp.ones(alpha_scr.shape, F32)
    p1_scr[...] = jnp.zeros(p1_scr.shape, BF16)
    scores(0, 0, 0, s0_scr)

    def body(i, carry):
        qt, c = carry
        first = c == 0
        last = c == n_c - 1
        c_prev = jnp.where(first, n_c - 1, c - 1)
        qt_prev = jnp.maximum(jnp.where(first, qt - 1, qt), 0)
        c_next = jnp.where(last, 0, c + 1)
        qt_next = jnp.where(last, qt + 1, qt)
        scores(1, qt, c, s1_scr)
        softmax(0, s0_scr, p0_scr, first)
        accumulate(1, p1_scr, qt_prev, c_prev)
        scores(0, jnp.minimum(qt_next, n_q - 1), c_next, s0_scr)
        softmax(1, s1_scr, p1_scr, first)
        accumulate(0, p0_scr, qt, c)
        return qt_next, c_next

    lax.fori_loop(0, n_q * n_c, body, (jnp.int32(0), jnp.int32(0)))
    accumulate(1, p1_scr, n_q - 1, n_c - 1)


def _mla_attention(q, k_all, v_all, s):
    nb, p, _ = k_all.shape
    tq = min(512, s)
    tk = _kv_chunk(p)
    pairs = MLA_HEADS // 2
    kern = functools.partial(_mla_kernel, tq=tq, tk=tk, n_q=s // tq, n_c=p // tk)
    return pl.pallas_call(
        kern,
        grid=(nb, pairs),
        in_specs=[
            pl.BlockSpec((None, s, 2 * HEAD_PAD), lambda b, hp: (b, 0, hp)),
            pl.BlockSpec((None, p, 2 * HEAD_PAD), lambda b, hp: (b, 0, hp)),
            pl.BlockSpec((None, p, 2 * HEAD_PAD), lambda b, hp: (b, 0, hp)),
        ],
        out_specs=pl.BlockSpec((None, s, 2 * MLA_V), lambda b, hp: (b, 0, hp)),
        out_shape=jax.ShapeDtypeStruct((nb, s, MLA_HEADS * MLA_V), BF16),
        scratch_shapes=[
            pltpu.VMEM((tq, tk), F32),
            pltpu.VMEM((tq, tk), F32),
            pltpu.VMEM((tq, tk), BF16),
            pltpu.VMEM((tq, tk), BF16),
            pltpu.VMEM((2, tq, LANES), F32),
            pltpu.VMEM((2, tq, LANES), F32),
            pltpu.VMEM((2, tq, LANES), F32),
            pltpu.VMEM((2, tq, HEAD_PAD), F32),
        ],
        compiler_params=_cparams(2),
        name="mla_attention",
    )(q, k_all, v_all)


def _mla_ctx_kernel(q_ref, k_ref, v_ref, o_ref):
    for hd in range(2):
        sl = slice(hd * HEAD_PAD, (hd + 1) * HEAD_PAD)
        s = _dot_nt(q_ref[:, sl], k_ref[:, sl])
        p = jnp.exp2(s - jnp.max(s, axis=1, keepdims=True))
        acc = _dot(p.astype(BF16), v_ref[:, sl])
        _store_head_values(o_ref, slice(None), hd, acc / acc[:, MLA_V:MLA_V + 1])


def _mla_ctx_attention(q, k, v, s):
    nb, p, _ = q.shape
    c = p - s
    pairs = MLA_HEADS // 2
    spec = pl.BlockSpec((None, c, 2 * HEAD_PAD), lambda b, hp: (b, s // c, hp))
    return pl.pallas_call(
        _mla_ctx_kernel,
        grid=(nb, pairs),
        in_specs=[spec, spec, spec],
        out_specs=pl.BlockSpec((None, c, 2 * MLA_V), lambda b, hp: (b, 0, hp)),
        out_shape=jax.ShapeDtypeStruct((nb, c, MLA_HEADS * MLA_V), BF16),
        compiler_params=_cparams(2),
        name="mla_ctx_attention",
    )(q, k, v)


def _na_kernel(q_ref, k_ref, v_ref, kc_ref, vc_ref, bias_ref, rowmask_ref, o_ref, comb_scr, s0_scr, s1_scr, p0_scr,
               p1_scr, mx_scr, l_scr, *, grid_rows, n_blocks):
    tq = NA_ROWS_PER_BLOCK * GRID_W
    n_keys = NA_KEY_ROWS * GRID_W

    for hd in range(2):
        for var, row_off in enumerate((0, -(NA_WIN_R // 2), -(NA_KEY_ROWS - NA_ROWS_PER_BLOCK))):
            tile0 = row_off + (NA_WIN_R - 1) + NA_BIAS_SHIFT
            bias = jnp.concatenate(
                [jnp.concatenate([bias_ref[hd, tile0 + 2 * j - qr] for j in range(NA_KEY_ROWS // 2)], axis=1)
                 for qr in range(NA_ROWS_PER_BLOCK)], axis=0)
            comb_scr[hd, var] = bias + rowmask_ref[var].astype(F32)

    lane = lax.broadcasted_iota(jnp.int32, (tq, LANES), 1)

    def q_rows(rb):
        return pl.ds(pl.multiple_of(rb * tq, tq), tq)

    def key_rows(rb):
        start_row = jnp.clip(rb * NA_ROWS_PER_BLOCK - NA_WIN_R // 2, 0, grid_rows - NA_KEY_ROWS)
        return pl.ds(pl.multiple_of(start_row * GRID_W, (NA_WIN_R // 2) * GRID_W), n_keys)

    def scores(hd, rb, s_scr):
        var = jnp.where(rb == 0, 0, jnp.where(rb == n_blocks - 1, 2, 1))
        q = q_ref[q_rows(rb), :]
        in_head = (lane >= NA_HEAD_DIM) if hd else (lane < NA_HEAD_DIM)
        qh = jnp.where(in_head, q, jnp.zeros_like(q))
        s = _dot_nt(qh, k_ref[key_rows(rb), :]) + comb_scr[hd, var]
        sc = _dot_nt(qh, kc_ref[...])
        s_scr[:, :n_keys] = s
        s_scr[:, n_keys:] = sc
        m = jnp.maximum(jnp.max(s, axis=1, keepdims=True), jnp.max(sc, axis=1, keepdims=True))
        mx_scr[hd] = jnp.broadcast_to(m, mx_scr.shape[1:])

    def softmax(hd, s_scr, p_scr):
        p = jnp.exp(s_scr[...] - jnp.tile(mx_scr[hd], (1, s_scr.shape[1] // LANES)))
        l_scr[hd] = jnp.broadcast_to(jnp.sum(p, axis=1, keepdims=True), l_scr.shape[1:])
        p_scr[...] = p.astype(BF16)

    def accumulate(hd, p_scr, rb):
        acc = _dot(p_scr[:, :n_keys], v_ref[key_rows(rb), :]) + _dot(p_scr[:, n_keys:], vc_ref[...])
        out = (acc / l_scr[hd]).astype(BF16)
        sl = slice(hd * NA_HEAD_DIM, (hd + 1) * NA_HEAD_DIM)
        o_ref[q_rows(rb), sl] = out[:, sl]

    p1_scr[...] = jnp.zeros(p1_scr.shape, BF16)
    l_scr[...] = jnp.ones(l_scr.shape, F32)
    scores(0, 0, s0_scr)

    def body(rb, carry):
        scores(1, rb, s1_scr)
        softmax(0, s0_scr, p0_scr)
        accumulate(1, p1_scr, jnp.maximum(rb - 1, 0))
        scores(0, jnp.minimum(rb + 1, n_blocks - 1), s0_scr)
        softmax(1, s1_scr, p1_scr)
        accumulate(0, p0_scr, rb)
        return carry

    lax.fori_loop(0, n_blocks, body, 0)
    accumulate(1, p1_scr, n_blocks - 1)


def _na_attention(q, k, v, bias, row_mask, layer, s):
    nb, p, _ = q.shape
    c = p - s
    grid_rows = s // GRID_W
    tq = NA_ROWS_PER_BLOCK * GRID_W
    n_blocks = grid_rows // NA_ROWS_PER_BLOCK
    n_keys = NA_KEY_ROWS * GRID_W
    pairs = NA_HEADS // 2
    kern = functools.partial(_na_kernel, grid_rows=grid_rows, n_blocks=n_blocks)
    lat = pl.BlockSpec((None, s, LANES), lambda b, hp: (b, 0, hp))
    ctx = pl.BlockSpec((None, c, LANES), lambda b, hp: (b, s // c, hp))
    return pl.pallas_call(
        kern,
        grid=(nb, pairs),
        in_specs=[
            lat, lat, lat, ctx, ctx,
            pl.BlockSpec((None, 2) + bias.shape[2:], lambda b, hp: (layer, hp, 0, 0, 0)),
            pl.BlockSpec(row_mask.shape, lambda b, hp: (0, 0, 0)),
        ],
        out_specs=pl.BlockSpec((None, s, LANES), lambda b, hp: (b, 0, hp)),
        out_shape=jax.ShapeDtypeStruct((nb, s, NA_WIDTH), BF16),
        scratch_shapes=[
            pltpu.VMEM((2, 3, tq, n_keys), F32),
            pltpu.VMEM((tq, n_keys + c), F32),
            pltpu.VMEM((tq, n_keys + c), F32),
            pltpu.VMEM((tq, n_keys + c), BF16),
            pltpu.VMEM((tq, n_keys + c), BF16),
            pltpu.VMEM((2, tq, LANES), F32),
            pltpu.VMEM((2, tq, LANES), F32),
        ],
        compiler_params=_cparams(2),
        name="na_attention",
    )(q, k, v, k, v, bias, row_mask)


def _na_ctx_kernel(q_ref, k_ref, v_ref, o_ref):
    q = q_ref[...]
    k = k_ref[...]
    v = v_ref[...]
    lane = lax.broadcasted_iota(jnp.int32, q.shape, 1)
    outs = []
    for hd in range(2):
        in_head = (lane >= NA_HEAD_DIM) if hd else (lane < NA_HEAD_DIM)
        qh = jnp.where(in_head, q, jnp.zeros_like(q))
        s = _dot_nt(qh, k)
        p = jnp.exp(s - jnp.max(s, axis=1, keepdims=True))
        outs.append(_dot(p.astype(BF16), v) / jnp.sum(p, axis=1, keepdims=True))
    o_ref[...] = jnp.where(lane < NA_HEAD_DIM, outs[0], outs[1]).astype(BF16)


def _na_ctx_attention(q, k, v, s):
    nb, p, _ = q.shape
    c = p - s
    spec = pl.BlockSpec((None, c, LANES), lambda b, hp: (b, s // c, hp))
    return pl.pallas_call(
        _na_ctx_kernel,
        grid=(nb, NA_HEADS // 2),
        in_specs=[spec, spec, spec],
        out_specs=pl.BlockSpec((None, c, LANES), lambda b, hp: (b, 0, hp)),
        out_shape=jax.ShapeDtypeStruct((nb, c, NA_WIDTH), BF16),
        compiler_params=_cparams(2),
        name="na_ctx_attention",
    )(q, k, v)


def _na_row_masks(grid_rows):
    qr = np.arange(NA_ROWS_PER_BLOCK)[:, None]
    kr = np.arange(NA_KEY_ROWS)[None, :]
    row_valid = np.zeros((3, NA_ROWS_PER_BLOCK, NA_KEY_ROWS), bool)
    n_blocks = grid_rows // NA_ROWS_PER_BLOCK
    for vi, rb in enumerate((0, 1, n_blocks - 1)):
        r = rb * NA_ROWS_PER_BLOCK + qr
        start_row = min(max(rb * NA_ROWS_PER_BLOCK - NA_WIN_R // 2, 0), grid_rows - NA_KEY_ROWS)
        key_row = start_row + kr
        win = np.clip(r - NA_WIN_R // 2, 0, grid_rows - NA_WIN_R)
        row_valid[vi] = (key_row >= win) & (key_row < win + NA_WIN_R)
    mask = np.where(row_valid, 0.0, MASK_VALUE).astype(np.float32)
    mask = np.broadcast_to(mask[:, :, None, :, None], (3, NA_ROWS_PER_BLOCK, GRID_W, NA_KEY_ROWS, GRID_W))
    return jnp.asarray(mask.reshape(3, NA_ROWS_PER_BLOCK * GRID_W, NA_KEY_ROWS * GRID_W), BF16)


def _na_bias_tables(rpb):
    n_dc = 2 * NA_WIN_C - 1
    qc = np.arange(GRID_W)[:, None]
    kcol = np.arange(GRID_W)[None, :]
    wstart = np.clip(qc - NA_WIN_C // 2, 0, GRID_W - NA_WIN_C)
    col_valid = (kcol >= wstart) & (kcol < wstart + NA_WIN_C)
    dc = np.clip(kcol - qc, -(NA_WIN_C - 1), NA_WIN_C - 1) + (NA_WIN_C - 1)
    col_onehot = ((dc[..., None] == np.arange(n_dc)) & col_valid[..., None]).astype(np.float32)
    t = jnp.einsum("lhij,cdj->lhicd", rpb.astype(F32), jnp.asarray(col_onehot), precision=lax.Precision.HIGHEST)
    t = jnp.where(jnp.asarray(col_valid), t, MASK_VALUE)
    n_dr = t.shape[2]
    t = jnp.pad(t, ((0, 0), (0, 0), (NA_BIAS_SHIFT, NA_BIAS_TILES + 1 - NA_BIAS_SHIFT - n_dr), (0, 0), (0, 0)),
                constant_values=MASK_VALUE)
    return jnp.concatenate([t[:, :, :NA_BIAS_TILES], t[:, :, 1:NA_BIAS_TILES + 1]], axis=-1)


FT_N1 = 64


def _ft_stage1_kernel(w_ref, x_ref, o_ref):
    o_ref[...] = _dot(w_ref[...], x_ref[...])


def _ft_stage2_kernel(are_ref, aim_ref, tc_ref, ts_ref, w_ref, cd_ref, sd_ref, o_ref, *, scale):
    n2 = are_ref.shape[1]
    for g in range(are_ref.shape[0]):
        are = are_ref[g]
        aim = aim_ref[g]
        tc = jnp.concatenate([tc_ref[g]] * (FT_WIDTH // LANES), axis=1)
        ts = jnp.concatenate([ts_ref[g]] * (FT_WIDTH // LANES), axis=1)
        bre = are * tc + aim * ts
        bim = aim * tc - are * ts
        bb = jnp.concatenate([bre, bim], axis=0).astype(BF16)
        xk = _dot(w_ref[...], bb)
        f = _dot(xk[:n2].astype(BF16), cd_ref[...]) + _dot(xk[n2:].astype(BF16), sd_ref[...])
        o_ref[:, g * FT_WIDTH:(g + 1) * FT_WIDTH] = (f * scale).astype(BF16)


def _dft_mats(n):
    ang = 2.0 * np.pi * np.outer(np.arange(n), np.arange(n)) / n
    return np.cos(ang), np.sin(ang)


def _channel_dft():
    c, s = _dft_mats(FT_GROUP_DIM)
    eye = np.eye(FT_GROUPS)
    return jnp.asarray(np.kron(eye, c), BF16), jnp.asarray(np.kron(eye, s), BF16)


def _fourier_lat(u, n):
    nb, p, _ = u.shape
    n2 = n // FT_N1
    cols = n2 * FT_WIDTH
    c1, s1 = _dft_mats(FT_N1)
    w1 = jnp.asarray(np.concatenate([c1, -s1], axis=0), BF16)
    tcol = min(2048, cols)
    a = pl.pallas_call(
        _ft_stage1_kernel,
        grid=(nb, cols // tcol),
        in_specs=[
            pl.BlockSpec(w1.shape, lambda b, j: (0, 0)),
            pl.BlockSpec((None, FT_N1, tcol), lambda b, j: (b, 0, j)),
        ],
        out_specs=pl.BlockSpec((None, 2 * FT_N1, tcol), lambda b, j: (b, 0, j)),
        out_shape=jax.ShapeDtypeStruct((nb, 2 * FT_N1, cols), F32),
        compiler_params=_cparams(2),
        name="fourier_stage1",
    )(w1, u.reshape(nb, p // n2, cols))
    a = a.reshape(nb, 2 * FT_N1, n2, FT_WIDTH)

    ang = 2.0 * np.pi * np.outer(np.arange(FT_N1), np.arange(n2)) / n
    tc = jnp.asarray(np.repeat(np.cos(ang)[:, :, None], LANES, axis=2), F32)
    ts = jnp.asarray(np.repeat(np.sin(ang)[:, :, None], LANES, axis=2), F32)
    c3, s3 = _dft_mats(n2)
    w3 = jnp.asarray(np.block([[c3, s3], [-s3, c3]]), BF16)
    cd, sd = _channel_dft()
    kern = functools.partial(_ft_stage2_kernel, scale=1.0 / math.sqrt(n * FT_GROUP_DIM))
    g = FT_K1_PER_STEP
    n_steps = FT_N1 // g
    out = pl.pallas_call(
        kern,
        grid=(nb, n_steps),
        in_specs=[
            pl.BlockSpec((None, g, n2, FT_WIDTH), lambda b, k: (b, k, 0, 0)),
            pl.BlockSpec((None, g, n2, FT_WIDTH), lambda b, k: (b, n_steps + k, 0, 0)),
            pl.BlockSpec((g, n2, LANES), lambda b, k: (k, 0, 0)),
            pl.BlockSpec((g, n2, LANES), lambda b, k: (k, 0, 0)),
            pl.BlockSpec(w3.shape, lambda b, k: (0, 0)),
            pl.BlockSpec(cd.shape, lambda b, k: (0, 0)),
            pl.BlockSpec(sd.shape, lambda b, k: (0, 0)),
        ],
        out_specs=pl.BlockSpec((None, n2, g * FT_WIDTH), lambda b, k: (b, 0, k)),
        out_shape=jax.ShapeDtypeStruct((nb, n2, FT_N1 * FT_WIDTH), BF16),
        compiler_params=_cparams(2),
        name="fourier_stage2",
    )(a, a, tc, ts, w3, cd, sd)
    return out.reshape(nb, n, FT_WIDTH)


def _ft_ctx_kernel(u_ref, cn_ref, sn_ref, cd_ref, sd_ref, o_ref, *, scale):
    u = u_ref[...]
    y1 = _dot(u, cd_ref[...]).astype(BF16)
    y2 = _dot(u, sd_ref[...]).astype(BF16)
    f = _dot(cn_ref[...], y1) - _dot(sn_ref[...], y2)
    o_ref[...] = (f * scale).astype(BF16)


def _fourier_ctx(u, s):
    nb, p, _ = u.shape
    n = p - s
    cn, sn = _dft_mats(n)
    cn, sn = jnp.asarray(cn, BF16), jnp.asarray(sn, BF16)
    cd, sd = _channel_dft()
    kern = functools.partial(_ft_ctx_kernel, scale=1.0 / math.sqrt(n * FT_GROUP_DIM))
    sq = pl.BlockSpec((n, n), lambda b: (0, 0))
    ch = pl.BlockSpec(cd.shape, lambda b: (0, 0))
    return pl.pallas_call(
        kern,
        grid=(nb,),
        in_specs=[pl.BlockSpec((None, n, FT_WIDTH), lambda b: (b, s // n, 0)), sq, sq, ch, ch],
        out_specs=pl.BlockSpec((None, n, FT_WIDTH), lambda b: (b, 0, 0)),
        out_shape=jax.ShapeDtypeStruct((nb, n, FT_WIDTH), BF16),
        compiler_params=_cparams(1),
        name="fourier_ctx",
    )(u, cn, sn, cd, sd)


def _merge_kernel(x_ref, oa_ref, ob_ref, oc_ref, ca_ref, cb_ref, cc_ref, mod_ref, n1g_ref, n2g_ref, wgate_ref,
                  bgate_ref, wba_ref, wbb_ref, wbc_ref, wout_ref, x1_ref, h2_ref, *, n_lat_tiles):
    d = x_ref.shape[-1]
    x = x_ref[...]
    is_ctx = pl.program_id(1) >= n_lat_tiles
    o_a = jnp.where(is_ctx, ca_ref[...], oa_ref[...])
    o_b = jnp.where(is_ctx, cb_ref[...], ob_ref[...])
    o_c = jnp.where(is_ctx, cc_ref[...], oc_ref[...])
    mod = mod_ref[...]
    sh1, sc1, g1, sh2, sc2 = (mod[:, i * d:(i + 1) * d] for i in range(5))
    h = _rms(x, n1g_ref[...]) * (1.0 + sc1) + sh1
    gate = jax.nn.sigmoid(_dot(h.astype(BF16), wgate_ref[...]) + bgate_ref[...])
    m = gate[:, 0:d] * _dot(o_a, wba_ref[...])
    m += gate[:, d:2 * d] * _dot(o_b, wbb_ref[...])
    m += gate[:, 2 * d:3 * d] * _dot(o_c, wbc_ref[...])
    y = _dot(m.astype(BF16), wout_ref[...])
    x1 = x + g1 * y
    x1_ref[...] = x1
    h2_ref[...] = (_rms(x1, n2g_ref[...]) * (1.0 + sc2) + sh2).astype(BF16)


def _merge(x, lat_outs, ctx_outs, mod, mod_row, lw, layer):
    nb, n, d = x.shape
    tm = TOKEN_TILE
    n_lat_tiles = lat_outs[0].shape[1] // tm

    def full(a):
        return _layer_spec(a, layer)

    def tok(width):
        return pl.BlockSpec((None, tm, width), lambda b, i: (b, i, 0))

    def lat(a):
        return pl.BlockSpec((None, tm, a.shape[-1]), lambda b, i: (b, jnp.minimum(i, n_lat_tiles - 1), 0))

    def ctx(a):
        return pl.BlockSpec((None, tm, a.shape[-1]), lambda b, i: (b, jnp.maximum(i - n_lat_tiles, 0), 0))

    weights = [lw["norm1_g"], lw["norm2_g"], lw["w_gate"], lw["b_gate"], lw["w_br_mla"], lw["w_br_na"],
               lw["w_br_ft"], lw["w_out"]]
    return pl.pallas_call(
        functools.partial(_merge_kernel, n_lat_tiles=n_lat_tiles),
        grid=(nb, n // tm),
        in_specs=[tok(d)] + [lat(a) for a in lat_outs] + [ctx(a) for a in ctx_outs] + [_mod_spec(mod, layer, mod_row)]
        + [full(w) for w in weights],
        out_specs=[tok(d), tok(d)],
        out_shape=[jax.ShapeDtypeStruct((nb, n, d), F32), jax.ShapeDtypeStruct((nb, n, d), BF16)],
        compiler_params=_cparams(2),
        name="merge",
    )(x, *lat_outs, *ctx_outs, mod, *weights)


def _route(logits, router_bias):
    row = lax.broadcasted_iota(jnp.int32, logits.shape, 0)
    neg = -jnp.inf
    big = jnp.int32(N_EXPERTS)
    scores = jax.nn.sigmoid(logits)
    sel = scores + router_bias

    def top2(vals):
        m1 = jnp.max(vals, axis=0, keepdims=True)
        i1 = jnp.min(jnp.where(vals == m1, row, big), axis=0, keepdims=True)
        rest = jnp.where(row == i1, neg, vals)
        m2 = jnp.max(rest, axis=0, keepdims=True)
        i2 = jnp.min(jnp.where(rest == m2, row, big), axis=0, keepdims=True)
        return m1, i1, m2, i2

    grp_scores = []
    for g in range(N_GROUPS):
        in_grp = (row >= g * EXPERTS_PER_GROUP) & (row < (g + 1) * EXPERTS_PER_GROUP)
        m1, _, m2, _ = top2(jnp.where(in_grp, sel, neg))
        grp_scores.append(m1 + m2)
    best = functools.reduce(jnp.maximum, grp_scores)
    grp = jnp.full(best.shape, N_GROUPS - 1, jnp.int32)
    for g in range(N_GROUPS - 2, -1, -1):
        grp = jnp.where(grp_scores[g] == best, g, grp)
    in_sel = (row >= grp * EXPERTS_PER_GROUP) & (row < (grp + 1) * EXPERTS_PER_GROUP)
    _, i1, _, i2 = top2(jnp.where(in_sel, sel, neg))
    hit1 = row == i1
    hit2 = row == i2
    w1 = jnp.sum(jnp.where(hit1, scores, 0.0), axis=0, keepdims=True)
    w2 = jnp.sum(jnp.where(hit2, scores, 0.0), axis=0, keepdims=True)
    total = w1 + w2
    return jnp.where(hit1, w1 / total, 0.0) + jnp.where(hit2, w2 / total, 0.0), grp


def _split_bf16(v):
    hi = v.astype(BF16)
    return hi, (v - hi.astype(F32)).astype(BF16)


def _moe_kernel(h_ref, x_ref, mod_ref, cmod_ref, rw_ref, rb_ref, fg_ref, tri_ref, wg_ref, wu_ref, wd_ref, o_ref,
                gate_scr, col_scr, row_scr, acc_scr, *, apply_final_norm, n_lat, cap):
    d = x_ref.shape[-1]
    tm = x_ref.shape[0]
    step = pl.program_id(2)
    n_steps = pl.num_programs(2)
    h = h_ref[...]

    @pl.when(step == 0)
    def _():
        gate_t, grp = _route(_dot_nt(rw_ref[...], h), rb_ref[...])
        member = jnp.where(lax.broadcasted_iota(jnp.int32, (SUBLANES, tm), 0) == grp, 1.0, 0.0)
        rank = _dot(member.astype(BF16), tri_ref[...])
        pad_rows = jnp.zeros((LANES - N_EXPERTS - 2 * SUBLANES, tm), F32)
        cols = jnp.concatenate([gate_t, rank, member, pad_rows], axis=0).T
        lane = lax.broadcasted_iota(jnp.int32, cols.shape, 1)
        empty_rows = jnp.zeros((SUBLANES - 2, tm), F32)
        for c in range(N_GROUPS):
            gate_scr[c] = pltpu.roll(cols, (LANES - c * EXPERTS_PER_STEP) % LANES, axis=1)
            rank_c = pltpu.roll(cols, LANES - (N_EXPERTS + c), axis=1)
            member_c = pltpu.roll(cols, LANES + 1 - (N_EXPERTS + SUBLANES + c), axis=1)
            col_scr[c] = jnp.where(lane == 0, rank_c, jnp.where(lane == 1, member_c, 0.0))
            row_scr[c] = jnp.concatenate([rank[c:c + 1], member[c:c + 1], empty_rows], axis=0)
        acc_scr[...] = jnp.zeros(acc_scr.shape, F32)

    def experts(hx, gates):
        out = jnp.zeros((hx.shape[0], d), F32)
        for j in range(EXPERTS_PER_STEP):
            a = _dot(hx, wg_ref[j])
            u = _dot(hx, wu_ref[j])
            hid = (a * jax.nn.sigmoid(a)) * u * gates[:, j:j + 1]
            out += _dot(hid.astype(BF16), wd_ref[j])
        return out

    rank_row = row_scr[step, 0:1, :]
    member_row = row_scr[step, 1:2, :]
    n_members = jnp.sum(member_row)

    @pl.when(n_members <= cap)
    def _():
        slot = lax.broadcasted_iota(jnp.int32, (cap, tm), 0).astype(F32)
        pick = jnp.where((slot == rank_row) & (member_row > 0.5), 1.0, 0.0).astype(BF16)
        hx = _dot(pick, h).astype(BF16)
        g_hi, g_lo = _split_bf16(gate_scr[step])
        y = experts(hx, _dot(pick, g_hi) + _dot(pick, g_lo))
        col = col_scr[step]
        slot_t = lax.broadcasted_iota(jnp.int32, (tm, cap), 1).astype(F32)
        put = jnp.where((slot_t == col[:, 0:1]) & (col[:, 1:2] > 0.5), 1.0, 0.0).astype(BF16)
        y_hi, y_lo = _split_bf16(y)
        acc_scr[...] += _dot(put, y_hi) + _dot(put, y_lo)

    @pl.when(n_members > cap)
    def _():
        acc_scr[...] += experts(h, gate_scr[step])

    @pl.when(step == n_steps - 1)
    def _():
        tok_row = pl.program_id(1) * tm + lax.broadcasted_iota(jnp.int32, (tm, 1), 0)
        g2 = jnp.where(tok_row >= n_lat, cmod_ref[:, 5 * d:6 * d], mod_ref[:, 5 * d:6 * d])
        x2 = x_ref[...] + g2 * acc_scr[...]
        o_ref[...] = _rms(x2, fg_ref[...]) if apply_final_norm else x2


def _moe_tile(p):
    return max(t for t in range(TOKEN_TILE, MOE_LAT_TILE + 1, TOKEN_TILE) if p % t == 0)


def _moe_capacity(tm):
    return min(tm, -(-(5 * tm // (4 * N_GROUPS)) // LANES) * LANES)


def _moe(h2, x1, mod, ctx_row, lw, layer, router_w, router_b, final_g, n_lat, n_rows, tm, apply_final_norm):
    assert EXPERTS_PER_STEP == EXPERTS_PER_GROUP
    nb, _, d = x1.shape
    n_steps = N_EXPERTS // EXPERTS_PER_STEP
    ff = lw["w_e_gate"].shape[-1]
    tok = pl.BlockSpec((None, tm, d), lambda b, i, c: (b, i, 0))
    earlier = jnp.asarray(np.triu(np.ones((tm, tm), np.float32), k=1), BF16)
    kern = functools.partial(_moe_kernel, apply_final_norm=apply_final_norm, n_lat=n_lat, cap=_moe_capacity(tm))
    return pl.pallas_call(
        kern,
        grid=(nb, n_rows // tm, n_steps),
        in_specs=[
            tok, tok,
            _mod_spec(mod, layer, lambda b, i: b),
            _mod_spec(mod, layer, lambda b, i: ctx_row),
            pl.BlockSpec(router_w.shape, lambda b, i, c: (0, 0)),
            pl.BlockSpec(router_b.shape, lambda b, i, c: (0, 0)),
            pl.BlockSpec(final_g.shape, lambda b, i, c: (0, 0)),
            pl.BlockSpec(earlier.shape, lambda b, i, c: (0, 0)),
            pl.BlockSpec((None, EXPERTS_PER_STEP, d, ff), lambda b, i, c: (layer, c, 0, 0)),
            pl.BlockSpec((None, EXPERTS_PER_STEP, d, ff), lambda b, i, c: (layer, c, 0, 0)),
            pl.BlockSpec((None, EXPERTS_PER_STEP, ff, d), lambda b, i, c: (layer, c, 0, 0)),
        ],
        out_specs=tok,
        out_shape=jax.ShapeDtypeStruct((nb, n_rows, d), F32),
        scratch_shapes=[
            pltpu.VMEM((n_steps, tm, LANES), F32),
            pltpu.VMEM((n_steps, tm, LANES), F32),
            pltpu.VMEM((n_steps, SUBLANES, tm), F32),
            pltpu.VMEM((tm, d), F32),
        ],
        compiler_params=_cparams(3),
        name="moe",
    )(h2, x1, mod, mod, router_w, router_b, final_g, earlier, lw["w_e_gate"], lw["w_e_up"], lw["w_e_down"])


def _rotate_cols(w):
    half = w.shape[-1] // 2
    return jnp.concatenate([-w[..., half:], w[..., :half]], axis=-1)


def _prepare_weights(w_in, q_norm_g, w_q_up, kv_norm_g, w_kv_up, norm1_g, norm2_g, w_gate, b_gate, w_br_mla,
                     w_br_na, w_br_ft, w_out, w_e_gate, w_e_up, w_e_down):
    n_layers, d, _ = w_in.shape
    splits = np.cumsum([MLA_Q_LORA, MLA_KV_LORA, MLA_ROPE, NA_WIDTH, NA_WIDTH, NA_WIDTH])
    w_q, w_kv, w_kr, w_nq, w_nk, w_nv, w_ft = jnp.split(w_in, [int(s) for s in splits], axis=-1)

    def in_rope_slot(w):
        zeros_lo = jnp.zeros(w.shape[:-1] + (MLA_NOPE,), w.dtype)
        zeros_hi = jnp.zeros(w.shape[:-1] + (HEAD_PAD - MLA_NOPE - MLA_ROPE,), w.dtype)
        return jnp.concatenate([zeros_lo, w, zeros_hi], axis=-1)

    win = jnp.concatenate([w_q, w_kv, in_rope_slot(w_kr), in_rope_slot(_rotate_cols(w_kr)), w_nq, w_nk, w_nv, w_ft],
                          axis=-1).astype(BF16)
    wq = w_q_up.reshape(n_layers, MLA_Q_LORA, MLA_HEADS, MLA_NOPE + MLA_ROPE)
    wq_nope, wq_pe = wq[..., :MLA_NOPE], wq[..., MLA_NOPE:]
    pad = jnp.zeros(wq_pe.shape[:-1] + (HEAD_PAD - MLA_NOPE - MLA_ROPE,), wq.dtype)
    wq_main = jnp.concatenate([wq_nope, wq_pe, pad], axis=-1)
    wq_rot = jnp.concatenate([jnp.zeros_like(wq_nope), _rotate_cols(wq_pe), pad], axis=-1)
    wkv = w_kv_up.reshape(n_layers, MLA_KV_LORA, MLA_HEADS, MLA_NOPE + MLA_V)
    wk_nope, wv = wkv[..., :MLA_NOPE], wkv[..., MLA_NOPE:]
    wkv_k = jnp.concatenate([wk_nope, jnp.zeros(wk_nope.shape[:-1] + (HEAD_PAD - MLA_NOPE,), wkv.dtype)], axis=-1)
    wide = MLA_HEADS * HEAD_PAD
    return {
        "w_in": win,
        "wq_main": wq_main.reshape(n_layers, MLA_Q_LORA, wide).astype(BF16),
        "wq_rot": wq_rot.reshape(n_layers, MLA_Q_LORA, wide).astype(BF16),
        "wkv_k": wkv_k.reshape(n_layers, MLA_KV_LORA, wide).astype(BF16),
        "wkv_v": jnp.concatenate([wv, jnp.zeros(wv.shape[:-1] + (HEAD_PAD - MLA_V,), wv.dtype)], axis=-1)
        .reshape(n_layers, MLA_KV_LORA, wide).astype(BF16),
        "q_norm_g": q_norm_g.reshape(n_layers, 1, -1),
        "kv_norm_g": kv_norm_g.reshape(n_layers, 1, -1),
        "norm1_g": norm1_g.reshape(n_layers, 1, d),
        "norm2_g": norm2_g.reshape(n_layers, 1, d),
        "w_gate": w_gate.astype(BF16),
        "b_gate": b_gate.reshape(n_layers, 1, -1),
        "w_br_mla": w_br_mla.astype(BF16),
        "w_br_na": w_br_na.astype(BF16),
        "w_br_ft": w_br_ft.astype(BF16),
        "w_out": w_out.astype(BF16),
        "w_e_gate": w_e_gate.astype(BF16),
        "w_e_up": w_e_up.astype(BF16),
        "w_e_down": w_e_down.astype(BF16),
    }


def _rope_tables(s, c):
    n_freq = MLA_ROPE // 4
    inv_freq = ROPE_THETA ** (-jnp.arange(n_freq, dtype=F32) / n_freq)
    t = jnp.arange(s, dtype=jnp.int32)
    row = (t // GRID_W).astype(F32)
    col = (t % GRID_W).astype(F32)
    ang = jnp.concatenate([row[:, None] * inv_freq, col[:, None] * inv_freq], axis=-1)
    cos, sin = jnp.cos(ang), jnp.sin(ang)
    pad = HEAD_PAD - MLA_NOPE - MLA_ROPE
    cos_lat = jnp.concatenate([jnp.ones((s, MLA_NOPE), F32), cos, cos, jnp.zeros((s, pad), F32)], axis=-1)
    sin_lat = jnp.concatenate([jnp.zeros((s, MLA_NOPE), F32), sin, sin, jnp.zeros((s, pad), F32)], axis=-1)
    cos_ctx = jnp.concatenate([jnp.ones((c, MLA_NOPE + MLA_ROPE), F32), jnp.zeros((c, pad), F32)], axis=-1)
    sin_ctx = jnp.zeros((c, HEAD_PAD), F32)
    return cos_lat, sin_lat, cos_ctx, sin_ctx


@jax.jit
def _forward(x, c, ctx, c_ctx, w_ada, b_ada, norm1_g, norm2_g, w_in, q_norm_g, w_q_up, kv_norm_g, w_kv_up,
             na_rpb, w_gate, b_gate, w_br_mla, w_br_na, w_br_ft, w_out, router_w, router_bias, w_e_gate,
             w_e_up, w_e_down, final_norm_g):
    nb, s, d = x.shape
    n_ctx = ctx.shape[1]
    n_layers = w_ada.shape[0]
    grid_rows = s // GRID_W

    mod_rows = 8
    c_rows = jnp.zeros((mod_rows, d), F32).at[:nb].set(c).at[nb].set(c_ctx)
    mod_all = _modulation(c_rows, w_ada, b_ada).reshape(n_layers, mod_rows, 1, 6 * d)

    weights = _prepare_weights(w_in, q_norm_g, w_q_up, kv_norm_g, w_kv_up, norm1_g, norm2_g, w_gate, b_gate,
                               w_br_mla, w_br_na, w_br_ft, w_out, w_e_gate, w_e_up, w_e_down)
    na_bias = _na_bias_tables(na_rpb)
    na_row_mask = _na_row_masks(grid_rows)
    cos_lat, sin_lat, cos_ctx, sin_ctx = _rope_tables(s, n_ctx)
    cos_t = jnp.concatenate([cos_lat, cos_ctx], axis=0)
    sin_t = jnp.concatenate([sin_lat, sin_ctx], axis=0)
    router_w_p = router_w.T.astype(BF16)
    router_b_p = router_bias.reshape(N_EXPERTS, 1).astype(F32)
    final_g = final_norm_g.reshape(1, d)

    p = s + n_ctx
    n_lat_tiles = s // TOKEN_TILE
    ctx_row = nb

    def tile_row(b, i):
        return jnp.where(i >= n_lat_tiles, ctx_row, b)

    x_all = jnp.concatenate([x, ctx], axis=1)
    zero_ctx = [jnp.zeros((nb, n_ctx, w), BF16) for w in (MLA_HEADS * MLA_V, NA_WIDTH, FT_WIDTH)]
    for l in range(n_layers):
        last = l == n_layers - 1
        q, k, v, nq, nk, nv, ft = _project(x_all, mod_all, tile_row, weights, l, cos_t, sin_t)
        lat_outs = [_mla_attention(q, k, v, s), _na_attention(nq, nk, nv, na_bias, na_row_mask, l, s),
                    _fourier_lat(ft, s)]
        if last:
            ctx_outs = zero_ctx
        else:
            ctx_outs = [_mla_ctx_attention(q, k, v, s), _na_ctx_attention(nq, nk, nv, s), _fourier_ctx(ft, s)]
        x1, h2 = _merge(x_all, lat_outs, ctx_outs, mod_all, tile_row, weights, l)
        if last:
            return _moe(h2, x1, mod_all, ctx_row, weights, l, router_w_p, router_b_p, final_g, s, s,
                        min(MOE_LAT_TILE, s), True)
        x_all = _moe(h2, x1, mod_all, ctx_row, weights, l, router_w_p, router_b_p, final_g, s, p,
                     _moe_tile(p), False)


def kernel(x, c, ctx, c_ctx, w_ada, b_ada, norm1_g, norm2_g, w_in, q_norm_g, w_q_up, kv_norm_g, w_kv_up, na_rpb, w_gate, b_gate, w_br_mla, w_br_na, w_br_ft, w_out, router_w, router_bias, w_e_gate, w_e_up, w_e_down, final_norm_g):
    return _forward(x, c, ctx, c_ctx, w_ada, b_ada, norm1_g, norm2_g, w_in, q_norm_g, w_q_up, kv_norm_g,
                    w_kv_up, na_rpb, w_gate, b_gate, w_br_mla, w_br_na, w_br_ft, w_out, router_w, router_bias,
                    w_e_gate, w_e_up, w_e_down, final_norm_g)
```

```python
import functools
import math

import jax
import jax.numpy as jnp
import numpy as np
from jax import lax
from jax.experimental import pallas as pl
from jax.experimental.pallas import tpu as pltpu

F32 = jnp.float32
BF16 = jnp.bfloat16

GRID_W = 64
MLA_HEADS = 8
MLA_NOPE = 64
MLA_ROPE = 32
MLA_V = 64
MLA_Q_LORA = 256
MLA_KV_LORA = 128
MLA_SCALE = (MLA_NOPE + MLA_ROPE) ** -0.5
NA_HEADS = 4
NA_HEAD_DIM = 64
NA_WIN_R = 8
NA_WIN_C = 16
NA_SCALE = NA_HEAD_DIM ** -0.5
NA_WIDTH = NA_HEADS * NA_HEAD_DIM
FT_GROUPS = 4
FT_GROUP_DIM = 64
FT_WIDTH = FT_GROUPS * FT_GROUP_DIM
N_EXPERTS = 16
N_GROUPS = 4
EXPERTS_PER_GROUP = N_EXPERTS // N_GROUPS
ROPE_THETA = 10000.0
NORM_EPS = 1e-6
MASK_VALUE = -1e30
LOG2E = math.log2(math.e)

LANES = 128
SUBLANES = 8
V7X_VMEM_LIMIT_BYTES = 56 * 1024 * 1024

HEAD_PAD = LANES
NA_ROWS_PER_BLOCK = 8
NA_KEY_ROWS = 16
NA_BIAS_SHIFT = (NA_KEY_ROWS - NA_ROWS_PER_BLOCK) + (NA_ROWS_PER_BLOCK - 1) - (NA_WIN_R - 1)
NA_BIAS_TILES = (NA_WIN_R - 1) + NA_BIAS_SHIFT + (NA_KEY_ROWS - 2) + 1
FT_K1_PER_STEP = 8
EXPERTS_PER_STEP = 4
MOE_MAX_TILE = 768
TOKEN_TILE = 256
MLA_MAX_KV_CHUNK = 2816


def _cparams(n_axes):
    return pltpu.CompilerParams(
        dimension_semantics=("arbitrary",) * n_axes,
        vmem_limit_bytes=V7X_VMEM_LIMIT_BYTES,
    )


def _rms(x, g):
    return x * lax.rsqrt(jnp.mean(x * x, axis=-1, keepdims=True) + NORM_EPS) * g


def _dot(a, b):
    return jnp.dot(a, b, preferred_element_type=F32)


def _dot_nt(a, b):
    return lax.dot_general(a, b, (((1,), (1,)), ((), ())), preferred_element_type=F32)


def _mod_kernel(c_ref, w_ref, b_ref, o_ref):
    c = c_ref[...]
    o_ref[...] = _dot(c * jax.nn.sigmoid(c), w_ref[...]) + b_ref[...]


def _modulation(c_rows, w_ada, b_ada):
    n_layers, d, width = w_ada.shape
    rows = c_rows.shape[0]
    tn = 1536
    return pl.pallas_call(
        _mod_kernel,
        grid=(n_layers, width // tn),
        in_specs=[
            pl.BlockSpec((rows, d), lambda l, j: (0, 0)),
            pl.BlockSpec((None, d, tn), lambda l, j: (l, 0, j)),
            pl.BlockSpec((None, 1, tn), lambda l, j: (l, 0, j)),
        ],
        out_specs=pl.BlockSpec((None, rows, tn), lambda l, j: (l, 0, j)),
        out_shape=jax.ShapeDtypeStruct((n_layers, rows, width), F32),
        compiler_params=_cparams(2),
        name="modulation",
    )(c_rows, w_ada, b_ada.reshape(n_layers, 1, width))


def _proj_kernel(x_ref, mod_ref, n1g_ref, win_ref, qg_ref, wqm_ref, wqr_ref, kvg_ref, wkk_ref, wkv_ref,
                 cos_ref, sin_ref, q_ref, k_ref, v_ref, nq_ref, nk_ref, nv_ref, ft_ref):
    d = x_ref.shape[-1]
    x = x_ref[...]
    mod = mod_ref[...]
    sh1, sc1 = mod[:, 0:d], mod[:, d:2 * d]
    h = _rms(x, n1g_ref[...]) * (1.0 + sc1) + sh1
    p = _dot(h.astype(BF16), win_ref[...])
    cos = cos_ref[...]
    sin = sin_ref[...]
    o = 0
    q_lat = p[:, o:o + MLA_Q_LORA]
    o += MLA_Q_LORA
    kv_lat = p[:, o:o + MLA_KV_LORA]
    o += MLA_KV_LORA
    kr_a = p[:, o:o + HEAD_PAD]
    o += HEAD_PAD
    kr_b = p[:, o:o + HEAD_PAD]
    o += HEAD_PAD
    nq_ref[...] = (p[:, o:o + NA_WIDTH] * NA_SCALE).astype(BF16)
    o += NA_WIDTH
    nk_ref[...] = p[:, o:o + NA_WIDTH].astype(BF16)
    o += NA_WIDTH
    nv_ref[...] = p[:, o:o + NA_WIDTH].astype(BF16)
    o += NA_WIDTH
    ft_ref[...] = p[:, o:o + FT_WIDTH].astype(BF16)

    qn = _rms(q_lat, qg_ref[...]).astype(BF16)
    qm = _dot(qn, wqm_ref[...])
    qr = _dot(qn, wqr_ref[...])
    kvn = _rms(kv_lat, kvg_ref[...]).astype(BF16)
    kn = _dot(kvn, wkk_ref[...])
    v = _dot(kvn, wkv_ref[...])
    v_lane = lax.broadcasted_iota(jnp.int32, v.shape, 1) & (HEAD_PAD - 1)
    v_ref[...] = jnp.where(v_lane == MLA_V, 1.0, v).astype(BF16)
    kr = kr_a * cos + kr_b * sin
    q_scale = MLA_SCALE * LOG2E
    for hd in range(MLA_HEADS):
        sl = slice(hd * HEAD_PAD, (hd + 1) * HEAD_PAD)
        q_ref[:, sl] = ((qm[:, sl] * cos + qr[:, sl] * sin) * q_scale).astype(BF16)
        k_ref[:, sl] = (kn[:, sl] + kr).astype(BF16)


def _layer_spec(a, layer):
    return pl.BlockSpec((None,) + a.shape[1:], lambda *_: (layer,) + (0,) * (a.ndim - 1))


def _mod_spec(mod, layer, mod_row):
    return pl.BlockSpec((None, None, 1, mod.shape[-1]), lambda b, i, *_: (layer, mod_row(b, i), 0, 0))


def _project(x, mod, mod_row, lw, layer, cos_t, sin_t):
    nb, n, d = x.shape
    tm = TOKEN_TILE
    win = lw["w_in"]
    wide = MLA_HEADS * HEAD_PAD

    def full(a):
        return _layer_spec(a, layer)

    def tok(width):
        return pl.BlockSpec((None, tm, width), lambda b, i: (b, i, 0))

    outs = [wide, wide, wide, NA_WIDTH, NA_WIDTH, NA_WIDTH, FT_WIDTH]
    return pl.pallas_call(
        _proj_kernel,
        grid=(nb, n // tm),
        in_specs=[
            tok(d),
            _mod_spec(mod, layer, mod_row),
            full(lw["norm1_g"]), full(win), full(lw["q_norm_g"]), full(lw["wq_main"]), full(lw["wq_rot"]),
            full(lw["kv_norm_g"]), full(lw["wkv_k"]), full(lw["wkv_v"]),
            pl.BlockSpec((tm, HEAD_PAD), lambda b, i: (i, 0)),
            pl.BlockSpec((tm, HEAD_PAD), lambda b, i: (i, 0)),
        ],
        out_specs=[tok(w) for w in outs],
        out_shape=[jax.ShapeDtypeStruct((nb, n, w), BF16) for w in outs],
        compiler_params=_cparams(2),
        name="project",
    )(x, mod, lw["norm1_g"], win, lw["q_norm_g"], lw["wq_main"], lw["wq_rot"], lw["kv_norm_g"],
      lw["wkv_k"], lw["wkv_v"], cos_t, sin_t)


def _kv_chunk(p):
    best = LANES
    for n in range(1, p // LANES + 1):
        tk = p // n
        if p % n == 0 and tk % LANES == 0 and tk <= MLA_MAX_KV_CHUNK:
            best = max(best, tk)
    return best


def _store_head_values(o_ref, rows, hd, out):
    if hd:
        out = pltpu.roll(out, MLA_V, axis=1)
    sl = slice(hd * MLA_V, (hd + 1) * MLA_V)
    o_ref[rows, sl] = out[:, sl].astype(BF16)


def _mla_kernel(q_ref, k_ref, v_ref, o_ref, s0_scr, s1_scr, p0_scr, p1_scr, mx_scr, m_scr, alpha_scr, acc_scr,
                *, tq, tk, n_q, n_c):
    def head(hd):
        return slice(hd * HEAD_PAD, (hd + 1) * HEAD_PAD)

    def q_rows(qt):
        return pl.ds(pl.multiple_of(qt * tq, tq), tq)

    def k_rows(c):
        return pl.ds(pl.multiple_of(c * tk, tk), tk)

    def scores(hd, qt, c, s_scr):
        s = _dot_nt(q_ref[q_rows(qt), head(hd)], k_ref[k_rows(c), head(hd)])
        s_scr[...] = s
        mx_scr[hd] = jnp.broadcast_to(jnp.max(s, axis=1, keepdims=True), mx_scr.shape[1:])

    def softmax(hd, s_scr, p_scr, first):
        m_prev = jnp.where(first, MASK_VALUE, m_scr[hd])
        m_new = jnp.maximum(m_prev, mx_scr[hd])
        alpha_scr[hd] = jnp.exp2(m_prev - m_new)
        m_scr[hd] = m_new
        p_scr[...] = jnp.exp2(s_scr[...] - jnp.tile(m_new, (1, tk // LANES))).astype(BF16)

    def accumulate(hd, p_scr, qt, c):
        acc = alpha_scr[hd] * acc_scr[hd] + _dot(p_scr[...], v_ref[k_rows(c), head(hd)])
        acc_scr[hd] = acc
        _store_head_values(o_ref, q_rows(qt), hd, acc / acc[:, MLA_V:MLA_V + 1])

    lane = lax.broadcasted_iota(jnp.int32, acc_scr.shape, 2)
    acc_scr[...] = jnp.where(lane == MLA_V, 1.0, 0.0)
    m_scr[...] = jnp.full(m_scr.shape, MASK_VALUE, F32)
    alpha_scr[...] = jnp.ones(alpha_scr.shape, F32)
    p1_scr[...] = jnp.zeros(p1_scr.shape, BF16)
    scores(0, 0, 0, s0_scr)

    def body(i, carry):
        qt, c = carry
        first = c == 0
        last = c == n_c - 1
        c_prev = jnp.where(first, n_c - 1, c - 1)
        qt_prev = jnp.maximum(jnp.where(first, qt - 1, qt), 0)
        c_next = jnp.where(last, 0, c + 1)
        qt_next = jnp.where(last, qt + 1, qt)
        scores(1, qt, c, s1_scr)
        softmax(0, s0_scr, p0_scr, first)
        accumulate(1, p1_scr, qt_prev, c_prev)
        scores(0, jnp.minimum(qt_next, n_q - 1), c_next, s0_scr)
        softmax(1, s1_scr, p1_scr, first)
        accumulate(0, p0_scr, qt, c)
        return qt_next, c_next

    lax.fori_loop(0, n_q * n_c, body, (jnp.int32(0), jnp.int32(0)))
    accumulate(1, p1_scr, n_q - 1, n_c - 1)


def _mla_attention(q, k_all, v_all, s):
    nb, p, _ = k_all.shape
    tq = min(512, s)
    tk = _kv_chunk(p)
    pairs = MLA_HEADS // 2
    kern = functools.partial(_mla_kernel, tq=tq, tk=tk, n_q=s // tq, n_c=p // tk)
    return pl.pallas_call(
        kern,
        grid=(nb, pairs),
        in_specs=[
            pl.BlockSpec((None, s, 2 * HEAD_PAD), lambda b, hp: (b, 0, hp)),
            pl.BlockSpec((None, p, 2 * HEAD_PAD), lambda b, hp: (b, 0, hp)),
            pl.BlockSpec((None, p, 2 * HEAD_PAD), lambda b, hp: (b, 0, hp)),
        ],
        out_specs=pl.BlockSpec((None, s, 2 * MLA_V), lambda b, hp: (b, 0, hp)),
        out_shape=jax.ShapeDtypeStruct((nb, s, MLA_HEADS * MLA_V), BF16),
        scratch_shapes=[
            pltpu.VMEM((tq, tk), F32),
            pltpu.VMEM((tq, tk), F32),
            pltpu.VMEM((tq, tk), BF16),
            pltpu.VMEM((tq, tk), BF16),
            pltpu.VMEM((2, tq, LANES), F32),
            pltpu.VMEM((2, tq, LANES), F32),
            pltpu.VMEM((2, tq, LANES), F32),
            pltpu.VMEM((2, tq, HEAD_PAD), F32),
        ],
        compiler_params=_cparams(2),
        name="mla_attention",
    )(q, k_all, v_all)


def _mla_ctx_kernel(q_ref, k_ref, v_ref, o_ref):
    for hd in range(2):
        sl = slice(hd * HEAD_PAD, (hd + 1) * HEAD_PAD)
        s = _dot_nt(q_ref[:, sl], k_ref[:, sl])
        p = jnp.exp2(s - jnp.max(s, axis=1, keepdims=True))
        acc = _dot(p.astype(BF16), v_ref[:, sl])
        _store_head_values(o_ref, slice(None), hd, acc / acc[:, MLA_V:MLA_V + 1])


def _mla_ctx_attention(q, k, v, s):
    nb, p, _ = q.shape
    c = p - s
    pairs = MLA_HEADS // 2
    spec = pl.BlockSpec((None, c, 2 * HEAD_PAD), lambda b, hp: (b, s // c, hp))
    return pl.pallas_call(
        _mla_ctx_kernel,
        grid=(nb, pairs),
        in_specs=[spec, spec, spec],
        out_specs=pl.BlockSpec((None, c, 2 * MLA_V), lambda b, hp: (b, 0, hp)),
        out_shape=jax.ShapeDtypeStruct((nb, c, MLA_HEADS * MLA_V), BF16),
        compiler_params=_cparams(2),
        name="mla_ctx_attention",
    )(q, k, v)


def _na_kernel(q_ref, k_ref, v_ref, kc_ref, vc_ref, bias_ref, rowmask_ref, o_ref, comb_scr, s0_scr, s1_scr, p0_scr,
               p1_scr, mx_scr, l_scr, *, grid_rows, n_blocks):
    tq = NA_ROWS_PER_BLOCK * GRID_W
    n_keys = NA_KEY_ROWS * GRID_W

    for hd in range(2):
        for var, row_off in enumerate((0, -(NA_WIN_R // 2), -(NA_KEY_ROWS - NA_ROWS_PER_BLOCK))):
            tile0 = row_off + (NA_WIN_R - 1) + NA_BIAS_SHIFT
            bias = jnp.concatenate(
                [jnp.concatenate([bias_ref[hd, tile0 + 2 * j - qr] for j in range(NA_KEY_ROWS // 2)], axis=1)
                 for qr in range(NA_ROWS_PER_BLOCK)], axis=0)
            comb_scr[hd, var] = bias + rowmask_ref[var].astype(F32)

    lane = lax.broadcasted_iota(jnp.int32, (tq, LANES), 1)

    def q_rows(rb):
        return pl.ds(pl.multiple_of(rb * tq, tq), tq)

    def key_rows(rb):
        start_row = jnp.clip(rb * NA_ROWS_PER_BLOCK - NA_WIN_R // 2, 0, grid_rows - NA_KEY_ROWS)
        return pl.ds(pl.multiple_of(start_row * GRID_W, (NA_WIN_R // 2) * GRID_W), n_keys)

    def scores(hd, rb, s_scr):
        var = jnp.where(rb == 0, 0, jnp.where(rb == n_blocks - 1, 2, 1))
        q = q_ref[q_rows(rb), :]
        in_head = (lane >= NA_HEAD_DIM) if hd else (lane < NA_HEAD_DIM)
        qh = jnp.where(in_head, q, jnp.zeros_like(q))
        s = _dot_nt(qh, k_ref[key_rows(rb), :]) + comb_scr[hd, var]
        sc = _dot_nt(qh, kc_ref[...])
        s_scr[:, :n_keys] = s
        s_scr[:, n_keys:] = sc
        m = jnp.maximum(jnp.max(s, axis=1, keepdims=True), jnp.max(sc, axis=1, keepdims=True))
        mx_scr[hd] = jnp.broadcast_to(m, mx_scr.shape[1:])

    def softmax(hd, s_scr, p_scr):
        p = jnp.exp(s_scr[...] - jnp.tile(mx_scr[hd], (1, s_scr.shape[1] // LANES)))
        l_scr[hd] = jnp.broadcast_to(jnp.sum(p, axis=1, keepdims=True), l_scr.shape[1:])
        p_scr[...] = p.astype(BF16)

    def accumulate(hd, p_scr, rb):
        acc = _dot(p_scr[:, :n_keys], v_ref[key_rows(rb), :]) + _dot(p_scr[:, n_keys:], vc_ref[...])
        out = (acc / l_scr[hd]).astype(BF16)
        sl = slice(hd * NA_HEAD_DIM, (hd + 1) * NA_HEAD_DIM)
        o_ref[q_rows(rb), sl] = out[:, sl]

    p1_scr[...] = jnp.zeros(p1_scr.shape, BF16)
    l_scr[...] = jnp.ones(l_scr.shape, F32)
    scores(0, 0, s0_scr)

    def body(rb, carry):
        scores(1, rb, s1_scr)
        softmax(0, s0_scr, p0_scr)
        accumulate(1, p1_scr, jnp.maximum(rb - 1, 0))
        scores(0, jnp.minimum(rb + 1, n_blocks - 1), s0_scr)
        softmax(1, s1_scr, p1_scr)
        accumulate(0, p0_scr, rb)
        return carry

    lax.fori_loop(0, n_blocks, body, 0)
    accumulate(1, p1_scr, n_blocks - 1)


def _na_attention(q, k, v, bias, row_mask, layer, s):
    nb, p, _ = q.shape
    c = p - s
    grid_rows = s // GRID_W
    tq = NA_ROWS_PER_BLOCK * GRID_W
    n_blocks = grid_rows // NA_ROWS_PER_BLOCK
    n_keys = NA_KEY_ROWS * GRID_W
    pairs = NA_HEADS // 2
    kern = functools.partial(_na_kernel, grid_rows=grid_rows, n_blocks=n_blocks)
    lat = pl.BlockSpec((None, s, LANES), lambda b, hp: (b, 0, hp))
    ctx = pl.BlockSpec((None, c, LANES), lambda b, hp: (b, s // c, hp))
    return pl.pallas_call(
        kern,
        grid=(nb, pairs),
        in_specs=[
            lat, lat, lat, ctx, ctx,
            pl.BlockSpec((None, 2) + bias.shape[2:], lambda b, hp: (layer, hp, 0, 0, 0)),
            pl.BlockSpec(row_mask.shape, lambda b, hp: (0, 0, 0)),
        ],
        out_specs=pl.BlockSpec((None, s, LANES), lambda b, hp: (b, 0, hp)),
        out_shape=jax.ShapeDtypeStruct((nb, s, NA_WIDTH), BF16),
        scratch_shapes=[
            pltpu.VMEM((2, 3, tq, n_keys), F32),
            pltpu.VMEM((tq, n_keys + c), F32),
            pltpu.VMEM((tq, n_keys + c), F32),
            pltpu.VMEM((tq, n_keys + c), BF16),
            pltpu.VMEM((tq, n_keys + c), BF16),
            pltpu.VMEM((2, tq, LANES), F32),
            pltpu.VMEM((2, tq, LANES), F32),
        ],
        compiler_params=_cparams(2),
        name="na_attention",
    )(q, k, v, k, v, bias, row_mask)


def _na_ctx_kernel(q_ref, k_ref, v_ref, o_ref):
    q = q_ref[...]
    k = k_ref[...]
    v = v_ref[...]
    lane = lax.broadcasted_iota(jnp.int32, q.shape, 1)
    outs = []
    for hd in range(2):
        in_head = (lane >= NA_HEAD_DIM) if hd else (lane < NA_HEAD_DIM)
        qh = jnp.where(in_head, q, jnp.zeros_like(q))
        s = _dot_nt(qh, k)
        p = jnp.exp(s - jnp.max(s, axis=1, keepdims=True))
        outs.append(_dot(p.astype(BF16), v) / jnp.sum(p, axis=1, keepdims=True))
    o_ref[...] = jnp.where(lane < NA_HEAD_DIM, outs[0], outs[1]).astype(BF16)


def _na_ctx_attention(q, k, v, s):
    nb, p, _ = q.shape
    c = p - s
    spec = pl.BlockSpec((None, c, LANES), lambda b, hp: (b, s // c, hp))
    return pl.pallas_call(
        _na_ctx_kernel,
        grid=(nb, NA_HEADS // 2),
        in_specs=[spec, spec, spec],
        out_specs=pl.BlockSpec((None, c, LANES), lambda b, hp: (b, 0, hp)),
        out_shape=jax.ShapeDtypeStruct((nb, c, NA_WIDTH), BF16),
        compiler_params=_cparams(2),
        name="na_ctx_attention",
    )(q, k, v)


def _na_row_masks(grid_rows):
    qr = np.arange(NA_ROWS_PER_BLOCK)[:, None]
    kr = np.arange(NA_KEY_ROWS)[None, :]
    row_valid = np.zeros((3, NA_ROWS_PER_BLOCK, NA_KEY_ROWS), bool)
    n_blocks = grid_rows // NA_ROWS_PER_BLOCK
    for vi, rb in enumerate((0, 1, n_blocks - 1)):
        r = rb * NA_ROWS_PER_BLOCK + qr
        start_row = min(max(rb * NA_ROWS_PER_BLOCK - NA_WIN_R // 2, 0), grid_rows - NA_KEY_ROWS)
        key_row = start_row + kr
        win = np.clip(r - NA_WIN_R // 2, 0, grid_rows - NA_WIN_R)
        row_valid[vi] = (key_row >= win) & (key_row < win + NA_WIN_R)
    mask = np.where(row_valid, 0.0, MASK_VALUE).astype(np.float32)
    mask = np.broadcast_to(mask[:, :, None, :, None], (3, NA_ROWS_PER_BLOCK, GRID_W, NA_KEY_ROWS, GRID_W))
    return jnp.asarray(mask.reshape(3, NA_ROWS_PER_BLOCK * GRID_W, NA_KEY_ROWS * GRID_W), BF16)


def _na_bias_tables(rpb):
    n_dc = 2 * NA_WIN_C - 1
    qc = np.arange(GRID_W)[:, None]
    kcol = np.arange(GRID_W)[None, :]
    wstart = np.clip(qc - NA_WIN_C // 2, 0, GRID_W - NA_WIN_C)
    col_valid = (kcol >= wstart) & (kcol < wstart + NA_WIN_C)
    dc = np.clip(kcol - qc, -(NA_WIN_C - 1), NA_WIN_C - 1) + (NA_WIN_C - 1)
    col_onehot = ((dc[..., None] == np.arange(n_dc)) & col_valid[..., None]).astype(np.float32)
    t = jnp.einsum("lhij,cdj->lhicd", rpb.astype(F32), jnp.asarray(col_onehot), precision=lax.Precision.HIGHEST)
    t = jnp.where(jnp.asarray(col_valid), t, MASK_VALUE)
    n_dr = t.shape[2]
    t = jnp.pad(t, ((0, 0), (0, 0), (NA_BIAS_SHIFT, NA_BIAS_TILES + 1 - NA_BIAS_SHIFT - n_dr), (0, 0), (0, 0)),
                constant_values=MASK_VALUE)
    return jnp.concatenate([t[:, :, :NA_BIAS_TILES], t[:, :, 1:NA_BIAS_TILES + 1]], axis=-1)


FT_N1 = 64


def _ft_stage1_kernel(w_ref, x_ref, o_ref):
    o_ref[...] = _dot(w_ref[...], x_ref[...])


def _ft_stage2_kernel(are_ref, aim_ref, tc_ref, ts_ref, w_ref, cd_ref, sd_ref, o_ref, *, scale):
    n2 = are_ref.shape[1]
    for g in range(are_ref.shape[0]):
        are = are_ref[g]
        aim = aim_ref[g]
        tc = jnp.concatenate([tc_ref[g]] * (FT_WIDTH // LANES), axis=1)
        ts = jnp.concatenate([ts_ref[g]] * (FT_WIDTH // LANES), axis=1)
        bre = are * tc + aim * ts
        bim = aim * tc - are * ts
        bb = jnp.concatenate([bre, bim], axis=0).astype(BF16)
        xk = _dot(w_ref[...], bb)
        f = _dot(xk[:n2].astype(BF16), cd_ref[...]) + _dot(xk[n2:].astype(BF16), sd_ref[...])
        o_ref[:, g * FT_WIDTH:(g + 1) * FT_WIDTH] = (f * scale).astype(BF16)


def _dft_mats(n):
    ang = 2.0 * np.pi * np.outer(np.arange(n), np.arange(n)) / n
    return np.cos(ang), np.sin(ang)


def _channel_dft():
    c, s = _dft_mats(FT_GROUP_DIM)
    eye = np.eye(FT_GROUPS)
    return jnp.asarray(np.kron(eye, c), BF16), jnp.asarray(np.kron(eye, s), BF16)


def _fourier_lat(u, n):
    nb, p, _ = u.shape
    n2 = n // FT_N1
    cols = n2 * FT_WIDTH
    c1, s1 = _dft_mats(FT_N1)
    w1 = jnp.asarray(np.concatenate([c1, -s1], axis=0), BF16)
    tcol = min(2048, cols)
    a = pl.pallas_call(
        _ft_stage1_kernel,
        grid=(nb, cols // tcol),
        in_specs=[
            pl.BlockSpec(w1.shape, lambda b, j: (0, 0)),
            pl.BlockSpec((None, FT_N1, tcol), lambda b, j: (b, 0, j)),
        ],
        out_specs=pl.BlockSpec((None, 2 * FT_N1, tcol), lambda b, j: (b, 0, j)),
        out_shape=jax.ShapeDtypeStruct((nb, 2 * FT_N1, cols), F32),
        compiler_params=_cparams(2),
        name="fourier_stage1",
    )(w1, u.reshape(nb, p // n2, cols))
    a = a.reshape(nb, 2 * FT_N1, n2, FT_WIDTH)

    ang = 2.0 * np.pi * np.outer(np.arange(FT_N1), np.arange(n2)) / n
    tc = jnp.asarray(np.repeat(np.cos(ang)[:, :, None], LANES, axis=2), F32)
    ts = jnp.asarray(np.repeat(np.sin(ang)[:, :, None], LANES, axis=2), F32)
    c3, s3 = _dft_mats(n2)
    w3 = jnp.asarray(np.block([[c3, s3], [-s3, c3]]), BF16)
    cd, sd = _channel_dft()
    kern = functools.partial(_ft_stage2_kernel, scale=1.0 / math.sqrt(n * FT_GROUP_DIM))
    g = FT_K1_PER_STEP
    n_steps = FT_N1 // g
    out = pl.pallas_call(
        kern,
        grid=(nb, n_steps),
        in_specs=[
            pl.BlockSpec((None, g, n2, FT_WIDTH), lambda b, k: (b, k, 0, 0)),
            pl.BlockSpec((None, g, n2, FT_WIDTH), lambda b, k: (b, n_steps + k, 0, 0)),
            pl.BlockSpec((g, n2, LANES), lambda b, k: (k, 0, 0)),
            pl.BlockSpec((g, n2, LANES), lambda b, k: (k, 0, 0)),
            pl.BlockSpec(w3.shape, lambda b, k: (0, 0)),
            pl.BlockSpec(cd.shape, lambda b, k: (0, 0)),
            pl.BlockSpec(sd.shape, lambda b, k: (0, 0)),
        ],
        out_specs=pl.BlockSpec((None, n2, g * FT_WIDTH), lambda b, k: (b, 0, k)),
        out_shape=jax.ShapeDtypeStruct((nb, n2, FT_N1 * FT_WIDTH), BF16),
        compiler_params=_cparams(2),
        name="fourier_stage2",
    )(a, a, tc, ts, w3, cd, sd)
    return out.reshape(nb, n, FT_WIDTH)


def _ft_ctx_kernel(u_ref, cn_ref, sn_ref, cd_ref, sd_ref, o_ref, *, scale):
    u = u_ref[...]
    y1 = _dot(u, cd_ref[...]).astype(BF16)
    y2 = _dot(u, sd_ref[...]).astype(BF16)
    f = _dot(cn_ref[...], y1) - _dot(sn_ref[...], y2)
    o_ref[...] = (f * scale).astype(BF16)


def _fourier_ctx(u, s):
    nb, p, _ = u.shape
    n = p - s
    cn, sn = _dft_mats(n)
    cn, sn = jnp.asarray(cn, BF16), jnp.asarray(sn, BF16)
    cd, sd = _channel_dft()
    kern = functools.partial(_ft_ctx_kernel, scale=1.0 / math.sqrt(n * FT_GROUP_DIM))
    sq = pl.BlockSpec((n, n), lambda b: (0, 0))
    ch = pl.BlockSpec(cd.shape, lambda b: (0, 0))
    return pl.pallas_call(
        kern,
        grid=(nb,),
        in_specs=[pl.BlockSpec((None, n, FT_WIDTH), lambda b: (b, s // n, 0)), sq, sq, ch, ch],
        out_specs=pl.BlockSpec((None, n, FT_WIDTH), lambda b: (b, 0, 0)),
        out_shape=jax.ShapeDtypeStruct((nb, n, FT_WIDTH), BF16),
        compiler_params=_cparams(1),
        name="fourier_ctx",
    )(u, cn, sn, cd, sd)


def _merge_kernel(x_ref, oa_ref, ob_ref, oc_ref, ca_ref, cb_ref, cc_ref, mod_ref, n1g_ref, n2g_ref, wgate_ref,
                  bgate_ref, wba_ref, wbb_ref, wbc_ref, wout_ref, x1_ref, h2_ref, *, n_lat_tiles):
    d = x_ref.shape[-1]
    x = x_ref[...]
    is_ctx = pl.program_id(1) >= n_lat_tiles
    o_a = jnp.where(is_ctx, ca_ref[...], oa_ref[...])
    o_b = jnp.where(is_ctx, cb_ref[...], ob_ref[...])
    o_c = jnp.where(is_ctx, cc_ref[...], oc_ref[...])
    mod = mod_ref[...]
    sh1, sc1, g1, sh2, sc2 = (mod[:, i * d:(i + 1) * d] for i in range(5))
    h = _rms(x, n1g_ref[...]) * (1.0 + sc1) + sh1
    gate = jax.nn.sigmoid(_dot(h.astype(BF16), wgate_ref[...]) + bgate_ref[...])
    m = gate[:, 0:d] * _dot(o_a, wba_ref[...])
    m += gate[:, d:2 * d] * _dot(o_b, wbb_ref[...])
    m += gate[:, 2 * d:3 * d] * _dot(o_c, wbc_ref[...])
    y = _dot(m.astype(BF16), wout_ref[...])
    x1 = x + g1 * y
    x1_ref[...] = x1
    h2_ref[...] = (_rms(x1, n2g_ref[...]) * (1.0 + sc2) + sh2).astype(BF16)


def _merge(x, lat_outs, ctx_outs, mod, mod_row, lw, layer):
    nb, n, d = x.shape
    tm = TOKEN_TILE
    n_lat_tiles = lat_outs[0].shape[1] // tm

    def full(a):
        return _layer_spec(a, layer)

    def tok(width):
        return pl.BlockSpec((None, tm, width), lambda b, i: (b, i, 0))

    def lat(a):
        return pl.BlockSpec((None, tm, a.shape[-1]), lambda b, i: (b, jnp.minimum(i, n_lat_tiles - 1), 0))

    def ctx(a):
        return pl.BlockSpec((None, tm, a.shape[-1]), lambda b, i: (b, jnp.maximum(i - n_lat_tiles, 0), 0))

    weights = [lw["norm1_g"], lw["norm2_g"], lw["w_gate"], lw["b_gate"], lw["w_br_mla"], lw["w_br_na"],
               lw["w_br_ft"], lw["w_out"]]
    return pl.pallas_call(
        functools.partial(_merge_kernel, n_lat_tiles=n_lat_tiles),
        grid=(nb, n // tm),
        in_specs=[tok(d)] + [lat(a) for a in lat_outs] + [ctx(a) for a in ctx_outs] + [_mod_spec(mod, layer, mod_row)]
        + [full(w) for w in weights],
        out_specs=[tok(d), tok(d)],
        out_shape=[jax.ShapeDtypeStruct((nb, n, d), F32), jax.ShapeDtypeStruct((nb, n, d), BF16)],
        compiler_params=_cparams(2),
        name="merge",
    )(x, *lat_outs, *ctx_outs, mod, *weights)


def _route(logits, router_bias):
    row = lax.broadcasted_iota(jnp.int32, logits.shape, 0)
    neg = -jnp.inf
    big = jnp.int32(N_EXPERTS)
    scores = jax.nn.sigmoid(logits)
    sel = scores + router_bias

    def top2(vals):
        m1 = jnp.max(vals, axis=0, keepdims=True)
        i1 = jnp.min(jnp.where(vals == m1, row, big), axis=0, keepdims=True)
        rest = jnp.where(row == i1, neg, vals)
        m2 = jnp.max(rest, axis=0, keepdims=True)
        i2 = jnp.min(jnp.where(rest == m2, row, big), axis=0, keepdims=True)
        return m1, i1, m2, i2

    grp_scores = []
    for g in range(N_GROUPS):
        in_grp = (row >= g * EXPERTS_PER_GROUP) & (row < (g + 1) * EXPERTS_PER_GROUP)
        m1, _, m2, _ = top2(jnp.where(in_grp, sel, neg))
        grp_scores.append(m1 + m2)
    best = functools.reduce(jnp.maximum, grp_scores)
    grp = jnp.full(best.shape, N_GROUPS - 1, jnp.int32)
    for g in range(N_GROUPS - 2, -1, -1):
        grp = jnp.where(grp_scores[g] == best, g, grp)
    in_sel = (row >= grp * EXPERTS_PER_GROUP) & (row < (grp + 1) * EXPERTS_PER_GROUP)
    _, i1, _, i2 = top2(jnp.where(in_sel, sel, neg))
    hit1 = row == i1
    hit2 = row == i2
    w1 = jnp.sum(jnp.where(hit1, scores, 0.0), axis=0, keepdims=True)
    w2 = jnp.sum(jnp.where(hit2, scores, 0.0), axis=0, keepdims=True)
    total = w1 + w2
    return jnp.where(hit1, w1 / total, 0.0) + jnp.where(hit2, w2 / total, 0.0), grp


def _split_bf16(v):
    hi = v.astype(BF16)
    return hi, (v - hi.astype(F32)).astype(BF16)


def _moe_kernel(h_ref, x_ref, mod_ref, cmod_ref, rw_ref, rb_ref, fg_ref, tri_ref, wg_ref, wu_ref, wd_ref, o_ref,
                col_scr, row_scr, acc_scr, *, apply_final_norm, n_lat, cap):
    d = x_ref.shape[-1]
    tm = x_ref.shape[0]
    step = pl.program_id(2)
    n_steps = pl.num_programs(2)
    h = h_ref[...]

    @pl.when(step == 0)
    def _():
        gate_t, grp = _route(_dot_nt(rw_ref[...], h), rb_ref[...])
        member = jnp.where(lax.broadcasted_iota(jnp.int32, (SUBLANES, tm), 0) == grp, 1.0, 0.0)
        rank = _dot(member.astype(BF16), tri_ref[...])
        empty_rows = jnp.zeros((SUBLANES - EXPERTS_PER_GROUP - 2, tm), F32)
        per_group = [jnp.concatenate([gate_t[c * EXPERTS_PER_GROUP:(c + 1) * EXPERTS_PER_GROUP], rank[c:c + 1],
                                      member[c:c + 1], empty_rows], axis=0) for c in range(N_GROUPS)]
        pad_rows = jnp.zeros((LANES - N_GROUPS * SUBLANES, tm), F32)
        cols = jnp.concatenate(per_group + [pad_rows], axis=0).T
        for c in range(N_GROUPS):
            row_scr[c] = per_group[c]
            col_scr[c] = pltpu.roll(cols, (LANES - c * SUBLANES) % LANES, axis=1) if c else cols
        acc_scr[...] = jnp.zeros(acc_scr.shape, F32)

    def experts(hx, gates):
        out = jnp.zeros((hx.shape[0], d), F32)
        for j in range(EXPERTS_PER_STEP):
            a = _dot(hx, wg_ref[j])
            u = _dot(hx, wu_ref[j])
            hid = (a * jax.nn.sigmoid(a)) * u * gates[:, j:j + 1]
            out += _dot(hid.astype(BF16), wd_ref[j])
        return out

    rank_lane, member_lane = EXPERTS_PER_GROUP, EXPERTS_PER_GROUP + 1
    rank_row = row_scr[step, rank_lane:rank_lane + 1, :]
    member_row = row_scr[step, member_lane:member_lane + 1, :]
    n_members = jnp.sum(member_row)

    @pl.when(n_members <= cap)
    def _():
        slot = lax.broadcasted_iota(jnp.int32, (cap, tm), 0).astype(F32)
        pick = jnp.where((slot == rank_row) & (member_row > 0.5), 1.0, 0.0).astype(BF16)
        hx = _dot(pick, h).astype(BF16)
        col = col_scr[step]
        g_hi, g_lo = _split_bf16(col)
        y = experts(hx, _dot(pick, g_hi) + _dot(pick, g_lo))
        slot_t = lax.broadcasted_iota(jnp.int32, (tm, cap), 1).astype(F32)
        put = jnp.where((slot_t == col[:, rank_lane:rank_lane + 1]) & (col[:, member_lane:member_lane + 1] > 0.5),
                        1.0, 0.0).astype(BF16)
        y_hi, y_lo = _split_bf16(y)
        acc_scr[...] += _dot(put, y_hi) + _dot(put, y_lo)

    @pl.when(n_members > cap)
    def _():
        acc_scr[...] += experts(h, col_scr[step])

    @pl.when(step == n_steps - 1)
    def _():
        tok_row = pl.program_id(1) * tm + lax.broadcasted_iota(jnp.int32, (tm, 1), 0)
        g2 = jnp.where(tok_row >= n_lat, cmod_ref[:, 5 * d:6 * d], mod_ref[:, 5 * d:6 * d])
        x2 = x_ref[...] + g2 * acc_scr[...]
        o_ref[...] = _rms(x2, fg_ref[...]) if apply_final_norm else x2


def _moe_tile(p):
    return max(t for t in range(TOKEN_TILE, MOE_MAX_TILE + 1, TOKEN_TILE) if p % t == 0)


def _moe_capacity(tm):
    return min(tm, -(-(5 * tm // (4 * N_GROUPS)) // LANES) * LANES)


def _moe(h2, x1, mod, ctx_row, lw, layer, router_w, router_b, final_g, n_lat, n_rows, tm, apply_final_norm):
    assert EXPERTS_PER_STEP == EXPERTS_PER_GROUP
    nb, _, d = x1.shape
    n_steps = N_EXPERTS // EXPERTS_PER_STEP
    ff = lw["w_e_gate"].shape[-1]
    tok = pl.BlockSpec((None, tm, d), lambda b, i, c: (b, i, 0))
    earlier = jnp.asarray(np.triu(np.ones((tm, tm), np.float32), k=1), BF16)
    kern = functools.partial(_moe_kernel, apply_final_norm=apply_final_norm, n_lat=n_lat, cap=_moe_capacity(tm))
    return pl.pallas_call(
        kern,
        grid=(nb, n_rows // tm, n_steps),
        in_specs=[
            tok, tok,
            _mod_spec(mod, layer, lambda b, i: b),
            _mod_spec(mod, layer, lambda b, i: ctx_row),
            pl.BlockSpec(router_w.shape, lambda b, i, c: (0, 0)),
            pl.BlockSpec(router_b.shape, lambda b, i, c: (0, 0)),
            pl.BlockSpec(final_g.shape, lambda b, i, c: (0, 0)),
            pl.BlockSpec(earlier.shape, lambda b, i, c: (0, 0)),
            pl.BlockSpec((None, EXPERTS_PER_STEP, d, ff), lambda b, i, c: (layer, c, 0, 0)),
            pl.BlockSpec((None, EXPERTS_PER_STEP, d, ff), lambda b, i, c: (layer, c, 0, 0)),
            pl.BlockSpec((None, EXPERTS_PER_STEP, ff, d), lambda b, i, c: (layer, c, 0, 0)),
        ],
        out_specs=tok,
        out_shape=jax.ShapeDtypeStruct((nb, n_rows, d), F32),
        scratch_shapes=[
            pltpu.VMEM((n_steps, tm, LANES), F32),
            pltpu.VMEM((n_steps, SUBLANES, tm), F32),
            pltpu.VMEM((tm, d), F32),
        ],
        compiler_params=_cparams(3),
        name="moe",
    )(h2, x1, mod, mod, router_w, router_b, final_g, earlier, lw["w_e_gate"], lw["w_e_up"], lw["w_e_down"])


def _rotate_cols(w):
    half = w.shape[-1] // 2
    return jnp.concatenate([-w[..., half:], w[..., :half]], axis=-1)


def _prepare_weights(w_in, q_norm_g, w_q_up, kv_norm_g, w_kv_up, norm1_g, norm2_g, w_gate, b_gate, w_br_mla,
                     w_br_na, w_br_ft, w_out, w_e_gate, w_e_up, w_e_down):
    n_layers, d, _ = w_in.shape
    splits = np.cumsum([MLA_Q_LORA, MLA_KV_LORA, MLA_ROPE, NA_WIDTH, NA_WIDTH, NA_WIDTH])
    w_q, w_kv, w_kr, w_nq, w_nk, w_nv, w_ft = jnp.split(w_in, [int(s) for s in splits], axis=-1)

    def in_rope_slot(w):
        zeros_lo = jnp.zeros(w.shape[:-1] + (MLA_NOPE,), w.dtype)
        zeros_hi = jnp.zeros(w.shape[:-1] + (HEAD_PAD - MLA_NOPE - MLA_ROPE,), w.dtype)
        return jnp.concatenate([zeros_lo, w, zeros_hi], axis=-1)

    win = jnp.concatenate([w_q, w_kv, in_rope_slot(w_kr), in_rope_slot(_rotate_cols(w_kr)), w_nq, w_nk, w_nv, w_ft],
                          axis=-1).astype(BF16)
    wq = w_q_up.reshape(n_layers, MLA_Q_LORA, MLA_HEADS, MLA_NOPE + MLA_ROPE)
    wq_nope, wq_pe = wq[..., :MLA_NOPE], wq[..., MLA_NOPE:]
    pad = jnp.zeros(wq_pe.shape[:-1] + (HEAD_PAD - MLA_NOPE - MLA_ROPE,), wq.dtype)
    wq_main = jnp.concatenate([wq_nope, wq_pe, pad], axis=-1)
    wq_rot = jnp.concatenate([jnp.zeros_like(wq_nope), _rotate_cols(wq_pe), pad], axis=-1)
    wkv = w_kv_up.reshape(n_layers, MLA_KV_LORA, MLA_HEADS, MLA_NOPE + MLA_V)
    wk_nope, wv = wkv[..., :MLA_NOPE], wkv[..., MLA_NOPE:]
    wkv_k = jnp.concatenate([wk_nope, jnp.zeros(wk_nope.shape[:-1] + (HEAD_PAD - MLA_NOPE,), wkv.dtype)], axis=-1)
    wide = MLA_HEADS * HEAD_PAD
    return {
        "w_in": win,
        "wq_main": wq_main.reshape(n_layers, MLA_Q_LORA, wide).astype(BF16),
        "wq_rot": wq_rot.reshape(n_layers, MLA_Q_LORA, wide).astype(BF16),
        "wkv_k": wkv_k.reshape(n_layers, MLA_KV_LORA, wide).astype(BF16),
        "wkv_v": jnp.concatenate([wv, jnp.zeros(wv.shape[:-1] + (HEAD_PAD - MLA_V,), wv.dtype)], axis=-1)
        .reshape(n_layers, MLA_KV_LORA, wide).astype(BF16),
        "q_norm_g": q_norm_g.reshape(n_layers, 1, -1),
        "kv_norm_g": kv_norm_g.reshape(n_layers, 1, -1),
        "norm1_g": norm1_g.reshape(n_layers, 1, d),
        "norm2_g": norm2_g.reshape(n_layers, 1, d),
        "w_gate": w_gate.astype(BF16),
        "b_gate": b_gate.reshape(n_layers, 1, -1),
        "w_br_mla": w_br_mla.astype(BF16),
        "w_br_na": w_br_na.astype(BF16),
        "w_br_ft": w_br_ft.astype(BF16),
        "w_out": w_out.astype(BF16),
        "w_e_gate": w_e_gate.astype(BF16),
        "w_e_up": w_e_up.astype(BF16),
        "w_e_down": w_e_down.astype(BF16),
    }


def _rope_tables(s, c):
    n_freq = MLA_ROPE // 4
    inv_freq = ROPE_THETA ** (-jnp.arange(n_freq, dtype=F32) / n_freq)
    t = jnp.arange(s, dtype=jnp.int32)
    row = (t // GRID_W).astype(F32)
    col = (t % GRID_W).astype(F32)
    ang = jnp.concatenate([row[:, None] * inv_freq, col[:, None] * inv_freq], axis=-1)
    cos, sin = jnp.cos(ang), jnp.sin(ang)
    pad = HEAD_PAD - MLA_NOPE - MLA_ROPE
    cos_lat = jnp.concatenate([jnp.ones((s, MLA_NOPE), F32), cos, cos, jnp.zeros((s, pad), F32)], axis=-1)
    sin_lat = jnp.concatenate([jnp.zeros((s, MLA_NOPE), F32), sin, sin, jnp.zeros((s, pad), F32)], axis=-1)
    cos_ctx = jnp.concatenate([jnp.ones((c, MLA_NOPE + MLA_ROPE), F32), jnp.zeros((c, pad), F32)], axis=-1)
    sin_ctx = jnp.zeros((c, HEAD_PAD), F32)
    return cos_lat, sin_lat, cos_ctx, sin_ctx


@jax.jit
def _forward(x, c, ctx, c_ctx, w_ada, b_ada, norm1_g, norm2_g, w_in, q_norm_g, w_q_up, kv_norm_g, w_kv_up,
             na_rpb, w_gate, b_gate, w_br_mla, w_br_na, w_br_ft, w_out, router_w, router_bias, w_e_gate,
             w_e_up, w_e_down, final_norm_g):
    nb, s, d = x.shape
    n_ctx = ctx.shape[1]
    n_layers = w_ada.shape[0]
    grid_rows = s // GRID_W

    mod_rows = 8
    c_rows = jnp.zeros((mod_rows, d), F32).at[:nb].set(c).at[nb].set(c_ctx)
    mod_all = _modulation(c_rows, w_ada, b_ada).reshape(n_layers, mod_rows, 1, 6 * d)

    weights = _prepare_weights(w_in, q_norm_g, w_q_up, kv_norm_g, w_kv_up, norm1_g, norm2_g, w_gate, b_gate,
                               w_br_mla, w_br_na, w_br_ft, w_out, w_e_gate, w_e_up, w_e_down)
    na_bias = _na_bias_tables(na_rpb)
    na_row_mask = _na_row_masks(grid_rows)
    cos_lat, sin_lat, cos_ctx, sin_ctx = _rope_tables(s, n_ctx)
    cos_t = jnp.concatenate([cos_lat, cos_ctx], axis=0)
    sin_t = jnp.concatenate([sin_lat, sin_ctx], axis=0)
    router_w_p = router_w.T.astype(BF16)
    router_b_p = router_bias.reshape(N_EXPERTS, 1).astype(F32)
    final_g = final_norm_g.reshape(1, d)

    p = s + n_ctx
    n_lat_tiles = s // TOKEN_TILE
    ctx_row = nb

    def tile_row(b, i):
        return jnp.where(i >= n_lat_tiles, ctx_row, b)

    x_all = jnp.concatenate([x, ctx], axis=1)
    zero_ctx = [jnp.zeros((nb, n_ctx, w), BF16) for w in (MLA_HEADS * MLA_V, NA_WIDTH, FT_WIDTH)]
    for l in range(n_layers):
        last = l == n_layers - 1
        q, k, v, nq, nk, nv, ft = _project(x_all, mod_all, tile_row, weights, l, cos_t, sin_t)
        lat_outs = [_mla_attention(q, k, v, s), _na_attention(nq, nk, nv, na_bias, na_row_mask, l, s),
                    _fourier_lat(ft, s)]
        if last:
            ctx_outs = zero_ctx
        else:
            ctx_outs = [_mla_ctx_attention(q, k, v, s), _na_ctx_attention(nq, nk, nv, s), _fourier_ctx(ft, s)]
        x1, h2 = _merge(x_all, lat_outs, ctx_outs, mod_all, tile_row, weights, l)
        if last:
            return _moe(h2, x1, mod_all, ctx_row, weights, l, router_w_p, router_b_p, final_g, s, s,
                        _moe_tile(s), True)
        x_all = _moe(h2, x1, mod_all, ctx_row, weights, l, router_w_p, router_b_p, final_g, s, p,
                     _moe_tile(p), False)


def kernel(x, c, ctx, c_ctx, w_ada, b_ada, norm1_g, norm2_g, w_in, q_norm_g, w_q_up, kv_norm_g, w_kv_up, na_rpb, w_gate, b_gate, w_br_mla, w_br_na, w_br_ft, w_out, router_w, router_bias, w_e_gate, w_e_up, w_e_down, final_norm_g):
    return _forward(x, c, ctx, c_ctx, w_ada, b_ada, norm1_g, norm2_g, w_in, q_norm_g, w_q_up, kv_norm_g,
                    w_kv_up, na_rpb, w_gate, b_gate, w_br_mla, w_br_na, w_br_ft, w_out, router_w, router_bias,
                    w_e_gate, w_e_up, w_e_down, final_norm_g)
```

```python
import functools
import math

import jax
import jax.numpy as jnp
import numpy as np
from jax import lax
from jax.experimental import pallas as pl
from jax.experimental.pallas import tpu as pltpu

F32 = jnp.float32
BF16 = jnp.bfloat16

GRID_W = 64
MLA_HEADS = 8
MLA_NOPE = 64
MLA_ROPE = 32
MLA_V = 64
MLA_Q_LORA = 256
MLA_KV_LORA = 128
MLA_SCALE = (MLA_NOPE + MLA_ROPE) ** -0.5
NA_HEADS = 4
NA_HEAD_DIM = 64
NA_WIN_R = 8
NA_WIN_C = 16
NA_SCALE = NA_HEAD_DIM ** -0.5
NA_WIDTH = NA_HEADS * NA_HEAD_DIM
FT_GROUPS = 4
FT_GROUP_DIM = 64
FT_WIDTH = FT_GROUPS * FT_GROUP_DIM
N_EXPERTS = 16
N_GROUPS = 4
EXPERTS_PER_GROUP = N_EXPERTS // N_GROUPS
ROPE_THETA = 10000.0
NORM_EPS = 1e-6
MASK_VALUE = -1e30
LOG2E = math.log2(math.e)

LANES = 128
SUBLANES = 8
V7X_VMEM_LIMIT_BYTES = 56 * 1024 * 1024

HEAD_PAD = LANES
NA_ROWS_PER_BLOCK = 8
NA_KEY_ROWS = 16
NA_BIAS_SHIFT = (NA_KEY_ROWS - NA_ROWS_PER_BLOCK) + (NA_ROWS_PER_BLOCK - 1) - (NA_WIN_R - 1)
NA_BIAS_TILES = (NA_WIN_R - 1) + NA_BIAS_SHIFT + (NA_KEY_ROWS - 2) + 1
FT_K1_PER_STEP = 8
EXPERTS_PER_STEP = 4
MOE_MAX_TILE = 768
TOKEN_TILE = 256
MLA_MAX_KV_CHUNK = 2816


def _cparams(n_axes):
    return pltpu.CompilerParams(
        dimension_semantics=("arbitrary",) * n_axes,
        vmem_limit_bytes=V7X_VMEM_LIMIT_BYTES,
    )


def _rms(x, g):
    return x * lax.rsqrt(jnp.mean(x * x, axis=-1, keepdims=True) + NORM_EPS) * g


def _dot(a, b):
    return jnp.dot(a, b, preferred_element_type=F32)


def _dot_nt(a, b):
    return lax.dot_general(a, b, (((1,), (1,)), ((), ())), preferred_element_type=F32)


def _mod_kernel(c_ref, w_ref, b_ref, o_ref):
    c = c_ref[...]
    o_ref[...] = _dot(c * jax.nn.sigmoid(c), w_ref[...]) + b_ref[...]


def _modulation(c_rows, w_ada, b_ada):
    n_layers, d, width = w_ada.shape
    rows = c_rows.shape[0]
    tn = 1536
    return pl.pallas_call(
        _mod_kernel,
        grid=(n_layers, width // tn),
        in_specs=[
            pl.BlockSpec((rows, d), lambda l, j: (0, 0)),
            pl.BlockSpec((None, d, tn), lambda l, j: (l, 0, j)),
            pl.BlockSpec((None, 1, tn), lambda l, j: (l, 0, j)),
        ],
        out_specs=pl.BlockSpec((None, rows, tn), lambda l, j: (l, 0, j)),
        out_shape=jax.ShapeDtypeStruct((n_layers, rows, width), F32),
        compiler_params=_cparams(2),
        name="modulation",
    )(c_rows, w_ada, b_ada.reshape(n_layers, 1, width))


def _proj_kernel(x_ref, mod_ref, n1g_ref, win_ref, qg_ref, wqm_ref, wqr_ref, kvg_ref, wkk_ref, wkv_ref,
                 cos_ref, sin_ref, q_ref, k_ref, v_ref, nq_ref, nk_ref, nv_ref, ft_ref):
    d = x_ref.shape[-1]
    x = x_ref[...]
    mod = mod_ref[...]
    sh1, sc1 = mod[:, 0:d], mod[:, d:2 * d]
    h = _rms(x, n1g_ref[...]) * (1.0 + sc1) + sh1
    p = _dot(h.astype(BF16), win_ref[...])
    cos = cos_ref[...]
    sin = sin_ref[...]
    o = 0
    q_lat = p[:, o:o + MLA_Q_LORA]
    o += MLA_Q_LORA
    kv_lat = p[:, o:o + MLA_KV_LORA]
    o += MLA_KV_LORA
    kr_a = p[:, o:o + HEAD_PAD]
    o += HEAD_PAD
    kr_b = p[:, o:o + HEAD_PAD]
    o += HEAD_PAD
    nq_ref[...] = (p[:, o:o + NA_WIDTH] * NA_SCALE).astype(BF16)
    o += NA_WIDTH
    nk_ref[...] = p[:, o:o + NA_WIDTH].astype(BF16)
    o += NA_WIDTH
    nv_ref[...] = p[:, o:o + NA_WIDTH].astype(BF16)
    o += NA_WIDTH
    ft_ref[...] = p[:, o:o + FT_WIDTH].astype(BF16)

    qn = _rms(q_lat, qg_ref[...]).astype(BF16)
    qm = _dot(qn, wqm_ref[...])
    qr = _dot(qn, wqr_ref[...])
    kvn = _rms(kv_lat, kvg_ref[...]).astype(BF16)
    kn = _dot(kvn, wkk_ref[...])
    v = _dot(kvn, wkv_ref[...])
    v_lane = lax.broadcasted_iota(jnp.int32, v.shape, 1) & (HEAD_PAD - 1)
    v_ref[...] = jnp.where(v_lane == MLA_V, 1.0, v).astype(BF16)
    kr = kr_a * cos + kr_b * sin
    q_scale = MLA_SCALE * LOG2E
    for hd in range(MLA_HEADS):
        sl = slice(hd * HEAD_PAD, (hd + 1) * HEAD_PAD)
        q_ref[:, sl] = ((qm[:, sl] * cos + qr[:, sl] * sin) * q_scale).astype(BF16)
        k_ref[:, sl] = (kn[:, sl] + kr).astype(BF16)


def _layer_spec(a, layer):
    return pl.BlockSpec((None,) + a.shape[1:], lambda *_: (layer,) + (0,) * (a.ndim - 1))


def _mod_spec(mod, layer, mod_row):
    return pl.BlockSpec((None, None, 1, mod.shape[-1]), lambda b, i, *_: (layer, mod_row(b, i), 0, 0))


def _project(x, mod, mod_row, lw, layer, cos_t, sin_t):
    nb, n, d = x.shape
    tm = TOKEN_TILE
    win = lw["w_in"]
    wide = MLA_HEADS * HEAD_PAD

    def full(a):
        return _layer_spec(a, layer)

    def tok(width):
        return pl.BlockSpec((None, tm, width), lambda b, i: (b, i, 0))

    outs = [wide, wide, wide, NA_WIDTH, NA_WIDTH, NA_WIDTH, FT_WIDTH]
    return pl.pallas_call(
        _proj_kernel,
        grid=(nb, n // tm),
        in_specs=[
            tok(d),
            _mod_spec(mod, layer, mod_row),
            full(lw["norm1_g"]), full(win), full(lw["q_norm_g"]), full(lw["wq_main"]), full(lw["wq_rot"]),
            full(lw["kv_norm_g"]), full(lw["wkv_k"]), full(lw["wkv_v"]),
            pl.BlockSpec((tm, HEAD_PAD), lambda b, i: (i, 0)),
            pl.BlockSpec((tm, HEAD_PAD), lambda b, i: (i, 0)),
        ],
        out_specs=[tok(w) for w in outs],
        out_shape=[jax.ShapeDtypeStruct((nb, n, w), BF16) for w in outs],
        compiler_params=_cparams(2),
        name="project",
    )(x, mod, lw["norm1_g"], win, lw["q_norm_g"], lw["wq_main"], lw["wq_rot"], lw["kv_norm_g"],
      lw["wkv_k"], lw["wkv_v"], cos_t, sin_t)


def _kv_chunk(p):
    best = LANES
    for n in range(1, p // LANES + 1):
        tk = p // n
        if p % n == 0 and tk % LANES == 0 and tk <= MLA_MAX_KV_CHUNK:
            best = max(best, tk)
    return best


def _store_head_values(o_ref, rows, hd, out):
    if hd:
        out = pltpu.roll(out, MLA_V, axis=1)
    sl = slice(hd * MLA_V, (hd + 1) * MLA_V)
    o_ref[rows, sl] = out[:, sl].astype(BF16)


def _mla_kernel(q_ref, k_ref, v_ref, o_ref, s0_scr, s1_scr, p0_scr, p1_scr, mx_scr, m_scr, alpha_scr, acc_scr,
                *, tq, tk, n_q, n_c):
    def head(hd):
        return slice(hd * HEAD_PAD, (hd + 1) * HEAD_PAD)

    def q_rows(qt):
        return pl.ds(pl.multiple_of(qt * tq, tq), tq)

    def k_rows(c):
        return pl.ds(pl.multiple_of(c * tk, tk), tk)

    def scores(hd, qt, c, s_scr):
        s = _dot_nt(q_ref[q_rows(qt), head(hd)], k_ref[k_rows(c), head(hd)])
        s_scr[...] = s
        mx_scr[hd] = jnp.broadcast_to(jnp.max(s, axis=1, keepdims=True), mx_scr.shape[1:])

    def softmax(hd, s_scr, p_scr, first):
        m_prev = jnp.where(first, MASK_VALUE, m_scr[hd])
        m_new = jnp.maximum(m_prev, mx_scr[hd])
        alpha_scr[hd] = jnp.exp2(m_prev - m_new)
        m_scr[hd] = m_new
        p_scr[...] = jnp.exp2(s_scr[...] - jnp.tile(m_new, (1, tk // LANES))).astype(BF16)

    def accumulate(hd, p_scr, qt, c):
        acc = alpha_scr[hd] * acc_scr[hd] + _dot(p_scr[...], v_ref[k_rows(c), head(hd)])
        acc_scr[hd] = acc
        _store_head_values(o_ref, q_rows(qt), hd, acc / acc[:, MLA_V:MLA_V + 1])

    lane = lax.broadcasted_iota(jnp.int32, acc_scr.shape, 2)
    acc_scr[...] = jnp.where(lane == MLA_V, 1.0, 0.0)
    m_scr[...] = jnp.full(m_scr.shape, MASK_VALUE, F32)
    alpha_scr[...] = jnp.ones(alpha_scr.shape, F32)
    p1_scr[...] = jnp.zeros(p1_scr.shape, BF16)
    scores(0, 0, 0, s0_scr)

    def body(i, carry):
        qt, c = carry
        first = c == 0
        last = c == n_c - 1
        c_prev = jnp.where(first, n_c - 1, c - 1)
        qt_prev = jnp.maximum(jnp.where(first, qt - 1, qt), 0)
        c_next = jnp.where(last, 0, c + 1)
        qt_next = jnp.where(last, qt + 1, qt)
        scores(1, qt, c, s1_scr)
        softmax(0, s0_scr, p0_scr, first)
        accumulate(1, p1_scr, qt_prev, c_prev)
        scores(0, jnp.minimum(qt_next, n_q - 1), c_next, s0_scr)
        softmax(1, s1_scr, p1_scr, first)
        accumulate(0, p0_scr, qt, c)
        return qt_next, c_next

    lax.fori_loop(0, n_q * n_c, body, (jnp.int32(0), jnp.int32(0)))
    accumulate(1, p1_scr, n_q - 1, n_c - 1)


def _mla_attention(q, k_all, v_all, s):
    nb, p, _ = k_all.shape
    tq = min(512, s)
    tk = _kv_chunk(p)
    pairs = MLA_HEADS // 2
    kern = functools.partial(_mla_kernel, tq=tq, tk=tk, n_q=s // tq, n_c=p // tk)
    return pl.pallas_call(
        kern,
        grid=(nb, pairs),
        in_specs=[
            pl.BlockSpec((None, s, 2 * HEAD_PAD), lambda b, hp: (b, 0, hp)),
            pl.BlockSpec((None, p, 2 * HEAD_PAD), lambda b, hp: (b, 0, hp)),
            pl.BlockSpec((None, p, 2 * HEAD_PAD), lambda b, hp: (b, 0, hp)),
        ],
        out_specs=pl.BlockSpec((None, s, 2 * MLA_V), lambda b, hp: (b, 0, hp)),
        out_shape=jax.ShapeDtypeStruct((nb, s, MLA_HEADS * MLA_V), BF16),
        scratch_shapes=[
            pltpu.VMEM((tq, tk), F32),
            pltpu.VMEM((tq, tk), F32),
            pltpu.VMEM((tq, tk), BF16),
            pltpu.VMEM((tq, tk), BF16),
            pltpu.VMEM((2, tq, LANES), F32),
            pltpu.VMEM((2, tq, LANES), F32),
            pltpu.VMEM((2, tq, LANES), F32),
            pltpu.VMEM((2, tq, HEAD_PAD), F32),
        ],
        compiler_params=_cparams(2),
        name="mla_attention",
    )(q, k_all, v_all)


def _mla_ctx_kernel(q_ref, k_ref, v_ref, o_ref):
    for hd in range(2):
        sl = slice(hd * HEAD_PAD, (hd + 1) * HEAD_PAD)
        s = _dot_nt(q_ref[:, sl], k_ref[:, sl])
        p = jnp.exp2(s - jnp.max(s, axis=1, keepdims=True))
        acc = _dot(p.astype(BF16), v_ref[:, sl])
        _store_head_values(o_ref, slice(None), hd, acc / acc[:, MLA_V:MLA_V + 1])


def _mla_ctx_attention(q, k, v, s):
    nb, p, _ = q.shape
    c = p - s
    pairs = MLA_HEADS // 2
    spec = pl.BlockSpec((None, c, 2 * HEAD_PAD), lambda b, hp: (b, s // c, hp))
    return pl.pallas_call(
        _mla_ctx_kernel,
        grid=(nb, pairs),
        in_specs=[spec, spec, spec],
        out_specs=pl.BlockSpec((None, c, 2 * MLA_V), lambda b, hp: (b, 0, hp)),
        out_shape=jax.ShapeDtypeStruct((nb, c, MLA_HEADS * MLA_V), BF16),
        compiler_params=_cparams(2),
        name="mla_ctx_attention",
    )(q, k, v)


def _na_kernel(q_ref, k_ref, v_ref, kc_ref, vc_ref, bias_ref, rowmask_ref, o_ref, comb_scr, s0_scr, s1_scr, p0_scr,
               p1_scr, mx_scr, l_scr, *, grid_rows, n_blocks):
    tq = NA_ROWS_PER_BLOCK * GRID_W
    n_keys = NA_KEY_ROWS * GRID_W

    for hd in range(2):
        for var, row_off in enumerate((0, -(NA_WIN_R // 2), -(NA_KEY_ROWS - NA_ROWS_PER_BLOCK))):
            tile0 = row_off + (NA_WIN_R - 1) + NA_BIAS_SHIFT
            bias = jnp.concatenate(
                [jnp.concatenate([bias_ref[hd, tile0 + 2 * j - qr] for j in range(NA_KEY_ROWS // 2)], axis=1)
                 for qr in range(NA_ROWS_PER_BLOCK)], axis=0)
            comb_scr[hd, var] = bias + rowmask_ref[var].astype(F32)

    lane = lax.broadcasted_iota(jnp.int32, (tq, LANES), 1)

    def q_rows(rb):
        return pl.ds(pl.multiple_of(rb * tq, tq), tq)

    def key_rows(rb):
        start_row = jnp.clip(rb * NA_ROWS_PER_BLOCK - NA_WIN_R // 2, 0, grid_rows - NA_KEY_ROWS)
        return pl.ds(pl.multiple_of(start_row * GRID_W, (NA_WIN_R // 2) * GRID_W), n_keys)

    def scores(hd, rb, s_scr):
        var = jnp.where(rb == 0, 0, jnp.where(rb == n_blocks - 1, 2, 1))
        q = q_ref[q_rows(rb), :]
        in_head = (lane >= NA_HEAD_DIM) if hd else (lane < NA_HEAD_DIM)
        qh = jnp.where(in_head, q, jnp.zeros_like(q))
        s = _dot_nt(qh, k_ref[key_rows(rb), :]) + comb_scr[hd, var]
        sc = _dot_nt(qh, kc_ref[...])
        s_scr[:, :n_keys] = s
        s_scr[:, n_keys:] = sc
        m = jnp.maximum(jnp.max(s, axis=1, keepdims=True), jnp.max(sc, axis=1, keepdims=True))
        mx_scr[hd] = jnp.broadcast_to(m, mx_scr.shape[1:])

    def softmax(hd, s_scr, p_scr):
        p = jnp.exp(s_scr[...] - jnp.tile(mx_scr[hd], (1, s_scr.shape[1] // LANES)))
        l_scr[hd] = jnp.broadcast_to(jnp.sum(p, axis=1, keepdims=True), l_scr.shape[1:])
        p_scr[...] = p.astype(BF16)

    def accumulate(hd, p_scr, rb):
        acc = _dot(p_scr[:, :n_keys], v_ref[key_rows(rb), :]) + _dot(p_scr[:, n_keys:], vc_ref[...])
        out = (acc / l_scr[hd]).astype(BF16)
        sl = slice(hd * NA_HEAD_DIM, (hd + 1) * NA_HEAD_DIM)
        o_ref[q_rows(rb), sl] = out[:, sl]

    p1_scr[...] = jnp.zeros(p1_scr.shape, BF16)
    l_scr[...] = jnp.ones(l_scr.shape, F32)
    scores(0, 0, s0_scr)

    def body(rb, carry):
        scores(1, rb, s1_scr)
        softmax(0, s0_scr, p0_scr)
        accumulate(1, p1_scr, jnp.maximum(rb - 1, 0))
        scores(0, jnp.minimum(rb + 1, n_blocks - 1), s0_scr)
        softmax(1, s1_scr, p1_scr)
        accumulate(0, p0_scr, rb)
        return carry

    lax.fori_loop(0, n_blocks, body, 0)
    accumulate(1, p1_scr, n_blocks - 1)


def _na_attention(q, k, v, bias, row_mask, layer, s):
    nb, p, _ = q.shape
    c = p - s
    grid_rows = s // GRID_W
    tq = NA_ROWS_PER_BLOCK * GRID_W
    n_blocks = grid_rows // NA_ROWS_PER_BLOCK
    n_keys = NA_KEY_ROWS * GRID_W
    pairs = NA_HEADS // 2
    kern = functools.partial(_na_kernel, grid_rows=grid_rows, n_blocks=n_blocks)
    lat = pl.BlockSpec((None, s, LANES), lambda b, hp: (b, 0, hp))
    ctx = pl.BlockSpec((None, c, LANES), lambda b, hp: (b, s // c, hp))
    return pl.pallas_call(
        kern,
        grid=(nb, pairs),
        in_specs=[
            lat, lat, lat, ctx, ctx,
            pl.BlockSpec((None, 2) + bias.shape[2:], lambda b, hp: (layer, hp, 0, 0, 0)),
            pl.BlockSpec(row_mask.shape, lambda b, hp: (0, 0, 0)),
        ],
        out_specs=pl.BlockSpec((None, s, LANES), lambda b, hp: (b, 0, hp)),
        out_shape=jax.ShapeDtypeStruct((nb, s, NA_WIDTH), BF16),
        scratch_shapes=[
            pltpu.VMEM((2, 3, tq, n_keys), F32),
            pltpu.VMEM((tq, n_keys + c), F32),
            pltpu.VMEM((tq, n_keys + c), F32),
            pltpu.VMEM((tq, n_keys + c), BF16),
            pltpu.VMEM((tq, n_keys + c), BF16),
            pltpu.VMEM((2, tq, LANES), F32),
            pltpu.VMEM((2, tq, LANES), F32),
        ],
        compiler_params=_cparams(2),
        name="na_attention",
    )(q, k, v, k, v, bias, row_mask)


def _na_ctx_kernel(q_ref, k_ref, v_ref, o_ref):
    q = q_ref[...]
    k = k_ref[...]
    v = v_ref[...]
    lane = lax.broadcasted_iota(jnp.int32, q.shape, 1)
    outs = []
    for hd in range(2):
        in_head = (lane >= NA_HEAD_DIM) if hd else (lane < NA_HEAD_DIM)
        qh = jnp.where(in_head, q, jnp.zeros_like(q))
        s = _dot_nt(qh, k)
        p = jnp.exp(s - jnp.max(s, axis=1, keepdims=True))
        outs.append(_dot(p.astype(BF16), v) / jnp.sum(p, axis=1, keepdims=True))
    o_ref[...] = jnp.where(lane < NA_HEAD_DIM, outs[0], outs[1]).astype(BF16)


def _na_ctx_attention(q, k, v, s):
    nb, p, _ = q.shape
    c = p - s
    spec = pl.BlockSpec((None, c, LANES), lambda b, hp: (b, s // c, hp))
    return pl.pallas_call(
        _na_ctx_kernel,
        grid=(nb, NA_HEADS // 2),
        in_specs=[spec, spec, spec],
        out_specs=pl.BlockSpec((None, c, LANES), lambda b, hp: (b, 0, hp)),
        out_shape=jax.ShapeDtypeStruct((nb, c, NA_WIDTH), BF16),
        compiler_params=_cparams(2),
        name="na_ctx_attention",
    )(q, k, v)


def _na_row_masks(grid_rows):
    qr = np.arange(NA_ROWS_PER_BLOCK)[:, None]
    kr = np.arange(NA_KEY_ROWS)[None, :]
    row_valid = np.zeros((3, NA_ROWS_PER_BLOCK, NA_KEY_ROWS), bool)
    n_blocks = grid_rows // NA_ROWS_PER_BLOCK
    for vi, rb in enumerate((0, 1, n_blocks - 1)):
        r = rb * NA_ROWS_PER_BLOCK + qr
        start_row = min(max(rb * NA_ROWS_PER_BLOCK - NA_WIN_R // 2, 0), grid_rows - NA_KEY_ROWS)
        key_row = start_row + kr
        win = np.clip(r - NA_WIN_R // 2, 0, grid_rows - NA_WIN_R)
        row_valid[vi] = (key_row >= win) & (key_row < win + NA_WIN_R)
    mask = np.where(row_valid, 0.0, MASK_VALUE).astype(np.float32)
    mask = np.broadcast_to(mask[:, :, None, :, None], (3, NA_ROWS_PER_BLOCK, GRID_W, NA_KEY_ROWS, GRID_W))
    return jnp.asarray(mask.reshape(3, NA_ROWS_PER_BLOCK * GRID_W, NA_KEY_ROWS * GRID_W), BF16)


def _na_bias_tables(rpb):
    n_dc = 2 * NA_WIN_C - 1
    qc = np.arange(GRID_W)[:, None]
    kcol = np.arange(GRID_W)[None, :]
    wstart = np.clip(qc - NA_WIN_C // 2, 0, GRID_W - NA_WIN_C)
    col_valid = (kcol >= wstart) & (kcol < wstart + NA_WIN_C)
    dc = np.clip(kcol - qc, -(NA_WIN_C - 1), NA_WIN_C - 1) + (NA_WIN_C - 1)
    col_onehot = ((dc[..., None] == np.arange(n_dc)) & col_valid[..., None]).astype(np.float32)
    t = jnp.einsum("lhij,cdj->lhicd", rpb.astype(F32), jnp.asarray(col_onehot), precision=lax.Precision.HIGHEST)
    t = jnp.where(jnp.asarray(col_valid), t, MASK_VALUE)
    n_dr = t.shape[2]
    t = jnp.pad(t, ((0, 0), (0, 0), (NA_BIAS_SHIFT, NA_BIAS_TILES + 1 - NA_BIAS_SHIFT - n_dr), (0, 0), (0, 0)),
                constant_values=MASK_VALUE)
    return jnp.concatenate([t[:, :, :NA_BIAS_TILES], t[:, :, 1:NA_BIAS_TILES + 1]], axis=-1)


FT_N1 = 64


def _ft_stage1_kernel(w_ref, x_ref, o_ref):
    o_ref[...] = _dot(w_ref[...], x_ref[...])


def _ft_stage2_kernel(are_ref, aim_ref, tc_ref, ts_ref, w_ref, cd_ref, sd_ref, o_ref, *, scale):
    n2 = are_ref.shape[1]
    for g in range(are_ref.shape[0]):
        are = are_ref[g]
        aim = aim_ref[g]
        tc = jnp.concatenate([tc_ref[g]] * (FT_WIDTH // LANES), axis=1)
        ts = jnp.concatenate([ts_ref[g]] * (FT_WIDTH // LANES), axis=1)
        bre = are * tc + aim * ts
        bim = aim * tc - are * ts
        bb = jnp.concatenate([bre, bim], axis=0).astype(BF16)
        xk = _dot(w_ref[...], bb)
        f = _dot(xk[:n2].astype(BF16), cd_ref[...]) + _dot(xk[n2:].astype(BF16), sd_ref[...])
        o_ref[:, g * FT_WIDTH:(g + 1) * FT_WIDTH] = (f * scale).astype(BF16)


def _dft_mats(n):
    ang = 2.0 * np.pi * np.outer(np.arange(n), np.arange(n)) / n
    return np.cos(ang), np.sin(ang)


def _channel_dft():
    c, s = _dft_mats(FT_GROUP_DIM)
    eye = np.eye(FT_GROUPS)
    return jnp.asarray(np.kron(eye, c), BF16), jnp.asarray(np.kron(eye, s), BF16)


def _fourier_lat(u, n):
    nb, p, _ = u.shape
    n2 = n // FT_N1
    cols = n2 * FT_WIDTH
    c1, s1 = _dft_mats(FT_N1)
    w1 = jnp.asarray(np.concatenate([c1, -s1], axis=0), BF16)
    tcol = min(2048, cols)
    a = pl.pallas_call(
        _ft_stage1_kernel,
        grid=(nb, cols // tcol),
        in_specs=[
            pl.BlockSpec(w1.shape, lambda b, j: (0, 0)),
            pl.BlockSpec((None, FT_N1, tcol), lambda b, j: (b, 0, j)),
        ],
        out_specs=pl.BlockSpec((None, 2 * FT_N1, tcol), lambda b, j: (b, 0, j)),
        out_shape=jax.ShapeDtypeStruct((nb, 2 * FT_N1, cols), F32),
        compiler_params=_cparams(2),
        name="fourier_stage1",
    )(w1, u.reshape(nb, p // n2, cols))
    a = a.reshape(nb, 2 * FT_N1, n2, FT_WIDTH)

    ang = 2.0 * np.pi * np.outer(np.arange(FT_N1), np.arange(n2)) / n
    tc = jnp.asarray(np.repeat(np.cos(ang)[:, :, None], LANES, axis=2), F32)
    ts = jnp.asarray(np.repeat(np.sin(ang)[:, :, None], LANES, axis=2), F32)
    c3, s3 = _dft_mats(n2)
    w3 = jnp.asarray(np.block([[c3, s3], [-s3, c3]]), BF16)
    cd, sd = _channel_dft()
    kern = functools.partial(_ft_stage2_kernel, scale=1.0 / math.sqrt(n * FT_GROUP_DIM))
    g = FT_K1_PER_STEP
    n_steps = FT_N1 // g
    out = pl.pallas_call(
        kern,
        grid=(nb, n_steps),
        in_specs=[
            pl.BlockSpec((None, g, n2, FT_WIDTH), lambda b, k: (b, k, 0, 0)),
            pl.BlockSpec((None, g, n2, FT_WIDTH), lambda b, k: (b, n_steps + k, 0, 0)),
            pl.BlockSpec((g, n2, LANES), lambda b, k: (k, 0, 0)),
            pl.BlockSpec((g, n2, LANES), lambda b, k: (k, 0, 0)),
            pl.BlockSpec(w3.shape, lambda b, k: (0, 0)),
            pl.BlockSpec(cd.shape, lambda b, k: (0, 0)),
            pl.BlockSpec(sd.shape, lambda b, k: (0, 0)),
        ],
        out_specs=pl.BlockSpec((None, n2, g * FT_WIDTH), lambda b, k: (b, 0, k)),
        out_shape=jax.ShapeDtypeStruct((nb, n2, FT_N1 * FT_WIDTH), BF16),
        compiler_params=_cparams(2),
        name="fourier_stage2",
    )(a, a, tc, ts, w3, cd, sd)
    return out.reshape(nb, n, FT_WIDTH)


def _ft_ctx_kernel(u_ref, cn_ref, sn_ref, cd_ref, sd_ref, o_ref, *, scale):
    u = u_ref[...]
    y1 = _dot(u, cd_ref[...]).astype(BF16)
    y2 = _dot(u, sd_ref[...]).astype(BF16)
    f = _dot(cn_ref[...], y1) - _dot(sn_ref[...], y2)
    o_ref[...] = (f * scale).astype(BF16)


def _fourier_ctx(u, s):
    nb, p, _ = u.shape
    n = p - s
    cn, sn = _dft_mats(n)
    cn, sn = jnp.asarray(cn, BF16), jnp.asarray(sn, BF16)
    cd, sd = _channel_dft()
    kern = functools.partial(_ft_ctx_kernel, scale=1.0 / math.sqrt(n * FT_GROUP_DIM))
    sq = pl.BlockSpec((n, n), lambda b: (0, 0))
    ch = pl.BlockSpec(cd.shape, lambda b: (0, 0))
    return pl.pallas_call(
        kern,
        grid=(nb,),
        in_specs=[pl.BlockSpec((None, n, FT_WIDTH), lambda b: (b, s // n, 0)), sq, sq, ch, ch],
        out_specs=pl.BlockSpec((None, n, FT_WIDTH), lambda b: (b, 0, 0)),
        out_shape=jax.ShapeDtypeStruct((nb, n, FT_WIDTH), BF16),
        compiler_params=_cparams(1),
        name="fourier_ctx",
    )(u, cn, sn, cd, sd)


def _merge_kernel(x_ref, oa_ref, ob_ref, oc_ref, ca_ref, cb_ref, cc_ref, mod_ref, n1g_ref, n2g_ref, wgate_ref,
                  bgate_ref, wba_ref, wbb_ref, wbc_ref, wout_ref, x1_ref, h2_ref, *, n_lat_tiles):
    d = x_ref.shape[-1]
    x = x_ref[...]
    is_ctx = pl.program_id(1) >= n_lat_tiles
    o_a = jnp.where(is_ctx, ca_ref[...], oa_ref[...])
    o_b = jnp.where(is_ctx, cb_ref[...], ob_ref[...])
    o_c = jnp.where(is_ctx, cc_ref[...], oc_ref[...])
    mod = mod_ref[...]
    sh1, sc1, g1, sh2, sc2 = (mod[:, i * d:(i + 1) * d] for i in range(5))
    h = _rms(x, n1g_ref[...]) * (1.0 + sc1) + sh1
    gate = jax.nn.sigmoid(_dot(h.astype(BF16), wgate_ref[...]) + bgate_ref[...])
    m = gate[:, 0:d] * _dot(o_a, wba_ref[...])
    m += gate[:, d:2 * d] * _dot(o_b, wbb_ref[...])
    m += gate[:, 2 * d:3 * d] * _dot(o_c, wbc_ref[...])
    y = _dot(m.astype(BF16), wout_ref[...])
    x1 = x + g1 * y
    x1_ref[...] = x1
    h2_ref[...] = (_rms(x1, n2g_ref[...]) * (1.0 + sc2) + sh2).astype(BF16)


def _merge(x, lat_outs, ctx_outs, mod, mod_row, lw, layer):
    nb, n, d = x.shape
    tm = TOKEN_TILE
    n_lat_tiles = lat_outs[0].shape[1] // tm

    def full(a):
        return _layer_spec(a, layer)

    def tok(width):
        return pl.BlockSpec((None, tm, width), lambda b, i: (b, i, 0))

    def lat(a):
        return pl.BlockSpec((None, tm, a.shape[-1]), lambda b, i: (b, jnp.minimum(i, n_lat_tiles - 1), 0))

    def ctx(a):
        return pl.BlockSpec((None, tm, a.shape[-1]), lambda b, i: (b, jnp.maximum(i - n_lat_tiles, 0), 0))

    weights = [lw["norm1_g"], lw["norm2_g"], lw["w_gate"], lw["b_gate"], lw["w_br_mla"], lw["w_br_na"],
               lw["w_br_ft"], lw["w_out"]]
    return pl.pallas_call(
        functools.partial(_merge_kernel, n_lat_tiles=n_lat_tiles),
        grid=(nb, n // tm),
        in_specs=[tok(d)] + [lat(a) for a in lat_outs] + [ctx(a) for a in ctx_outs] + [_mod_spec(mod, layer, mod_row)]
        + [full(w) for w in weights],
        out_specs=[tok(d), tok(d)],
        out_shape=[jax.ShapeDtypeStruct((nb, n, d), F32), jax.ShapeDtypeStruct((nb, n, d), BF16)],
        compiler_params=_cparams(2),
        name="merge",
    )(x, *lat_outs, *ctx_outs, mod, *weights)


def _route(logits, router_bias):
    row = lax.broadcasted_iota(jnp.int32, logits.shape, 0)
    neg = -jnp.inf
    big = jnp.int32(N_EXPERTS)
    scores = jax.nn.sigmoid(logits)
    sel = scores + router_bias

    def top2(vals):
        m1 = jnp.max(vals, axis=0, keepdims=True)
        i1 = jnp.min(jnp.where(vals == m1, row, big), axis=0, keepdims=True)
        rest = jnp.where(row == i1, neg, vals)
        m2 = jnp.max(rest, axis=0, keepdims=True)
        i2 = jnp.min(jnp.where(rest == m2, row, big), axis=0, keepdims=True)
        return m1, i1, m2, i2

    grp_scores = []
    for g in range(N_GROUPS):
        in_grp = (row >= g * EXPERTS_PER_GROUP) & (row < (g + 1) * EXPERTS_PER_GROUP)
        m1, _, m2, _ = top2(jnp.where(in_grp, sel, neg))
        grp_scores.append(m1 + m2)
    best = functools.reduce(jnp.maximum, grp_scores)
    grp = jnp.full(best.shape, N_GROUPS - 1, jnp.int32)
    for g in range(N_GROUPS - 2, -1, -1):
        grp = jnp.where(grp_scores[g] == best, g, grp)
    in_sel = (row >= grp * EXPERTS_PER_GROUP) & (row < (grp + 1) * EXPERTS_PER_GROUP)
    _, i1, _, i2 = top2(jnp.where(in_sel, sel, neg))
    hit1 = row == i1
    hit2 = row == i2
    w1 = jnp.sum(jnp.where(hit1, scores, 0.0), axis=0, keepdims=True)
    w2 = jnp.sum(jnp.where(hit2, scores, 0.0), axis=0, keepdims=True)
    total = w1 + w2
    return jnp.where(hit1, w1 / total, 0.0) + jnp.where(hit2, w2 / total, 0.0), grp


def _split_bf16(v):
    hi = v.astype(BF16)
    return hi, (v - hi.astype(F32)).astype(BF16)


def _moe_kernel(h_ref, x_ref, mod_ref, cmod_ref, rw_ref, rb_ref, fg_ref, tri_ref, wg_ref, wu_ref, wd_ref, o_ref,
                col_scr, row_scr, count_scr, acc_scr, *, apply_final_norm, n_lat, cap):
    d = x_ref.shape[-1]
    tm = x_ref.shape[0]
    step = pl.program_id(2)
    n_steps = pl.num_programs(2)
    h = h_ref[...]

    @pl.when(step == 0)
    def _():
        gate_t, grp = _route(_dot_nt(rw_ref[...], h), rb_ref[...])
        member = jnp.where(lax.broadcasted_iota(jnp.int32, (SUBLANES, tm), 0) == grp, 1.0, 0.0)
        rank = _dot(member.astype(BF16), tri_ref[...])
        empty_rows = jnp.zeros((SUBLANES - EXPERTS_PER_GROUP - 2, tm), F32)
        per_group = [jnp.concatenate([gate_t[c * EXPERTS_PER_GROUP:(c + 1) * EXPERTS_PER_GROUP], rank[c:c + 1],
                                      member[c:c + 1], empty_rows], axis=0) for c in range(N_GROUPS)]
        pad_rows = jnp.zeros((LANES - N_GROUPS * SUBLANES, tm), F32)
        cols = jnp.concatenate(per_group + [pad_rows], axis=0).T
        for c in range(N_GROUPS):
            count_scr[c] = jnp.sum(member[c:c + 1]).astype(jnp.int32)
            row_scr[c] = per_group[c]
            col_scr[c] = pltpu.roll(cols, (LANES - c * SUBLANES) % LANES, axis=1) if c else cols
        acc_scr[...] = jnp.zeros(acc_scr.shape, F32)

    def experts(hx, gates):
        out = jnp.zeros((hx.shape[0], d), F32)
        for j in range(EXPERTS_PER_STEP):
            a = _dot(hx, wg_ref[j])
            u = _dot(hx, wu_ref[j])
            hid = (a * jax.nn.sigmoid(a)) * u * gates[:, j:j + 1]
            out += _dot(hid.astype(BF16), wd_ref[j])
        return out

    rank_lane, member_lane = EXPERTS_PER_GROUP, EXPERTS_PER_GROUP + 1
    rank_row = row_scr[step, rank_lane:rank_lane + 1, :]
    member_row = row_scr[step, member_lane:member_lane + 1, :]
    n_members = count_scr[step]

    @pl.when(n_members <= cap)
    def _():
        slot = lax.broadcasted_iota(jnp.int32, (cap, tm), 0).astype(F32)
        pick = jnp.where((slot == rank_row) & (member_row > 0.5), 1.0, 0.0).astype(BF16)
        hx = _dot(pick, h).astype(BF16)
        col = col_scr[step]
        g_hi, g_lo = _split_bf16(col)
        y = experts(hx, _dot(pick, g_hi) + _dot(pick, g_lo))
        slot_t = lax.broadcasted_iota(jnp.int32, (tm, cap), 1).astype(F32)
        put = jnp.where((slot_t == col[:, rank_lane:rank_lane + 1]) & (col[:, member_lane:member_lane + 1] > 0.5),
                        1.0, 0.0).astype(BF16)
        y_hi, y_lo = _split_bf16(y)
        acc_scr[...] += _dot(put, y_hi) + _dot(put, y_lo)

    @pl.when(n_members > cap)
    def _():
        acc_scr[...] += experts(h, col_scr[step])

    @pl.when(step == n_steps - 1)
    def _():
        tok_row = pl.program_id(1) * tm + lax.broadcasted_iota(jnp.int32, (tm, 1), 0)
        g2 = jnp.where(tok_row >= n_lat, cmod_ref[:, 5 * d:6 * d], mod_ref[:, 5 * d:6 * d])
        x2 = x_ref[...] + g2 * acc_scr[...]
        o_ref[...] = _rms(x2, fg_ref[...]) if apply_final_norm else x2


def _moe_tile(p):
    return max(t for t in range(TOKEN_TILE, MOE_MAX_TILE + 1, TOKEN_TILE) if p % t == 0)


def _moe_capacity(tm):
    return min(tm, -(-(5 * tm // (4 * N_GROUPS)) // LANES) * LANES)


def _moe(h2, x1, mod, ctx_row, lw, layer, router_w, router_b, final_g, n_lat, n_rows, tm, apply_final_norm):
    assert EXPERTS_PER_STEP == EXPERTS_PER_GROUP
    nb, _, d = x1.shape
    n_steps = N_EXPERTS // EXPERTS_PER_STEP
    ff = lw["w_e_gate"].shape[-1]
    tok = pl.BlockSpec((None, tm, d), lambda b, i, c: (b, i, 0))
    earlier = jnp.asarray(np.triu(np.ones((tm, tm), np.float32), k=1), BF16)
    kern = functools.partial(_moe_kernel, apply_final_norm=apply_final_norm, n_lat=n_lat, cap=_moe_capacity(tm))
    return pl.pallas_call(
        kern,
        grid=(nb, n_rows // tm, n_steps),
        in_specs=[
            tok, tok,
            _mod_spec(mod, layer, lambda b, i: b),
            _mod_spec(mod, layer, lambda b, i: ctx_row),
            pl.BlockSpec(router_w.shape, lambda b, i, c: (0, 0)),
            pl.BlockSpec(router_b.shape, lambda b, i, c: (0, 0)),
            pl.BlockSpec(final_g.shape, lambda b, i, c: (0, 0)),
            pl.BlockSpec(earlier.shape, lambda b, i, c: (0, 0)),
            pl.BlockSpec((None, EXPERTS_PER_STEP, d, ff), lambda b, i, c: (layer, c, 0, 0)),
            pl.BlockSpec((None, EXPERTS_PER_STEP, d, ff), lambda b, i, c: (layer, c, 0, 0)),
            pl.BlockSpec((None, EXPERTS_PER_STEP, ff, d), lambda b, i, c: (layer, c, 0, 0)),
        ],
        out_specs=tok,
        out_shape=jax.ShapeDtypeStruct((nb, n_rows, d), F32),
        scratch_shapes=[
            pltpu.VMEM((n_steps, tm, LANES), F32),
            pltpu.VMEM((n_steps, SUBLANES, tm), F32),
            pltpu.SMEM((n_steps,), jnp.int32),
            pltpu.VMEM((tm, d), F32),
        ],
        compiler_params=_cparams(3),
        name="moe",
    )(h2, x1, mod, mod, router_w, router_b, final_g, earlier, lw["w_e_gate"], lw["w_e_up"], lw["w_e_down"])


def _rotate_cols(w):
    half = w.shape[-1] // 2
    return jnp.concatenate([-w[..., half:], w[..., :half]], axis=-1)


def _prepare_weights(w_in, q_norm_g, w_q_up, kv_norm_g, w_kv_up, norm1_g, norm2_g, w_gate, b_gate, w_br_mla,
                     w_br_na, w_br_ft, w_out, w_e_gate, w_e_up, w_e_down):
    n_layers, d, _ = w_in.shape
    splits = np.cumsum([MLA_Q_LORA, MLA_KV_LORA, MLA_ROPE, NA_WIDTH, NA_WIDTH, NA_WIDTH])
    w_q, w_kv, w_kr, w_nq, w_nk, w_nv, w_ft = jnp.split(w_in, [int(s) for s in splits], axis=-1)

    def in_rope_slot(w):
        zeros_lo = jnp.zeros(w.shape[:-1] + (MLA_NOPE,), w.dtype)
        zeros_hi = jnp.zeros(w.shape[:-1] + (HEAD_PAD - MLA_NOPE - MLA_ROPE,), w.dtype)
        return jnp.concatenate([zeros_lo, w, zeros_hi], axis=-1)

    win = jnp.concatenate([w_q, w_kv, in_rope_slot(w_kr), in_rope_slot(_rotate_cols(w_kr)), w_nq, w_nk, w_nv, w_ft],
                          axis=-1).astype(BF16)
    wq = w_q_up.reshape(n_layers, MLA_Q_LORA, MLA_HEADS, MLA_NOPE + MLA_ROPE)
    wq_nope, wq_pe = wq[..., :MLA_NOPE], wq[..., MLA_NOPE:]
    pad = jnp.zeros(wq_pe.shape[:-1] + (HEAD_PAD - MLA_NOPE - MLA_ROPE,), wq.dtype)
    wq_main = jnp.concatenate([wq_nope, wq_pe, pad], axis=-1)
    wq_rot = jnp.concatenate([jnp.zeros_like(wq_nope), _rotate_cols(wq_pe), pad], axis=-1)
    wkv = w_kv_up.reshape(n_layers, MLA_KV_LORA, MLA_HEADS, MLA_NOPE + MLA_V)
    wk_nope, wv = wkv[..., :MLA_NOPE], wkv[..., MLA_NOPE:]
    wkv_k = jnp.concatenate([wk_nope, jnp.zeros(wk_nope.shape[:-1] + (HEAD_PAD - MLA_NOPE,), wkv.dtype)], axis=-1)
    wide = MLA_HEADS * HEAD_PAD
    return {
        "w_in": win,
        "wq_main": wq_main.reshape(n_layers, MLA_Q_LORA, wide).astype(BF16),
        "wq_rot": wq_rot.reshape(n_layers, MLA_Q_LORA, wide).astype(BF16),
        "wkv_k": wkv_k.reshape(n_layers, MLA_KV_LORA, wide).astype(BF16),
        "wkv_v": jnp.concatenate([wv, jnp.zeros(wv.shape[:-1] + (HEAD_PAD - MLA_V,), wv.dtype)], axis=-1)
        .reshape(n_layers, MLA_KV_LORA, wide).astype(BF16),
        "q_norm_g": q_norm_g.reshape(n_layers, 1, -1),
        "kv_norm_g": kv_norm_g.reshape(n_layers, 1, -1),
        "norm1_g": norm1_g.reshape(n_layers, 1, d),
        "norm2_g": norm2_g.reshape(n_layers, 1, d),
        "w_gate": w_gate.astype(BF16),
        "b_gate": b_gate.reshape(n_layers, 1, -1),
        "w_br_mla": w_br_mla.astype(BF16),
        "w_br_na": w_br_na.astype(BF16),
        "w_br_ft": w_br_ft.astype(BF16),
        "w_out": w_out.astype(BF16),
        "w_e_gate": w_e_gate.astype(BF16),
        "w_e_up": w_e_up.astype(BF16),
        "w_e_down": w_e_down.astype(BF16),
    }


def _rope_tables(s, c):
    n_freq = MLA_ROPE // 4
    inv_freq = ROPE_THETA ** (-jnp.arange(n_freq, dtype=F32) / n_freq)
    t = jnp.arange(s, dtype=jnp.int32)
    row = (t // GRID_W).astype(F32)
    col = (t % GRID_W).astype(F32)
    ang = jnp.concatenate([row[:, None] * inv_freq, col[:, None] * inv_freq], axis=-1)
    cos, sin = jnp.cos(ang), jnp.sin(ang)
    pad = HEAD_PAD - MLA_NOPE - MLA_ROPE
    cos_lat = jnp.concatenate([jnp.ones((s, MLA_NOPE), F32), cos, cos, jnp.zeros((s, pad), F32)], axis=-1)
    sin_lat = jnp.concatenate([jnp.zeros((s, MLA_NOPE), F32), sin, sin, jnp.zeros((s, pad), F32)], axis=-1)
    cos_ctx = jnp.concatenate([jnp.ones((c, MLA_NOPE + MLA_ROPE), F32), jnp.zeros((c, pad), F32)], axis=-1)
    sin_ctx = jnp.zeros((c, HEAD_PAD), F32)
    return cos_lat, sin_lat, cos_ctx, sin_ctx


@jax.jit
def _forward(x, c, ctx, c_ctx, w_ada, b_ada, norm1_g, norm2_g, w_in, q_norm_g, w_q_up, kv_norm_g, w_kv_up,
             na_rpb, w_gate, b_gate, w_br_mla, w_br_na, w_br_ft, w_out, router_w, router_bias, w_e_gate,
             w_e_up, w_e_down, final_norm_g):
    nb, s, d = x.shape
    n_ctx = ctx.shape[1]
    n_layers = w_ada.shape[0]
    grid_rows = s // GRID_W

    mod_rows = 8
    c_rows = jnp.zeros((mod_rows, d), F32).at[:nb].set(c).at[nb].set(c_ctx)
    mod_all = _modulation(c_rows, w_ada, b_ada).reshape(n_layers, mod_rows, 1, 6 * d)

    weights = _prepare_weights(w_in, q_norm_g, w_q_up, kv_norm_g, w_kv_up, norm1_g, norm2_g, w_gate, b_gate,
                               w_br_mla, w_br_na, w_br_ft, w_out, w_e_gate, w_e_up, w_e_down)
    na_bias = _na_bias_tables(na_rpb)
    na_row_mask = _na_row_masks(grid_rows)
    cos_lat, sin_lat, cos_ctx, sin_ctx = _rope_tables(s, n_ctx)
    cos_t = jnp.concatenate([cos_lat, cos_ctx], axis=0)
    sin_t = jnp.concatenate([sin_lat, sin_ctx], axis=0)
    router_w_p = router_w.T.astype(BF16)
    router_b_p = router_bias.reshape(N_EXPERTS, 1).astype(F32)
    final_g = final_norm_g.reshape(1, d)

    p = s + n_ctx
    n_lat_tiles = s // TOKEN_TILE
    ctx_row = nb

    def tile_row(b, i):
        return jnp.where(i >= n_lat_tiles, ctx_row, b)

    x_all = jnp.concatenate([x, ctx], axis=1)
    zero_ctx = [jnp.zeros((nb, n_ctx, w), BF16) for w in (MLA_HEADS * MLA_V, NA_WIDTH, FT_WIDTH)]
    for l in range(n_layers):
        last = l == n_layers - 1
        q, k, v, nq, nk, nv, ft = _project(x_all, mod_all, tile_row, weights, l, cos_t, sin_t)
        lat_outs = [_mla_attention(q, k, v, s), _na_attention(nq, nk, nv, na_bias, na_row_mask, l, s),
                    _fourier_lat(ft, s)]
        if last:
            ctx_outs = zero_ctx
        else:
            ctx_outs = [_mla_ctx_attention(q, k, v, s), _na_ctx_attention(nq, nk, nv, s), _fourier_ctx(ft, s)]
        x1, h2 = _merge(x_all, lat_outs, ctx_outs, mod_all, tile_row, weights, l)
        if last:
            return _moe(h2, x1, mod_all, ctx_row, weights, l, router_w_p, router_b_p, final_g, s, s,
                        _moe_tile(s), True)
        x_all = _moe(h2, x1, mod_all, ctx_row, weights, l, router_w_p, router_b_p, final_g, s, p,
                     _moe_tile(p), False)


def kernel(x, c, ctx, c_ctx, w_ada, b_ada, norm1_g, norm2_g, w_in, q_norm_g, w_q_up, kv_norm_g, w_kv_up, na_rpb, w_gate, b_gate, w_br_mla, w_br_na, w_br_ft, w_out, router_w, router_bias, w_e_gate, w_e_up, w_e_down, final_norm_g):
    return _forward(x, c, ctx, c_ctx, w_ada, b_ada, norm1_g, norm2_g, w_in, q_norm_g, w_q_up, kv_norm_g,
                    w_kv_up, na_rpb, w_gate, b_gate, w_br_mla, w_br_na, w_br_ft, w_out, router_w, router_bias,
                    w_e_gate, w_e_up, w_e_down, final_norm_g)
```

```python
import functools
import math

import jax
import jax.numpy as jnp
import numpy as np
from jax import lax
from jax.experimental import pallas as pl
from jax.experimental.pallas import tpu as pltpu

F32 = jnp.float32
BF16 = jnp.bfloat16

GRID_W = 64
MLA_HEADS = 8
MLA_NOPE = 64
MLA_ROPE = 32
MLA_V = 64
MLA_Q_LORA = 256
MLA_KV_LORA = 128
MLA_SCALE = (MLA_NOPE + MLA_ROPE) ** -0.5
NA_HEADS = 4
NA_HEAD_DIM = 64
NA_WIN_R = 8
NA_WIN_C = 16
NA_SCALE = NA_HEAD_DIM ** -0.5
NA_WIDTH = NA_HEADS * NA_HEAD_DIM
FT_GROUPS = 4
FT_GROUP_DIM = 64
FT_WIDTH = FT_GROUPS * FT_GROUP_DIM
N_EXPERTS = 16
N_GROUPS = 4
EXPERTS_PER_GROUP = N_EXPERTS // N_GROUPS
ROPE_THETA = 10000.0
NORM_EPS = 1e-6
MASK_VALUE = -1e30
LOG2E = math.log2(math.e)

LANES = 128
SUBLANES = 8
V7X_VMEM_LIMIT_BYTES = 56 * 1024 * 1024

HEAD_PAD = LANES
NA_ROWS_PER_BLOCK = 8
NA_KEY_ROWS = 16
NA_BIAS_SHIFT = (NA_KEY_ROWS - NA_ROWS_PER_BLOCK) + (NA_ROWS_PER_BLOCK - 1) - (NA_WIN_R - 1)
NA_BIAS_TILES = (NA_WIN_R - 1) + NA_BIAS_SHIFT + (NA_KEY_ROWS - 2) + 1
FT_K1_PER_STEP = 8
EXPERTS_PER_STEP = 4
MOE_MAX_TILE = 768
TOKEN_TILE = 256
MLA_MAX_KV_CHUNK = 2816


def _cparams(n_axes):
    return pltpu.CompilerParams(
        dimension_semantics=("arbitrary",) * n_axes,
        vmem_limit_bytes=V7X_VMEM_LIMIT_BYTES,
    )


def _rms(x, g):
    return x * lax.rsqrt(jnp.mean(x * x, axis=-1, keepdims=True) + NORM_EPS) * g


def _dot(a, b):
    return jnp.dot(a, b, preferred_element_type=F32)


def _dot_nt(a, b):
    return lax.dot_general(a, b, (((1,), (1,)), ((), ())), preferred_element_type=F32)


def _mod_kernel(c_ref, w_ref, b_ref, o_ref):
    c = c_ref[...]
    o_ref[...] = _dot(c * jax.nn.sigmoid(c), w_ref[...]) + b_ref[...]


def _modulation(c_rows, w_ada, b_ada):
    n_layers, d, width = w_ada.shape
    rows = c_rows.shape[0]
    tn = 1536
    return pl.pallas_call(
        _mod_kernel,
        grid=(n_layers, width // tn),
        in_specs=[
            pl.BlockSpec((rows, d), lambda l, j: (0, 0)),
            pl.BlockSpec((None, d, tn), lambda l, j: (l, 0, j)),
            pl.BlockSpec((None, 1, tn), lambda l, j: (l, 0, j)),
        ],
        out_specs=pl.BlockSpec((None, rows, tn), lambda l, j: (l, 0, j)),
        out_shape=jax.ShapeDtypeStruct((n_layers, rows, width), F32),
        compiler_params=_cparams(2),
        name="modulation",
    )(c_rows, w_ada, b_ada.reshape(n_layers, 1, width))


def _proj_kernel(x_ref, mod_ref, n1g_ref, win_ref, qg_ref, wqm_ref, wqr_ref, kvg_ref, wkk_ref, wkv_ref,
                 cos_ref, sin_ref, q_ref, k_ref, v_ref, nq_ref, nk_ref, nv_ref, ft_ref):
    d = x_ref.shape[-1]
    x = x_ref[...]
    mod = mod_ref[...]
    sh1, sc1 = mod[:, 0:d], mod[:, d:2 * d]
    h = _rms(x, n1g_ref[...]) * (1.0 + sc1) + sh1
    p = _dot(h.astype(BF16), win_ref[...])
    cos = cos_ref[...]
    sin = sin_ref[...]
    o = 0
    q_lat = p[:, o:o + MLA_Q_LORA]
    o += MLA_Q_LORA
    kv_lat = p[:, o:o + MLA_KV_LORA]
    o += MLA_KV_LORA
    kr_a = p[:, o:o + HEAD_PAD]
    o += HEAD_PAD
    kr_b = p[:, o:o + HEAD_PAD]
    o += HEAD_PAD
    nq_ref[...] = (p[:, o:o + NA_WIDTH] * NA_SCALE).astype(BF16)
    o += NA_WIDTH
    nk_ref[...] = p[:, o:o + NA_WIDTH].astype(BF16)
    o += NA_WIDTH
    nv_ref[...] = p[:, o:o + NA_WIDTH].astype(BF16)
    o += NA_WIDTH
    ft_ref[...] = p[:, o:o + FT_WIDTH].astype(BF16)

    qn = _rms(q_lat, qg_ref[...]).astype(BF16)
    qm = _dot(qn, wqm_ref[...])
    qr = _dot(qn, wqr_ref[...])
    kvn = _rms(kv_lat, kvg_ref[...]).astype(BF16)
    kn = _dot(kvn, wkk_ref[...])
    v = _dot(kvn, wkv_ref[...])
    v_lane = lax.broadcasted_iota(jnp.int32, v.shape, 1) & (HEAD_PAD - 1)
    v_ref[...] = jnp.where(v_lane == MLA_V, 1.0, v).astype(BF16)
    kr = kr_a * cos + kr_b * sin
    q_scale = MLA_SCALE * LOG2E
    for hd in range(MLA_HEADS):
        sl = slice(hd * HEAD_PAD, (hd + 1) * HEAD_PAD)
        q_ref[:, sl] = ((qm[:, sl] * cos + qr[:, sl] * sin) * q_scale).astype(BF16)
        k_ref[:, sl] = (kn[:, sl] + kr).astype(BF16)


def _layer_spec(a, layer):
    return pl.BlockSpec((None,) + a.shape[1:], lambda *_: (layer,) + (0,) * (a.ndim - 1))


def _mod_spec(mod, layer, mod_row):
    return pl.BlockSpec((None, None, 1, mod.shape[-1]), lambda b, i, *_: (layer, mod_row(b, i), 0, 0))


def _project(x, mod, mod_row, lw, layer, cos_t, sin_t):
    nb, n, d = x.shape
    tm = TOKEN_TILE
    win = lw["w_in"]
    wide = MLA_HEADS * HEAD_PAD

    def full(a):
        return _layer_spec(a, layer)

    def tok(width):
        return pl.BlockSpec((None, tm, width), lambda b, i: (b, i, 0))

    outs = [wide, wide, wide, NA_WIDTH, NA_WIDTH, NA_WIDTH, FT_WIDTH]
    return pl.pallas_call(
        _proj_kernel,
        grid=(nb, n // tm),
        in_specs=[
            tok(d),
            _mod_spec(mod, layer, mod_row),
            full(lw["norm1_g"]), full(win), full(lw["q_norm_g"]), full(lw["wq_main"]), full(lw["wq_rot"]),
            full(lw["kv_norm_g"]), full(lw["wkv_k"]), full(lw["wkv_v"]),
            pl.BlockSpec((tm, HEAD_PAD), lambda b, i: (i, 0)),
            pl.BlockSpec((tm, HEAD_PAD), lambda b, i: (i, 0)),
        ],
        out_specs=[tok(w) for w in outs],
        out_shape=[jax.ShapeDtypeStruct((nb, n, w), BF16) for w in outs],
        compiler_params=_cparams(2),
        name="project",
    )(x, mod, lw["norm1_g"], win, lw["q_norm_g"], lw["wq_main"], lw["wq_rot"], lw["kv_norm_g"],
      lw["wkv_k"], lw["wkv_v"], cos_t, sin_t)


def _kv_chunk(p):
    best = LANES
    for n in range(1, p // LANES + 1):
        tk = p // n
        if p % n == 0 and tk % LANES == 0 and tk <= MLA_MAX_KV_CHUNK:
            best = max(best, tk)
    return best


def _store_head_values(o_ref, rows, hd, out):
    if hd:
        out = pltpu.roll(out, MLA_V, axis=1)
    sl = slice(hd * MLA_V, (hd + 1) * MLA_V)
    o_ref[rows, sl] = out[:, sl].astype(BF16)


def _mla_kernel(q_ref, k_ref, v_ref, o_ref, s0_scr, s1_scr, p0_scr, p1_scr, mx_scr, m_scr, alpha_scr, acc_scr,
                *, tq, tk, n_q, n_c):
    def head(hd):
        return slice(hd * HEAD_PAD, (hd + 1) * HEAD_PAD)

    def q_rows(qt):
        return pl.ds(pl.multiple_of(qt * tq, tq), tq)

    def k_rows(c):
        return pl.ds(pl.multiple_of(c * tk, tk), tk)

    def scores(hd, qt, c, s_scr):
        s = _dot_nt(q_ref[q_rows(qt), head(hd)], k_ref[k_rows(c), head(hd)])
        s_scr[...] = s
        mx_scr[hd] = jnp.broadcast_to(jnp.max(s, axis=1, keepdims=True), mx_scr.shape[1:])

    def softmax(hd, s_scr, p_scr, first):
        m_prev = jnp.where(first, MASK_VALUE, m_scr[hd])
        m_new = jnp.maximum(m_prev, mx_scr[hd])
        alpha_scr[hd] = jnp.exp2(m_prev - m_new)
        m_scr[hd] = m_new
        p_scr[...] = jnp.exp2(s_scr[...] - jnp.tile(m_new, (1, tk // LANES))).astype(BF16)

    def accumulate(hd, p_scr, qt, c):
        acc = alpha_scr[hd] * acc_scr[hd] + _dot(p_scr[...], v_ref[k_rows(c), head(hd)])
        acc_scr[hd] = acc
        _store_head_values(o_ref, q_rows(qt), hd, acc / acc[:, MLA_V:MLA_V + 1])

    lane = lax.broadcasted_iota(jnp.int32, acc_scr.shape, 2)
    acc_scr[...] = jnp.where(lane == MLA_V, 1.0, 0.0)
    m_scr[...] = jnp.full(m_scr.shape, MASK_VALUE, F32)
    alpha_scr[...] = jnp.ones(alpha_scr.shape, F32)
    p1_scr[...] = jnp.zeros(p1_scr.shape, BF16)
    scores(0, 0, 0, s0_scr)

    def body(i, carry):
        qt, c = carry
        first = c == 0
        last = c == n_c - 1
        c_prev = jnp.where(first, n_c - 1, c - 1)
        qt_prev = jnp.maximum(jnp.where(first, qt - 1, qt), 0)
        c_next = jnp.where(last, 0, c + 1)
        qt_next = jnp.where(last, qt + 1, qt)
        scores(1, qt, c, s1_scr)
        softmax(0, s0_scr, p0_scr, first)
        accumulate(1, p1_scr, qt_prev, c_prev)
        scores(0, jnp.minimum(qt_next, n_q - 1), c_next, s0_scr)
        softmax(1, s1_scr, p1_scr, first)
        accumulate(0, p0_scr, qt, c)
        return qt_next, c_next

    lax.fori_loop(0, n_q * n_c, body, (jnp.int32(0), jnp.int32(0)))
    accumulate(1, p1_scr, n_q - 1, n_c - 1)


def _mla_attention(q, k_all, v_all, s):
    nb, p, _ = k_all.shape
    tq = min(512, s)
    tk = _kv_chunk(p)
    pairs = MLA_HEADS // 2
    kern = functools.partial(_mla_kernel, tq=tq, tk=tk, n_q=s // tq, n_c=p // tk)
    return pl.pallas_call(
        kern,
        grid=(nb, pairs),
        in_specs=[
            pl.BlockSpec((None, s, 2 * HEAD_PAD), lambda b, hp: (b, 0, hp)),
            pl.BlockSpec((None, p, 2 * HEAD_PAD), lambda b, hp: (b, 0, hp)),
            pl.BlockSpec((None, p, 2 * HEAD_PAD), lambda b, hp: (b, 0, hp)),
        ],
        out_specs=pl.BlockSpec((None, s, 2 * MLA_V), lambda b, hp: (b, 0, hp)),
        out_shape=jax.ShapeDtypeStruct((nb, s, MLA_HEADS * MLA_V), BF16),
        scratch_shapes=[
            pltpu.VMEM((tq, tk), F32),
            pltpu.VMEM((tq, tk), F32),
            pltpu.VMEM((tq, tk), BF16),
            pltpu.VMEM((tq, tk), BF16),
            pltpu.VMEM((2, tq, LANES), F32),
            pltpu.VMEM((2, tq, LANES), F32),
            pltpu.VMEM((2, tq, LANES), F32),
            pltpu.VMEM((2, tq, HEAD_PAD), F32),
        ],
        compiler_params=_cparams(2),
        name="mla_attention",
    )(q, k_all, v_all)


def _mla_ctx_kernel(q_ref, k_ref, v_ref, o_ref):
    for hd in range(2):
        sl = slice(hd * HEAD_PAD, (hd + 1) * HEAD_PAD)
        s = _dot_nt(q_ref[:, sl], k_ref[:, sl])
        p = jnp.exp2(s - jnp.max(s, axis=1, keepdims=True))
        acc = _dot(p.astype(BF16), v_ref[:, sl])
        _store_head_values(o_ref, slice(None), hd, acc / acc[:, MLA_V:MLA_V + 1])


def _mla_ctx_attention(q, k, v, s):
    nb, p, _ = q.shape
    c = p - s
    pairs = MLA_HEADS // 2
    spec = pl.BlockSpec((None, c, 2 * HEAD_PAD), lambda b, hp: (b, s // c, hp))
    return pl.pallas_call(
        _mla_ctx_kernel,
        grid=(nb, pairs),
        in_specs=[spec, spec, spec],
        out_specs=pl.BlockSpec((None, c, 2 * MLA_V), lambda b, hp: (b, 0, hp)),
        out_shape=jax.ShapeDtypeStruct((nb, c, MLA_HEADS * MLA_V), BF16),
        compiler_params=_cparams(2),
        name="mla_ctx_attention",
    )(q, k, v)


def _na_kernel(q_ref, k_ref, v_ref, kc_ref, vc_ref, bias_ref, rowmask_ref, o_ref, comb_scr, s0_scr, s1_scr, p0_scr,
               p1_scr, mx_scr, l_scr, *, grid_rows, n_blocks):
    tq = NA_ROWS_PER_BLOCK * GRID_W
    n_keys = NA_KEY_ROWS * GRID_W

    for hd in range(2):
        for var, row_off in enumerate((0, -(NA_WIN_R // 2), -(NA_KEY_ROWS - NA_ROWS_PER_BLOCK))):
            tile0 = row_off + (NA_WIN_R - 1) + NA_BIAS_SHIFT
            bias = jnp.concatenate(
                [jnp.concatenate([bias_ref[hd, tile0 + 2 * j - qr] for j in range(NA_KEY_ROWS // 2)], axis=1)
                 for qr in range(NA_ROWS_PER_BLOCK)], axis=0)
            comb_scr[hd, var] = bias + rowmask_ref[var].astype(F32)

    lane = lax.broadcasted_iota(jnp.int32, (tq, LANES), 1)

    def q_rows(rb):
        return pl.ds(pl.multiple_of(rb * tq, tq), tq)

    def key_rows(rb):
        start_row = jnp.clip(rb * NA_ROWS_PER_BLOCK - NA_WIN_R // 2, 0, grid_rows - NA_KEY_ROWS)
        return pl.ds(pl.multiple_of(start_row * GRID_W, (NA_WIN_R // 2) * GRID_W), n_keys)

    def scores(hd, rb, s_scr):
        var = jnp.where(rb == 0, 0, jnp.where(rb == n_blocks - 1, 2, 1))
        q = q_ref[q_rows(rb), :]
        in_head = (lane >= NA_HEAD_DIM) if hd else (lane < NA_HEAD_DIM)
        qh = jnp.where(in_head, q, jnp.zeros_like(q))
        s = _dot_nt(qh, k_ref[key_rows(rb), :]) + comb_scr[hd, var]
        sc = _dot_nt(qh, kc_ref[...])
        s_scr[:, :n_keys] = s
        s_scr[:, n_keys:] = sc
        m = jnp.maximum(jnp.max(s, axis=1, keepdims=True), jnp.max(sc, axis=1, keepdims=True))
        mx_scr[hd] = jnp.broadcast_to(m, mx_scr.shape[1:])

    def softmax(hd, s_scr, p_scr):
        p = jnp.exp(s_scr[...] - jnp.tile(mx_scr[hd], (1, s_scr.shape[1] // LANES)))
        l_scr[hd] = jnp.broadcast_to(jnp.sum(p, axis=1, keepdims=True), l_scr.shape[1:])
        p_scr[...] = p.astype(BF16)

    def accumulate(hd, p_scr, rb):
        acc = _dot(p_scr[:, :n_keys], v_ref[key_rows(rb), :]) + _dot(p_scr[:, n_keys:], vc_ref[...])
        out = (acc / l_scr[hd]).astype(BF16)
        sl = slice(hd * NA_HEAD_DIM, (hd + 1) * NA_HEAD_DIM)
        o_ref[q_rows(rb), sl] = out[:, sl]

    p1_scr[...] = jnp.zeros(p1_scr.shape, BF16)
    l_scr[...] = jnp.ones(l_scr.shape, F32)
    scores(0, 0, s0_scr)

    def body(rb, carry):
        scores(1, rb, s1_scr)
        softmax(0, s0_scr, p0_scr)
        accumulate(1, p1_scr, jnp.maximum(rb - 1, 0))
        scores(0, jnp.minimum(rb + 1, n_blocks - 1), s0_scr)
        softmax(1, s1_scr, p1_scr)
        accumulate(0, p0_scr, rb)
        return carry

    lax.fori_loop(0, n_blocks, body, 0)
    accumulate(1, p1_scr, n_blocks - 1)


def _na_attention(q, k, v, bias, row_mask, layer, s):
    nb, p, _ = q.shape
    c = p - s
    grid_rows = s // GRID_W
    tq = NA_ROWS_PER_BLOCK * GRID_W
    n_blocks = grid_rows // NA_ROWS_PER_BLOCK
    n_keys = NA_KEY_ROWS * GRID_W
    pairs = NA_HEADS // 2
    kern = functools.partial(_na_kernel, grid_rows=grid_rows, n_blocks=n_blocks)
    lat = pl.BlockSpec((None, s, LANES), lambda b, hp: (b, 0, hp))
    ctx = pl.BlockSpec((None, c, LANES), lambda b, hp: (b, s // c, hp))
    return pl.pallas_call(
        kern,
        grid=(nb, pairs),
        in_specs=[
            lat, lat, lat, ctx, ctx,
            pl.BlockSpec((None, 2) + bias.shape[2:], lambda b, hp: (layer, hp, 0, 0, 0)),
            pl.BlockSpec(row_mask.shape, lambda b, hp: (0, 0, 0)),
        ],
        out_specs=pl.BlockSpec((None, s, LANES), lambda b, hp: (b, 0, hp)),
        out_shape=jax.ShapeDtypeStruct((nb, s, NA_WIDTH), BF16),
        scratch_shapes=[
            pltpu.VMEM((2, 3, tq, n_keys), F32),
            pltpu.VMEM((tq, n_keys + c), F32),
            pltpu.VMEM((tq, n_keys + c), F32),
            pltpu.VMEM((tq, n_keys + c), BF16),
            pltpu.VMEM((tq, n_keys + c), BF16),
            pltpu.VMEM((2, tq, LANES), F32),
            pltpu.VMEM((2, tq, LANES), F32),
        ],
        compiler_params=_cparams(2),
        name="na_attention",
    )(q, k, v, k, v, bias, row_mask)


def _na_ctx_kernel(q_ref, k_ref, v_ref, o_ref):
    q = q_ref[...]
    k = k_ref[...]
    v = v_ref[...]
    lane = lax.broadcasted_iota(jnp.int32, q.shape, 1)
    outs = []
    for hd in range(2):
        in_head = (lane >= NA_HEAD_DIM) if hd else (lane < NA_HEAD_DIM)
        qh = jnp.where(in_head, q, jnp.zeros_like(q))
        s = _dot_nt(qh, k)
        p = jnp.exp(s - jnp.max(s, axis=1, keepdims=True))
        outs.append(_dot(p.astype(BF16), v) / jnp.sum(p, axis=1, keepdims=True))
    o_ref[...] = jnp.where(lane < NA_HEAD_DIM, outs[0], outs[1]).astype(BF16)


def _na_ctx_attention(q, k, v, s):
    nb, p, _ = q.shape
    c = p - s
    spec = pl.BlockSpec((None, c, LANES), lambda b, hp: (b, s // c, hp))
    return pl.pallas_call(
        _na_ctx_kernel,
        grid=(nb, NA_HEADS // 2),
        in_specs=[spec, spec, spec],
        out_specs=pl.BlockSpec((None, c, LANES), lambda b, hp: (b, 0, hp)),
        out_shape=jax.ShapeDtypeStruct((nb, c, NA_WIDTH), BF16),
        compiler_params=_cparams(2),
        name="na_ctx_attention",
    )(q, k, v)


def _na_row_masks(grid_rows):
    qr = np.arange(NA_ROWS_PER_BLOCK)[:, None]
    kr = np.arange(NA_KEY_ROWS)[None, :]
    row_valid = np.zeros((3, NA_ROWS_PER_BLOCK, NA_KEY_ROWS), bool)
    n_blocks = grid_rows // NA_ROWS_PER_BLOCK
    for vi, rb in enumerate((0, 1, n_blocks - 1)):
        r = rb * NA_ROWS_PER_BLOCK + qr
        start_row = min(max(rb * NA_ROWS_PER_BLOCK - NA_WIN_R // 2, 0), grid_rows - NA_KEY_ROWS)
        key_row = start_row + kr
        win = np.clip(r - NA_WIN_R // 2, 0, grid_rows - NA_WIN_R)
        row_valid[vi] = (key_row >= win) & (key_row < win + NA_WIN_R)
    mask = np.where(row_valid, 0.0, MASK_VALUE).astype(np.float32)
    mask = np.broadcast_to(mask[:, :, None, :, None], (3, NA_ROWS_PER_BLOCK, GRID_W, NA_KEY_ROWS, GRID_W))
    return jnp.asarray(mask.reshape(3, NA_ROWS_PER_BLOCK * GRID_W, NA_KEY_ROWS * GRID_W), BF16)


def _na_bias_tables(rpb):
    n_dc = 2 * NA_WIN_C - 1
    qc = np.arange(GRID_W)[:, None]
    kcol = np.arange(GRID_W)[None, :]
    wstart = np.clip(qc - NA_WIN_C // 2, 0, GRID_W - NA_WIN_C)
    col_valid = (kcol >= wstart) & (kcol < wstart + NA_WIN_C)
    dc = np.clip(kcol - qc, -(NA_WIN_C - 1), NA_WIN_C - 1) + (NA_WIN_C - 1)
    col_onehot = ((dc[..., None] == np.arange(n_dc)) & col_valid[..., None]).astype(np.float32)
    t = jnp.einsum("lhij,cdj->lhicd", rpb.astype(F32), jnp.asarray(col_onehot), precision=lax.Precision.HIGHEST)
    t = jnp.where(jnp.asarray(col_valid), t, MASK_VALUE)
    n_dr = t.shape[2]
    t = jnp.pad(t, ((0, 0), (0, 0), (NA_BIAS_SHIFT, NA_BIAS_TILES + 1 - NA_BIAS_SHIFT - n_dr), (0, 0), (0, 0)),
                constant_values=MASK_VALUE)
    return jnp.concatenate([t[:, :, :NA_BIAS_TILES], t[:, :, 1:NA_BIAS_TILES + 1]], axis=-1)


FT_N1 = 64


def _ft_stage1_kernel(w_ref, x_ref, o_ref):
    o_ref[...] = _dot(w_ref[...], x_ref[...])


def _ft_stage2_kernel(are_ref, aim_ref, tc_ref, ts_ref, w_ref, cd_ref, sd_ref, o_ref, *, scale):
    n2 = are_ref.shape[1]
    for g in range(are_ref.shape[0]):
        are = are_ref[g]
        aim = aim_ref[g]
        tc = jnp.concatenate([tc_ref[g]] * (FT_WIDTH // LANES), axis=1)
        ts = jnp.concatenate([ts_ref[g]] * (FT_WIDTH // LANES), axis=1)
        bre = are * tc + aim * ts
        bim = aim * tc - are * ts
        bb = jnp.concatenate([bre, bim], axis=0).astype(BF16)
        xk = _dot(w_ref[...], bb)
        f = _dot(xk[:n2].astype(BF16), cd_ref[...]) + _dot(xk[n2:].astype(BF16), sd_ref[...])
        o_ref[:, g * FT_WIDTH:(g + 1) * FT_WIDTH] = (f * scale).astype(BF16)


def _dft_mats(n):
    ang = 2.0 * np.pi * np.outer(np.arange(n), np.arange(n)) / n
    return np.cos(ang), np.sin(ang)


def _channel_dft():
    c, s = _dft_mats(FT_GROUP_DIM)
    eye = np.eye(FT_GROUPS)
    return jnp.asarray(np.kron(eye, c), BF16), jnp.asarray(np.kron(eye, s), BF16)


def _fourier_lat(u, n):
    nb, p, _ = u.shape
    n2 = n // FT_N1
    cols = n2 * FT_WIDTH
    c1, s1 = _dft_mats(FT_N1)
    w1 = jnp.asarray(np.concatenate([c1, -s1], axis=0), BF16)
    tcol = min(2048, cols)
    a = pl.pallas_call(
        _ft_stage1_kernel,
        grid=(nb, cols // tcol),
        in_specs=[
            pl.BlockSpec(w1.shape, lambda b, j: (0, 0)),
            pl.BlockSpec((None, FT_N1, tcol), lambda b, j: (b, 0, j)),
        ],
        out_specs=pl.BlockSpec((None, 2 * FT_N1, tcol), lambda b, j: (b, 0, j)),
        out_shape=jax.ShapeDtypeStruct((nb, 2 * FT_N1, cols), F32),
        compiler_params=_cparams(2),
        name="fourier_stage1",
    )(w1, u.reshape(nb, p // n2, cols))
    a = a.reshape(nb, 2 * FT_N1, n2, FT_WIDTH)

    ang = 2.0 * np.pi * np.outer(np.arange(FT_N1), np.arange(n2)) / n
    tc = jnp.asarray(np.repeat(np.cos(ang)[:, :, None], LANES, axis=2), F32)
    ts = jnp.asarray(np.repeat(np.sin(ang)[:, :, None], LANES, axis=2), F32)
    c3, s3 = _dft_mats(n2)
    w3 = jnp.asarray(np.block([[c3, s3], [-s3, c3]]), BF16)
    cd, sd = _channel_dft()
    kern = functools.partial(_ft_stage2_kernel, scale=1.0 / math.sqrt(n * FT_GROUP_DIM))
    g = FT_K1_PER_STEP
    n_steps = FT_N1 // g
    out = pl.pallas_call(
        kern,
        grid=(nb, n_steps),
        in_specs=[
            pl.BlockSpec((None, g, n2, FT_WIDTH), lambda b, k: (b, k, 0, 0)),
            pl.BlockSpec((None, g, n2, FT_WIDTH), lambda b, k: (b, n_steps + k, 0, 0)),
            pl.BlockSpec((g, n2, LANES), lambda b, k: (k, 0, 0)),
            pl.BlockSpec((g, n2, LANES), lambda b, k: (k, 0, 0)),
            pl.BlockSpec(w3.shape, lambda b, k: (0, 0)),
            pl.BlockSpec(cd.shape, lambda b, k: (0, 0)),
            pl.BlockSpec(sd.shape, lambda b, k: (0, 0)),
        ],
        out_specs=pl.BlockSpec((None, n2, g * FT_WIDTH), lambda b, k: (b, 0, k)),
        out_shape=jax.ShapeDtypeStruct((nb, n2, FT_N1 * FT_WIDTH), BF16),
        compiler_params=_cparams(2),
        name="fourier_stage2",
    )(a, a, tc, ts, w3, cd, sd)
    return out.reshape(nb, n, FT_WIDTH)


def _ft_ctx_kernel(u_ref, cn_ref, sn_ref, cd_ref, sd_ref, o_ref, *, scale):
    u = u_ref[...]
    y1 = _dot(u, cd_ref[...]).astype(BF16)
    y2 = _dot(u, sd_ref[...]).astype(BF16)
    f = _dot(cn_ref[...], y1) - _dot(sn_ref[...], y2)
    o_ref[...] = (f * scale).astype(BF16)


def _fourier_ctx(u, s):
    nb, p, _ = u.shape
    n = p - s
    cn, sn = _dft_mats(n)
    cn, sn = jnp.asarray(cn, BF16), jnp.asarray(sn, BF16)
    cd, sd = _channel_dft()
    kern = functools.partial(_ft_ctx_kernel, scale=1.0 / math.sqrt(n * FT_GROUP_DIM))
    sq = pl.BlockSpec((n, n), lambda b: (0, 0))
    ch = pl.BlockSpec(cd.shape, lambda b: (0, 0))
    return pl.pallas_call(
        kern,
        grid=(nb,),
        in_specs=[pl.BlockSpec((None, n, FT_WIDTH), lambda b: (b, s // n, 0)), sq, sq, ch, ch],
        out_specs=pl.BlockSpec((None, n, FT_WIDTH), lambda b: (b, 0, 0)),
        out_shape=jax.ShapeDtypeStruct((nb, n, FT_WIDTH), BF16),
        compiler_params=_cparams(1),
        name="fourier_ctx",
    )(u, cn, sn, cd, sd)


def _merge_kernel(x_ref, oa_ref, ob_ref, oc_ref, ca_ref, cb_ref, cc_ref, mod_ref, n1g_ref, n2g_ref, wgate_ref,
                  bgate_ref, wba_ref, wbb_ref, wbc_ref, wout_ref, x1_ref, h2_ref, *, n_lat_tiles):
    d = x_ref.shape[-1]
    x = x_ref[...]
    is_ctx = pl.program_id(1) >= n_lat_tiles
    o_a = jnp.where(is_ctx, ca_ref[...], oa_ref[...])
    o_b = jnp.where(is_ctx, cb_ref[...], ob_ref[...])
    o_c = jnp.where(is_ctx, cc_ref[...], oc_ref[...])
    mod = mod_ref[...]
    sh1, sc1, g1, sh2, sc2 = (mod[:, i * d:(i + 1) * d] for i in range(5))
    h = _rms(x, n1g_ref[...]) * (1.0 + sc1) + sh1
    gate = jax.nn.sigmoid(_dot(h.astype(BF16), wgate_ref[...]) + bgate_ref[...])
    m = gate[:, 0:d] * _dot(o_a, wba_ref[...])
    m += gate[:, d:2 * d] * _dot(o_b, wbb_ref[...])
    m += gate[:, 2 * d:3 * d] * _dot(o_c, wbc_ref[...])
    y = _dot(m.astype(BF16), wout_ref[...])
    x1 = x + g1 * y
    x1_ref[...] = x1
    h2_ref[...] = (_rms(x1, n2g_ref[...]) * (1.0 + sc2) + sh2).astype(BF16)


def _merge(x, lat_outs, ctx_outs, mod, mod_row, lw, layer):
    nb, n, d = x.shape
    tm = TOKEN_TILE
    n_lat_tiles = lat_outs[0].shape[1] // tm

    def full(a):
        return _layer_spec(a, layer)

    def tok(width):
        return pl.BlockSpec((None, tm, width), lambda b, i: (b, i, 0))

    def lat(a):
        return pl.BlockSpec((None, tm, a.shape[-1]), lambda b, i: (b, jnp.minimum(i, n_lat_tiles - 1), 0))

    def ctx(a):
        return pl.BlockSpec((None, tm, a.shape[-1]), lambda b, i: (b, jnp.maximum(i - n_lat_tiles, 0), 0))

    weights = [lw["norm1_g"], lw["norm2_g"], lw["w_gate"], lw["b_gate"], lw["w_br_mla"], lw["w_br_na"],
               lw["w_br_ft"], lw["w_out"]]
    return pl.pallas_call(
        functools.partial(_merge_kernel, n_lat_tiles=n_lat_tiles),
        grid=(nb, n // tm),
        in_specs=[tok(d)] + [lat(a) for a in lat_outs] + [ctx(a) for a in ctx_outs] + [_mod_spec(mod, layer, mod_row)]
        + [full(w) for w in weights],
        out_specs=[tok(d), tok(d)],
        out_shape=[jax.ShapeDtypeStruct((nb, n, d), F32), jax.ShapeDtypeStruct((nb, n, d), BF16)],
        compiler_params=_cparams(2),
        name="merge",
    )(x, *lat_outs, *ctx_outs, mod, *weights)


def _route(logits, router_bias):
    row = lax.broadcasted_iota(jnp.int32, logits.shape, 0)
    neg = -jnp.inf
    big = jnp.int32(N_EXPERTS)
    scores = jax.nn.sigmoid(logits)
    sel = scores + router_bias

    def top2(vals):
        m1 = jnp.max(vals, axis=0, keepdims=True)
        i1 = jnp.min(jnp.where(vals == m1, row, big), axis=0, keepdims=True)
        rest = jnp.where(row == i1, neg, vals)
        m2 = jnp.max(rest, axis=0, keepdims=True)
        i2 = jnp.min(jnp.where(rest == m2, row, big), axis=0, keepdims=True)
        return m1, i1, m2, i2

    grp_scores = []
    for g in range(N_GROUPS):
        in_grp = (row >= g * EXPERTS_PER_GROUP) & (row < (g + 1) * EXPERTS_PER_GROUP)
        m1, _, m2, _ = top2(jnp.where(in_grp, sel, neg))
        grp_scores.append(m1 + m2)
    best = functools.reduce(jnp.maximum, grp_scores)
    grp = jnp.full(best.shape, N_GROUPS - 1, jnp.int32)
    for g in range(N_GROUPS - 2, -1, -1):
        grp = jnp.where(grp_scores[g] == best, g, grp)
    in_sel = (row >= grp * EXPERTS_PER_GROUP) & (row < (grp + 1) * EXPERTS_PER_GROUP)
    _, i1, _, i2 = top2(jnp.where(in_sel, sel, neg))
    hit1 = row == i1
    hit2 = row == i2
    w1 = jnp.sum(jnp.where(hit1, scores, 0.0), axis=0, keepdims=True)
    w2 = jnp.sum(jnp.where(hit2, scores, 0.0), axis=0, keepdims=True)
    total = w1 + w2
    return jnp.where(hit1, w1 / total, 0.0) + jnp.where(hit2, w2 / total, 0.0), grp


def _split_bf16(v):
    hi = v.astype(BF16)
    return hi, (v - hi.astype(F32)).astype(BF16)


def _moe_kernel(h_ref, x_ref, mod_ref, cmod_ref, rw_ref, rb_ref, fg_ref, tri_ref, wg_ref, wu_ref, wd_ref, o_ref,
                col_scr, row_scr, count_scr, acc_scr, *, apply_final_norm, n_lat, cap):
    d = x_ref.shape[-1]
    tm = x_ref.shape[0]
    step = pl.program_id(2)
    n_steps = pl.num_programs(2)
    h = h_ref[...]

    @pl.when(step == 0)
    def _():
        gate_t, grp = _route(_dot_nt(rw_ref[...], h), rb_ref[...])
        member = jnp.where(lax.broadcasted_iota(jnp.int32, (SUBLANES, tm), 0) == grp, 1.0, 0.0)
        rank = _dot(member.astype(BF16), tri_ref[...])
        empty_rows = jnp.zeros((SUBLANES - EXPERTS_PER_GROUP - 2, tm), F32)
        per_group = [jnp.concatenate([gate_t[c * EXPERTS_PER_GROUP:(c + 1) * EXPERTS_PER_GROUP], rank[c:c + 1],
                                      member[c:c + 1], empty_rows], axis=0) for c in range(N_GROUPS)]
        pad_rows = jnp.zeros((LANES - N_GROUPS * SUBLANES, tm), F32)
        cols = jnp.concatenate(per_group + [pad_rows], axis=0).T
        for c in range(N_GROUPS):
            count_scr[c] = jnp.sum(member[c:c + 1]).astype(jnp.int32)
            row_scr[c] = per_group[c]
            col_scr[c] = pltpu.roll(cols, (LANES - c * SUBLANES) % LANES, axis=1) if c else cols
        acc_scr[...] = jnp.zeros(acc_scr.shape, F32)

    def experts(hx, gates):
        out = jnp.zeros((hx.shape[0], d), F32)
        for j in range(EXPERTS_PER_STEP):
            a = _dot(hx, wg_ref[j])
            u = _dot(hx, wu_ref[j])
            hid = (a * jax.nn.sigmoid(a)) * u * gates[:, j:j + 1]
            out += _dot(hid.astype(BF16), wd_ref[j])
        return out

    rank_lane, member_lane = EXPERTS_PER_GROUP, EXPERTS_PER_GROUP + 1
    rank_row = row_scr[step, rank_lane:rank_lane + 1, :]
    member_row = row_scr[step, member_lane:member_lane + 1, :]
    slot = lax.broadcasted_iota(jnp.int32, (cap, tm), 0).astype(F32)
    pick = jnp.where((slot == rank_row) & (member_row > 0.5), 1.0, 0.0).astype(BF16)
    hx = _dot(pick, h).astype(BF16)
    col = col_scr[step]
    g_hi, g_lo = _split_bf16(col)
    y = experts(hx, _dot(pick, g_hi) + _dot(pick, g_lo))
    slot_t = lax.broadcasted_iota(jnp.int32, (tm, cap), 1).astype(F32)
    rank_col = col[:, rank_lane:rank_lane + 1]
    member_col = col[:, member_lane:member_lane + 1] > 0.5
    put = jnp.where((slot_t == rank_col) & member_col, 1.0, 0.0).astype(BF16)
    y_hi, y_lo = _split_bf16(y)
    acc_scr[...] += _dot(put, y_hi) + _dot(put, y_lo)

    @pl.when(count_scr[step] > cap)
    def _():
        overflow = member_col & (rank_col >= cap)
        acc_scr[...] += experts(h, jnp.where(overflow, col_scr[step], 0.0))

    @pl.when(step == n_steps - 1)
    def _():
        tok_row = pl.program_id(1) * tm + lax.broadcasted_iota(jnp.int32, (tm, 1), 0)
        g2 = jnp.where(tok_row >= n_lat, cmod_ref[:, 5 * d:6 * d], mod_ref[:, 5 * d:6 * d])
        x2 = x_ref[...] + g2 * acc_scr[...]
        o_ref[...] = _rms(x2, fg_ref[...]) if apply_final_norm else x2


def _moe_tile(p):
    return max(t for t in range(TOKEN_TILE, MOE_MAX_TILE + 1, TOKEN_TILE) if p % t == 0)


def _moe_capacity(tm):
    return min(tm, -(-(5 * tm // (4 * N_GROUPS)) // LANES) * LANES)


def _moe(h2, x1, mod, ctx_row, lw, layer, router_w, router_b, final_g, n_lat, n_rows, tm, apply_final_norm):
    assert EXPERTS_PER_STEP == EXPERTS_PER_GROUP
    nb, _, d = x1.shape
    n_steps = N_EXPERTS // EXPERTS_PER_STEP
    ff = lw["w_e_gate"].shape[-1]
    tok = pl.BlockSpec((None, tm, d), lambda b, i, c: (b, i, 0))
    earlier = jnp.asarray(np.triu(np.ones((tm, tm), np.float32), k=1), BF16)
    kern = functools.partial(_moe_kernel, apply_final_norm=apply_final_norm, n_lat=n_lat, cap=_moe_capacity(tm))
    return pl.pallas_call(
        kern,
        grid=(nb, n_rows // tm, n_steps),
        in_specs=[
            tok, tok,
            _mod_spec(mod, layer, lambda b, i: b),
            _mod_spec(mod, layer, lambda b, i: ctx_row),
            pl.BlockSpec(router_w.shape, lambda b, i, c: (0, 0)),
            pl.BlockSpec(router_b.shape, lambda b, i, c: (0, 0)),
            pl.BlockSpec(final_g.shape, lambda b, i, c: (0, 0)),
            pl.BlockSpec(earlier.shape, lambda b, i, c: (0, 0)),
            pl.BlockSpec((None, EXPERTS_PER_STEP, d, ff), lambda b, i, c: (layer, c, 0, 0)),
            pl.BlockSpec((None, EXPERTS_PER_STEP, d, ff), lambda b, i, c: (layer, c, 0, 0)),
            pl.BlockSpec((None, EXPERTS_PER_STEP, ff, d), lambda b, i, c: (layer, c, 0, 0)),
        ],
        out_specs=tok,
        out_shape=jax.ShapeDtypeStruct((nb, n_rows, d), F32),
        scratch_shapes=[
            pltpu.VMEM((n_steps, tm, LANES), F32),
            pltpu.VMEM((n_steps, SUBLANES, tm), F32),
            pltpu.SMEM((n_steps,), jnp.int32),
            pltpu.VMEM((tm, d), F32),
        ],
        compiler_params=_cparams(3),
        name="moe",
    )(h2, x1, mod, mod, router_w, router_b, final_g, earlier, lw["w_e_gate"], lw["w_e_up"], lw["w_e_down"])


def _rotate_cols(w):
    half = w.shape[-1] // 2
    return jnp.concatenate([-w[..., half:], w[..., :half]], axis=-1)


def _prepare_weights(w_in, q_norm_g, w_q_up, kv_norm_g, w_kv_up, norm1_g, norm2_g, w_gate, b_gate, w_br_mla,
                     w_br_na, w_br_ft, w_out, w_e_gate, w_e_up, w_e_down):
    n_layers, d, _ = w_in.shape
    splits = np.cumsum([MLA_Q_LORA, MLA_KV_LORA, MLA_ROPE, NA_WIDTH, NA_WIDTH, NA_WIDTH])
    w_q, w_kv, w_kr, w_nq, w_nk, w_nv, w_ft = jnp.split(w_in, [int(s) for s in splits], axis=-1)

    def in_rope_slot(w):
        zeros_lo = jnp.zeros(w.shape[:-1] + (MLA_NOPE,), w.dtype)
        zeros_hi = jnp.zeros(w.shape[:-1] + (HEAD_PAD - MLA_NOPE - MLA_ROPE,), w.dtype)
        return jnp.concatenate([zeros_lo, w, zeros_hi], axis=-1)

    win = jnp.concatenate([w_q, w_kv, in_rope_slot(w_kr), in_rope_slot(_rotate_cols(w_kr)), w_nq, w_nk, w_nv, w_ft],
                          axis=-1).astype(BF16)
    wq = w_q_up.reshape(n_layers, MLA_Q_LORA, MLA_HEADS, MLA_NOPE + MLA_ROPE)
    wq_nope, wq_pe = wq[..., :MLA_NOPE], wq[..., MLA_NOPE:]
    pad = jnp.zeros(wq_pe.shape[:-1] + (HEAD_PAD - MLA_NOPE - MLA_ROPE,), wq.dtype)
    wq_main = jnp.concatenate([wq_nope, wq_pe, pad], axis=-1)
    wq_rot = jnp.concatenate([jnp.zeros_like(wq_nope), _rotate_cols(wq_pe), pad], axis=-1)
    wkv = w_kv_up.reshape(n_layers, MLA_KV_LORA, MLA_HEADS, MLA_NOPE + MLA_V)
    wk_nope, wv = wkv[..., :MLA_NOPE], wkv[..., MLA_NOPE:]
    wkv_k = jnp.concatenate([wk_nope, jnp.zeros(wk_nope.shape[:-1] + (HEAD_PAD - MLA_NOPE,), wkv.dtype)], axis=-1)
    wide = MLA_HEADS * HEAD_PAD
    return {
        "w_in": win,
        "wq_main": wq_main.reshape(n_layers, MLA_Q_LORA, wide).astype(BF16),
        "wq_rot": wq_rot.reshape(n_layers, MLA_Q_LORA, wide).astype(BF16),
        "wkv_k": wkv_k.reshape(n_layers, MLA_KV_LORA, wide).astype(BF16),
        "wkv_v": jnp.concatenate([wv, jnp.zeros(wv.shape[:-1] + (HEAD_PAD - MLA_V,), wv.dtype)], axis=-1)
        .reshape(n_layers, MLA_KV_LORA, wide).astype(BF16),
        "q_norm_g": q_norm_g.reshape(n_layers, 1, -1),
        "kv_norm_g": kv_norm_g.reshape(n_layers, 1, -1),
        "norm1_g": norm1_g.reshape(n_layers, 1, d),
        "norm2_g": norm2_g.reshape(n_layers, 1, d),
        "w_gate": w_gate.astype(BF16),
        "b_gate": b_gate.reshape(n_layers, 1, -1),
        "w_br_mla": w_br_mla.astype(BF16),
        "w_br_na": w_br_na.astype(BF16),
        "w_br_ft": w_br_ft.astype(BF16),
        "w_out": w_out.astype(BF16),
        "w_e_gate": w_e_gate.astype(BF16),
        "w_e_up": w_e_up.astype(BF16),
        "w_e_down": w_e_down.astype(BF16),
    }


def _rope_tables(s, c):
    n_freq = MLA_ROPE // 4
    inv_freq = ROPE_THETA ** (-jnp.arange(n_freq, dtype=F32) / n_freq)
    t = jnp.arange(s, dtype=jnp.int32)
    row = (t // GRID_W).astype(F32)
    col = (t % GRID_W).astype(F32)
    ang = jnp.concatenate([row[:, None] * inv_freq, col[:, None] * inv_freq], axis=-1)
    cos, sin = jnp.cos(ang), jnp.sin(ang)
    pad = HEAD_PAD - MLA_NOPE - MLA_ROPE
    cos_lat = jnp.concatenate([jnp.ones((s, MLA_NOPE), F32), cos, cos, jnp.zeros((s, pad), F32)], axis=-1)
    sin_lat = jnp.concatenate([jnp.zeros((s, MLA_NOPE), F32), sin, sin, jnp.zeros((s, pad), F32)], axis=-1)
    cos_ctx = jnp.concatenate([jnp.ones((c, MLA_NOPE + MLA_ROPE), F32), jnp.zeros((c, pad), F32)], axis=-1)
    sin_ctx = jnp.zeros((c, HEAD_PAD), F32)
    return cos_lat, sin_lat, cos_ctx, sin_ctx


@jax.jit
def _forward(x, c, ctx, c_ctx, w_ada, b_ada, norm1_g, norm2_g, w_in, q_norm_g, w_q_up, kv_norm_g, w_kv_up,
             na_rpb, w_gate, b_gate, w_br_mla, w_br_na, w_br_ft, w_out, router_w, router_bias, w_e_gate,
             w_e_up, w_e_down, final_norm_g):
    nb, s, d = x.shape
    n_ctx = ctx.shape[1]
    n_layers = w_ada.shape[0]
    grid_rows = s // GRID_W

    mod_rows = 8
    c_rows = jnp.zeros((mod_rows, d), F32).at[:nb].set(c).at[nb].set(c_ctx)
    mod_all = _modulation(c_rows, w_ada, b_ada).reshape(n_layers, mod_rows, 1, 6 * d)

    weights = _prepare_weights(w_in, q_norm_g, w_q_up, kv_norm_g, w_kv_up, norm1_g, norm2_g, w_gate, b_gate,
                               w_br_mla, w_br_na, w_br_ft, w_out, w_e_gate, w_e_up, w_e_down)
    na_bias = _na_bias_tables(na_rpb)
    na_row_mask = _na_row_masks(grid_rows)
    cos_lat, sin_lat, cos_ctx, sin_ctx = _rope_tables(s, n_ctx)
    cos_t = jnp.concatenate([cos_lat, cos_ctx], axis=0)
    sin_t = jnp.concatenate([sin_lat, sin_ctx], axis=0)
    router_w_p = router_w.T.astype(BF16)
    router_b_p = router_bias.reshape(N_EXPERTS, 1).astype(F32)
    final_g = final_norm_g.reshape(1, d)

    p = s + n_ctx
    n_lat_tiles = s // TOKEN_TILE
    ctx_row = nb

    def tile_row(b, i):
        return jnp.where(i >= n_lat_tiles, ctx_row, b)

    x_all = jnp.concatenate([x, ctx], axis=1)
    zero_ctx = [jnp.zeros((nb, n_ctx, w), BF16) for w in (MLA_HEADS * MLA_V, NA_WIDTH, FT_WIDTH)]
    for l in range(n_layers):
        last = l == n_layers - 1
        q, k, v, nq, nk, nv, ft = _project(x_all, mod_all, tile_row, weights, l, cos_t, sin_t)
        lat_outs = [_mla_attention(q, k, v, s), _na_attention(nq, nk, nv, na_bias, na_row_mask, l, s),
                    _fourier_lat(ft, s)]
        if last:
            ctx_outs = zero_ctx
        else:
            ctx_outs = [_mla_ctx_attention(q, k, v, s), _na_ctx_attention(nq, nk, nv, s), _fourier_ctx(ft, s)]
        x1, h2 = _merge(x_all, lat_outs, ctx_outs, mod_all, tile_row, weights, l)
        if last:
            return _moe(h2, x1, mod_all, ctx_row, weights, l, router_w_p, router_b_p, final_g, s, s,
                        _moe_tile(s), True)
        x_all = _moe(h2, x1, mod_all, ctx_row, weights, l, router_w_p, router_b_p, final_g, s, p,
                     _moe_tile(p), False)


def kernel(x, c, ctx, c_ctx, w_ada, b_ada, norm1_g, norm2_g, w_in, q_norm_g, w_q_up, kv_norm_g, w_kv_up, na_rpb, w_gate, b_gate, w_br_mla, w_br_na, w_br_ft, w_out, router_w, router_bias, w_e_gate, w_e_up, w_e_down, final_norm_g):
    return _forward(x, c, ctx, c_ctx, w_ada, b_ada, norm1_g, norm2_g, w_in, q_norm_g, w_q_up, kv_norm_g,
                    w_kv_up, na_rpb, w_gate, b_gate, w_br_mla, w_br_na, w_br_ft, w_out, router_w, router_bias,
                    w_e_gate, w_e_up, w_e_down, final_norm_g)
```

```python
import functools
import math

import jax
import jax.numpy as jnp
import numpy as np
from jax import lax
from jax.experimental import pallas as pl
from jax.experimental.pallas import tpu as pltpu

F32 = jnp.float32
BF16 = jnp.bfloat16

GRID_W = 64
MLA_HEADS = 8
MLA_NOPE = 64
MLA_ROPE = 32
MLA_V = 64
MLA_Q_LORA = 256
MLA_KV_LORA = 128
MLA_SCALE = (MLA_NOPE + MLA_ROPE) ** -0.5
NA_HEADS = 4
NA_HEAD_DIM = 64
NA_WIN_R = 8
NA_WIN_C = 16
NA_SCALE = NA_HEAD_DIM ** -0.5
NA_WIDTH = NA_HEADS * NA_HEAD_DIM
FT_GROUPS = 4
FT_GROUP_DIM = 64
FT_WIDTH = FT_GROUPS * FT_GROUP_DIM
N_EXPERTS = 16
N_GROUPS = 4
EXPERTS_PER_GROUP = N_EXPERTS // N_GROUPS
ROPE_THETA = 10000.0
NORM_EPS = 1e-6
MASK_VALUE = -1e30
LOG2E = math.log2(math.e)

LANES = 128
SUBLANES = 8
V7X_VMEM_LIMIT_BYTES = 56 * 1024 * 1024

HEAD_PAD = LANES
NA_ROWS_PER_BLOCK = 8
NA_KEY_ROWS = 16
NA_BIAS_SHIFT = (NA_KEY_ROWS - NA_ROWS_PER_BLOCK) + (NA_ROWS_PER_BLOCK - 1) - (NA_WIN_R - 1)
NA_BIAS_TILES = (NA_WIN_R - 1) + NA_BIAS_SHIFT + (NA_KEY_ROWS - 2) + 1
FT_K1_PER_STEP = 8
EXPERTS_PER_STEP = 4
MOE_MAX_TILE = 768
TOKEN_TILE = 256
MLA_MAX_KV_CHUNK = 2816


def _cparams(n_axes):
    return pltpu.CompilerParams(
        dimension_semantics=("arbitrary",) * n_axes,
        vmem_limit_bytes=V7X_VMEM_LIMIT_BYTES,
    )


def _rms(x, g):
    return x * lax.rsqrt(jnp.mean(x * x, axis=-1, keepdims=True) + NORM_EPS) * g


def _dot(a, b):
    return jnp.dot(a, b, preferred_element_type=F32)


def _dot_nt(a, b):
    return lax.dot_general(a, b, (((1,), (1,)), ((), ())), preferred_element_type=F32)


def _mod_kernel(c_ref, w_ref, b_ref, o_ref):
    c = c_ref[...]
    o_ref[...] = _dot(c * jax.nn.sigmoid(c), w_ref[...]) + b_ref[...]


def _modulation(c_rows, w_ada, b_ada):
    n_layers, d, width = w_ada.shape
    rows = c_rows.shape[0]
    tn = 1536
    return pl.pallas_call(
        _mod_kernel,
        grid=(n_layers, width // tn),
        in_specs=[
            pl.BlockSpec((rows, d), lambda l, j: (0, 0)),
            pl.BlockSpec((None, d, tn), lambda l, j: (l, 0, j)),
            pl.BlockSpec((None, 1, tn), lambda l, j: (l, 0, j)),
        ],
        out_specs=pl.BlockSpec((None, rows, tn), lambda l, j: (l, 0, j)),
        out_shape=jax.ShapeDtypeStruct((n_layers, rows, width), F32),
        compiler_params=_cparams(2),
        name="modulation",
    )(c_rows, w_ada, b_ada.reshape(n_layers, 1, width))


def _proj_kernel(x_ref, mod_ref, n1g_ref, win_ref, qg_ref, wqm_ref, wqr_ref, kvg_ref, wkk_ref, wkv_ref,
                 cos_ref, sin_ref, q_ref, k_ref, v_ref, nq_ref, nk_ref, nv_ref, ft_ref):
    d = x_ref.shape[-1]
    x = x_ref[...]
    mod = mod_ref[...]
    sh1, sc1 = mod[:, 0:d], mod[:, d:2 * d]
    h = _rms(x, n1g_ref[...]) * (1.0 + sc1) + sh1
    p = _dot(h.astype(BF16), win_ref[...])
    cos = cos_ref[...]
    sin = sin_ref[...]
    o = 0
    q_lat = p[:, o:o + MLA_Q_LORA]
    o += MLA_Q_LORA
    kv_lat = p[:, o:o + MLA_KV_LORA]
    o += MLA_KV_LORA
    kr_a = p[:, o:o + HEAD_PAD]
    o += HEAD_PAD
    kr_b = p[:, o:o + HEAD_PAD]
    o += HEAD_PAD
    nq_ref[...] = (p[:, o:o + NA_WIDTH] * NA_SCALE).astype(BF16)
    o += NA_WIDTH
    nk_ref[...] = p[:, o:o + NA_WIDTH].astype(BF16)
    o += NA_WIDTH
    nv_ref[...] = p[:, o:o + NA_WIDTH].astype(BF16)
    o += NA_WIDTH
    ft_ref[...] = p[:, o:o + FT_WIDTH].astype(BF16)

    qn = _rms(q_lat, qg_ref[...]).astype(BF16)
    qm = _dot(qn, wqm_ref[...])
    qr = _dot(qn, wqr_ref[...])
    kvn = _rms(kv_lat, kvg_ref[...]).astype(BF16)
    kn = _dot(kvn, wkk_ref[...])
    v = _dot(kvn, wkv_ref[...])
    v_lane = lax.broadcasted_iota(jnp.int32, v.shape, 1) & (HEAD_PAD - 1)
    v_ref[...] = jnp.where(v_lane == MLA_V, 1.0, v).astype(BF16)
    kr = kr_a * cos + kr_b * sin
    q_scale = MLA_SCALE * LOG2E
    for hd in range(MLA_HEADS):
        sl = slice(hd * HEAD_PAD, (hd + 1) * HEAD_PAD)
        q_ref[:, sl] = ((qm[:, sl] * cos + qr[:, sl] * sin) * q_scale).astype(BF16)
        k_ref[:, sl] = (kn[:, sl] + kr).astype(BF16)


def _layer_spec(a, layer):
    return pl.BlockSpec((None,) + a.shape[1:], lambda *_: (layer,) + (0,) * (a.ndim - 1))


def _mod_spec(mod, layer, mod_row):
    return pl.BlockSpec((None, None, 1, mod.shape[-1]), lambda b, i, *_: (layer, mod_row(b, i), 0, 0))


def _project(x, mod, mod_row, lw, layer, cos_t, sin_t):
    nb, n, d = x.shape
    tm = TOKEN_TILE
    win = lw["w_in"]
    wide = MLA_HEADS * HEAD_PAD

    def full(a):
        return _layer_spec(a, layer)

    def tok(width):
        return pl.BlockSpec((None, tm, width), lambda b, i: (b, i, 0))

    outs = [wide, wide, wide, NA_WIDTH, NA_WIDTH, NA_WIDTH, FT_WIDTH]
    return pl.pallas_call(
        _proj_kernel,
        grid=(nb, n // tm),
        in_specs=[
            tok(d),
            _mod_spec(mod, layer, mod_row),
            full(lw["norm1_g"]), full(win), full(lw["q_norm_g"]), full(lw["wq_main"]), full(lw["wq_rot"]),
            full(lw["kv_norm_g"]), full(lw["wkv_k"]), full(lw["wkv_v"]),
            pl.BlockSpec((tm, HEAD_PAD), lambda b, i: (i, 0)),
            pl.BlockSpec((tm, HEAD_PAD), lambda b, i: (i, 0)),
        ],
        out_specs=[tok(w) for w in outs],
        out_shape=[jax.ShapeDtypeStruct((nb, n, w), BF16) for w in outs],
        compiler_params=_cparams(2),
        name="project",
    )(x, mod, lw["norm1_g"], win, lw["q_norm_g"], lw["wq_main"], lw["wq_rot"], lw["kv_norm_g"],
      lw["wkv_k"], lw["wkv_v"], cos_t, sin_t)


def _kv_chunk(p):
    best = LANES
    for n in range(1, p // LANES + 1):
        tk = p // n
        if p % n == 0 and tk % LANES == 0 and tk <= MLA_MAX_KV_CHUNK:
            best = max(best, tk)
    return best


def _store_head_values(o_ref, rows, hd, out):
    if hd:
        out = pltpu.roll(out, MLA_V, axis=1)
    sl = slice(hd * MLA_V, (hd + 1) * MLA_V)
    o_ref[rows, sl] = out[:, sl].astype(BF16)


def _mla_kernel(q_ref, k_ref, v_ref, o_ref, s0_scr, s1_scr, p0_scr, p1_scr, mx_scr, m_scr, alpha_scr, acc_scr,
                *, tq, tk, n_q, n_c):
    def head(hd):
        return slice(hd * HEAD_PAD, (hd + 1) * HEAD_PAD)

    def q_rows(qt):
        return pl.ds(pl.multiple_of(qt * tq, tq), tq)

    def k_rows(c):
        return pl.ds(pl.multiple_of(c * tk, tk), tk)

    def scores(hd, qt, c, s_scr):
        s = _dot_nt(q_ref[q_rows(qt), head(hd)], k_ref[k_rows(c), head(hd)])
        s_scr[...] = s
        mx_scr[hd] = jnp.broadcast_to(jnp.max(s, axis=1, keepdims=True), mx_scr.shape[1:])

    def softmax(hd, s_scr, p_scr, first):
        m_prev = jnp.where(first, MASK_VALUE, m_scr[hd])
        m_new = jnp.maximum(m_prev, mx_scr[hd])
        alpha_scr[hd] = jnp.exp2(m_prev - m_new)
        m_scr[hd] = m_new
        p_scr[...] = jnp.exp2(s_scr[...] - jnp.tile(m_new, (1, tk // LANES))).astype(BF16)

    def accumulate(hd, p_scr, qt, c):
        acc = alpha_scr[hd] * acc_scr[hd] + _dot(p_scr[...], v_ref[k_rows(c), head(hd)])
        acc_scr[hd] = acc
        _store_head_values(o_ref, q_rows(qt), hd, acc / acc[:, MLA_V:MLA_V + 1])

    lane = lax.broadcasted_iota(jnp.int32, acc_scr.shape, 2)
    acc_scr[...] = jnp.where(lane == MLA_V, 1.0, 0.0)
    m_scr[...] = jnp.full(m_scr.shape, MASK_VALUE, F32)
    alpha_scr[...] = jnp.ones(alpha_scr.shape, F32)
    p1_scr[...] = jnp.zeros(p1_scr.shape, BF16)
    scores(0, 0, 0, s0_scr)

    def body(i, carry):
        qt, c = carry
        first = c == 0
        last = c == n_c - 1
        c_prev = jnp.where(first, n_c - 1, c - 1)
        qt_prev = jnp.maximum(jnp.where(first, qt - 1, qt), 0)
        c_next = jnp.where(last, 0, c + 1)
        qt_next = jnp.where(last, qt + 1, qt)
        scores(1, qt, c, s1_scr)
        softmax(0, s0_scr, p0_scr, first)
        accumulate(1, p1_scr, qt_prev, c_prev)
        scores(0, jnp.minimum(qt_next, n_q - 1), c_next, s0_scr)
        softmax(1, s1_scr, p1_scr, first)
        accumulate(0, p0_scr, qt, c)
        return qt_next, c_next

    lax.fori_loop(0, n_q * n_c, body, (jnp.int32(0), jnp.int32(0)))
    accumulate(1, p1_scr, n_q - 1, n_c - 1)


def _mla_attention(q, k_all, v_all, s):
    nb, p, _ = k_all.shape
    tq = min(512, s)
    tk = _kv_chunk(p)
    pairs = MLA_HEADS // 2
    kern = functools.partial(_mla_kernel, tq=tq, tk=tk, n_q=s // tq, n_c=p // tk)
    return pl.pallas_call(
        kern,
        grid=(nb, pairs),
        in_specs=[
            pl.BlockSpec((None, s, 2 * HEAD_PAD), lambda b, hp: (b, 0, hp)),
            pl.BlockSpec((None, p, 2 * HEAD_PAD), lambda b, hp: (b, 0, hp)),
            pl.BlockSpec((None, p, 2 * HEAD_PAD), lambda b, hp: (b, 0, hp)),
        ],
        out_specs=pl.BlockSpec((None, s, 2 * MLA_V), lambda b, hp: (b, 0, hp)),
        out_shape=jax.ShapeDtypeStruct((nb, s, MLA_HEADS * MLA_V), BF16),
        scratch_shapes=[
            pltpu.VMEM((tq, tk), F32),
            pltpu.VMEM((tq, tk), F32),
            pltpu.VMEM((tq, tk), BF16),
            pltpu.VMEM((tq, tk), BF16),
            pltpu.VMEM((2, tq, LANES), F32),
            pltpu.VMEM((2, tq, LANES), F32),
            pltpu.VMEM((2, tq, LANES), F32),
            pltpu.VMEM((2, tq, HEAD_PAD), F32),
        ],
        compiler_params=_cparams(2),
        name="mla_attention",
    )(q, k_all, v_all)


def _mla_ctx_kernel(q_ref, k_ref, v_ref, o_ref):
    for hd in range(2):
        sl = slice(hd * HEAD_PAD, (hd + 1) * HEAD_PAD)
        s = _dot_nt(q_ref[:, sl], k_ref[:, sl])
        p = jnp.exp2(s - jnp.max(s, axis=1, keepdims=True))
        acc = _dot(p.astype(BF16), v_ref[:, sl])
        _store_head_values(o_ref, slice(None), hd, acc / acc[:, MLA_V:MLA_V + 1])


def _mla_ctx_attention(q, k, v, s):
    nb, p, _ = q.shape
    c = p - s
    pairs = MLA_HEADS // 2
    spec = pl.BlockSpec((None, c, 2 * HEAD_PAD), lambda b, hp: (b, s // c, hp))
    return pl.pallas_call(
        _mla_ctx_kernel,
        grid=(nb, pairs),
        in_specs=[spec, spec, spec],
        out_specs=pl.BlockSpec((None, c, 2 * MLA_V), lambda b, hp: (b, 0, hp)),
        out_shape=jax.ShapeDtypeStruct((nb, c, MLA_HEADS * MLA_V), BF16),
        compiler_params=_cparams(2),
        name="mla_ctx_attention",
    )(q, k, v)


def _na_kernel(q_ref, k_ref, v_ref, kc_ref, vc_ref, bias_ref, rowmask_ref, o_ref, comb_scr, s0_scr, s1_scr, p0_scr,
               p1_scr, mx_scr, l_scr, *, grid_rows, n_blocks):
    tq = NA_ROWS_PER_BLOCK * GRID_W
    n_keys = NA_KEY_ROWS * GRID_W

    for hd in range(2):
        for var, row_off in enumerate((0, -(NA_WIN_R // 2), -(NA_KEY_ROWS - NA_ROWS_PER_BLOCK))):
            tile0 = row_off + (NA_WIN_R - 1) + NA_BIAS_SHIFT
            bias = jnp.concatenate(
                [jnp.concatenate([bias_ref[hd, tile0 + 2 * j - qr] for j in range(NA_KEY_ROWS // 2)], axis=1)
                 for qr in range(NA_ROWS_PER_BLOCK)], axis=0)
            comb_scr[hd, var] = bias + rowmask_ref[var].astype(F32)

    lane = lax.broadcasted_iota(jnp.int32, (tq, LANES), 1)

    def q_rows(rb):
        return pl.ds(pl.multiple_of(rb * tq, tq), tq)

    def key_rows(rb):
        start_row = jnp.clip(rb * NA_ROWS_PER_BLOCK - NA_WIN_R // 2, 0, grid_rows - NA_KEY_ROWS)
        return pl.ds(pl.multiple_of(start_row * GRID_W, (NA_WIN_R // 2) * GRID_W), n_keys)

    def scores(hd, rb, s_scr):
        var = jnp.where(rb == 0, 0, jnp.where(rb == n_blocks - 1, 2, 1))
        q = q_ref[q_rows(rb), :]
        in_head = (lane >= NA_HEAD_DIM) if hd else (lane < NA_HEAD_DIM)
        qh = jnp.where(in_head, q, jnp.zeros_like(q))
        s = _dot_nt(qh, k_ref[key_rows(rb), :]) + comb_scr[hd, var]
        sc = _dot_nt(qh, kc_ref[...])
        s_scr[:, :n_keys] = s
        s_scr[:, n_keys:] = sc
        m = jnp.maximum(jnp.max(s, axis=1, keepdims=True), jnp.max(sc, axis=1, keepdims=True))
        mx_scr[hd] = jnp.broadcast_to(m, mx_scr.shape[1:])

    def softmax(hd, s_scr, p_scr):
        p = jnp.exp(s_scr[...] - jnp.tile(mx_scr[hd], (1, s_scr.shape[1] // LANES)))
        l_scr[hd] = jnp.broadcast_to(jnp.sum(p, axis=1, keepdims=True), l_scr.shape[1:])
        p_scr[...] = p.astype(BF16)

    def accumulate(hd, p_scr, rb):
        acc = _dot(p_scr[:, :n_keys], v_ref[key_rows(rb), :]) + _dot(p_scr[:, n_keys:], vc_ref[...])
        out = (acc / l_scr[hd]).astype(BF16)
        sl = slice(hd * NA_HEAD_DIM, (hd + 1) * NA_HEAD_DIM)
        o_ref[q_rows(rb), sl] = out[:, sl]

    p1_scr[...] = jnp.zeros(p1_scr.shape, BF16)
    l_scr[...] = jnp.ones(l_scr.shape, F32)
    scores(0, 0, s0_scr)

    def body(rb, carry):
        scores(1, rb, s1_scr)
        softmax(0, s0_scr, p0_scr)
        accumulate(1, p1_scr, jnp.maximum(rb - 1, 0))
        scores(0, jnp.minimum(rb + 1, n_blocks - 1), s0_scr)
        softmax(1, s1_scr, p1_scr)
        accumulate(0, p0_scr, rb)
        return carry

    lax.fori_loop(0, n_blocks, body, 0)
    accumulate(1, p1_scr, n_blocks - 1)


def _na_attention(q, k, v, bias, row_mask, layer, s):
    nb, p, _ = q.shape
    c = p - s
    grid_rows = s // GRID_W
    tq = NA_ROWS_PER_BLOCK * GRID_W
    n_blocks = grid_rows // NA_ROWS_PER_BLOCK
    n_keys = NA_KEY_ROWS * GRID_W
    pairs = NA_HEADS // 2
    kern = functools.partial(_na_kernel, grid_rows=grid_rows, n_blocks=n_blocks)
    lat = pl.BlockSpec((None, s, LANES), lambda b, hp: (b, 0, hp))
    ctx = pl.BlockSpec((None, c, LANES), lambda b, hp: (b, s // c, hp))
    return pl.pallas_call(
        kern,
        grid=(nb, pairs),
        in_specs=[
            lat, lat, lat, ctx, ctx,
            pl.BlockSpec((None, 2) + bias.shape[2:], lambda b, hp: (layer, hp, 0, 0, 0)),
            pl.BlockSpec(row_mask.shape, lambda b, hp: (0, 0, 0)),
        ],
        out_specs=pl.BlockSpec((None, s, LANES), lambda b, hp: (b, 0, hp)),
        out_shape=jax.ShapeDtypeStruct((nb, s, NA_WIDTH), BF16),
        scratch_shapes=[
            pltpu.VMEM((2, 3, tq, n_keys), F32),
            pltpu.VMEM((tq, n_keys + c), F32),
            pltpu.VMEM((tq, n_keys + c), F32),
            pltpu.VMEM((tq, n_keys + c), BF16),
            pltpu.VMEM((tq, n_keys + c), BF16),
            pltpu.VMEM((2, tq, LANES), F32),
            pltpu.VMEM((2, tq, LANES), F32),
        ],
        compiler_params=_cparams(2),
        name="na_attention",
    )(q, k, v, k, v, bias, row_mask)


def _na_ctx_kernel(q_ref, k_ref, v_ref, o_ref):
    q = q_ref[...]
    k = k_ref[...]
    v = v_ref[...]
    lane = lax.broadcasted_iota(jnp.int32, q.shape, 1)
    outs = []
    for hd in range(2):
        in_head = (lane >= NA_HEAD_DIM) if hd else (lane < NA_HEAD_DIM)
        qh = jnp.where(in_head, q, jnp.zeros_like(q))
        s = _dot_nt(qh, k)
        p = jnp.exp(s - jnp.max(s, axis=1, keepdims=True))
        outs.append(_dot(p.astype(BF16), v) / jnp.sum(p, axis=1, keepdims=True))
    o_ref[...] = jnp.where(lane < NA_HEAD_DIM, outs[0], outs[1]).astype(BF16)


def _na_ctx_attention(q, k, v, s):
    nb, p, _ = q.shape
    c = p - s
    spec = pl.BlockSpec((None, c, LANES), lambda b, hp: (b, s // c, hp))
    return pl.pallas_call(
        _na_ctx_kernel,
        grid=(nb, NA_HEADS // 2),
        in_specs=[spec, spec, spec],
        out_specs=pl.BlockSpec((None, c, LANES), lambda b, hp: (b, 0, hp)),
        out_shape=jax.ShapeDtypeStruct((nb, c, NA_WIDTH), BF16),
        compiler_params=_cparams(2),
        name="na_ctx_attention",
    )(q, k, v)


def _na_row_masks(grid_rows):
    qr = np.arange(NA_ROWS_PER_BLOCK)[:, None]
    kr = np.arange(NA_KEY_ROWS)[None, :]
    row_valid = np.zeros((3, NA_ROWS_PER_BLOCK, NA_KEY_ROWS), bool)
    n_blocks = grid_rows // NA_ROWS_PER_BLOCK
    for vi, rb in enumerate((0, 1, n_blocks - 1)):
        r = rb * NA_ROWS_PER_BLOCK + qr
        start_row = min(max(rb * NA_ROWS_PER_BLOCK - NA_WIN_R // 2, 0), grid_rows - NA_KEY_ROWS)
        key_row = start_row + kr
        win = np.clip(r - NA_WIN_R // 2, 0, grid_rows - NA_WIN_R)
        row_valid[vi] = (key_row >= win) & (key_row < win + NA_WIN_R)
    mask = np.where(row_valid, 0.0, MASK_VALUE).astype(np.float32)
    mask = np.broadcast_to(mask[:, :, None, :, None], (3, NA_ROWS_PER_BLOCK, GRID_W, NA_KEY_ROWS, GRID_W))
    return jnp.asarray(mask.reshape(3, NA_ROWS_PER_BLOCK * GRID_W, NA_KEY_ROWS * GRID_W), BF16)


def _na_bias_tables(rpb):
    n_dc = 2 * NA_WIN_C - 1
    qc = np.arange(GRID_W)[:, None]
    kcol = np.arange(GRID_W)[None, :]
    wstart = np.clip(qc - NA_WIN_C // 2, 0, GRID_W - NA_WIN_C)
    col_valid = (kcol >= wstart) & (kcol < wstart + NA_WIN_C)
    dc = np.clip(kcol - qc, -(NA_WIN_C - 1), NA_WIN_C - 1) + (NA_WIN_C - 1)
    col_onehot = ((dc[..., None] == np.arange(n_dc)) & col_valid[..., None]).astype(np.float32)
    t = jnp.einsum("lhij,cdj->lhicd", rpb.astype(F32), jnp.asarray(col_onehot), precision=lax.Precision.HIGHEST)
    t = jnp.where(jnp.asarray(col_valid), t, MASK_VALUE)
    n_dr = t.shape[2]
    t = jnp.pad(t, ((0, 0), (0, 0), (NA_BIAS_SHIFT, NA_BIAS_TILES + 1 - NA_BIAS_SHIFT - n_dr), (0, 0), (0, 0)),
                constant_values=MASK_VALUE)
    return jnp.concatenate([t[:, :, :NA_BIAS_TILES], t[:, :, 1:NA_BIAS_TILES + 1]], axis=-1)


FT_N1 = 64


def _ft_stage1_kernel(w_ref, x_ref, o_ref):
    o_ref[...] = _dot(w_ref[...], x_ref[...])


def _ft_stage2_kernel(are_ref, aim_ref, tc_ref, ts_ref, w_ref, cd_ref, sd_ref, o_ref, *, scale):
    n2 = are_ref.shape[1]
    for g in range(are_ref.shape[0]):
        are = are_ref[g]
        aim = aim_ref[g]
        tc = jnp.concatenate([tc_ref[g]] * (FT_WIDTH // LANES), axis=1)
        ts = jnp.concatenate([ts_ref[g]] * (FT_WIDTH // LANES), axis=1)
        bre = are * tc + aim * ts
        bim = aim * tc - are * ts
        bb = jnp.concatenate([bre, bim], axis=0).astype(BF16)
        xk = _dot(w_ref[...], bb)
        f = _dot(xk[:n2].astype(BF16), cd_ref[...]) + _dot(xk[n2:].astype(BF16), sd_ref[...])
        o_ref[:, g * FT_WIDTH:(g + 1) * FT_WIDTH] = (f * scale).astype(BF16)


def _dft_mats(n):
    ang = 2.0 * np.pi * np.outer(np.arange(n), np.arange(n)) / n
    return np.cos(ang), np.sin(ang)


def _channel_dft():
    c, s = _dft_mats(FT_GROUP_DIM)
    eye = np.eye(FT_GROUPS)
    return jnp.asarray(np.kron(eye, c), BF16), jnp.asarray(np.kron(eye, s), BF16)


def _fourier_lat(u, n):
    nb, p, _ = u.shape
    n2 = n // FT_N1
    cols = n2 * FT_WIDTH
    c1, s1 = _dft_mats(FT_N1)
    w1 = jnp.asarray(np.concatenate([c1, -s1], axis=0), BF16)
    tcol = min(2048, cols)
    a = pl.pallas_call(
        _ft_stage1_kernel,
        grid=(nb, cols // tcol),
        in_specs=[
            pl.BlockSpec(w1.shape, lambda b, j: (0, 0)),
            pl.BlockSpec((None, FT_N1, tcol), lambda b, j: (b, 0, j)),
        ],
        out_specs=pl.BlockSpec((None, 2 * FT_N1, tcol), lambda b, j: (b, 0, j)),
        out_shape=jax.ShapeDtypeStruct((nb, 2 * FT_N1, cols), F32),
        compiler_params=_cparams(2),
        name="fourier_stage1",
    )(w1, u.reshape(nb, p // n2, cols))
    a = a.reshape(nb, 2 * FT_N1, n2, FT_WIDTH)

    ang = 2.0 * np.pi * np.outer(np.arange(FT_N1), np.arange(n2)) / n
    tc = jnp.asarray(np.repeat(np.cos(ang)[:, :, None], LANES, axis=2), F32)
    ts = jnp.asarray(np.repeat(np.sin(ang)[:, :, None], LANES, axis=2), F32)
    c3, s3 = _dft_mats(n2)
    w3 = jnp.asarray(np.block([[c3, s3], [-s3, c3]]), BF16)
    cd, sd = _channel_dft()
    kern = functools.partial(_ft_stage2_kernel, scale=1.0 / math.sqrt(n * FT_GROUP_DIM))
    g = FT_K1_PER_STEP
    n_steps = FT_N1 // g
    out = pl.pallas_call(
        kern,
        grid=(nb, n_steps),
        in_specs=[
            pl.BlockSpec((None, g, n2, FT_WIDTH), lambda b, k: (b, k, 0, 0)),
            pl.BlockSpec((None, g, n2, FT_WIDTH), lambda b, k: (b, n_steps + k, 0, 0)),
            pl.BlockSpec((g, n2, LANES), lambda b, k: (k, 0, 0)),
            pl.BlockSpec((g, n2, LANES), lambda b, k: (k, 0, 0)),
            pl.BlockSpec(w3.shape, lambda b, k: (0, 0)),
            pl.BlockSpec(cd.shape, lambda b, k: (0, 0)),
            pl.BlockSpec(sd.shape, lambda b, k: (0, 0)),
        ],
        out_specs=pl.BlockSpec((None, n2, g * FT_WIDTH), lambda b, k: (b, 0, k)),
        out_shape=jax.ShapeDtypeStruct((nb, n2, FT_N1 * FT_WIDTH), BF16),
        compiler_params=_cparams(2),
        name="fourier_stage2",
    )(a, a, tc, ts, w3, cd, sd)
    return out.reshape(nb, n, FT_WIDTH)


def _ft_ctx_kernel(u_ref, cn_ref, sn_ref, cd_ref, sd_ref, o_ref, *, scale):
    u = u_ref[...]
    y1 = _dot(u, cd_ref[...]).astype(BF16)
    y2 = _dot(u, sd_ref[...]).astype(BF16)
    f = _dot(cn_ref[...], y1) - _dot(sn_ref[...], y2)
    o_ref[...] = (f * scale).astype(BF16)


def _fourier_ctx(u, s):
    nb, p, _ = u.shape
    n = p - s
    cn, sn = _dft_mats(n)
    cn, sn = jnp.asarray(cn, BF16), jnp.asarray(sn, BF16)
    cd, sd = _channel_dft()
    kern = functools.partial(_ft_ctx_kernel, scale=1.0 / math.sqrt(n * FT_GROUP_DIM))
    sq = pl.BlockSpec((n, n), lambda b: (0, 0))
    ch = pl.BlockSpec(cd.shape, lambda b: (0, 0))
    return pl.pallas_call(
        kern,
        grid=(nb,),
        in_specs=[pl.BlockSpec((None, n, FT_WIDTH), lambda b: (b, s // n, 0)), sq, sq, ch, ch],
        out_specs=pl.BlockSpec((None, n, FT_WIDTH), lambda b: (b, 0, 0)),
        out_shape=jax.ShapeDtypeStruct((nb, n, FT_WIDTH), BF16),
        compiler_params=_cparams(1),
        name="fourier_ctx",
    )(u, cn, sn, cd, sd)


def _merge_kernel(x_ref, oa_ref, ob_ref, oc_ref, ca_ref, cb_ref, cc_ref, mod_ref, n1g_ref, n2g_ref, wgate_ref,
                  bgate_ref, wba_ref, wbb_ref, wbc_ref, wout_ref, x1_ref, h2_ref, *, n_lat_tiles):
    d = x_ref.shape[-1]
    x = x_ref[...]
    is_ctx = pl.program_id(1) >= n_lat_tiles
    o_a = jnp.where(is_ctx, ca_ref[...], oa_ref[...])
    o_b = jnp.where(is_ctx, cb_ref[...], ob_ref[...])
    o_c = jnp.where(is_ctx, cc_ref[...], oc_ref[...])
    mod = mod_ref[...]
    sh1, sc1, g1, sh2, sc2 = (mod[:, i * d:(i + 1) * d] for i in range(5))
    h = _rms(x, n1g_ref[...]) * (1.0 + sc1) + sh1
    gate = jax.nn.sigmoid(_dot(h.astype(BF16), wgate_ref[...]) + bgate_ref[...])
    m = gate[:, 0:d] * _dot(o_a, wba_ref[...])
    m += gate[:, d:2 * d] * _dot(o_b, wbb_ref[...])
    m += gate[:, 2 * d:3 * d] * _dot(o_c, wbc_ref[...])
    y = _dot(m.astype(BF16), wout_ref[...])
    x1 = x + g1 * y
    x1_ref[...] = x1
    h2_ref[...] = (_rms(x1, n2g_ref[...]) * (1.0 + sc2) + sh2).astype(BF16)


def _merge(x, lat_outs, ctx_outs, mod, mod_row, lw, layer):
    nb, n, d = x.shape
    tm = TOKEN_TILE
    n_lat_tiles = lat_outs[0].shape[1] // tm

    def full(a):
        return _layer_spec(a, layer)

    def tok(width):
        return pl.BlockSpec((None, tm, width), lambda b, i: (b, i, 0))

    def lat(a):
        return pl.BlockSpec((None, tm, a.shape[-1]), lambda b, i: (b, jnp.minimum(i, n_lat_tiles - 1), 0))

    def ctx(a):
        return pl.BlockSpec((None, tm, a.shape[-1]), lambda b, i: (b, jnp.maximum(i - n_lat_tiles, 0), 0))

    weights = [lw["norm1_g"], lw["norm2_g"], lw["w_gate"], lw["b_gate"], lw["w_br_mla"], lw["w_br_na"],
               lw["w_br_ft"], lw["w_out"]]
    return pl.pallas_call(
        functools.partial(_merge_kernel, n_lat_tiles=n_lat_tiles),
        grid=(nb, n // tm),
        in_specs=[tok(d)] + [lat(a) for a in lat_outs] + [ctx(a) for a in ctx_outs] + [_mod_spec(mod, layer, mod_row)]
        + [full(w) for w in weights],
        out_specs=[tok(d), tok(d)],
        out_shape=[jax.ShapeDtypeStruct((nb, n, d), F32), jax.ShapeDtypeStruct((nb, n, d), BF16)],
        compiler_params=_cparams(2),
        name="merge",
    )(x, *lat_outs, *ctx_outs, mod, *weights)


def _route(logits, router_bias):
    row = lax.broadcasted_iota(jnp.int32, logits.shape, 0)
    neg = -jnp.inf
    big = jnp.int32(N_EXPERTS)
    scores = jax.nn.sigmoid(logits)
    sel = scores + router_bias

    def top2(vals):
        m1 = jnp.max(vals, axis=0, keepdims=True)
        i1 = jnp.min(jnp.where(vals == m1, row, big), axis=0, keepdims=True)
        rest = jnp.where(row == i1, neg, vals)
        m2 = jnp.max(rest, axis=0, keepdims=True)
        i2 = jnp.min(jnp.where(rest == m2, row, big), axis=0, keepdims=True)
        return m1, i1, m2, i2

    grp_scores = []
    for g in range(N_GROUPS):
        in_grp = (row >= g * EXPERTS_PER_GROUP) & (row < (g + 1) * EXPERTS_PER_GROUP)
        m1, _, m2, _ = top2(jnp.where(in_grp, sel, neg))
        grp_scores.append(m1 + m2)
    best = functools.reduce(jnp.maximum, grp_scores)
    grp = jnp.full(best.shape, N_GROUPS - 1, jnp.int32)
    for g in range(N_GROUPS - 2, -1, -1):
        grp = jnp.where(grp_scores[g] == best, g, grp)
    in_sel = (row >= grp * EXPERTS_PER_GROUP) & (row < (grp + 1) * EXPERTS_PER_GROUP)
    _, i1, _, i2 = top2(jnp.where(in_sel, sel, neg))
    hit1 = row == i1
    hit2 = row == i2
    w1 = jnp.sum(jnp.where(hit1, scores, 0.0), axis=0, keepdims=True)
    w2 = jnp.sum(jnp.where(hit2, scores, 0.0), axis=0, keepdims=True)
    total = w1 + w2
    return jnp.where(hit1, w1 / total, 0.0) + jnp.where(hit2, w2 / total, 0.0), grp


def _split_bf16(v):
    hi = v.astype(BF16)
    return hi, (v - hi.astype(F32)).astype(BF16)


def _moe_kernel(h_ref, x_ref, mod_ref, cmod_ref, rw_ref, rb_ref, fg_ref, tri_ref, wg_ref, wu_ref, wd_ref, o_ref,
                col_scr, row_scr, count_scr, acc_scr, *, apply_final_norm, n_lat, cap):
    d = x_ref.shape[-1]
    tm = x_ref.shape[0]
    step = pl.program_id(2)
    n_steps = pl.num_programs(2)
    h = h_ref[...]

    @pl.when(step == 0)
    def _():
        gate_t, grp = _route(_dot_nt(rw_ref[...], h), rb_ref[...])
        member = jnp.where(lax.broadcasted_iota(jnp.int32, (SUBLANES, tm), 0) == grp, 1.0, 0.0)
        rank = _dot(member.astype(BF16), tri_ref[...])
        empty_rows = jnp.zeros((SUBLANES - EXPERTS_PER_GROUP - 2, tm), F32)
        per_group = [jnp.concatenate([gate_t[c * EXPERTS_PER_GROUP:(c + 1) * EXPERTS_PER_GROUP], rank[c:c + 1],
                                      member[c:c + 1], empty_rows], axis=0) for c in range(N_GROUPS)]
        pad_rows = jnp.zeros((LANES - N_GROUPS * SUBLANES, tm), F32)
        cols = jnp.concatenate(per_group + [pad_rows], axis=0).T
        for c in range(N_GROUPS):
            count_scr[c] = jnp.sum(member[c:c + 1]).astype(jnp.int32)
            row_scr[c] = per_group[c]
            col_scr[c] = pltpu.roll(cols, (LANES - c * SUBLANES) % LANES, axis=1) if c else cols
        acc_scr[...] = jnp.zeros(acc_scr.shape, F32)

    def experts(hx, gates):
        out = jnp.zeros((hx.shape[0], d), F32)
        for j in range(EXPERTS_PER_STEP):
            a = _dot(hx, wg_ref[j])
            u = _dot(hx, wu_ref[j])
            hid = (a * jax.nn.sigmoid(a)) * u * gates[:, j:j + 1]
            out += _dot(hid.astype(BF16), wd_ref[j])
        return out

    rank_lane, member_lane = EXPERTS_PER_GROUP, EXPERTS_PER_GROUP + 1

    def compact_round(first_rank):
        rank_row = row_scr[step, rank_lane:rank_lane + 1, :] - first_rank
        member_row = row_scr[step, member_lane:member_lane + 1, :]
        slot = lax.broadcasted_iota(jnp.int32, (cap, tm), 0).astype(F32)
        pick = jnp.where((slot == rank_row) & (member_row > 0.5), 1.0, 0.0).astype(BF16)
        hx = _dot(pick, h_ref[...]).astype(BF16)
        col = col_scr[step]
        g_hi, g_lo = _split_bf16(col)
        y = experts(hx, _dot(pick, g_hi) + _dot(pick, g_lo))
        slot_t = lax.broadcasted_iota(jnp.int32, (tm, cap), 1).astype(F32)
        rank_col = col[:, rank_lane:rank_lane + 1] - first_rank
        member_col = col[:, member_lane:member_lane + 1] > 0.5
        put = jnp.where((slot_t == rank_col) & member_col, 1.0, 0.0).astype(BF16)
        y_hi, y_lo = _split_bf16(y)
        acc_scr[...] += _dot(put, y_hi) + _dot(put, y_lo)

    compact_round(0.0)
    n_rounds = lax.div(count_scr[step] + (cap - 1), cap)

    def extra_round(r, carry):
        compact_round((r * cap).astype(F32))
        return carry

    lax.fori_loop(1, n_rounds, extra_round, 0)

    @pl.when(step == n_steps - 1)
    def _():
        tok_row = pl.program_id(1) * tm + lax.broadcasted_iota(jnp.int32, (tm, 1), 0)
        g2 = jnp.where(tok_row >= n_lat, cmod_ref[:, 5 * d:6 * d], mod_ref[:, 5 * d:6 * d])
        x2 = x_ref[...] + g2 * acc_scr[...]
        o_ref[...] = _rms(x2, fg_ref[...]) if apply_final_norm else x2


def _moe_tile(p):
    return max(t for t in range(TOKEN_TILE, MOE_MAX_TILE + 1, TOKEN_TILE) if p % t == 0)


def _moe_capacity(tm):
    return min(tm, -(-(5 * tm // (4 * N_GROUPS)) // LANES) * LANES)


def _moe(h2, x1, mod, ctx_row, lw, layer, router_w, router_b, final_g, n_lat, n_rows, tm, apply_final_norm):
    assert EXPERTS_PER_STEP == EXPERTS_PER_GROUP
    nb, _, d = x1.shape
    n_steps = N_EXPERTS // EXPERTS_PER_STEP
    ff = lw["w_e_gate"].shape[-1]
    tok = pl.BlockSpec((None, tm, d), lambda b, i, c: (b, i, 0))
    earlier = jnp.asarray(np.triu(np.ones((tm, tm), np.float32), k=1), BF16)
    kern = functools.partial(_moe_kernel, apply_final_norm=apply_final_norm, n_lat=n_lat, cap=_moe_capacity(tm))
    return pl.pallas_call(
        kern,
        grid=(nb, n_rows // tm, n_steps),
        in_specs=[
            tok, tok,
            _mod_spec(mod, layer, lambda b, i: b),
            _mod_spec(mod, layer, lambda b, i: ctx_row),
            pl.BlockSpec(router_w.shape, lambda b, i, c: (0, 0)),
            pl.BlockSpec(router_b.shape, lambda b, i, c: (0, 0)),
            pl.BlockSpec(final_g.shape, lambda b, i, c: (0, 0)),
            pl.BlockSpec(earlier.shape, lambda b, i, c: (0, 0)),
            pl.BlockSpec((None, EXPERTS_PER_STEP, d, ff), lambda b, i, c: (layer, c, 0, 0)),
            pl.BlockSpec((None, EXPERTS_PER_STEP, d, ff), lambda b, i, c: (layer, c, 0, 0)),
            pl.BlockSpec((None, EXPERTS_PER_STEP, ff, d), lambda b, i, c: (layer, c, 0, 0)),
        ],
        out_specs=tok,
        out_shape=jax.ShapeDtypeStruct((nb, n_rows, d), F32),
        scratch_shapes=[
            pltpu.VMEM((n_steps, tm, LANES), F32),
            pltpu.VMEM((n_steps, SUBLANES, tm), F32),
            pltpu.SMEM((n_steps,), jnp.int32),
            pltpu.VMEM((tm, d), F32),
        ],
        compiler_params=_cparams(3),
        name="moe",
    )(h2, x1, mod, mod, router_w, router_b, final_g, earlier, lw["w_e_gate"], lw["w_e_up"], lw["w_e_down"])


def _rotate_cols(w):
    half = w.shape[-1] // 2
    return jnp.concatenate([-w[..., half:], w[..., :half]], axis=-1)


def _prepare_weights(w_in, q_norm_g, w_q_up, kv_norm_g, w_kv_up, norm1_g, norm2_g, w_gate, b_gate, w_br_mla,
                     w_br_na, w_br_ft, w_out, w_e_gate, w_e_up, w_e_down):
    n_layers, d, _ = w_in.shape
    splits = np.cumsum([MLA_Q_LORA, MLA_KV_LORA, MLA_ROPE, NA_WIDTH, NA_WIDTH, NA_WIDTH])
    w_q, w_kv, w_kr, w_nq, w_nk, w_nv, w_ft = jnp.split(w_in, [int(s) for s in splits], axis=-1)

    def in_rope_slot(w):
        zeros_lo = jnp.zeros(w.shape[:-1] + (MLA_NOPE,), w.dtype)
        zeros_hi = jnp.zeros(w.shape[:-1] + (HEAD_PAD - MLA_NOPE - MLA_ROPE,), w.dtype)
        return jnp.concatenate([zeros_lo, w, zeros_hi], axis=-1)

    win = jnp.concatenate([w_q, w_kv, in_rope_slot(w_kr), in_rope_slot(_rotate_cols(w_kr)), w_nq, w_nk, w_nv, w_ft],
                          axis=-1).astype(BF16)
    wq = w_q_up.reshape(n_layers, MLA_Q_LORA, MLA_HEADS, MLA_NOPE + MLA_ROPE)
    wq_nope, wq_pe = wq[..., :MLA_NOPE], wq[..., MLA_NOPE:]
    pad = jnp.zeros(wq_pe.shape[:-1] + (HEAD_PAD - MLA_NOPE - MLA_ROPE,), wq.dtype)
    wq_main = jnp.concatenate([wq_nope, wq_pe, pad], axis=-1)
    wq_rot = jnp.concatenate([jnp.zeros_like(wq_nope), _rotate_cols(wq_pe), pad], axis=-1)
    wkv = w_kv_up.reshape(n_layers, MLA_KV_LORA, MLA_HEADS, MLA_NOPE + MLA_V)
    wk_nope, wv = wkv[..., :MLA_NOPE], wkv[..., MLA_NOPE:]
    wkv_k = jnp.concatenate([wk_nope, jnp.zeros(wk_nope.shape[:-1] + (HEAD_PAD - MLA_NOPE,), wkv.dtype)], axis=-1)
    wide = MLA_HEADS * HEAD_PAD
    return {
        "w_in": win,
        "wq_main": wq_main.reshape(n_layers, MLA_Q_LORA, wide).astype(BF16),
        "wq_rot": wq_rot.reshape(n_layers, MLA_Q_LORA, wide).astype(BF16),
        "wkv_k": wkv_k.reshape(n_layers, MLA_KV_LORA, wide).astype(BF16),
        "wkv_v": jnp.concatenate([wv, jnp.zeros(wv.shape[:-1] + (HEAD_PAD - MLA_V,), wv.dtype)], axis=-1)
        .reshape(n_layers, MLA_KV_LORA, wide).astype(BF16),
        "q_norm_g": q_norm_g.reshape(n_layers, 1, -1),
        "kv_norm_g": kv_norm_g.reshape(n_layers, 1, -1),
        "norm1_g": norm1_g.reshape(n_layers, 1, d),
        "norm2_g": norm2_g.reshape(n_layers, 1, d),
        "w_gate": w_gate.astype(BF16),
        "b_gate": b_gate.reshape(n_layers, 1, -1),
        "w_br_mla": w_br_mla.astype(BF16),
        "w_br_na": w_br_na.astype(BF16),
        "w_br_ft": w_br_ft.astype(BF16),
        "w_out": w_out.astype(BF16),
        "w_e_gate": w_e_gate.astype(BF16),
        "w_e_up": w_e_up.astype(BF16),
        "w_e_down": w_e_down.astype(BF16),
    }


def _rope_tables(s, c):
    n_freq = MLA_ROPE // 4
    inv_freq = ROPE_THETA ** (-jnp.arange(n_freq, dtype=F32) / n_freq)
    t = jnp.arange(s, dtype=jnp.int32)
    row = (t // GRID_W).astype(F32)
    col = (t % GRID_W).astype(F32)
    ang = jnp.concatenate([row[:, None] * inv_freq, col[:, None] * inv_freq], axis=-1)
    cos, sin = jnp.cos(ang), jnp.sin(ang)
    pad = HEAD_PAD - MLA_NOPE - MLA_ROPE
    cos_lat = jnp.concatenate([jnp.ones((s, MLA_NOPE), F32), cos, cos, jnp.zeros((s, pad), F32)], axis=-1)
    sin_lat = jnp.concatenate([jnp.zeros((s, MLA_NOPE), F32), sin, sin, jnp.zeros((s, pad), F32)], axis=-1)
    cos_ctx = jnp.concatenate([jnp.ones((c, MLA_NOPE + MLA_ROPE), F32), jnp.zeros((c, pad), F32)], axis=-1)
    sin_ctx = jnp.zeros((c, HEAD_PAD), F32)
    return cos_lat, sin_lat, cos_ctx, sin_ctx


@jax.jit
def _forward(x, c, ctx, c_ctx, w_ada, b_ada, norm1_g, norm2_g, w_in, q_norm_g, w_q_up, kv_norm_g, w_kv_up,
             na_rpb, w_gate, b_gate, w_br_mla, w_br_na, w_br_ft, w_out, router_w, router_bias, w_e_gate,
             w_e_up, w_e_down, final_norm_g):
    nb, s, d = x.shape
    n_ctx = ctx.shape[1]
    n_layers = w_ada.shape[0]
    grid_rows = s // GRID_W

    mod_rows = 8
    c_rows = jnp.zeros((mod_rows, d), F32).at[:nb].set(c).at[nb].set(c_ctx)
    mod_all = _modulation(c_rows, w_ada, b_ada).reshape(n_layers, mod_rows, 1, 6 * d)

    weights = _prepare_weights(w_in, q_norm_g, w_q_up, kv_norm_g, w_kv_up, norm1_g, norm2_g, w_gate, b_gate,
                               w_br_mla, w_br_na, w_br_ft, w_out, w_e_gate, w_e_up, w_e_down)
    na_bias = _na_bias_tables(na_rpb)
    na_row_mask = _na_row_masks(grid_rows)
    cos_lat, sin_lat, cos_ctx, sin_ctx = _rope_tables(s, n_ctx)
    cos_t = jnp.concatenate([cos_lat, cos_ctx], axis=0)
    sin_t = jnp.concatenate([sin_lat, sin_ctx], axis=0)
    router_w_p = router_w.T.astype(BF16)
    router_b_p = router_bias.reshape(N_EXPERTS, 1).astype(F32)
    final_g = final_norm_g.reshape(1, d)

    p = s + n_ctx
    n_lat_tiles = s // TOKEN_TILE
    ctx_row = nb

    def tile_row(b, i):
        return jnp.where(i >= n_lat_tiles, ctx_row, b)

    x_all = jnp.concatenate([x, ctx], axis=1)
    zero_ctx = [jnp.zeros((nb, n_ctx, w), BF16) for w in (MLA_HEADS * MLA_V, NA_WIDTH, FT_WIDTH)]
    for l in range(n_layers):
        last = l == n_layers - 1
        q, k, v, nq, nk, nv, ft = _project(x_all, mod_all, tile_row, weights, l, cos_t, sin_t)
        lat_outs = [_mla_attention(q, k, v, s), _na_attention(nq, nk, nv, na_bias, na_row_mask, l, s),
                    _fourier_lat(ft, s)]
        if last:
            ctx_outs = zero_ctx
        else:
            ctx_outs = [_mla_ctx_attention(q, k, v, s), _na_ctx_attention(nq, nk, nv, s), _fourier_ctx(ft, s)]
        x1, h2 = _merge(x_all, lat_outs, ctx_outs, mod_all, tile_row, weights, l)
        if last:
            return _moe(h2, x1, mod_all, ctx_row, weights, l, router_w_p, router_b_p, final_g, s, s,
                        _moe_tile(s), True)
        x_all = _moe(h2, x1, mod_all, ctx_row, weights, l, router_w_p, router_b_p, final_g, s, p,
                     _moe_tile(p), False)


def kernel(x, c, ctx, c_ctx, w_ada, b_ada, norm1_g, norm2_g, w_in, q_norm_g, w_q_up, kv_norm_g, w_kv_up, na_rpb, w_gate, b_gate, w_br_mla, w_br_na, w_br_ft, w_out, router_w, router_bias, w_e_gate, w_e_up, w_e_down, final_norm_g):
    return _forward(x, c, ctx, c_ctx, w_ada, b_ada, norm1_g, norm2_g, w_in, q_norm_g, w_q_up, kv_norm_g,
                    w_kv_up, na_rpb, w_gate, b_gate, w_br_mla, w_br_na, w_br_ft, w_out, router_w, router_bias,
                    w_e_gate, w_e_up, w_e_down, final_norm_g)
```

```python
import functools
import math

import jax
import jax.numpy as jnp
import numpy as np
from jax import lax
from jax.experimental import pallas as pl
from jax.experimental.pallas import tpu as pltpu

F32 = jnp.float32
BF16 = jnp.bfloat16

GRID_W = 64
MLA_HEADS = 8
MLA_NOPE = 64
MLA_ROPE = 32
MLA_V = 64
MLA_Q_LORA = 256
MLA_KV_LORA = 128
MLA_SCALE = (MLA_NOPE + MLA_ROPE) ** -0.5
NA_HEADS = 4
NA_HEAD_DIM = 64
NA_WIN_R = 8
NA_WIN_C = 16
NA_SCALE = NA_HEAD_DIM ** -0.5
NA_WIDTH = NA_HEADS * NA_HEAD_DIM
FT_GROUPS = 4
FT_GROUP_DIM = 64
FT_WIDTH = FT_GROUPS * FT_GROUP_DIM
N_EXPERTS = 16
N_GROUPS = 4
EXPERTS_PER_GROUP = N_EXPERTS // N_GROUPS
ROPE_THETA = 10000.0
NORM_EPS = 1e-6
MASK_VALUE = -1e30
LOG2E = math.log2(math.e)

LANES = 128
SUBLANES = 8
V7X_VMEM_LIMIT_BYTES = 56 * 1024 * 1024

HEAD_PAD = LANES
NA_ROWS_PER_BLOCK = 8
NA_KEY_ROWS = 16
NA_BIAS_SHIFT = (NA_KEY_ROWS - NA_ROWS_PER_BLOCK) + (NA_ROWS_PER_BLOCK - 1) - (NA_WIN_R - 1)
NA_BIAS_TILES = (NA_WIN_R - 1) + NA_BIAS_SHIFT + (NA_KEY_ROWS - 2) + 1
FT_K1_PER_STEP = 8
EXPERTS_PER_STEP = 4
MOE_MAX_TILE = 768
TOKEN_TILE = 256
MLA_MAX_KV_CHUNK = 2816


def _cparams(n_axes):
    return pltpu.CompilerParams(
        dimension_semantics=("arbitrary",) * n_axes,
        vmem_limit_bytes=V7X_VMEM_LIMIT_BYTES,
    )


def _rms(x, g):
    return x * lax.rsqrt(jnp.mean(x * x, axis=-1, keepdims=True) + NORM_EPS) * g


def _dot(a, b):
    return jnp.dot(a, b, preferred_element_type=F32)


def _dot_nt(a, b):
    return lax.dot_general(a, b, (((1,), (1,)), ((), ())), preferred_element_type=F32)


def _mod_kernel(c_ref, w_ref, b_ref, o_ref):
    c = c_ref[...]
    o_ref[...] = _dot(c * jax.nn.sigmoid(c), w_ref[...]) + b_ref[...]


def _modulation(c_rows, w_ada, b_ada):
    n_layers, d, width = w_ada.shape
    rows = c_rows.shape[0]
    tn = 1536
    return pl.pallas_call(
        _mod_kernel,
        grid=(n_layers, width // tn),
        in_specs=[
            pl.BlockSpec((rows, d), lambda l, j: (0, 0)),
            pl.BlockSpec((None, d, tn), lambda l, j: (l, 0, j)),
            pl.BlockSpec((None, 1, tn), lambda l, j: (l, 0, j)),
        ],
        out_specs=pl.BlockSpec((None, rows, tn), lambda l, j: (l, 0, j)),
        out_shape=jax.ShapeDtypeStruct((n_layers, rows, width), F32),
        compiler_params=_cparams(2),
        name="modulation",
    )(c_rows, w_ada, b_ada.reshape(n_layers, 1, width))


def _proj_kernel(x_ref, mod_ref, n1g_ref, win_ref, qg_ref, wqm_ref, wqr_ref, kvg_ref, wkk_ref, wkv_ref,
                 cos_ref, sin_ref, q_ref, k_ref, v_ref, nq_ref, nk_ref, nv_ref, ft_ref):
    d = x_ref.shape[-1]
    x = x_ref[...]
    mod = mod_ref[...]
    sh1, sc1 = mod[:, 0:d], mod[:, d:2 * d]
    h = _rms(x, n1g_ref[...]) * (1.0 + sc1) + sh1
    p = _dot(h.astype(BF16), win_ref[...])
    cos = cos_ref[...]
    sin = sin_ref[...]
    o = 0
    q_lat = p[:, o:o + MLA_Q_LORA]
    o += MLA_Q_LORA
    kv_lat = p[:, o:o + MLA_KV_LORA]
    o += MLA_KV_LORA
    kr_a = p[:, o:o + HEAD_PAD]
    o += HEAD_PAD
    kr_b = p[:, o:o + HEAD_PAD]
    o += HEAD_PAD
    nq_ref[...] = (p[:, o:o + NA_WIDTH] * NA_SCALE).astype(BF16)
    o += NA_WIDTH
    nk_ref[...] = p[:, o:o + NA_WIDTH].astype(BF16)
    o += NA_WIDTH
    nv_ref[...] = p[:, o:o + NA_WIDTH].astype(BF16)
    o += NA_WIDTH
    ft_ref[...] = p[:, o:o + FT_WIDTH].astype(BF16)

    qn = _rms(q_lat, qg_ref[...]).astype(BF16)
    qm = _dot(qn, wqm_ref[...])
    qr = _dot(qn, wqr_ref[...])
    kvn = _rms(kv_lat, kvg_ref[...]).astype(BF16)
    kn = _dot(kvn, wkk_ref[...])
    v = _dot(kvn, wkv_ref[...])
    v_lane = lax.broadcasted_iota(jnp.int32, v.shape, 1) & (HEAD_PAD - 1)
    v_ref[...] = jnp.where(v_lane == MLA_V, 1.0, v).astype(BF16)
    kr = kr_a * cos + kr_b * sin
    q_scale = MLA_SCALE * LOG2E
    for hd in range(MLA_HEADS):
        sl = slice(hd * HEAD_PAD, (hd + 1) * HEAD_PAD)
        q_ref[:, sl] = ((qm[:, sl] * cos + qr[:, sl] * sin) * q_scale).astype(BF16)
        k_ref[:, sl] = (kn[:, sl] + kr).astype(BF16)


def _layer_spec(a, layer):
    return pl.BlockSpec((None,) + a.shape[1:], lambda *_: (layer,) + (0,) * (a.ndim - 1))


def _mod_spec(mod, layer, mod_row):
    return pl.BlockSpec((None, None, 1, mod.shape[-1]), lambda b, i, *_: (layer, mod_row(b, i), 0, 0))


def _project(x, mod, mod_row, lw, layer, cos_t, sin_t):
    nb, n, d = x.shape
    tm = TOKEN_TILE
    win = lw["w_in"]
    wide = MLA_HEADS * HEAD_PAD

    def full(a):
        return _layer_spec(a, layer)

    def tok(width):
        return pl.BlockSpec((None, tm, width), lambda b, i: (b, i, 0))

    outs = [wide, wide, wide, NA_WIDTH, NA_WIDTH, NA_WIDTH, FT_WIDTH]
    return pl.pallas_call(
        _proj_kernel,
        grid=(nb, n // tm),
        in_specs=[
            tok(d),
            _mod_spec(mod, layer, mod_row),
            full(lw["norm1_g"]), full(win), full(lw["q_norm_g"]), full(lw["wq_main"]), full(lw["wq_rot"]),
            full(lw["kv_norm_g"]), full(lw["wkv_k"]), full(lw["wkv_v"]),
            pl.BlockSpec((tm, HEAD_PAD), lambda b, i: (i, 0)),
            pl.BlockSpec((tm, HEAD_PAD), lambda b, i: (i, 0)),
        ],
        out_specs=[tok(w) for w in outs],
        out_shape=[jax.ShapeDtypeStruct((nb, n, w), BF16) for w in outs],
        compiler_params=_cparams(2),
        name="project",
    )(x, mod, lw["norm1_g"], win, lw["q_norm_g"], lw["wq_main"], lw["wq_rot"], lw["kv_norm_g"],
      lw["wkv_k"], lw["wkv_v"], cos_t, sin_t)


def _kv_chunk(p):
    best = LANES
    for n in range(1, p // LANES + 1):
        tk = p // n
        if p % n == 0 and tk % LANES == 0 and tk <= MLA_MAX_KV_CHUNK:
            best = max(best, tk)
    return best


def _store_head_values(o_ref, rows, hd, out):
    if hd:
        out = pltpu.roll(out, MLA_V, axis=1)
    sl = slice(hd * MLA_V, (hd + 1) * MLA_V)
    o_ref[rows, sl] = out[:, sl].astype(BF16)


def _mla_kernel(q_ref, k_ref, v_ref, o_ref, s0_scr, s1_scr, p0_scr, p1_scr, mx_scr, m_scr, alpha_scr, acc_scr,
                *, tq, tk, n_q, n_c):
    def head(hd):
        return slice(hd * HEAD_PAD, (hd + 1) * HEAD_PAD)

    def q_rows(qt):
        return pl.ds(pl.multiple_of(qt * tq, tq), tq)

    def k_rows(c):
        return pl.ds(pl.multiple_of(c * tk, tk), tk)

    def scores(hd, qt, c, s_scr):
        s = _dot_nt(q_ref[q_rows(qt), head(hd)], k_ref[k_rows(c), head(hd)])
        s_scr[...] = s
        mx_scr[hd] = jnp.broadcast_to(jnp.max(s, axis=1, keepdims=True), mx_scr.shape[1:])

    def softmax(hd, s_scr, p_scr, first):
        m_prev = jnp.where(first, MASK_VALUE, m_scr[hd])
        m_new = jnp.maximum(m_prev, mx_scr[hd])
        alpha_scr[hd] = jnp.exp2(m_prev - m_new)
        m_scr[hd] = m_new
        p_scr[...] = jnp.exp2(s_scr[...] - jnp.tile(m_new, (1, tk // LANES))).astype(BF16)

    def accumulate(hd, p_scr, qt, c):
        acc = alpha_scr[hd] * acc_scr[hd] + _dot(p_scr[...], v_ref[k_rows(c), head(hd)])
        acc_scr[hd] = acc
        _store_head_values(o_ref, q_rows(qt), hd, acc / acc[:, MLA_V:MLA_V + 1])

    lane = lax.broadcasted_iota(jnp.int32, acc_scr.shape, 2)
    acc_scr[...] = jnp.where(lane == MLA_V, 1.0, 0.0)
    m_scr[...] = jnp.full(m_scr.shape, MASK_VALUE, F32)
    alpha_scr[...] = jnp.ones(alpha_scr.shape, F32)
    p1_scr[...] = jnp.zeros(p1_scr.shape, BF16)
    scores(0, 0, 0, s0_scr)

    def body(i, carry):
        qt, c = carry
        first = c == 0
        last = c == n_c - 1
        c_prev = jnp.where(first, n_c - 1, c - 1)
        qt_prev = jnp.maximum(jnp.where(first, qt - 1, qt), 0)
        c_next = jnp.where(last, 0, c + 1)
        qt_next = jnp.where(last, qt + 1, qt)
        scores(1, qt, c, s1_scr)
        softmax(0, s0_scr, p0_scr, first)
        accumulate(1, p1_scr, qt_prev, c_prev)
        scores(0, jnp.minimum(qt_next, n_q - 1), c_next, s0_scr)
        softmax(1, s1_scr, p1_scr, first)
        accumulate(0, p0_scr, qt, c)
        return qt_next, c_next

    lax.fori_loop(0, n_q * n_c, body, (jnp.int32(0), jnp.int32(0)))
    accumulate(1, p1_scr, n_q - 1, n_c - 1)


def _mla_attention(q, k_all, v_all, s):
    nb, p, _ = k_all.shape
    tq = min(512, s)
    tk = _kv_chunk(p)
    pairs = MLA_HEADS // 2
    kern = functools.partial(_mla_kernel, tq=tq, tk=tk, n_q=s // tq, n_c=p // tk)
    return pl.pallas_call(
        kern,
        grid=(nb, pairs),
        in_specs=[
            pl.BlockSpec((None, s, 2 * HEAD_PAD), lambda b, hp: (b, 0, hp)),
            pl.BlockSpec((None, p, 2 * HEAD_PAD), lambda b, hp: (b, 0, hp)),
            pl.BlockSpec((None, p, 2 * HEAD_PAD), lambda b, hp: (b, 0, hp)),
        ],
        out_specs=pl.BlockSpec((None, s, 2 * MLA_V), lambda b, hp: (b, 0, hp)),
        out_shape=jax.ShapeDtypeStruct((nb, s, MLA_HEADS * MLA_V), BF16),
        scratch_shapes=[
            pltpu.VMEM((tq, tk), F32),
            pltpu.VMEM((tq, tk), F32),
            pltpu.VMEM((tq, tk), BF16),
            pltpu.VMEM((tq, tk), BF16),
            pltpu.VMEM((2, tq, LANES), F32),
            pltpu.VMEM((2, tq, LANES), F32),
            pltpu.VMEM((2, tq, LANES), F32),
            pltpu.VMEM((2, tq, HEAD_PAD), F32),
        ],
        compiler_params=_cparams(2),
        name="mla_attention",
    )(q, k_all, v_all)


def _mla_ctx_kernel(q_ref, k_ref, v_ref, o_ref):
    for hd in range(2):
        sl = slice(hd * HEAD_PAD, (hd + 1) * HEAD_PAD)
        s = _dot_nt(q_ref[:, sl], k_ref[:, sl])
        p = jnp.exp2(s - jnp.max(s, axis=1, keepdims=True))
        acc = _dot(p.astype(BF16), v_ref[:, sl])
        _store_head_values(o_ref, slice(None), hd, acc / acc[:, MLA_V:MLA_V + 1])


def _mla_ctx_attention(q, k, v, s):
    nb, p, _ = q.shape
    c = p - s
    pairs = MLA_HEADS // 2
    spec = pl.BlockSpec((None, c, 2 * HEAD_PAD), lambda b, hp: (b, s // c, hp))
    return pl.pallas_call(
        _mla_ctx_kernel,
        grid=(nb, pairs),
        in_specs=[spec, spec, spec],
        out_specs=pl.BlockSpec((None, c, 2 * MLA_V), lambda b, hp: (b, 0, hp)),
        out_shape=jax.ShapeDtypeStruct((nb, c, MLA_HEADS * MLA_V), BF16),
        compiler_params=_cparams(2),
        name="mla_ctx_attention",
    )(q, k, v)


def _na_kernel(q_ref, k_ref, v_ref, kc_ref, vc_ref, bias_ref, rowmask_ref, o_ref, comb_scr, s0_scr, s1_scr, p0_scr,
               p1_scr, mx_scr, l_scr, *, grid_rows, n_blocks):
    tq = NA_ROWS_PER_BLOCK * GRID_W
    n_keys = NA_KEY_ROWS * GRID_W

    for hd in range(2):
        for var, row_off in enumerate((0, -(NA_WIN_R // 2), -(NA_KEY_ROWS - NA_ROWS_PER_BLOCK))):
            tile0 = row_off + (NA_WIN_R - 1) + NA_BIAS_SHIFT
            bias = jnp.concatenate(
                [jnp.concatenate([bias_ref[hd, tile0 + 2 * j - qr] for j in range(NA_KEY_ROWS // 2)], axis=1)
                 for qr in range(NA_ROWS_PER_BLOCK)], axis=0)
            comb_scr[hd, var] = bias + rowmask_ref[var].astype(F32)

    lane = lax.broadcasted_iota(jnp.int32, (tq, LANES), 1)

    def q_rows(rb):
        return pl.ds(pl.multiple_of(rb * tq, tq), tq)

    def key_rows(rb):
        start_row = jnp.clip(rb * NA_ROWS_PER_BLOCK - NA_WIN_R // 2, 0, grid_rows - NA_KEY_ROWS)
        return pl.ds(pl.multiple_of(start_row * GRID_W, (NA_WIN_R // 2) * GRID_W), n_keys)

    def scores(hd, rb, s_scr):
        var = jnp.where(rb == 0, 0, jnp.where(rb == n_blocks - 1, 2, 1))
        q = q_ref[q_rows(rb), :]
        in_head = (lane >= NA_HEAD_DIM) if hd else (lane < NA_HEAD_DIM)
        qh = jnp.where(in_head, q, jnp.zeros_like(q))
        s = _dot_nt(qh, k_ref[key_rows(rb), :]) + comb_scr[hd, var]
        sc = _dot_nt(qh, kc_ref[...])
        s_scr[:, :n_keys] = s
        s_scr[:, n_keys:] = sc
        m = jnp.maximum(jnp.max(s, axis=1, keepdims=True), jnp.max(sc, axis=1, keepdims=True))
        mx_scr[hd] = jnp.broadcast_to(m, mx_scr.shape[1:])

    def softmax(hd, s_scr, p_scr):
        p = jnp.exp(s_scr[...] - jnp.tile(mx_scr[hd], (1, s_scr.shape[1] // LANES)))
        l_scr[hd] = jnp.broadcast_to(jnp.sum(p, axis=1, keepdims=True), l_scr.shape[1:])
        p_scr[...] = p.astype(BF16)

    def accumulate(hd, p_scr, rb):
        acc = _dot(p_scr[:, :n_keys], v_ref[key_rows(rb), :]) + _dot(p_scr[:, n_keys:], vc_ref[...])
        out = (acc / l_scr[hd]).astype(BF16)
        sl = slice(hd * NA_HEAD_DIM, (hd + 1) * NA_HEAD_DIM)
        o_ref[q_rows(rb), sl] = out[:, sl]

    p1_scr[...] = jnp.zeros(p1_scr.shape, BF16)
    l_scr[...] = jnp.ones(l_scr.shape, F32)
    scores(0, 0, s0_scr)

    def body(rb, carry):
        scores(1, rb, s1_scr)
        softmax(0, s0_scr, p0_scr)
        accumulate(1, p1_scr, jnp.maximum(rb - 1, 0))
        scores(0, jnp.minimum(rb + 1, n_blocks - 1), s0_scr)
        softmax(1, s1_scr, p1_scr)
        accumulate(0, p0_scr, rb)
        return carry

    lax.fori_loop(0, n_blocks, body, 0)
    accumulate(1, p1_scr, n_blocks - 1)


def _na_attention(q, k, v, bias, row_mask, layer, s):
    nb, p, _ = q.shape
    c = p - s
    grid_rows = s // GRID_W
    tq = NA_ROWS_PER_BLOCK * GRID_W
    n_blocks = grid_rows // NA_ROWS_PER_BLOCK
    n_keys = NA_KEY_ROWS * GRID_W
    pairs = NA_HEADS // 2
    kern = functools.partial(_na_kernel, grid_rows=grid_rows, n_blocks=n_blocks)
    lat = pl.BlockSpec((None, s, LANES), lambda b, hp: (b, 0, hp))
    ctx = pl.BlockSpec((None, c, LANES), lambda b, hp: (b, s // c, hp))
    return pl.pallas_call(
        kern,
        grid=(nb, pairs),
        in_specs=[
            lat, lat, lat, ctx, ctx,
            pl.BlockSpec((None, 2) + bias.shape[2:], lambda b, hp: (layer, hp, 0, 0, 0)),
            pl.BlockSpec(row_mask.shape, lambda b, hp: (0, 0, 0)),
        ],
        out_specs=pl.BlockSpec((None, s, LANES), lambda b, hp: (b, 0, hp)),
        out_shape=jax.ShapeDtypeStruct((nb, s, NA_WIDTH), BF16),
        scratch_shapes=[
            pltpu.VMEM((2, 3, tq, n_keys), F32),
            pltpu.VMEM((tq, n_keys + c), F32),
            pltpu.VMEM((tq, n_keys + c), F32),
            pltpu.VMEM((tq, n_keys + c), BF16),
            pltpu.VMEM((tq, n_keys + c), BF16),
            pltpu.VMEM((2, tq, LANES), F32),
            pltpu.VMEM((2, tq, LANES), F32),
        ],
        compiler_params=_cparams(2),
        name="na_attention",
    )(q, k, v, k, v, bias, row_mask)


def _na_ctx_kernel(q_ref, k_ref, v_ref, o_ref):
    q = q_ref[...]
    k = k_ref[...]
    v = v_ref[...]
    lane = lax.broadcasted_iota(jnp.int32, q.shape, 1)
    outs = []
    for hd in range(2):
        in_head = (lane >= NA_HEAD_DIM) if hd else (lane < NA_HEAD_DIM)
        qh = jnp.where(in_head, q, jnp.zeros_like(q))
        s = _dot_nt(qh, k)
        p = jnp.exp(s - jnp.max(s, axis=1, keepdims=True))
        outs.append(_dot(p.astype(BF16), v) / jnp.sum(p, axis=1, keepdims=True))
    o_ref[...] = jnp.where(lane < NA_HEAD_DIM, outs[0], outs[1]).astype(BF16)


def _na_ctx_attention(q, k, v, s):
    nb, p, _ = q.shape
    c = p - s
    spec = pl.BlockSpec((None, c, LANES), lambda b, hp: (b, s // c, hp))
    return pl.pallas_call(
        _na_ctx_kernel,
        grid=(nb, NA_HEADS // 2),
        in_specs=[spec, spec, spec],
        out_specs=pl.BlockSpec((None, c, LANES), lambda b, hp: (b, 0, hp)),
        out_shape=jax.ShapeDtypeStruct((nb, c, NA_WIDTH), BF16),
        compiler_params=_cparams(2),
        name="na_ctx_attention",
    )(q, k, v)


def _na_row_masks(grid_rows):
    qr = np.arange(NA_ROWS_PER_BLOCK)[:, None]
    kr = np.arange(NA_KEY_ROWS)[None, :]
    row_valid = np.zeros((3, NA_ROWS_PER_BLOCK, NA_KEY_ROWS), bool)
    n_blocks = grid_rows // NA_ROWS_PER_BLOCK
    for vi, rb in enumerate((0, 1, n_blocks - 1)):
        r = rb * NA_ROWS_PER_BLOCK + qr
        start_row = min(max(rb * NA_ROWS_PER_BLOCK - NA_WIN_R // 2, 0), grid_rows - NA_KEY_ROWS)
        key_row = start_row + kr
        win = np.clip(r - NA_WIN_R // 2, 0, grid_rows - NA_WIN_R)
        row_valid[vi] = (key_row >= win) & (key_row < win + NA_WIN_R)
    mask = np.where(row_valid, 0.0, MASK_VALUE).astype(np.float32)
    mask = np.broadcast_to(mask[:, :, None, :, None], (3, NA_ROWS_PER_BLOCK, GRID_W, NA_KEY_ROWS, GRID_W))
    return jnp.asarray(mask.reshape(3, NA_ROWS_PER_BLOCK * GRID_W, NA_KEY_ROWS * GRID_W), BF16)


def _na_bias_tables(rpb):
    n_dc = 2 * NA_WIN_C - 1
    qc = np.arange(GRID_W)[:, None]
    kcol = np.arange(GRID_W)[None, :]
    wstart = np.clip(qc - NA_WIN_C // 2, 0, GRID_W - NA_WIN_C)
    col_valid = (kcol >= wstart) & (kcol < wstart + NA_WIN_C)
    dc = np.clip(kcol - qc, -(NA_WIN_C - 1), NA_WIN_C - 1) + (NA_WIN_C - 1)
    col_onehot = ((dc[..., None] == np.arange(n_dc)) & col_valid[..., None]).astype(np.float32)
    t = jnp.einsum("lhij,cdj->lhicd", rpb.astype(F32), jnp.asarray(col_onehot), precision=lax.Precision.HIGHEST)
    t = jnp.where(jnp.asarray(col_valid), t, MASK_VALUE)
    n_dr = t.shape[2]
    t = jnp.pad(t, ((0, 0), (0, 0), (NA_BIAS_SHIFT, NA_BIAS_TILES + 1 - NA_BIAS_SHIFT - n_dr), (0, 0), (0, 0)),
                constant_values=MASK_VALUE)
    return jnp.concatenate([t[:, :, :NA_BIAS_TILES], t[:, :, 1:NA_BIAS_TILES + 1]], axis=-1)


FT_N1 = 64


def _ft_stage1_kernel(w_ref, x_ref, o_ref):
    o_ref[...] = _dot(w_ref[...], x_ref[...])


def _ft_stage2_kernel(are_ref, aim_ref, tc_ref, ts_ref, w_ref, cd_ref, sd_ref, o_ref, *, scale):
    n2 = are_ref.shape[1]
    for g in range(are_ref.shape[0]):
        are = are_ref[g]
        aim = aim_ref[g]
        tc = jnp.concatenate([tc_ref[g]] * (FT_WIDTH // LANES), axis=1)
        ts = jnp.concatenate([ts_ref[g]] * (FT_WIDTH // LANES), axis=1)
        bre = are * tc + aim * ts
        bim = aim * tc - are * ts
        bb = jnp.concatenate([bre, bim], axis=0).astype(BF16)
        xk = _dot(w_ref[...], bb)
        f = _dot(xk[:n2].astype(BF16), cd_ref[...]) + _dot(xk[n2:].astype(BF16), sd_ref[...])
        o_ref[:, g * FT_WIDTH:(g + 1) * FT_WIDTH] = (f * scale).astype(BF16)


def _dft_mats(n):
    ang = 2.0 * np.pi * np.outer(np.arange(n), np.arange(n)) / n
    return np.cos(ang), np.sin(ang)


def _channel_dft():
    c, s = _dft_mats(FT_GROUP_DIM)
    eye = np.eye(FT_GROUPS)
    return jnp.asarray(np.kron(eye, c), BF16), jnp.asarray(np.kron(eye, s), BF16)


def _fourier_lat(u, n):
    nb, p, _ = u.shape
    n2 = n // FT_N1
    cols = n2 * FT_WIDTH
    c1, s1 = _dft_mats(FT_N1)
    w1 = jnp.asarray(np.concatenate([c1, -s1], axis=0), BF16)
    tcol = min(2048, cols)
    a = pl.pallas_call(
        _ft_stage1_kernel,
        grid=(nb, cols // tcol),
        in_specs=[
            pl.BlockSpec(w1.shape, lambda b, j: (0, 0)),
            pl.BlockSpec((None, FT_N1, tcol), lambda b, j: (b, 0, j)),
        ],
        out_specs=pl.BlockSpec((None, 2 * FT_N1, tcol), lambda b, j: (b, 0, j)),
        out_shape=jax.ShapeDtypeStruct((nb, 2 * FT_N1, cols), F32),
        compiler_params=_cparams(2),
        name="fourier_stage1",
    )(w1, u.reshape(nb, p // n2, cols))
    a = a.reshape(nb, 2 * FT_N1, n2, FT_WIDTH)

    ang = 2.0 * np.pi * np.outer(np.arange(FT_N1), np.arange(n2)) / n
    tc = jnp.asarray(np.repeat(np.cos(ang)[:, :, None], LANES, axis=2), F32)
    ts = jnp.asarray(np.repeat(np.sin(ang)[:, :, None], LANES, axis=2), F32)
    c3, s3 = _dft_mats(n2)
    w3 = jnp.asarray(np.block([[c3, s3], [-s3, c3]]), BF16)
    cd, sd = _channel_dft()
    kern = functools.partial(_ft_stage2_kernel, scale=1.0 / math.sqrt(n * FT_GROUP_DIM))
    g = FT_K1_PER_STEP
    n_steps = FT_N1 // g
    out = pl.pallas_call(
        kern,
        grid=(nb, n_steps),
        in_specs=[
            pl.BlockSpec((None, g, n2, FT_WIDTH), lambda b, k: (b, k, 0, 0)),
            pl.BlockSpec((None, g, n2, FT_WIDTH), lambda b, k: (b, n_steps + k, 0, 0)),
            pl.BlockSpec((g, n2, LANES), lambda b, k: (k, 0, 0)),
            pl.BlockSpec((g, n2, LANES), lambda b, k: (k, 0, 0)),
            pl.BlockSpec(w3.shape, lambda b, k: (0, 0)),
            pl.BlockSpec(cd.shape, lambda b, k: (0, 0)),
            pl.BlockSpec(sd.shape, lambda b, k: (0, 0)),
        ],
        out_specs=pl.BlockSpec((None, n2, g * FT_WIDTH), lambda b, k: (b, 0, k)),
        out_shape=jax.ShapeDtypeStruct((nb, n2, FT_N1 * FT_WIDTH), BF16),
        compiler_params=_cparams(2),
        name="fourier_stage2",
    )(a, a, tc, ts, w3, cd, sd)
    return out.reshape(nb, n, FT_WIDTH)


def _ft_ctx_kernel(u_ref, cn_ref, sn_ref, cd_ref, sd_ref, o_ref, *, scale):
    u = u_ref[...]
    y1 = _dot(u, cd_ref[...]).astype(BF16)
    y2 = _dot(u, sd_ref[...]).astype(BF16)
    f = _dot(cn_ref[...], y1) - _dot(sn_ref[...], y2)
    o_ref[...] = (f * scale).astype(BF16)


def _fourier_ctx(u, s):
    nb, p, _ = u.shape
    n = p - s
    cn, sn = _dft_mats(n)
    cn, sn = jnp.asarray(cn, BF16), jnp.asarray(sn, BF16)
    cd, sd = _channel_dft()
    kern = functools.partial(_ft_ctx_kernel, scale=1.0 / math.sqrt(n * FT_GROUP_DIM))
    sq = pl.BlockSpec((n, n), lambda b: (0, 0))
    ch = pl.BlockSpec(cd.shape, lambda b: (0, 0))
    return pl.pallas_call(
        kern,
        grid=(nb,),
        in_specs=[pl.BlockSpec((None, n, FT_WIDTH), lambda b: (b, s // n, 0)), sq, sq, ch, ch],
        out_specs=pl.BlockSpec((None, n, FT_WIDTH), lambda b: (b, 0, 0)),
        out_shape=jax.ShapeDtypeStruct((nb, n, FT_WIDTH), BF16),
        compiler_params=_cparams(1),
        name="fourier_ctx",
    )(u, cn, sn, cd, sd)


def _merge_kernel(x_ref, oa_ref, ob_ref, oc_ref, ca_ref, cb_ref, cc_ref, mod_ref, n1g_ref, n2g_ref, wgate_ref,
                  bgate_ref, wba_ref, wbb_ref, wbc_ref, wout_ref, x1_ref, h2_ref, *, n_lat_tiles):
    d = x_ref.shape[-1]
    x = x_ref[...]
    is_ctx = pl.program_id(1) >= n_lat_tiles
    o_a = jnp.where(is_ctx, ca_ref[...], oa_ref[...])
    o_b = jnp.where(is_ctx, cb_ref[...], ob_ref[...])
    o_c = jnp.where(is_ctx, cc_ref[...], oc_ref[...])
    mod = mod_ref[...]
    sh1, sc1, g1, sh2, sc2 = (mod[:, i * d:(i + 1) * d] for i in range(5))
    h = _rms(x, n1g_ref[...]) * (1.0 + sc1) + sh1
    gate = jax.nn.sigmoid(_dot(h.astype(BF16), wgate_ref[...]) + bgate_ref[...])
    m = gate[:, 0:d] * _dot(o_a, wba_ref[...])
    m += gate[:, d:2 * d] * _dot(o_b, wbb_ref[...])
    m += gate[:, 2 * d:3 * d] * _dot(o_c, wbc_ref[...])
    y = _dot(m.astype(BF16), wout_ref[...])
    x1 = x + g1 * y
    x1_ref[...] = x1
    h2_ref[...] = (_rms(x1, n2g_ref[...]) * (1.0 + sc2) + sh2).astype(BF16)


def _merge(x, lat_outs, ctx_outs, mod, mod_row, lw, layer):
    nb, n, d = x.shape
    tm = TOKEN_TILE
    n_lat_tiles = lat_outs[0].shape[1] // tm

    def full(a):
        return _layer_spec(a, layer)

    def tok(width):
        return pl.BlockSpec((None, tm, width), lambda b, i: (b, i, 0))

    def lat(a):
        return pl.BlockSpec((None, tm, a.shape[-1]), lambda b, i: (b, jnp.minimum(i, n_lat_tiles - 1), 0))

    def ctx(a):
        return pl.BlockSpec((None, tm, a.shape[-1]), lambda b, i: (b, jnp.maximum(i - n_lat_tiles, 0), 0))

    weights = [lw["norm1_g"], lw["norm2_g"], lw["w_gate"], lw["b_gate"], lw["w_br_mla"], lw["w_br_na"],
               lw["w_br_ft"], lw["w_out"]]
    return pl.pallas_call(
        functools.partial(_merge_kernel, n_lat_tiles=n_lat_tiles),
        grid=(nb, n // tm),
        in_specs=[tok(d)] + [lat(a) for a in lat_outs] + [ctx(a) for a in ctx_outs] + [_mod_spec(mod, layer, mod_row)]
        + [full(w) for w in weights],
        out_specs=[tok(d), tok(d)],
        out_shape=[jax.ShapeDtypeStruct((nb, n, d), F32), jax.ShapeDtypeStruct((nb, n, d), BF16)],
        compiler_params=_cparams(2),
        name="merge",
    )(x, *lat_outs, *ctx_outs, mod, *weights)


def _route(logits, router_bias):
    row = lax.broadcasted_iota(jnp.int32, logits.shape, 0)
    neg = -jnp.inf
    big = jnp.int32(N_EXPERTS)
    scores = jax.nn.sigmoid(logits)
    sel = scores + router_bias

    def top2(vals):
        m1 = jnp.max(vals, axis=0, keepdims=True)
        i1 = jnp.min(jnp.where(vals == m1, row, big), axis=0, keepdims=True)
        rest = jnp.where(row == i1, neg, vals)
        m2 = jnp.max(rest, axis=0, keepdims=True)
        i2 = jnp.min(jnp.where(rest == m2, row, big), axis=0, keepdims=True)
        return m1, i1, m2, i2

    grp_scores = []
    for g in range(N_GROUPS):
        in_grp = (row >= g * EXPERTS_PER_GROUP) & (row < (g + 1) * EXPERTS_PER_GROUP)
        m1, _, m2, _ = top2(jnp.where(in_grp, sel, neg))
        grp_scores.append(m1 + m2)
    best = functools.reduce(jnp.maximum, grp_scores)
    grp = jnp.full(best.shape, N_GROUPS - 1, jnp.int32)
    for g in range(N_GROUPS - 2, -1, -1):
        grp = jnp.where(grp_scores[g] == best, g, grp)
    in_sel = (row >= grp * EXPERTS_PER_GROUP) & (row < (grp + 1) * EXPERTS_PER_GROUP)
    _, i1, _, i2 = top2(jnp.where(in_sel, sel, neg))
    hit1 = row == i1
    hit2 = row == i2
    w1 = jnp.sum(jnp.where(hit1, scores, 0.0), axis=0, keepdims=True)
    w2 = jnp.sum(jnp.where(hit2, scores, 0.0), axis=0, keepdims=True)
    total = w1 + w2
    return jnp.where(hit1, w1 / total, 0.0) + jnp.where(hit2, w2 / total, 0.0), grp


def _split_bf16(v):
    hi = v.astype(BF16)
    return hi, (v - hi.astype(F32)).astype(BF16)


def _moe_kernel(h_ref, x_ref, mod_ref, cmod_ref, rw_ref, rb_ref, fg_ref, tri_ref, wg_hbm, wu_hbm, wd_hbm, o_ref,
                col_scr, row_scr, count_scr, acc_scr, wg_ref, wu_ref, wd_ref, w_sem, *, apply_final_norm, n_lat, cap,
                layer):
    d = x_ref.shape[-1]
    tm = x_ref.shape[0]
    step = pl.program_id(2)
    n_steps = pl.num_programs(2)
    h = h_ref[...]

    @pl.when((pl.program_id(0) == 0) & (pl.program_id(1) == 0) & (step == 0))
    def _():
        copies = [pltpu.make_async_copy(src.at[layer], dst, w_sem.at[k])
                  for k, (src, dst) in enumerate(((wg_hbm, wg_ref), (wu_hbm, wu_ref), (wd_hbm, wd_ref)))]
        for cp in copies:
            cp.start()
        for cp in copies:
            cp.wait()

    @pl.when(step == 0)
    def _():
        gate_t, grp = _route(_dot_nt(rw_ref[...], h), rb_ref[...])
        member = jnp.where(lax.broadcasted_iota(jnp.int32, (SUBLANES, tm), 0) == grp, 1.0, 0.0)
        rank = _dot(member.astype(BF16), tri_ref[...])
        empty_rows = jnp.zeros((SUBLANES - EXPERTS_PER_GROUP - 2, tm), F32)
        per_group = [jnp.concatenate([gate_t[c * EXPERTS_PER_GROUP:(c + 1) * EXPERTS_PER_GROUP], rank[c:c + 1],
                                      member[c:c + 1], empty_rows], axis=0) for c in range(N_GROUPS)]
        pad_rows = jnp.zeros((LANES - N_GROUPS * SUBLANES, tm), F32)
        cols = jnp.concatenate(per_group + [pad_rows], axis=0).T
        for c in range(N_GROUPS):
            count_scr[c] = jnp.sum(member[c:c + 1]).astype(jnp.int32)
            row_scr[c] = per_group[c]
            col_scr[c] = pltpu.roll(cols, (LANES - c * SUBLANES) % LANES, axis=1) if c else cols
        acc_scr[...] = jnp.zeros(acc_scr.shape, F32)

    def experts(hx, gates):
        out = jnp.zeros((hx.shape[0], d), F32)
        for j in range(EXPERTS_PER_STEP):
            e = step * EXPERTS_PER_STEP + j
            a = _dot(hx, wg_ref[e])
            u = _dot(hx, wu_ref[e])
            hid = (a * jax.nn.sigmoid(a)) * u * gates[:, j:j + 1]
            out += _dot(hid.astype(BF16), wd_ref[e])
        return out

    rank_lane, member_lane = EXPERTS_PER_GROUP, EXPERTS_PER_GROUP + 1

    n_members = count_scr[step]

    def compacted(rows):
        rank_row = row_scr[step, rank_lane:rank_lane + 1, :]
        member_row = row_scr[step, member_lane:member_lane + 1, :]
        slot = lax.broadcasted_iota(jnp.int32, (rows, tm), 0).astype(F32)
        pick = jnp.where((slot == rank_row) & (member_row > 0.5), 1.0, 0.0).astype(BF16)
        hx = _dot(pick, h).astype(BF16)
        col = col_scr[step]
        g_hi, g_lo = _split_bf16(col)
        y = experts(hx, _dot(pick, g_hi) + _dot(pick, g_lo))
        slot_t = lax.broadcasted_iota(jnp.int32, (tm, rows), 1).astype(F32)
        put = jnp.where((slot_t == col[:, rank_lane:rank_lane + 1]) & (col[:, member_lane:member_lane + 1] > 0.5),
                        1.0, 0.0).astype(BF16)
        y_hi, y_lo = _split_bf16(y)
        acc_scr[...] += _dot(put, y_hi) + _dot(put, y_lo)

    small = cap // 2

    @pl.when(n_members <= small)
    def _():
        compacted(small)

    @pl.when((n_members > small) & (n_members <= cap))
    def _():
        compacted(cap)

    @pl.when(n_members > cap)
    def _():
        acc_scr[...] += experts(h, col_scr[step])

    @pl.when(step == n_steps - 1)
    def _():
        tok_row = pl.program_id(1) * tm + lax.broadcasted_iota(jnp.int32, (tm, 1), 0)
        g2 = jnp.where(tok_row >= n_lat, cmod_ref[:, 5 * d:6 * d], mod_ref[:, 5 * d:6 * d])
        x2 = x_ref[...] + g2 * acc_scr[...]
        o_ref[...] = _rms(x2, fg_ref[...]) if apply_final_norm else x2


def _moe_tile(p):
    return max(t for t in range(TOKEN_TILE, MOE_MAX_TILE + 1, TOKEN_TILE) if p % t == 0)


def _moe_capacity(tm):
    return min(tm, -(-(5 * tm // (4 * N_GROUPS)) // LANES) * LANES)


def _moe(h2, x1, mod, ctx_row, lw, layer, router_w, router_b, final_g, n_lat, n_rows, tm, apply_final_norm):
    assert EXPERTS_PER_STEP == EXPERTS_PER_GROUP
    nb, _, d = x1.shape
    n_steps = N_EXPERTS // EXPERTS_PER_STEP
    ff = lw["w_e_gate"].shape[-1]
    tok = pl.BlockSpec((None, tm, d), lambda b, i, c: (b, i, 0))
    earlier = jnp.asarray(np.triu(np.ones((tm, tm), np.float32), k=1), BF16)
    kern = functools.partial(_moe_kernel, apply_final_norm=apply_final_norm, n_lat=n_lat, cap=_moe_capacity(tm),
                             layer=layer)
    in_hbm = pl.BlockSpec(memory_space=pl.ANY)
    return pl.pallas_call(
        kern,
        grid=(nb, n_rows // tm, n_steps),
        in_specs=[
            tok, tok,
            _mod_spec(mod, layer, lambda b, i: b),
            _mod_spec(mod, layer, lambda b, i: ctx_row),
            pl.BlockSpec(router_w.shape, lambda b, i, c: (0, 0)),
            pl.BlockSpec(router_b.shape, lambda b, i, c: (0, 0)),
            pl.BlockSpec(final_g.shape, lambda b, i, c: (0, 0)),
            pl.BlockSpec(earlier.shape, lambda b, i, c: (0, 0)),
            in_hbm, in_hbm, in_hbm,
        ],
        out_specs=tok,
        out_shape=jax.ShapeDtypeStruct((nb, n_rows, d), F32),
        scratch_shapes=[
            pltpu.VMEM((n_steps, tm, LANES), F32),
            pltpu.VMEM((n_steps, SUBLANES, tm), F32),
            pltpu.SMEM((n_steps,), jnp.int32),
            pltpu.VMEM((tm, d), F32),
            pltpu.VMEM((N_EXPERTS, d, ff), BF16),
            pltpu.VMEM((N_EXPERTS, d, ff), BF16),
            pltpu.VMEM((N_EXPERTS, ff, d), BF16),
            pltpu.SemaphoreType.DMA((3,)),
        ],
        compiler_params=_cparams(3),
        name="moe",
    )(h2, x1, mod, mod, router_w, router_b, final_g, earlier, lw["w_e_gate"], lw["w_e_up"], lw["w_e_down"])


def _rotate_cols(w):
    half = w.shape[-1] // 2
    return jnp.concatenate([-w[..., half:], w[..., :half]], axis=-1)


def _prepare_weights(w_in, q_norm_g, w_q_up, kv_norm_g, w_kv_up, norm1_g, norm2_g, w_gate, b_gate, w_br_mla,
                     w_br_na, w_br_ft, w_out, w_e_gate, w_e_up, w_e_down):
    n_layers, d, _ = w_in.shape
    splits = np.cumsum([MLA_Q_LORA, MLA_KV_LORA, MLA_ROPE, NA_WIDTH, NA_WIDTH, NA_WIDTH])
    w_q, w_kv, w_kr, w_nq, w_nk, w_nv, w_ft = jnp.split(w_in, [int(s) for s in splits], axis=-1)

    def in_rope_slot(w):
        zeros_lo = jnp.zeros(w.shape[:-1] + (MLA_NOPE,), w.dtype)
        zeros_hi = jnp.zeros(w.shape[:-1] + (HEAD_PAD - MLA_NOPE - MLA_ROPE,), w.dtype)
        return jnp.concatenate([zeros_lo, w, zeros_hi], axis=-1)

    win = jnp.concatenate([w_q, w_kv, in_rope_slot(w_kr), in_rope_slot(_rotate_cols(w_kr)), w_nq, w_nk, w_nv, w_ft],
                          axis=-1).astype(BF16)
    wq = w_q_up.reshape(n_layers, MLA_Q_LORA, MLA_HEADS, MLA_NOPE + MLA_ROPE)
    wq_nope, wq_pe = wq[..., :MLA_NOPE], wq[..., MLA_NOPE:]
    pad = jnp.zeros(wq_pe.shape[:-1] + (HEAD_PAD - MLA_NOPE - MLA_ROPE,), wq.dtype)
    wq_main = jnp.concatenate([wq_nope, wq_pe, pad], axis=-1)
    wq_rot = jnp.concatenate([jnp.zeros_like(wq_nope), _rotate_cols(wq_pe), pad], axis=-1)
    wkv = w_kv_up.reshape(n_layers, MLA_KV_LORA, MLA_HEADS, MLA_NOPE + MLA_V)
    wk_nope, wv = wkv[..., :MLA_NOPE], wkv[..., MLA_NOPE:]
    wkv_k = jnp.concatenate([wk_nope, jnp.zeros(wk_nope.shape[:-1] + (HEAD_PAD - MLA_NOPE,), wkv.dtype)], axis=-1)
    wide = MLA_HEADS * HEAD_PAD
    return {
        "w_in": win,
        "wq_main": wq_main.reshape(n_layers, MLA_Q_LORA, wide).astype(BF16),
        "wq_rot": wq_rot.reshape(n_layers, MLA_Q_LORA, wide).astype(BF16),
        "wkv_k": wkv_k.reshape(n_layers, MLA_KV_LORA, wide).astype(BF16),
        "wkv_v": jnp.concatenate([wv, jnp.zeros(wv.shape[:-1] + (HEAD_PAD - MLA_V,), wv.dtype)], axis=-1)
        .reshape(n_layers, MLA_KV_LORA, wide).astype(BF16),
        "q_norm_g": q_norm_g.reshape(n_layers, 1, -1),
        "kv_norm_g": kv_norm_g.reshape(n_layers, 1, -1),
        "norm1_g": norm1_g.reshape(n_layers, 1, d),
        "norm2_g": norm2_g.reshape(n_layers, 1, d),
        "w_gate": w_gate.astype(BF16),
        "b_gate": b_gate.reshape(n_layers, 1, -1),
        "w_br_mla": w_br_mla.astype(BF16),
        "w_br_na": w_br_na.astype(BF16),
        "w_br_ft": w_br_ft.astype(BF16),
        "w_out": w_out.astype(BF16),
        "w_e_gate": w_e_gate.astype(BF16),
        "w_e_up": w_e_up.astype(BF16),
        "w_e_down": w_e_down.astype(BF16),
    }


def _rope_tables(s, c):
    n_freq = MLA_ROPE // 4
    inv_freq = ROPE_THETA ** (-jnp.arange(n_freq, dtype=F32) / n_freq)
    t = jnp.arange(s, dtype=jnp.int32)
    row = (t // GRID_W).astype(F32)
    col = (t % GRID_W).astype(F32)
    ang = jnp.concatenate([row[:, None] * inv_freq, col[:, None] * inv_freq], axis=-1)
    cos, sin = jnp.cos(ang), jnp.sin(ang)
    pad = HEAD_PAD - MLA_NOPE - MLA_ROPE
    cos_lat = jnp.concatenate([jnp.ones((s, MLA_NOPE), F32), cos, cos, jnp.zeros((s, pad), F32)], axis=-1)
    sin_lat = jnp.concatenate([jnp.zeros((s, MLA_NOPE), F32), sin, sin, jnp.zeros((s, pad), F32)], axis=-1)
    cos_ctx = jnp.concatenate([jnp.ones((c, MLA_NOPE + MLA_ROPE), F32), jnp.zeros((c, pad), F32)], axis=-1)
    sin_ctx = jnp.zeros((c, HEAD_PAD), F32)
    return cos_lat, sin_lat, cos_ctx, sin_ctx


@jax.jit
def _forward(x, c, ctx, c_ctx, w_ada, b_ada, norm1_g, norm2_g, w_in, q_norm_g, w_q_up, kv_norm_g, w_kv_up,
             na_rpb, w_gate, b_gate, w_br_mla, w_br_na, w_br_ft, w_out, router_w, router_bias, w_e_gate,
             w_e_up, w_e_down, final_norm_g):
    nb, s, d = x.shape
    n_ctx = ctx.shape[1]
    n_layers = w_ada.shape[0]
    grid_rows = s // GRID_W

    mod_rows = 8
    c_rows = jnp.zeros((mod_rows, d), F32).at[:nb].set(c).at[nb].set(c_ctx)
    mod_all = _modulation(c_rows, w_ada, b_ada).reshape(n_layers, mod_rows, 1, 6 * d)

    weights = _prepare_weights(w_in, q_norm_g, w_q_up, kv_norm_g, w_kv_up, norm1_g, norm2_g, w_gate, b_gate,
                               w_br_mla, w_br_na, w_br_ft, w_out, w_e_gate, w_e_up, w_e_down)
    na_bias = _na_bias_tables(na_rpb)
    na_row_mask = _na_row_masks(grid_rows)
    cos_lat, sin_lat, cos_ctx, sin_ctx = _rope_tables(s, n_ctx)
    cos_t = jnp.concatenate([cos_lat, cos_ctx], axis=0)
    sin_t = jnp.concatenate([sin_lat, sin_ctx], axis=0)
    router_w_p = router_w.T.astype(BF16)
    router_b_p = router_bias.reshape(N_EXPERTS, 1).astype(F32)
    final_g = final_norm_g.reshape(1, d)

    p = s + n_ctx
    n_lat_tiles = s // TOKEN_TILE
    ctx_row = nb

    def tile_row(b, i):
        return jnp.where(i >= n_lat_tiles, ctx_row, b)

    x_all = jnp.concatenate([x, ctx], axis=1)
    zero_ctx = [jnp.zeros((nb, n_ctx, w), BF16) for w in (MLA_HEADS * MLA_V, NA_WIDTH, FT_WIDTH)]
    for l in range(n_layers):
        last = l == n_layers - 1
        q, k, v, nq, nk, nv, ft = _project(x_all, mod_all, tile_row, weights, l, cos_t, sin_t)
        lat_outs = [_mla_attention(q, k, v, s), _na_attention(nq, nk, nv, na_bias, na_row_mask, l, s),
                    _fourier_lat(ft, s)]
        if last:
            ctx_outs = zero_ctx
        else:
            ctx_outs = [_mla_ctx_attention(q, k, v, s), _na_ctx_attention(nq, nk, nv, s), _fourier_ctx(ft, s)]
        x1, h2 = _merge(x_all, lat_outs, ctx_outs, mod_all, tile_row, weights, l)
        if last:
            return _moe(h2, x1, mod_all, ctx_row, weights, l, router_w_p, router_b_p, final_g, s, s,
                        _moe_tile(s), True)
        x_all = _moe(h2, x1, mod_all, ctx_row, weights, l, router_w_p, router_b_p, final_g, s, p,
                     _moe_tile(p), False)


def kernel(x, c, ctx, c_ctx, w_ada, b_ada, norm1_g, norm2_g, w_in, q_norm_g, w_q_up, kv_norm_g, w_kv_up, na_rpb, w_gate, b_gate, w_br_mla, w_br_na, w_br_ft, w_out, router_w, router_bias, w_e_gate, w_e_up, w_e_down, final_norm_g):
    return _forward(x, c, ctx, c_ctx, w_ada, b_ada, norm1_g, norm2_g, w_in, q_norm_g, w_q_up, kv_norm_g,
                    w_kv_up, na_rpb, w_gate, b_gate, w_br_mla, w_br_na, w_br_ft, w_out, router_w, router_bias,
                    w_e_gate, w_e_up, w_e_down, final_norm_g)
```

```python
import functools
import math

import jax
import jax.numpy as jnp
import numpy as np
from jax import lax
from jax.experimental import pallas as pl
from jax.experimental.pallas import tpu as pltpu

F32 = jnp.float32
BF16 = jnp.bfloat16

GRID_W = 64
MLA_HEADS = 8
MLA_NOPE = 64
MLA_ROPE = 32
MLA_V = 64
MLA_Q_LORA = 256
MLA_KV_LORA = 128
MLA_SCALE = (MLA_NOPE + MLA_ROPE) ** -0.5
NA_HEADS = 4
NA_HEAD_DIM = 64
NA_WIN_R = 8
NA_WIN_C = 16
NA_SCALE = NA_HEAD_DIM ** -0.5
NA_WIDTH = NA_HEADS * NA_HEAD_DIM
FT_GROUPS = 4
FT_GROUP_DIM = 64
FT_WIDTH = FT_GROUPS * FT_GROUP_DIM
N_EXPERTS = 16
N_GROUPS = 4
EXPERTS_PER_GROUP = N_EXPERTS // N_GROUPS
ROPE_THETA = 10000.0
NORM_EPS = 1e-6
MASK_VALUE = -1e30
LOG2E = math.log2(math.e)

LANES = 128
SUBLANES = 8
V7X_VMEM_LIMIT_BYTES = 56 * 1024 * 1024

HEAD_PAD = LANES
NA_ROWS_PER_BLOCK = 8
NA_KEY_ROWS = 16
NA_BIAS_SHIFT = (NA_KEY_ROWS - NA_ROWS_PER_BLOCK) + (NA_ROWS_PER_BLOCK - 1) - (NA_WIN_R - 1)
NA_BIAS_TILES = (NA_WIN_R - 1) + NA_BIAS_SHIFT + (NA_KEY_ROWS - 2) + 1
FT_K1_PER_STEP = 8
EXPERTS_PER_STEP = 4
MOE_MAX_TILE = 768
TOKEN_TILE = 256
MLA_MAX_KV_CHUNK = 2816


def _cparams(n_axes):
    return pltpu.CompilerParams(
        dimension_semantics=("arbitrary",) * n_axes,
        vmem_limit_bytes=V7X_VMEM_LIMIT_BYTES,
    )


def _rms(x, g):
    return x * lax.rsqrt(jnp.mean(x * x, axis=-1, keepdims=True) + NORM_EPS) * g


def _dot(a, b):
    return jnp.dot(a, b, preferred_element_type=F32)


def _dot_nt(a, b):
    return lax.dot_general(a, b, (((1,), (1,)), ((), ())), preferred_element_type=F32)


def _mod_kernel(c_ref, w_ref, b_ref, o_ref):
    c = c_ref[...]
    o_ref[...] = _dot(c * jax.nn.sigmoid(c), w_ref[...]) + b_ref[...]


def _modulation(c_rows, w_ada, b_ada):
    n_layers, d, width = w_ada.shape
    rows = c_rows.shape[0]
    tn = 1536
    return pl.pallas_call(
        _mod_kernel,
        grid=(n_layers, width // tn),
        in_specs=[
            pl.BlockSpec((rows, d), lambda l, j: (0, 0)),
            pl.BlockSpec((None, d, tn), lambda l, j: (l, 0, j)),
            pl.BlockSpec((None, 1, tn), lambda l, j: (l, 0, j)),
        ],
        out_specs=pl.BlockSpec((None, rows, tn), lambda l, j: (l, 0, j)),
        out_shape=jax.ShapeDtypeStruct((n_layers, rows, width), F32),
        compiler_params=_cparams(2),
        name="modulation",
    )(c_rows, w_ada, b_ada.reshape(n_layers, 1, width))


def _proj_kernel(x_ref, mod_ref, n1g_ref, win_ref, qg_ref, wqm_ref, wqr_ref, kvg_ref, wkk_ref, wkv_ref,
                 cos_ref, sin_ref, q_ref, k_ref, v_ref, nq_ref, nk_ref, nv_ref, ft_ref):
    d = x_ref.shape[-1]
    x = x_ref[...]
    mod = mod_ref[...]
    sh1, sc1 = mod[:, 0:d], mod[:, d:2 * d]
    h = _rms(x, n1g_ref[...]) * (1.0 + sc1) + sh1
    p = _dot(h.astype(BF16), win_ref[...])
    cos = cos_ref[...]
    sin = sin_ref[...]
    o = 0
    q_lat = p[:, o:o + MLA_Q_LORA]
    o += MLA_Q_LORA
    kv_lat = p[:, o:o + MLA_KV_LORA]
    o += MLA_KV_LORA
    kr_a = p[:, o:o + HEAD_PAD]
    o += HEAD_PAD
    kr_b = p[:, o:o + HEAD_PAD]
    o += HEAD_PAD
    nq_ref[...] = (p[:, o:o + NA_WIDTH] * NA_SCALE).astype(BF16)
    o += NA_WIDTH
    nk_ref[...] = p[:, o:o + NA_WIDTH].astype(BF16)
    o += NA_WIDTH
    nv_ref[...] = p[:, o:o + NA_WIDTH].astype(BF16)
    o += NA_WIDTH
    ft_ref[...] = p[:, o:o + FT_WIDTH].astype(BF16)

    qn = _rms(q_lat, qg_ref[...]).astype(BF16)
    qm = _dot(qn, wqm_ref[...])
    qr = _dot(qn, wqr_ref[...])
    kvn = _rms(kv_lat, kvg_ref[...]).astype(BF16)
    kn = _dot(kvn, wkk_ref[...])
    v = _dot(kvn, wkv_ref[...])
    v_lane = lax.broadcasted_iota(jnp.int32, v.shape, 1) & (HEAD_PAD - 1)
    v_ref[...] = jnp.where(v_lane == MLA_V, 1.0, v).astype(BF16)
    kr = kr_a * cos + kr_b * sin
    q_scale = MLA_SCALE * LOG2E
    for hd in range(MLA_HEADS):
        sl = slice(hd * HEAD_PAD, (hd + 1) * HEAD_PAD)
        q_ref[:, sl] = ((qm[:, sl] * cos + qr[:, sl] * sin) * q_scale).astype(BF16)
        k_ref[:, sl] = (kn[:, sl] + kr).astype(BF16)


def _layer_spec(a, layer):
    return pl.BlockSpec((None,) + a.shape[1:], lambda *_: (layer,) + (0,) * (a.ndim - 1))


def _mod_spec(mod, layer, mod_row):
    return pl.BlockSpec((None, None, 1, mod.shape[-1]), lambda b, i, *_: (layer, mod_row(b, i), 0, 0))


def _project(x, mod, mod_row, lw, layer, cos_t, sin_t):
    nb, n, d = x.shape
    tm = TOKEN_TILE
    win = lw["w_in"]
    wide = MLA_HEADS * HEAD_PAD

    def full(a):
        return _layer_spec(a, layer)

    def tok(width):
        return pl.BlockSpec((None, tm, width), lambda b, i: (b, i, 0))

    outs = [wide, wide, wide, NA_WIDTH, NA_WIDTH, NA_WIDTH, FT_WIDTH]
    return pl.pallas_call(
        _proj_kernel,
        grid=(nb, n // tm),
        in_specs=[
            tok(d),
            _mod_spec(mod, layer, mod_row),
            full(lw["norm1_g"]), full(win), full(lw["q_norm_g"]), full(lw["wq_main"]), full(lw["wq_rot"]),
            full(lw["kv_norm_g"]), full(lw["wkv_k"]), full(lw["wkv_v"]),
            pl.BlockSpec((tm, HEAD_PAD), lambda b, i: (i, 0)),
            pl.BlockSpec((tm, HEAD_PAD), lambda b, i: (i, 0)),
        ],
        out_specs=[tok(w) for w in outs],
        out_shape=[jax.ShapeDtypeStruct((nb, n, w), BF16) for w in outs],
        compiler_params=_cparams(2),
        name="project",
    )(x, mod, lw["norm1_g"], win, lw["q_norm_g"], lw["wq_main"], lw["wq_rot"], lw["kv_norm_g"],
      lw["wkv_k"], lw["wkv_v"], cos_t, sin_t)


def _kv_chunk(p):
    best = LANES
    for n in range(1, p // LANES + 1):
        tk = p // n
        if p % n == 0 and tk % LANES == 0 and tk <= MLA_MAX_KV_CHUNK:
            best = max(best, tk)
    return best


def _store_head_values(o_ref, rows, hd, out):
    if hd:
        out = pltpu.roll(out, MLA_V, axis=1)
    sl = slice(hd * MLA_V, (hd + 1) * MLA_V)
    o_ref[rows, sl] = out[:, sl].astype(BF16)


def _mla_kernel(q_ref, k_ref, v_ref, o_ref, s0_scr, s1_scr, p0_scr, p1_scr, mx_scr, m_scr, alpha_scr, acc_scr,
                *, tq, tk, n_q, n_c):
    def head(hd):
        return slice(hd * HEAD_PAD, (hd + 1) * HEAD_PAD)

    def q_rows(qt):
        return pl.ds(pl.multiple_of(qt * tq, tq), tq)

    def k_rows(c):
        return pl.ds(pl.multiple_of(c * tk, tk), tk)

    def scores(hd, qt, c, s_scr):
        s = _dot_nt(q_ref[q_rows(qt), head(hd)], k_ref[k_rows(c), head(hd)])
        s_scr[...] = s
        mx_scr[hd] = jnp.broadcast_to(jnp.max(s, axis=1, keepdims=True), mx_scr.shape[1:])

    def softmax(hd, s_scr, p_scr, first):
        m_prev = jnp.where(first, MASK_VALUE, m_scr[hd])
        m_new = jnp.maximum(m_prev, mx_scr[hd])
        alpha_scr[hd] = jnp.exp2(m_prev - m_new)
        m_scr[hd] = m_new
        p_scr[...] = jnp.exp2(s_scr[...] - jnp.tile(m_new, (1, tk // LANES))).astype(BF16)

    def accumulate(hd, p_scr, qt, c):
        acc = alpha_scr[hd] * acc_scr[hd] + _dot(p_scr[...], v_ref[k_rows(c), head(hd)])
        acc_scr[hd] = acc
        _store_head_values(o_ref, q_rows(qt), hd, acc / acc[:, MLA_V:MLA_V + 1])

    lane = lax.broadcasted_iota(jnp.int32, acc_scr.shape, 2)
    acc_scr[...] = jnp.where(lane == MLA_V, 1.0, 0.0)
    m_scr[...] = jnp.full(m_scr.shape, MASK_VALUE, F32)
    alpha_scr[...] = jnp.ones(alpha_scr.shape, F32)
    p1_scr[...] = jnp.zeros(p1_scr.shape, BF16)
    scores(0, 0, 0, s0_scr)

    def body(i, carry):
        qt, c = carry
        first = c == 0
        last = c == n_c - 1
        c_prev = jnp.where(first, n_c - 1, c - 1)
        qt_prev = jnp.maximum(jnp.where(first, qt - 1, qt), 0)
        c_next = jnp.where(last, 0, c + 1)
        qt_next = jnp.where(last, qt + 1, qt)
        scores(1, qt, c, s1_scr)
        softmax(0, s0_scr, p0_scr, first)
        accumulate(1, p1_scr, qt_prev, c_prev)
        scores(0, jnp.minimum(qt_next, n_q - 1), c_next, s0_scr)
        softmax(1, s1_scr, p1_scr, first)
        accumulate(0, p0_scr, qt, c)
        return qt_next, c_next

    lax.fori_loop(0, n_q * n_c, body, (jnp.int32(0), jnp.int32(0)))
    accumulate(1, p1_scr, n_q - 1, n_c - 1)


def _mla_attention(q, k_all, v_all, s):
    nb, p, _ = k_all.shape
    tq = min(512, s)
    tk = _kv_chunk(p)
    pairs = MLA_HEADS // 2
    kern = functools.partial(_mla_kernel, tq=tq, tk=tk, n_q=s // tq, n_c=p // tk)
    return pl.pallas_call(
        kern,
        grid=(nb, pairs),
        in_specs=[
            pl.BlockSpec((None, s, 2 * HEAD_PAD), lambda b, hp: (b, 0, hp)),
            pl.BlockSpec((None, p, 2 * HEAD_PAD), lambda b, hp: (b, 0, hp)),
            pl.BlockSpec((None, p, 2 * HEAD_PAD), lambda b, hp: (b, 0, hp)),
        ],
        out_specs=pl.BlockSpec((None, s, 2 * MLA_V), lambda b, hp: (b, 0, hp)),
        out_shape=jax.ShapeDtypeStruct((nb, s, MLA_HEADS * MLA_V), BF16),
        scratch_shapes=[
            pltpu.VMEM((tq, tk), F32),
            pltpu.VMEM((tq, tk), F32),
            pltpu.VMEM((tq, tk), BF16),
            pltpu.VMEM((tq, tk), BF16),
            pltpu.VMEM((2, tq, LANES), F32),
            pltpu.VMEM((2, tq, LANES), F32),
            pltpu.VMEM((2, tq, LANES), F32),
            pltpu.VMEM((2, tq, HEAD_PAD), F32),
        ],
        compiler_params=_cparams(2),
        name="mla_attention",
    )(q, k_all, v_all)


def _mla_ctx_kernel(q_ref, k_ref, v_ref, o_ref):
    for hd in range(2):
        sl = slice(hd * HEAD_PAD, (hd + 1) * HEAD_PAD)
        s = _dot_nt(q_ref[:, sl], k_ref[:, sl])
        p = jnp.exp2(s - jnp.max(s, axis=1, keepdims=True))
        acc = _dot(p.astype(BF16), v_ref[:, sl])
        _store_head_values(o_ref, slice(None), hd, acc / acc[:, MLA_V:MLA_V + 1])


def _mla_ctx_attention(q, k, v, s):
    nb, p, _ = q.shape
    c = p - s
    pairs = MLA_HEADS // 2
    spec = pl.BlockSpec((None, c, 2 * HEAD_PAD), lambda b, hp: (b, s // c, hp))
    return pl.pallas_call(
        _mla_ctx_kernel,
        grid=(nb, pairs),
        in_specs=[spec, spec, spec],
        out_specs=pl.BlockSpec((None, c, 2 * MLA_V), lambda b, hp: (b, 0, hp)),
        out_shape=jax.ShapeDtypeStruct((nb, c, MLA_HEADS * MLA_V), BF16),
        compiler_params=_cparams(2),
        name="mla_ctx_attention",
    )(q, k, v)


def _na_kernel(q_ref, k_ref, v_ref, kc_ref, vc_ref, bias_ref, rowmask_ref, o_ref, comb_scr, s0_scr, s1_scr, p0_scr,
               p1_scr, mx_scr, l_scr, *, grid_rows, n_blocks):
    tq = NA_ROWS_PER_BLOCK * GRID_W
    n_keys = NA_KEY_ROWS * GRID_W

    for hd in range(2):
        for var, row_off in enumerate((0, -(NA_WIN_R // 2), -(NA_KEY_ROWS - NA_ROWS_PER_BLOCK))):
            tile0 = row_off + (NA_WIN_R - 1) + NA_BIAS_SHIFT
            bias = jnp.concatenate(
                [jnp.concatenate([bias_ref[hd, tile0 + 2 * j - qr] for j in range(NA_KEY_ROWS // 2)], axis=1)
                 for qr in range(NA_ROWS_PER_BLOCK)], axis=0)
            comb_scr[hd, var] = bias + rowmask_ref[var].astype(F32)

    lane = lax.broadcasted_iota(jnp.int32, (tq, LANES), 1)

    def q_rows(rb):
        return pl.ds(pl.multiple_of(rb * tq, tq), tq)

    def key_rows(rb):
        start_row = jnp.clip(rb * NA_ROWS_PER_BLOCK - NA_WIN_R // 2, 0, grid_rows - NA_KEY_ROWS)
        return pl.ds(pl.multiple_of(start_row * GRID_W, (NA_WIN_R // 2) * GRID_W), n_keys)

    def scores(hd, rb, s_scr):
        var = jnp.where(rb == 0, 0, jnp.where(rb == n_blocks - 1, 2, 1))
        q = q_ref[q_rows(rb), :]
        in_head = (lane >= NA_HEAD_DIM) if hd else (lane < NA_HEAD_DIM)
        qh = jnp.where(in_head, q, jnp.zeros_like(q))
        s = _dot_nt(qh, k_ref[key_rows(rb), :]) + comb_scr[hd, var]
        sc = _dot_nt(qh, kc_ref[...])
        s_scr[:, :n_keys] = s
        s_scr[:, n_keys:] = sc
        m = jnp.maximum(jnp.max(s, axis=1, keepdims=True), jnp.max(sc, axis=1, keepdims=True))
        mx_scr[hd] = jnp.broadcast_to(m, mx_scr.shape[1:])

    def softmax(hd, s_scr, p_scr):
        p = jnp.exp(s_scr[...] - jnp.tile(mx_scr[hd], (1, s_scr.shape[1] // LANES)))
        l_scr[hd] = jnp.broadcast_to(jnp.sum(p, axis=1, keepdims=True), l_scr.shape[1:])
        p_scr[...] = p.astype(BF16)

    def accumulate(hd, p_scr, rb):
        acc = _dot(p_scr[:, :n_keys], v_ref[key_rows(rb), :]) + _dot(p_scr[:, n_keys:], vc_ref[...])
        out = (acc / l_scr[hd]).astype(BF16)
        sl = slice(hd * NA_HEAD_DIM, (hd + 1) * NA_HEAD_DIM)
        o_ref[q_rows(rb), sl] = out[:, sl]

    p1_scr[...] = jnp.zeros(p1_scr.shape, BF16)
    l_scr[...] = jnp.ones(l_scr.shape, F32)
    scores(0, 0, s0_scr)

    def body(rb, carry):
        scores(1, rb, s1_scr)
        softmax(0, s0_scr, p0_scr)
        accumulate(1, p1_scr, jnp.maximum(rb - 1, 0))
        scores(0, jnp.minimum(rb + 1, n_blocks - 1), s0_scr)
        softmax(1, s1_scr, p1_scr)
        accumulate(0, p0_scr, rb)
        return carry

    lax.fori_loop(0, n_blocks, body, 0)
    accumulate(1, p1_scr, n_blocks - 1)


def _na_attention(q, k, v, bias, row_mask, layer, s):
    nb, p, _ = q.shape
    c = p - s
    grid_rows = s // GRID_W
    tq = NA_ROWS_PER_BLOCK * GRID_W
    n_blocks = grid_rows // NA_ROWS_PER_BLOCK
    n_keys = NA_KEY_ROWS * GRID_W
    pairs = NA_HEADS // 2
    kern = functools.partial(_na_kernel, grid_rows=grid_rows, n_blocks=n_blocks)
    lat = pl.BlockSpec((None, s, LANES), lambda b, hp: (b, 0, hp))
    ctx = pl.BlockSpec((None, c, LANES), lambda b, hp: (b, s // c, hp))
    return pl.pallas_call(
        kern,
        grid=(nb, pairs),
        in_specs=[
            lat, lat, lat, ctx, ctx,
            pl.BlockSpec((None, 2) + bias.shape[2:], lambda b, hp: (layer, hp, 0, 0, 0)),
            pl.BlockSpec(row_mask.shape, lambda b, hp: (0, 0, 0)),
        ],
        out_specs=pl.BlockSpec((None, s, LANES), lambda b, hp: (b, 0, hp)),
        out_shape=jax.ShapeDtypeStruct((nb, s, NA_WIDTH), BF16),
        scratch_shapes=[
            pltpu.VMEM((2, 3, tq, n_keys), F32),
            pltpu.VMEM((tq, n_keys + c), F32),
            pltpu.VMEM((tq, n_keys + c), F32),
            pltpu.VMEM((tq, n_keys + c), BF16),
            pltpu.VMEM((tq, n_keys + c), BF16),
            pltpu.VMEM((2, tq, LANES), F32),
            pltpu.VMEM((2, tq, LANES), F32),
        ],
        compiler_params=_cparams(2),
        name="na_attention",
    )(q, k, v, k, v, bias, row_mask)


def _na_ctx_kernel(q_ref, k_ref, v_ref, o_ref):
    q = q_ref[...]
    k = k_ref[...]
    v = v_ref[...]
    lane = lax.broadcasted_iota(jnp.int32, q.shape, 1)
    outs = []
    for hd in range(2):
        in_head = (lane >= NA_HEAD_DIM) if hd else (lane < NA_HEAD_DIM)
        qh = jnp.where(in_head, q, jnp.zeros_like(q))
        s = _dot_nt(qh, k)
        p = jnp.exp(s - jnp.max(s, axis=1, keepdims=True))
        outs.append(_dot(p.astype(BF16), v) / jnp.sum(p, axis=1, keepdims=True))
    o_ref[...] = jnp.where(lane < NA_HEAD_DIM, outs[0], outs[1]).astype(BF16)


def _na_ctx_attention(q, k, v, s):
    nb, p, _ = q.shape
    c = p - s
    spec = pl.BlockSpec((None, c, LANES), lambda b, hp: (b, s // c, hp))
    return pl.pallas_call(
        _na_ctx_kernel,
        grid=(nb, NA_HEADS // 2),
        in_specs=[spec, spec, spec],
        out_specs=pl.BlockSpec((None, c, LANES), lambda b, hp: (b, 0, hp)),
        out_shape=jax.ShapeDtypeStruct((nb, c, NA_WIDTH), BF16),
        compiler_params=_cparams(2),
        name="na_ctx_attention",
    )(q, k, v)


def _na_row_masks(grid_rows):
    qr = np.arange(NA_ROWS_PER_BLOCK)[:, None]
    kr = np.arange(NA_KEY_ROWS)[None, :]
    row_valid = np.zeros((3, NA_ROWS_PER_BLOCK, NA_KEY_ROWS), bool)
    n_blocks = grid_rows // NA_ROWS_PER_BLOCK
    for vi, rb in enumerate((0, 1, n_blocks - 1)):
        r = rb * NA_ROWS_PER_BLOCK + qr
        start_row = min(max(rb * NA_ROWS_PER_BLOCK - NA_WIN_R // 2, 0), grid_rows - NA_KEY_ROWS)
        key_row = start_row + kr
        win = np.clip(r - NA_WIN_R // 2, 0, grid_rows - NA_WIN_R)
        row_valid[vi] = (key_row >= win) & (key_row < win + NA_WIN_R)
    mask = np.where(row_valid, 0.0, MASK_VALUE).astype(np.float32)
    mask = np.broadcast_to(mask[:, :, None, :, None], (3, NA_ROWS_PER_BLOCK, GRID_W, NA_KEY_ROWS, GRID_W))
    return jnp.asarray(mask.reshape(3, NA_ROWS_PER_BLOCK * GRID_W, NA_KEY_ROWS * GRID_W), BF16)


def _na_bias_tables(rpb):
    n_dc = 2 * NA_WIN_C - 1
    qc = np.arange(GRID_W)[:, None]
    kcol = np.arange(GRID_W)[None, :]
    wstart = np.clip(qc - NA_WIN_C // 2, 0, GRID_W - NA_WIN_C)
    col_valid = (kcol >= wstart) & (kcol < wstart + NA_WIN_C)
    dc = np.clip(kcol - qc, -(NA_WIN_C - 1), NA_WIN_C - 1) + (NA_WIN_C - 1)
    col_onehot = ((dc[..., None] == np.arange(n_dc)) & col_valid[..., None]).astype(np.float32)
    t = jnp.einsum("lhij,cdj->lhicd", rpb.astype(F32), jnp.asarray(col_onehot), precision=lax.Precision.HIGHEST)
    t = jnp.where(jnp.asarray(col_valid), t, MASK_VALUE)
    n_dr = t.shape[2]
    t = jnp.pad(t, ((0, 0), (0, 0), (NA_BIAS_SHIFT, NA_BIAS_TILES + 1 - NA_BIAS_SHIFT - n_dr), (0, 0), (0, 0)),
                constant_values=MASK_VALUE)
    return jnp.concatenate([t[:, :, :NA_BIAS_TILES], t[:, :, 1:NA_BIAS_TILES + 1]], axis=-1)


FT_N1 = 64


def _ft_stage1_kernel(w_ref, x_ref, o_ref):
    o_ref[...] = _dot(w_ref[...], x_ref[...])


def _ft_stage2_kernel(are_ref, aim_ref, tc_ref, ts_ref, w_ref, cd_ref, sd_ref, o_ref, *, scale):
    n2 = are_ref.shape[1]
    for g in range(are_ref.shape[0]):
        are = are_ref[g]
        aim = aim_ref[g]
        tc = jnp.concatenate([tc_ref[g]] * (FT_WIDTH // LANES), axis=1)
        ts = jnp.concatenate([ts_ref[g]] * (FT_WIDTH // LANES), axis=1)
        bre = are * tc + aim * ts
        bim = aim * tc - are * ts
        bb = jnp.concatenate([bre, bim], axis=0).astype(BF16)
        xk = _dot(w_ref[...], bb)
        f = _dot(xk[:n2].astype(BF16), cd_ref[...]) + _dot(xk[n2:].astype(BF16), sd_ref[...])
        o_ref[:, g * FT_WIDTH:(g + 1) * FT_WIDTH] = (f * scale).astype(BF16)


def _dft_mats(n):
    ang = 2.0 * np.pi * np.outer(np.arange(n), np.arange(n)) / n
    return np.cos(ang), np.sin(ang)


def _channel_dft():
    c, s = _dft_mats(FT_GROUP_DIM)
    eye = np.eye(FT_GROUPS)
    return jnp.asarray(np.kron(eye, c), BF16), jnp.asarray(np.kron(eye, s), BF16)


def _fourier_lat(u, n):
    nb, p, _ = u.shape
    n2 = n // FT_N1
    cols = n2 * FT_WIDTH
    c1, s1 = _dft_mats(FT_N1)
    w1 = jnp.asarray(np.concatenate([c1, -s1], axis=0), BF16)
    tcol = min(2048, cols)
    a = pl.pallas_call(
        _ft_stage1_kernel,
        grid=(nb, cols // tcol),
        in_specs=[
            pl.BlockSpec(w1.shape, lambda b, j: (0, 0)),
            pl.BlockSpec((None, FT_N1, tcol), lambda b, j: (b, 0, j)),
        ],
        out_specs=pl.BlockSpec((None, 2 * FT_N1, tcol), lambda b, j: (b, 0, j)),
        out_shape=jax.ShapeDtypeStruct((nb, 2 * FT_N1, cols), F32),
        compiler_params=_cparams(2),
        name="fourier_stage1",
    )(w1, u.reshape(nb, p // n2, cols))
    a = a.reshape(nb, 2 * FT_N1, n2, FT_WIDTH)

    ang = 2.0 * np.pi * np.outer(np.arange(FT_N1), np.arange(n2)) / n
    tc = jnp.asarray(np.repeat(np.cos(ang)[:, :, None], LANES, axis=2), F32)
    ts = jnp.asarray(np.repeat(np.sin(ang)[:, :, None], LANES, axis=2), F32)
    c3, s3 = _dft_mats(n2)
    w3 = jnp.asarray(np.block([[c3, s3], [-s3, c3]]), BF16)
    cd, sd = _channel_dft()
    kern = functools.partial(_ft_stage2_kernel, scale=1.0 / math.sqrt(n * FT_GROUP_DIM))
    g = FT_K1_PER_STEP
    n_steps = FT_N1 // g
    out = pl.pallas_call(
        kern,
        grid=(nb, n_steps),
        in_specs=[
            pl.BlockSpec((None, g, n2, FT_WIDTH), lambda b, k: (b, k, 0, 0)),
            pl.BlockSpec((None, g, n2, FT_WIDTH), lambda b, k: (b, n_steps + k, 0, 0)),
            pl.BlockSpec((g, n2, LANES), lambda b, k: (k, 0, 0)),
            pl.BlockSpec((g, n2, LANES), lambda b, k: (k, 0, 0)),
            pl.BlockSpec(w3.shape, lambda b, k: (0, 0)),
            pl.BlockSpec(cd.shape, lambda b, k: (0, 0)),
            pl.BlockSpec(sd.shape, lambda b, k: (0, 0)),
        ],
        out_specs=pl.BlockSpec((None, n2, g * FT_WIDTH), lambda b, k: (b, 0, k)),
        out_shape=jax.ShapeDtypeStruct((nb, n2, FT_N1 * FT_WIDTH), BF16),
        compiler_params=_cparams(2),
        name="fourier_stage2",
    )(a, a, tc, ts, w3, cd, sd)
    return out.reshape(nb, n, FT_WIDTH)


def _ft_ctx_kernel(u_ref, cn_ref, sn_ref, cd_ref, sd_ref, o_ref, *, scale):
    u = u_ref[...]
    y1 = _dot(u, cd_ref[...]).astype(BF16)
    y2 = _dot(u, sd_ref[...]).astype(BF16)
    f = _dot(cn_ref[...], y1) - _dot(sn_ref[...], y2)
    o_ref[...] = (f * scale).astype(BF16)


def _fourier_ctx(u, s):
    nb, p, _ = u.shape
    n = p - s
    cn, sn = _dft_mats(n)
    cn, sn = jnp.asarray(cn, BF16), jnp.asarray(sn, BF16)
    cd, sd = _channel_dft()
    kern = functools.partial(_ft_ctx_kernel, scale=1.0 / math.sqrt(n * FT_GROUP_DIM))
    sq = pl.BlockSpec((n, n), lambda b: (0, 0))
    ch = pl.BlockSpec(cd.shape, lambda b: (0, 0))
    return pl.pallas_call(
        kern,
        grid=(nb,),
        in_specs=[pl.BlockSpec((None, n, FT_WIDTH), lambda b: (b, s // n, 0)), sq, sq, ch, ch],
        out_specs=pl.BlockSpec((None, n, FT_WIDTH), lambda b: (b, 0, 0)),
        out_shape=jax.ShapeDtypeStruct((nb, n, FT_WIDTH), BF16),
        compiler_params=_cparams(1),
        name="fourier_ctx",
    )(u, cn, sn, cd, sd)


def _merge_kernel(x_ref, oa_ref, ob_ref, oc_ref, ca_ref, cb_ref, cc_ref, mod_ref, n1g_ref, n2g_ref, wgate_ref,
                  bgate_ref, wba_ref, wbb_ref, wbc_ref, wout_ref, x1_ref, h2_ref, *, n_lat_tiles):
    d = x_ref.shape[-1]
    x = x_ref[...]
    is_ctx = pl.program_id(1) >= n_lat_tiles
    o_a = jnp.where(is_ctx, ca_ref[...], oa_ref[...])
    o_b = jnp.where(is_ctx, cb_ref[...], ob_ref[...])
    o_c = jnp.where(is_ctx, cc_ref[...], oc_ref[...])
    mod = mod_ref[...]
    sh1, sc1, g1, sh2, sc2 = (mod[:, i * d:(i + 1) * d] for i in range(5))
    hb = (_rms(x, n1g_ref[...]) * (1.0 + sc1) + sh1).astype(BF16)
    m = None
    for i, (o, w_ref) in enumerate(((o_a, wba_ref), (o_b, wbb_ref), (o_c, wbc_ref))):
        cols = slice(i * d, (i + 1) * d)
        gate = jax.nn.sigmoid(_dot(hb, wgate_ref[:, cols]) + bgate_ref[:, cols])
        term = gate * _dot(o, w_ref[...])
        m = term if m is None else m + term
    y = _dot(m.astype(BF16), wout_ref[...])
    x1 = x + g1 * y
    x1_ref[...] = x1
    h2_ref[...] = (_rms(x1, n2g_ref[...]) * (1.0 + sc2) + sh2).astype(BF16)


def _merge(x, lat_outs, ctx_outs, mod, mod_row, lw, layer):
    nb, n, d = x.shape
    tm = TOKEN_TILE
    n_lat_tiles = lat_outs[0].shape[1] // tm

    def full(a):
        return _layer_spec(a, layer)

    def tok(width):
        return pl.BlockSpec((None, tm, width), lambda b, i: (b, i, 0))

    def lat(a):
        return pl.BlockSpec((None, tm, a.shape[-1]), lambda b, i: (b, jnp.minimum(i, n_lat_tiles - 1), 0))

    def ctx(a):
        return pl.BlockSpec((None, tm, a.shape[-1]), lambda b, i: (b, jnp.maximum(i - n_lat_tiles, 0), 0))

    weights = [lw["norm1_g"], lw["norm2_g"], lw["w_gate"], lw["b_gate"], lw["w_br_mla"], lw["w_br_na"],
               lw["w_br_ft"], lw["w_out"]]
    return pl.pallas_call(
        functools.partial(_merge_kernel, n_lat_tiles=n_lat_tiles),
        grid=(nb, n // tm),
        in_specs=[tok(d)] + [lat(a) for a in lat_outs] + [ctx(a) for a in ctx_outs] + [_mod_spec(mod, layer, mod_row)]
        + [full(w) for w in weights],
        out_specs=[tok(d), tok(d)],
        out_shape=[jax.ShapeDtypeStruct((nb, n, d), F32), jax.ShapeDtypeStruct((nb, n, d), BF16)],
        compiler_params=_cparams(2),
        name="merge",
    )(x, *lat_outs, *ctx_outs, mod, *weights)


def _route(logits, router_bias):
    row = lax.broadcasted_iota(jnp.int32, logits.shape, 0)
    neg = -jnp.inf
    big = jnp.int32(N_EXPERTS)
    scores = jax.nn.sigmoid(logits)
    sel = scores + router_bias

    def top2(vals):
        m1 = jnp.max(vals, axis=0, keepdims=True)
        i1 = jnp.min(jnp.where(vals == m1, row, big), axis=0, keepdims=True)
        rest = jnp.where(row == i1, neg, vals)
        m2 = jnp.max(rest, axis=0, keepdims=True)
        i2 = jnp.min(jnp.where(rest == m2, row, big), axis=0, keepdims=True)
        return m1, i1, m2, i2

    grp_scores = []
    for g in range(N_GROUPS):
        in_grp = (row >= g * EXPERTS_PER_GROUP) & (row < (g + 1) * EXPERTS_PER_GROUP)
        m1, _, m2, _ = top2(jnp.where(in_grp, sel, neg))
        grp_scores.append(m1 + m2)
    best = functools.reduce(jnp.maximum, grp_scores)
    grp = jnp.full(best.shape, N_GROUPS - 1, jnp.int32)
    for g in range(N_GROUPS - 2, -1, -1):
        grp = jnp.where(grp_scores[g] == best, g, grp)
    in_sel = (row >= grp * EXPERTS_PER_GROUP) & (row < (grp + 1) * EXPERTS_PER_GROUP)
    _, i1, _, i2 = top2(jnp.where(in_sel, sel, neg))
    hit1 = row == i1
    hit2 = row == i2
    w1 = jnp.sum(jnp.where(hit1, scores, 0.0), axis=0, keepdims=True)
    w2 = jnp.sum(jnp.where(hit2, scores, 0.0), axis=0, keepdims=True)
    total = w1 + w2
    return jnp.where(hit1, w1 / total, 0.0) + jnp.where(hit2, w2 / total, 0.0), grp


def _split_bf16(v):
    hi = v.astype(BF16)
    return hi, (v - hi.astype(F32)).astype(BF16)


def _moe_kernel(h_ref, x_ref, mod_ref, cmod_ref, rw_ref, rb_ref, fg_ref, tri_ref, wg_ref, wu_ref, wd_ref, o_ref,
                col_scr, row_scr, count_scr, acc_scr, *, apply_final_norm, n_lat, cap):
    d = x_ref.shape[-1]
    tm = x_ref.shape[0]
    step = pl.program_id(2)
    n_steps = pl.num_programs(2)
    h = h_ref[...]

    @pl.when(step == 0)
    def _():
        gate_t, grp = _route(_dot_nt(rw_ref[...], h), rb_ref[...])
        member = jnp.where(lax.broadcasted_iota(jnp.int32, (SUBLANES, tm), 0) == grp, 1.0, 0.0)
        rank = _dot(member.astype(BF16), tri_ref[...])
        empty_rows = jnp.zeros((SUBLANES - EXPERTS_PER_GROUP - 2, tm), F32)
        per_group = [jnp.concatenate([gate_t[c * EXPERTS_PER_GROUP:(c + 1) * EXPERTS_PER_GROUP], rank[c:c + 1],
                                      member[c:c + 1], empty_rows], axis=0) for c in range(N_GROUPS)]
        pad_rows = jnp.zeros((LANES - N_GROUPS * SUBLANES, tm), F32)
        cols = jnp.concatenate(per_group + [pad_rows], axis=0).T
        for c in range(N_GROUPS):
            count_scr[c] = jnp.sum(member[c:c + 1]).astype(jnp.int32)
            row_scr[c] = per_group[c]
            col_scr[c] = pltpu.roll(cols, (LANES - c * SUBLANES) % LANES, axis=1) if c else cols
        acc_scr[...] = jnp.zeros(acc_scr.shape, F32)

    def experts(hx, gates):
        out = jnp.zeros((hx.shape[0], d), F32)
        for j in range(EXPERTS_PER_STEP):
            a = _dot(hx, wg_ref[j])
            u = _dot(hx, wu_ref[j])
            hid = (a * jax.nn.sigmoid(a)) * u * gates[:, j:j + 1]
            out += _dot(hid.astype(BF16), wd_ref[j])
        return out

    rank_lane, member_lane = EXPERTS_PER_GROUP, EXPERTS_PER_GROUP + 1

    n_members = count_scr[step]

    def compacted(rows):
        rank_row = row_scr[step, rank_lane:rank_lane + 1, :]
        member_row = row_scr[step, member_lane:member_lane + 1, :]
        slot = lax.broadcasted_iota(jnp.int32, (rows, tm), 0).astype(F32)
        pick = jnp.where((slot == rank_row) & (member_row > 0.5), 1.0, 0.0).astype(BF16)
        hx = _dot(pick, h).astype(BF16)
        col = col_scr[step]
        g_hi, g_lo = _split_bf16(col)
        y = experts(hx, _dot(pick, g_hi) + _dot(pick, g_lo))
        slot_t = lax.broadcasted_iota(jnp.int32, (tm, rows), 1).astype(F32)
        put = jnp.where((slot_t == col[:, rank_lane:rank_lane + 1]) & (col[:, member_lane:member_lane + 1] > 0.5),
                        1.0, 0.0).astype(BF16)
        y_hi, y_lo = _split_bf16(y)
        acc_scr[...] += _dot(put, y_hi) + _dot(put, y_lo)

    small = cap // 2

    @pl.when(n_members <= small)
    def _():
        compacted(small)

    @pl.when((n_members > small) & (n_members <= cap))
    def _():
        compacted(cap)

    @pl.when(n_members > cap)
    def _():
        acc_scr[...] += experts(h, col_scr[step])

    @pl.when(step == n_steps - 1)
    def _():
        tok_row = pl.program_id(1) * tm + lax.broadcasted_iota(jnp.int32, (tm, 1), 0)
        g2 = jnp.where(tok_row >= n_lat, cmod_ref[:, 5 * d:6 * d], mod_ref[:, 5 * d:6 * d])
        x2 = x_ref[...] + g2 * acc_scr[...]
        o_ref[...] = _rms(x2, fg_ref[...]) if apply_final_norm else x2


def _moe_tile(p):
    return max(t for t in range(TOKEN_TILE, MOE_MAX_TILE + 1, TOKEN_TILE) if p % t == 0)


def _moe_capacity(tm):
    return min(tm, -(-(5 * tm // (4 * N_GROUPS)) // LANES) * LANES)


def _moe(h2, x1, mod, ctx_row, lw, layer, router_w, router_b, final_g, n_lat, n_rows, tm, apply_final_norm):
    assert EXPERTS_PER_STEP == EXPERTS_PER_GROUP
    nb, _, d = x1.shape
    n_steps = N_EXPERTS // EXPERTS_PER_STEP
    ff = lw["w_e_gate"].shape[-1]
    tok = pl.BlockSpec((None, tm, d), lambda b, i, c: (b, i, 0))
    earlier = jnp.asarray(np.triu(np.ones((tm, tm), np.float32), k=1), BF16)
    kern = functools.partial(_moe_kernel, apply_final_norm=apply_final_norm, n_lat=n_lat, cap=_moe_capacity(tm))
    return pl.pallas_call(
        kern,
        grid=(nb, n_rows // tm, n_steps),
        in_specs=[
            tok, tok,
            _mod_spec(mod, layer, lambda b, i: b),
            _mod_spec(mod, layer, lambda b, i: ctx_row),
            pl.BlockSpec(router_w.shape, lambda b, i, c: (0, 0)),
            pl.BlockSpec(router_b.shape, lambda b, i, c: (0, 0)),
            pl.BlockSpec(final_g.shape, lambda b, i, c: (0, 0)),
            pl.BlockSpec(earlier.shape, lambda b, i, c: (0, 0)),
            pl.BlockSpec((None, EXPERTS_PER_STEP, d, ff), lambda b, i, c: (layer, c, 0, 0)),
            pl.BlockSpec((None, EXPERTS_PER_STEP, d, ff), lambda b, i, c: (layer, c, 0, 0)),
            pl.BlockSpec((None, EXPERTS_PER_STEP, ff, d), lambda b, i, c: (layer, c, 0, 0)),
        ],
        out_specs=tok,
        out_shape=jax.ShapeDtypeStruct((nb, n_rows, d), F32),
        scratch_shapes=[
            pltpu.VMEM((n_steps, tm, LANES), F32),
            pltpu.VMEM((n_steps, SUBLANES, tm), F32),
            pltpu.SMEM((n_steps,), jnp.int32),
            pltpu.VMEM((tm, d), F32),
        ],
        compiler_params=_cparams(3),
        name="moe",
    )(h2, x1, mod, mod, router_w, router_b, final_g, earlier, lw["w_e_gate"], lw["w_e_up"], lw["w_e_down"])


def _rotate_cols(w):
    half = w.shape[-1] // 2
    return jnp.concatenate([-w[..., half:], w[..., :half]], axis=-1)


def _prepare_weights(w_in, q_norm_g, w_q_up, kv_norm_g, w_kv_up, norm1_g, norm2_g, w_gate, b_gate, w_br_mla,
                     w_br_na, w_br_ft, w_out, w_e_gate, w_e_up, w_e_down):
    n_layers, d, _ = w_in.shape
    splits = np.cumsum([MLA_Q_LORA, MLA_KV_LORA, MLA_ROPE, NA_WIDTH, NA_WIDTH, NA_WIDTH])
    w_q, w_kv, w_kr, w_nq, w_nk, w_nv, w_ft = jnp.split(w_in, [int(s) for s in splits], axis=-1)

    def in_rope_slot(w):
        zeros_lo = jnp.zeros(w.shape[:-1] + (MLA_NOPE,), w.dtype)
        zeros_hi = jnp.zeros(w.shape[:-1] + (HEAD_PAD - MLA_NOPE - MLA_ROPE,), w.dtype)
        return jnp.concatenate([zeros_lo, w, zeros_hi], axis=-1)

    win = jnp.concatenate([w_q, w_kv, in_rope_slot(w_kr), in_rope_slot(_rotate_cols(w_kr)), w_nq, w_nk, w_nv, w_ft],
                          axis=-1).astype(BF16)
    wq = w_q_up.reshape(n_layers, MLA_Q_LORA, MLA_HEADS, MLA_NOPE + MLA_ROPE)
    wq_nope, wq_pe = wq[..., :MLA_NOPE], wq[..., MLA_NOPE:]
    pad = jnp.zeros(wq_pe.shape[:-1] + (HEAD_PAD - MLA_NOPE - MLA_ROPE,), wq.dtype)
    wq_main = jnp.concatenate([wq_nope, wq_pe, pad], axis=-1)
    wq_rot = jnp.concatenate([jnp.zeros_like(wq_nope), _rotate_cols(wq_pe), pad], axis=-1)
    wkv = w_kv_up.reshape(n_layers, MLA_KV_LORA, MLA_HEADS, MLA_NOPE + MLA_V)
    wk_nope, wv = wkv[..., :MLA_NOPE], wkv[..., MLA_NOPE:]
    wkv_k = jnp.concatenate([wk_nope, jnp.zeros(wk_nope.shape[:-1] + (HEAD_PAD - MLA_NOPE,), wkv.dtype)], axis=-1)
    wide = MLA_HEADS * HEAD_PAD
    return {
        "w_in": win,
        "wq_main": wq_main.reshape(n_layers, MLA_Q_LORA, wide).astype(BF16),
        "wq_rot": wq_rot.reshape(n_layers, MLA_Q_LORA, wide).astype(BF16),
        "wkv_k": wkv_k.reshape(n_layers, MLA_KV_LORA, wide).astype(BF16),
        "wkv_v": jnp.concatenate([wv, jnp.zeros(wv.shape[:-1] + (HEAD_PAD - MLA_V,), wv.dtype)], axis=-1)
        .reshape(n_layers, MLA_KV_LORA, wide).astype(BF16),
        "q_norm_g": q_norm_g.reshape(n_layers, 1, -1),
        "kv_norm_g": kv_norm_g.reshape(n_layers, 1, -1),
        "norm1_g": norm1_g.reshape(n_layers, 1, d),
        "norm2_g": norm2_g.reshape(n_layers, 1, d),
        "w_gate": w_gate.astype(BF16),
        "b_gate": b_gate.reshape(n_layers, 1, -1),
        "w_br_mla": w_br_mla.astype(BF16),
        "w_br_na": w_br_na.astype(BF16),
        "w_br_ft": w_br_ft.astype(BF16),
        "w_out": w_out.astype(BF16),
        "w_e_gate": w_e_gate.astype(BF16),
        "w_e_up": w_e_up.astype(BF16),
        "w_e_down": w_e_down.astype(BF16),
    }


def _rope_tables(s, c):
    n_freq = MLA_ROPE // 4
    inv_freq = ROPE_THETA ** (-jnp.arange(n_freq, dtype=F32) / n_freq)
    t = jnp.arange(s, dtype=jnp.int32)
    row = (t // GRID_W).astype(F32)
    col = (t % GRID_W).astype(F32)
    ang = jnp.concatenate([row[:, None] * inv_freq, col[:, None] * inv_freq], axis=-1)
    cos, sin = jnp.cos(ang), jnp.sin(ang)
    pad = HEAD_PAD - MLA_NOPE - MLA_ROPE
    cos_lat = jnp.concatenate([jnp.ones((s, MLA_NOPE), F32), cos, cos, jnp.zeros((s, pad), F32)], axis=-1)
    sin_lat = jnp.concatenate([jnp.zeros((s, MLA_NOPE), F32), sin, sin, jnp.zeros((s, pad), F32)], axis=-1)
    cos_ctx = jnp.concatenate([jnp.ones((c, MLA_NOPE + MLA_ROPE), F32), jnp.zeros((c, pad), F32)], axis=-1)
    sin_ctx = jnp.zeros((c, HEAD_PAD), F32)
    return cos_lat, sin_lat, cos_ctx, sin_ctx


@jax.jit
def _forward(x, c, ctx, c_ctx, w_ada, b_ada, norm1_g, norm2_g, w_in, q_norm_g, w_q_up, kv_norm_g, w_kv_up,
             na_rpb, w_gate, b_gate, w_br_mla, w_br_na, w_br_ft, w_out, router_w, router_bias, w_e_gate,
             w_e_up, w_e_down, final_norm_g):
    nb, s, d = x.shape
    n_ctx = ctx.shape[1]
    n_layers = w_ada.shape[0]
    grid_rows = s // GRID_W

    mod_rows = 8
    c_rows = jnp.zeros((mod_rows, d), F32).at[:nb].set(c).at[nb].set(c_ctx)
    mod_all = _modulation(c_rows, w_ada, b_ada).reshape(n_layers, mod_rows, 1, 6 * d)

    weights = _prepare_weights(w_in, q_norm_g, w_q_up, kv_norm_g, w_kv_up, norm1_g, norm2_g, w_gate, b_gate,
                               w_br_mla, w_br_na, w_br_ft, w_out, w_e_gate, w_e_up, w_e_down)
    na_bias = _na_bias_tables(na_rpb)
    na_row_mask = _na_row_masks(grid_rows)
    cos_lat, sin_lat, cos_ctx, sin_ctx = _rope_tables(s, n_ctx)
    cos_t = jnp.concatenate([cos_lat, cos_ctx], axis=0)
    sin_t = jnp.concatenate([sin_lat, sin_ctx], axis=0)
    router_w_p = router_w.T.astype(BF16)
    router_b_p = router_bias.reshape(N_EXPERTS, 1).astype(F32)
    final_g = final_norm_g.reshape(1, d)

    p = s + n_ctx
    n_lat_tiles = s // TOKEN_TILE
    ctx_row = nb

    def tile_row(b, i):
        return jnp.where(i >= n_lat_tiles, ctx_row, b)

    x_all = jnp.concatenate([x, ctx], axis=1)
    zero_ctx = [jnp.zeros((nb, n_ctx, w), BF16) for w in (MLA_HEADS * MLA_V, NA_WIDTH, FT_WIDTH)]
    for l in range(n_layers):
        last = l == n_layers - 1
        q, k, v, nq, nk, nv, ft = _project(x_all, mod_all, tile_row, weights, l, cos_t, sin_t)
        lat_outs = [_mla_attention(q, k, v, s), _na_attention(nq, nk, nv, na_bias, na_row_mask, l, s),
                    _fourier_lat(ft, s)]
        if last:
            ctx_outs = zero_ctx
        else:
            ctx_outs = [_mla_ctx_attention(q, k, v, s), _na_ctx_attention(nq, nk, nv, s), _fourier_ctx(ft, s)]
        x1, h2 = _merge(x_all, lat_outs, ctx_outs, mod_all, tile_row, weights, l)
        if last:
            return _moe(h2, x1, mod_all, ctx_row, weights, l, router_w_p, router_b_p, final_g, s, s,
                        _moe_tile(s), True)
        x_all = _moe(h2, x1, mod_all, ctx_row, weights, l, router_w_p, router_b_p, final_g, s, p,
                     _moe_tile(p), False)


def kernel(x, c, ctx, c_ctx, w_ada, b_ada, norm1_g, norm2_g, w_in, q_norm_g, w_q_up, kv_norm_g, w_kv_up, na_rpb, w_gate, b_gate, w_br_mla, w_br_na, w_br_ft, w_out, router_w, router_bias, w_e_gate, w_e_up, w_e_down, final_norm_g):
    return _forward(x, c, ctx, c_ctx, w_ada, b_ada, norm1_g, norm2_g, w_in, q_norm_g, w_q_up, kv_norm_g,
                    w_kv_up, na_rpb, w_gate, b_gate, w_br_mla, w_br_na, w_br_ft, w_out, router_w, router_bias,
                    w_e_gate, w_e_up, w_e_down, final_norm_g)
```
